```python
import jax
import jax.numpy as jnp
from jax import lax
import numpy as np

D_MODEL = 1024
BATCH = 8
SEQ = 2048
DEPTH = 1

PLE_DIM = 256
SGU_WIDTH = D_MODEL
SGU_CHUNK = 128
SGU_GROUPS = 8
SGU_GROUP_WIDTH = SGU_WIDTH // SGU_GROUPS
V_DIM = 128
MLA_HEADS = D_MODEL // V_DIM
QK_NOPE = 128
QK_ROPE = 64
QK_DIM = QK_NOPE + QK_ROPE
Q_LORA = 384
KV_LORA = 256
ROPE_THETA = 10000.0
Q_BLOCK = 128
N_EXPERTS = 64
N_GROUPS = 8
EXPERTS_PER_GROUP = N_EXPERTS // N_GROUPS
TOPK_GROUPS = 4
TOP_K = 8
EXPERT_FF = 256
SHARED_FF = 256
ROUTED_SCALE = 2.5

NORM_EPS = 1e-6
LN_EPS = 1e-5
NEG_INF = -1e30

kernel_name = 'hybrid_sgu_mla_moe_block'


def _in_sizes():
    return [SGU_WIDTH, SGU_WIDTH, Q_LORA, KV_LORA, QK_ROPE, SGU_WIDTH, MLA_HEADS * V_DIM]


def _in_offsets():
    offs, acc = [], 0
    for s in _in_sizes()[:-1]:
        acc += s
        offs.append(acc)
    return offs


def rms_norm(x, g):
    xf = x.astype(jnp.float32)
    y = xf * lax.rsqrt(jnp.mean(xf * xf, axis=-1, keepdims=True) + NORM_EPS)
    return (y * g.astype(jnp.float32)).astype(x.dtype)


def layer_norm(x, g, b):
    xf = x.astype(jnp.float32)
    mu = jnp.mean(xf, axis=-1, keepdims=True)
    xc = xf - mu
    var = jnp.mean(xc * xc, axis=-1, keepdims=True)
    y = xc * lax.rsqrt(var + LN_EPS) * g.astype(jnp.float32) + b.astype(jnp.float32)
    return y.astype(x.dtype)


def rope_tables(positions, dtype):
    half = QK_ROPE // 2
    inv_freq = 1.0 / (ROPE_THETA ** (jnp.arange(half, dtype=jnp.float32) * (2.0 / QK_ROPE)))
    ang = positions.astype(jnp.float32)[..., None] * inv_freq
    return jnp.cos(ang)[:, :, None, :].astype(dtype), jnp.sin(ang)[:, :, None, :].astype(dtype)


def apply_rope(x, cos, sin):
    half = x.shape[-1] // 2
    x1, x2 = x[..., :half], x[..., half:]
    return jnp.concatenate([x1 * cos - x2 * sin, x2 * cos + x1 * sin], axis=-1)


def spatial_gating(u, v, ln_g, ln_b, w_sp, b_sp):
    B, S, _ = u.shape
    v = layer_norm(v, ln_g, ln_b)
    causal = jnp.tril(jnp.ones((SGU_CHUNK, SGU_CHUNK), dtype=bool))
    w = jnp.where(causal[None], w_sp, 0.0).astype(v.dtype)
    vc = v.reshape(B, S // SGU_CHUNK, SGU_CHUNK, SGU_GROUPS, SGU_GROUP_WIDTH)
    mixed = jnp.einsum('gij,bnjgc->bnigc', w, vc) + b_sp.T[:, :, None].astype(v.dtype)
    return u * mixed.reshape(B, S, SGU_WIDTH)


def mla_attention(q_lat, kv_lat, k_rope, cos, sin, g_qa, w_qb, g_kva, w_kvb,
                  g_q_nope, g_k_nope, g_q_rope, g_k_rope):
    B, S, _ = q_lat.shape
    H = MLA_HEADS
    q = (rms_norm(q_lat, g_qa) @ w_qb).reshape(B, S, H, QK_DIM)
    kv = (rms_norm(kv_lat, g_kva) @ w_kvb).reshape(B, S, H, QK_NOPE + V_DIM)
    q_nope = rms_norm(q[..., :QK_NOPE], g_q_nope)
    q_pe = apply_rope(rms_norm(q[..., QK_NOPE:], g_q_rope), cos, sin)
    k_nope = rms_norm(kv[..., :QK_NOPE], g_k_nope)
    v = kv[..., QK_NOPE:]
    k_pe = apply_rope(rms_norm(k_rope, g_k_rope)[:, :, None, :], cos, sin)
    k = jnp.concatenate([k_nope, jnp.broadcast_to(k_pe, (B, S, H, QK_ROPE))], axis=-1)
    qh = jnp.concatenate([q_nope, q_pe], axis=-1) * (QK_DIM ** -0.5)
    n_blk = S // Q_BLOCK
    q_blocks = qh.reshape(B, n_blk, Q_BLOCK, H, QK_DIM).transpose(1, 0, 3, 2, 4)
    kh = k.transpose(0, 2, 1, 3)
    vh = v.transpose(0, 2, 1, 3)
    kpos = jnp.arange(S)

    def block(args):
        q_i, i = args
        s = jnp.einsum('bhqd,bhkd->bhqk', q_i, kh).astype(jnp.float32)
        qpos = i * Q_BLOCK + jnp.arange(Q_BLOCK)
        s = jnp.where(kpos[None, :] <= qpos[:, None], s, NEG_INF)
        prob = jax.nn.softmax(s, axis=-1).astype(vh.dtype)
        return jnp.einsum('bhqk,bhkd->bhqd', prob, vh)

    o = lax.map(block, (q_blocks, jnp.arange(n_blk)))
    return o.transpose(1, 0, 3, 2, 4).reshape(B, S, H * V_DIM)


def moe(h, w_router, b_router, w_gate, w_up, w_down, w_sh_gate, w_sh_up, w_sh_down):
    B, S, D = h.shape
    t = h.reshape(B * S, D)
    scores = jax.nn.sigmoid((t @ w_router).astype(jnp.float32))
    sel = scores + b_router.astype(jnp.float32)
    grp = sel.reshape(-1, N_GROUPS, EXPERTS_PER_GROUP)
    group_score = lax.top_k(grp, 2)[0].sum(-1)
    _, gidx = lax.top_k(group_score, TOPK_GROUPS)
    gmask = jax.nn.one_hot(gidx, N_GROUPS, dtype=jnp.float32).sum(1) > 0
    emask = jnp.repeat(gmask, EXPERTS_PER_GROUP, axis=1)
    _, eidx = lax.top_k(jnp.where(emask, sel, -jnp.inf), TOP_K)
    w = jnp.take_along_axis(scores, eidx, axis=1)
    w = w / jnp.sum(w, axis=-1, keepdims=True) * ROUTED_SCALE
    combine = (jax.nn.one_hot(eidx, N_EXPERTS, dtype=jnp.float32) * w[..., None]).sum(1).astype(t.dtype)
    y = (jax.nn.silu(t @ w_sh_gate) * (t @ w_sh_up)) @ w_sh_down
    for g in range(N_GROUPS):
        sl = slice(g * EXPERTS_PER_GROUP, (g + 1) * EXPERTS_PER_GROUP)
        a = jax.nn.silu(jnp.einsum('td,edf->tef', t, w_gate[sl])) * jnp.einsum('td,edf->tef', t, w_up[sl])
        y = y + jnp.einsum('tef,efd->td', a * combine[:, sl, None], w_down[sl])
    return y.reshape(B, S, D)


def setup_inputs(seed: int = 0) -> dict:
    key = jax.random.key(seed)
    ks = iter(jax.random.split(key, 48))
    f32 = jnp.float32
    L = DEPTH

    def nrm(shape, fan_in):
        return jax.random.normal(next(ks), shape, f32) * (fan_in ** -0.5)

    def gain(shape):
        return 1.0 + 0.05 * jax.random.normal(next(ks), shape, f32)

    def small(shape, s):
        return s * jax.random.normal(next(ks), shape, f32)

    in_width = sum(_in_sizes())
    x = jax.random.normal(next(ks), (BATCH, SEQ, D_MODEL), f32)
    p = jax.random.normal(next(ks), (L, BATCH, SEQ, PLE_DIM), f32)
    offsets = jax.random.randint(next(ks), (BATCH, 1), 0, 4096, dtype=jnp.int32)
    positions = offsets + jnp.arange(SEQ, dtype=jnp.int32)[None, :]
    return {
        'x': x,
        'p': p,
        'positions': positions,
        'g_mix': gain((L, D_MODEL)),
        'w_in': nrm((L, D_MODEL, in_width), D_MODEL),
        'sgu_ln_g': gain((L, SGU_WIDTH)),
        'sgu_ln_b': small((L, SGU_WIDTH), 0.02),
        'sgu_w': nrm((L, SGU_GROUPS, SGU_CHUNK, SGU_CHUNK), SGU_CHUNK),
        'sgu_b': small((L, SGU_GROUPS, SGU_CHUNK), 0.02),
        'mla_g_qa': gain((L, Q_LORA)),
        'mla_w_qb': nrm((L, Q_LORA, MLA_HEADS * QK_DIM), Q_LORA),
        'mla_g_kva': gain((L, KV_LORA)),
        'mla_w_kvb': nrm((L, KV_LORA, MLA_HEADS * (QK_NOPE + V_DIM)), KV_LORA),
        'qk_g_q_nope': gain((L, QK_NOPE)),
        'qk_g_k_nope': gain((L, QK_NOPE)),
        'qk_g_q_rope': gain((L, QK_ROPE)),
        'qk_g_k_rope': gain((L, QK_ROPE)),
        'w_o': nrm((L, D_MODEL, D_MODEL), D_MODEL),
        'g_ffn': gain((L, D_MODEL)),
        'w_router': nrm((L, D_MODEL, N_EXPERTS), D_MODEL),
        'b_router': small((L, N_EXPERTS), 0.01),
        'w_exp_gate': nrm((L, N_EXPERTS, D_MODEL, EXPERT_FF), D_MODEL),
        'w_exp_up': nrm((L, N_EXPERTS, D_MODEL, EXPERT_FF), D_MODEL),
        'w_exp_down': nrm((L, N_EXPERTS, EXPERT_FF, D_MODEL), EXPERT_FF),
        'w_sh_gate': nrm((L, D_MODEL, SHARED_FF), D_MODEL),
        'w_sh_up': nrm((L, D_MODEL, SHARED_FF), D_MODEL),
        'w_sh_down': nrm((L, SHARED_FF, D_MODEL), SHARED_FF),
        'g_ple_in': gain((L, D_MODEL)),
        'w_ple_gate': nrm((L, D_MODEL, D_MODEL), D_MODEL),
        'w_ple_proj': nrm((L, PLE_DIM, D_MODEL), PLE_DIM),
        'g_ple_out': gain((L, D_MODEL)),
    }


def reference(x, p, positions, g_mix, w_in, sgu_ln_g, sgu_ln_b, sgu_w, sgu_b,
              mla_g_qa, mla_w_qb, mla_g_kva, mla_w_kvb,
              qk_g_q_nope, qk_g_k_nope, qk_g_q_rope, qk_g_k_rope,
              w_o, g_ffn, w_router, b_router, w_exp_gate, w_exp_up, w_exp_down,
              w_sh_gate, w_sh_up, w_sh_down, g_ple_in, w_ple_gate, w_ple_proj, g_ple_out):
    cos, sin = rope_tables(positions, x.dtype)
    offs = _in_offsets()
    for l in range(DEPTH):
        h = rms_norm(x, g_mix[l])
        z = h @ w_in[l]
        u, v, q_lat, kv_lat, k_rope, gate_a, gate_b = jnp.split(z, offs, axis=-1)
        y_a = spatial_gating(jax.nn.gelu(u), jax.nn.gelu(v), sgu_ln_g[l], sgu_ln_b[l], sgu_w[l], sgu_b[l])
        y_b = mla_attention(q_lat, kv_lat, k_rope, cos, sin, mla_g_qa[l], mla_w_qb[l],
                            mla_g_kva[l], mla_w_kvb[l], qk_g_q_nope[l], qk_g_k_nope[l],
                            qk_g_q_rope[l], qk_g_k_rope[l])
        merged = jax.nn.sigmoid(gate_a) * y_a + jax.nn.sigmoid(gate_b) * y_b
        x = x + merged @ w_o[l]
        x = x + moe(rms_norm(x, g_ffn[l]), w_router[l], b_router[l], w_exp_gate[l], w_exp_up[l],
                    w_exp_down[l], w_sh_gate[l], w_sh_up[l], w_sh_down[l])
        ple_gate = jax.nn.sigmoid(rms_norm(x, g_ple_in[l]) @ w_ple_gate[l])
        x = x + ple_gate * rms_norm(p[l] @ w_ple_proj[l], g_ple_out[l])
    return x
```

```python
import functools
import math

import jax
import jax.numpy as jnp
from jax import lax
from jax.experimental import pallas as pl
from jax.experimental.pallas import tpu as pltpu

D_MODEL = 1024
PLE_DIM = 256
SGU_CHUNK = 128
SGU_GROUPS = 8
V_DIM = 128
HEADS = 8
QK_NOPE = 128
QK_ROPE = 64
QK_DIM = QK_NOPE + QK_ROPE
QK_PAD = 256
Q_LORA = 384
KV_LORA = 256
ROPE_THETA = 10000.0
N_EXPERTS = 64
N_GROUPS = 8
EXPERTS_PER_GROUP = 8
TOPK_GROUPS = 4
TOP_K = 8
EXPERT_FF = 256
ROUTED_SCALE = 2.5
NORM_EPS = 1e-6
LN_EPS = 1e-5

LANES = 128
VMEM_LIMIT = 56 * 1024 * 1024

F32 = jnp.float32
BF16 = jnp.bfloat16


def _dot(a, b):
    return jnp.dot(a, b, preferred_element_type=F32)


def _rms(xf, g, width=None):
    width = xf.shape[-1] if width is None else width
    ms = jnp.sum(xf * xf, axis=-1, keepdims=True) * (1.0 / width)
    return xf * lax.rsqrt(ms + NORM_EPS) * g


def _rope_tables(pos_f):
    lane = lax.broadcasted_iota(jnp.int32, (1, LANES), 1)
    half = QK_ROPE // 2
    freq = (lane % half).astype(F32)
    inv_freq = jnp.exp(freq * (-math.log(ROPE_THETA) * 2.0 / QK_ROPE))
    ang = pos_f * inv_freq
    c, s = jnp.cos(ang), jnp.sin(ang)
    cos_t = jnp.where(lane < QK_ROPE, c, 0.0)
    sin_t = jnp.where(lane < half, -s, jnp.where(lane < QK_ROPE, s, 0.0))
    return cos_t, sin_t


def _rope(piece, cos_t, sin_t):
    partner = pltpu.roll(piece, LANES - QK_ROPE // 2, axis=1) + pltpu.roll(piece, QK_ROPE // 2, axis=1)
    return piece * cos_t + partner * sin_t


def _mixer_prep_kernel(x_ref, pos_ref, g_mix_ref, wu_ref, wv_ref, wq_ref, wkv_ref, wr_ref, wga_ref, wgb_ref,
                       lng_ref, lnb_ref, sw_ref, sb_ref, gqa_ref, wqb_ref, gkva_ref, wkb_ref, wvb_ref,
                       gqn_ref, gkn_ref, gqr_ref, gkr_ref,
                       ma_ref, gb_ref, q_ref, k_ref, v_ref):
    tm = x_ref.shape[1]
    xn = _rms(x_ref[0], g_mix_ref[...]).astype(BF16)

    gv = jax.nn.gelu(_dot(xn, wv_ref[...]))
    mu = jnp.mean(gv, axis=-1, keepdims=True)
    vc = gv - mu
    var = jnp.mean(vc * vc, axis=-1, keepdims=True)
    vn = (vc * lax.rsqrt(var + LN_EPS) * lng_ref[...] + lnb_ref[...]).astype(BF16)
    row = lax.broadcasted_iota(jnp.int32, (SGU_CHUNK, SGU_CHUNK), 0)
    col = lax.broadcasted_iota(jnp.int32, (SGU_CHUNK, SGU_CHUNK), 1)
    causal = col <= row
    for g in range(SGU_GROUPS):
        cs = slice(g * SGU_CHUNK, (g + 1) * SGU_CHUNK)
        wg = jnp.where(causal, sw_ref[g], 0.0).astype(BF16)
        gu = jax.nn.gelu(_dot(xn, wu_ref[:, cs]))
        ga = jax.nn.sigmoid(_dot(xn, wga_ref[:, cs]))
        for c in range(tm // SGU_CHUNK):
            rs = slice(c * SGU_CHUNK, (c + 1) * SGU_CHUNK)
            mixed = _dot(wg, vn[rs, cs]) + sb_ref[:, cs]
            ma_ref[0, rs, cs] = (ga[rs] * gu[rs] * mixed).astype(BF16)

    gb_ref[0] = jax.nn.sigmoid(_dot(xn, wgb_ref[...])).astype(BF16)

    cos_t, sin_t = _rope_tables(pos_ref[0].astype(F32))
    scale = QK_DIM ** -0.5
    qn = _rms(_dot(xn, wq_ref[...]), gqa_ref[...]).astype(BF16)
    kvn = _rms(_dot(xn, wkv_ref[...]), gkva_ref[...]).astype(BF16)
    kpe = _rope(_rms(_dot(xn, wr_ref[...]), gkr_ref[...], QK_ROPE), cos_t, sin_t).astype(BF16)
    for h in range(HEADS):
        qh = _dot(qn, wqb_ref[:, h * QK_PAD:(h + 1) * QK_PAD])
        q_nope = _rms(qh[:, :QK_NOPE], gqn_ref[...]) * scale
        q_pe = _rope(_rms(qh[:, QK_NOPE:], gqr_ref[...], QK_ROPE), cos_t, sin_t) * scale
        q_ref[0, h, :, :QK_NOPE] = q_nope.astype(BF16)
        q_ref[0, h, :, QK_NOPE:] = q_pe.astype(BF16)
        hs = slice(h * QK_NOPE, (h + 1) * QK_NOPE)
        k_ref[0, h, :, :QK_NOPE] = _rms(_dot(kvn, wkb_ref[:, hs]), gkn_ref[...]).astype(BF16)
        k_ref[0, h, :, QK_NOPE:] = kpe
        v_ref[0, h] = _dot(kvn, wvb_ref[:, hs]).astype(BF16)


def _const_spec(shape):
    return pl.BlockSpec(shape, lambda *_: (0,) * len(shape))


def _mixer_prep(x, pos, consts, tm):
    b, s, d = x.shape
    grid = (b, s // tm)
    row_spec = lambda w: pl.BlockSpec((1, tm, w), lambda i, j: (i, j, 0))
    head_spec = lambda w: pl.BlockSpec((1, HEADS, tm, w), lambda i, j: (i, 0, j, 0))
    return pl.pallas_call(
        _mixer_prep_kernel,
        grid=grid,
        in_specs=[row_spec(d), row_spec(1)] + [_const_spec(c.shape) for c in consts],
        out_specs=[row_spec(d), row_spec(d), head_spec(QK_PAD), head_spec(QK_PAD), head_spec(V_DIM)],
        out_shape=[jax.ShapeDtypeStruct((b, s, d), BF16), jax.ShapeDtypeStruct((b, s, d), BF16),
                   jax.ShapeDtypeStruct((b, HEADS, s, QK_PAD), BF16),
                   jax.ShapeDtypeStruct((b, HEADS, s, QK_PAD), BF16),
                   jax.ShapeDtypeStruct((b, HEADS, s, V_DIM), BF16)],
        compiler_params=pltpu.CompilerParams(dimension_semantics=("arbitrary", "arbitrary"),
                                             vmem_limit_bytes=VMEM_LIMIT),
        name="mixer_prep",
    )(x, pos, *consts)


def _attn_kernel(q_ref, k_ref, v_ref, o_ref, *, tq):
    s = q_ref.shape[2]
    row = lax.broadcasted_iota(jnp.int32, (tq, tq), 0)
    col = lax.broadcasted_iota(jnp.int32, (tq, tq), 1)
    diag_mask = col <= row

    def scores(q, start):
        kt = k_ref[0, 0, pl.ds(start, tq), :]
        return lax.dot_general(q, kt, (((1,), (1,)), ((), ())), preferred_element_type=F32)

    def update(carry, sc, start):
        m, l, acc = carry
        m_new = jnp.maximum(m, jnp.max(sc, axis=-1, keepdims=True))
        alpha = jnp.exp(m - m_new)
        p = jnp.exp(sc - m_new)
        l = alpha * l + jnp.sum(p, axis=-1, keepdims=True)
        vt = v_ref[0, 0, pl.ds(start, tq), :]
        acc = alpha * acc + _dot(p.astype(BF16), vt)
        return m_new, l, acc

    for qi in range(s // tq):
        q = q_ref[0, 0, qi * tq:(qi + 1) * tq, :]
        sc = jnp.where(diag_mask, scores(q, qi * tq), -jnp.inf)
        m0 = jnp.max(sc, axis=-1, keepdims=True)
        p0 = jnp.exp(sc - m0)
        carry = (m0, jnp.sum(p0, axis=-1, keepdims=True),
                 _dot(p0.astype(BF16), v_ref[0, 0, qi * tq:(qi + 1) * tq, :]))

        def body(kj, carry, q=q):
            start = pl.multiple_of(kj * tq, tq)
            return update(carry, scores(q, start), start)

        m, l, acc = lax.fori_loop(0, qi, body, carry)
        o_ref[0, qi * tq:(qi + 1) * tq, :] = (acc / l).astype(BF16)


def _attention(q, k, v, tq):
    b, h, s, _ = q.shape
    return pl.pallas_call(
        functools.partial(_attn_kernel, tq=tq),
        grid=(b, h),
        in_specs=[pl.BlockSpec((1, 1, s, QK_PAD), lambda i, j: (i, j, 0, 0)),
                  pl.BlockSpec((1, 1, s, QK_PAD), lambda i, j: (i, j, 0, 0)),
                  pl.BlockSpec((1, 1, s, V_DIM), lambda i, j: (i, j, 0, 0))],
        out_specs=pl.BlockSpec((1, s, V_DIM), lambda i, j: (i, 0, j)),
        out_shape=jax.ShapeDtypeStruct((b, s, h * V_DIM), BF16),
        compiler_params=pltpu.CompilerParams(dimension_semantics=("arbitrary", "arbitrary"),
                                             vmem_limit_bytes=VMEM_LIMIT),
        name="mla_attention",
    )(q, k, v)


def _route(sel, scores):
    e, tm = sel.shape
    sel3 = sel.reshape(N_GROUPS, EXPERTS_PER_GROUP, tm)
    sub = lax.broadcasted_iota(jnp.int32, sel3.shape, 1)
    m1 = jnp.max(sel3, axis=1, keepdims=True)
    first = jnp.min(jnp.where(sel3 == m1, sub, EXPERTS_PER_GROUP), axis=1, keepdims=True)
    m2 = jnp.max(jnp.where(sub == first, -jnp.inf, sel3), axis=1, keepdims=True)
    gscore = (m1 + m2).reshape(N_GROUPS, tm)
    gid = lax.broadcasted_iota(jnp.int32, (N_GROUPS, tm), 0)
    grank = jnp.zeros((N_GROUPS, tm), jnp.int32)
    for g in range(N_GROUPS):
        other = gscore[g:g + 1]
        grank += ((other > gscore) | ((other == gscore) & (g < gid))).astype(jnp.int32)
    gmask = grank < TOPK_GROUPS
    emask = jnp.broadcast_to(gmask[:, None, :], sel3.shape).reshape(e, tm)
    msel = jnp.where(emask, sel, -jnp.inf)
    eid = lax.broadcasted_iota(jnp.int32, (e, tm), 0)
    erank = jnp.zeros((e, tm), jnp.int32)
    for j in range(e):
        other = msel[j:j + 1]
        erank += ((other > msel) | ((other == msel) & (j < eid))).astype(jnp.int32)
    w = jnp.where(erank < TOP_K, scores, 0.0)
    return w / jnp.sum(w, axis=0, keepdims=True) * ROUTED_SCALE


def _post_attn_kernel(x_ref, ma_ref, gb_ref, o_ref, wo_ref, gffn_ref, wrt_ref, br_ref,
                      x1_ref, h2_ref, comb_ref):
    merged = ma_ref[...].astype(F32) + gb_ref[...].astype(F32) * o_ref[...].astype(F32)
    x1 = x_ref[...] + _dot(merged.astype(BF16), wo_ref[...])
    x1_ref[...] = x1
    h2 = _rms(x1, gffn_ref[...])
    h2_ref[...] = h2.astype(BF16)
    logits_t = lax.dot_general(wrt_ref[...], h2, (((1,), (1,)), ((), ())),
                               preferred_element_type=F32, precision=lax.Precision.HIGHEST)
    scores = jax.nn.sigmoid(logits_t)
    comb_t = _route(scores + br_ref[...], scores)
    comb_ref[...] = comb_t.T


def _post_attn(x, ma, gb, o, consts, tm):
    t, d = x.shape
    row_spec = lambda w: pl.BlockSpec((tm, w), lambda i: (i, 0))
    return pl.pallas_call(
        _post_attn_kernel,
        grid=(t // tm,),
        in_specs=[row_spec(d)] * 4 + [_const_spec(c.shape) for c in consts],
        out_specs=[row_spec(d), row_spec(d), row_spec(N_EXPERTS)],
        out_shape=[jax.ShapeDtypeStruct((t, d), F32), jax.ShapeDtypeStruct((t, d), BF16),
                   jax.ShapeDtypeStruct((t, N_EXPERTS), F32)],
        compiler_params=pltpu.CompilerParams(dimension_semantics=("arbitrary",), vmem_limit_bytes=VMEM_LIMIT),
        name="post_attn_router",
    )(x, ma, gb, o, *consts)


def _moe_kernel(h_ref, comb_ref, x1_ref, p_ref, wg_ref, wu_ref, wd_ref, wsg_ref, wsu_ref, wsd_ref,
                gpi_ref, wpg_ref, wpp_ref, gpo_ref, out_ref, acc_ref):
    e = pl.program_id(1)
    h = h_ref[...]

    @pl.when(e == 0)
    def _():
        a = jax.nn.silu(_dot(h, wsg_ref[...])) * _dot(h, wsu_ref[...])
        acc_ref[...] = _dot(a.astype(BF16), wsd_ref[...])

    onehot = (lax.broadcasted_iota(jnp.int32, (N_EXPERTS, EXPERT_FF), 0) == e).astype(F32)
    cw = jnp.dot(comb_ref[...], onehot, preferred_element_type=F32, precision=lax.Precision.HIGHEST)
    a = jax.nn.silu(_dot(h, wg_ref[0])) * _dot(h, wu_ref[0]) * cw
    acc_ref[...] += _dot(a.astype(BF16), wd_ref[0])

    @pl.when(e == pl.num_programs(1) - 1)
    def _():
        x2 = x1_ref[...] + acc_ref[...]
        gate = jax.nn.sigmoid(_dot(_rms(x2, gpi_ref[...]).astype(BF16), wpg_ref[...]))
        proj = _rms(_dot(p_ref[...].astype(BF16), wpp_ref[...]), gpo_ref[...])
        out_ref[...] = x2 + gate * proj


def _moe(h2, comb, x1, p, wg, wu, wd, consts, tm):
    t, d = h2.shape
    row_spec = lambda w: pl.BlockSpec((tm, w), lambda i, e: (i, 0))
    exp_spec = lambda a: pl.BlockSpec((1,) + a.shape[1:], lambda i, e: (e, 0, 0))
    return pl.pallas_call(
        _moe_kernel,
        grid=(t // tm, N_EXPERTS),
        in_specs=[row_spec(d), row_spec(N_EXPERTS), row_spec(d), row_spec(PLE_DIM),
                  exp_spec(wg), exp_spec(wu), exp_spec(wd)] + [_const_spec(c.shape) for c in consts],
        out_specs=row_spec(d),
        out_shape=jax.ShapeDtypeStruct((t, d), F32),
        scratch_shapes=[pltpu.VMEM((tm, d), F32)],
        compiler_params=pltpu.CompilerParams(dimension_semantics=("arbitrary", "arbitrary"),
                                             vmem_limit_bytes=VMEM_LIMIT),
        name="moe_ple",
    )(h2, comb, x1, p, wg, wu, wd, *consts)


def _pad_lanes(a, width):
    return jnp.pad(a, ((0, 0), (0, width - a.shape[-1])))


def _layer(x, p, pos, g_mix, w_in, sgu_ln_g, sgu_ln_b, sgu_w, sgu_b, mla_g_qa, mla_w_qb, mla_g_kva, mla_w_kvb,
           qk_g_q_nope, qk_g_k_nope, qk_g_q_rope, qk_g_k_rope, w_o, g_ffn, w_router, b_router,
           w_exp_gate, w_exp_up, w_exp_down, w_sh_gate, w_sh_up, w_sh_down,
           g_ple_in, w_ple_gate, w_ple_proj, g_ple_out):
    b, s, d = x.shape
    t = b * s
    row = lambda a: a.reshape(1, -1)
    sizes = [d, d, Q_LORA, KV_LORA, QK_ROPE, d, d]
    offs = [0]
    for sz in sizes:
        offs.append(offs[-1] + sz)
    w_u, w_v, w_q, w_kv, w_r, w_ga, w_gb = [w_in[:, offs[i]:offs[i + 1]].astype(BF16) for i in range(7)]
    w_r = _pad_lanes(w_r, LANES)
    wqb = jnp.pad(mla_w_qb.reshape(Q_LORA, HEADS, QK_DIM), ((0, 0), (0, 0), (0, QK_PAD - QK_DIM)))
    wqb = wqb.reshape(Q_LORA, HEADS * QK_PAD).astype(BF16)
    wkvb = mla_w_kvb.reshape(KV_LORA, HEADS, QK_NOPE + V_DIM)
    wkb = wkvb[:, :, :QK_NOPE].reshape(KV_LORA, HEADS * QK_NOPE).astype(BF16)
    wvb = wkvb[:, :, QK_NOPE:].reshape(KV_LORA, HEADS * V_DIM).astype(BF16)
    sgu_bias = jnp.repeat(sgu_b.T, d // SGU_GROUPS, axis=1)

    consts1 = [row(g_mix), w_u, w_v, w_q, w_kv, w_r, w_ga, w_gb, row(sgu_ln_g), row(sgu_ln_b), sgu_w, sgu_bias,
               row(mla_g_qa), wqb, row(mla_g_kva), wkb, wvb, row(qk_g_q_nope), row(qk_g_k_nope),
               _pad_lanes(row(qk_g_q_rope), LANES), _pad_lanes(row(qk_g_k_rope), LANES)]
    ma, gb, q, k, v = _mixer_prep(x, pos.reshape(b, s, 1), consts1, tm=256)
    o = _attention(q, k, v, tq=512)

    consts3 = [w_o.astype(BF16), row(g_ffn), w_router.T, b_router.reshape(-1, 1)]
    x1, h2, comb = _post_attn(x.reshape(t, d), ma.reshape(t, d), gb.reshape(t, d), o.reshape(t, d), consts3, tm=512)

    consts4 = [w_sh_gate.astype(BF16), w_sh_up.astype(BF16), w_sh_down.astype(BF16),
               row(g_ple_in), w_ple_gate.astype(BF16), w_ple_proj.astype(BF16), row(g_ple_out)]
    out = _moe(h2, comb, x1, p.reshape(t, PLE_DIM), w_exp_gate.astype(BF16), w_exp_up.astype(BF16),
               w_exp_down.astype(BF16), consts4, tm=1024)
    return out.reshape(b, s, d)


def kernel(x, p, positions, g_mix, w_in, sgu_ln_g, sgu_ln_b, sgu_w, sgu_b, mla_g_qa, mla_w_qb, mla_g_kva, mla_w_kvb, qk_g_q_nope, qk_g_k_nope, qk_g_q_rope, qk_g_k_rope, w_o, g_ffn, w_router, b_router, w_exp_gate, w_exp_up, w_exp_down, w_sh_gate, w_sh_up, w_sh_down, g_ple_in, w_ple_gate, w_ple_proj, g_ple_out):
    params = (g_mix, w_in, sgu_ln_g, sgu_ln_b, sgu_w, sgu_b, mla_g_qa, mla_w_qb, mla_g_kva, mla_w_kvb,
              qk_g_q_nope, qk_g_k_nope, qk_g_q_rope, qk_g_k_rope, w_o, g_ffn, w_router, b_router,
              w_exp_gate, w_exp_up, w_exp_down, w_sh_gate, w_sh_up, w_sh_down,
              g_ple_in, w_ple_gate, w_ple_proj, g_ple_out)
    for l in range(g_mix.shape[0]):
        x = _layer(x, p[l], positions, *[a[l] for a in params])
    return x
```

```python
import functools
import math

import jax
import jax.numpy as jnp
from jax import lax
from jax.experimental import pallas as pl
from jax.experimental.pallas import tpu as pltpu
from jax.experimental.pallas import tpu_sc as plsc

D_MODEL = 1024
PLE_DIM = 256
SGU_CHUNK = 128
SGU_GROUPS = 8
V_DIM = 128
HEADS = 8
QK_NOPE = 128
QK_ROPE = 64
QK_DIM = QK_NOPE + QK_ROPE
QK_PAD = 256
Q_LORA = 384
KV_LORA = 256
ROPE_THETA = 10000.0
N_EXPERTS = 64
N_GROUPS = 8
EXPERTS_PER_GROUP = 8
TOPK_GROUPS = 4
TOP_K = 8
EXPERT_FF = 256
ROUTED_SCALE = 2.5
NORM_EPS = 1e-6
LN_EPS = 1e-5

LANES = 128
VMEM_LIMIT = 56 * 1024 * 1024
SC_CORES = 2
SC_SUBCORES = 16
SC_WINDOW = 64
ROW_TILE = 256
HALF = D_MODEL // 2

F32 = jnp.float32
BF16 = jnp.bfloat16
U32 = jnp.uint32
I32 = jnp.int32


def _dot(a, b):
    return jnp.dot(a, b, preferred_element_type=F32)


def _rms(xf, g, width=None):
    width = xf.shape[-1] if width is None else width
    ms = jnp.sum(xf * xf, axis=-1, keepdims=True) * (1.0 / width)
    return xf * lax.rsqrt(ms + NORM_EPS) * g


def _pack_rows(y):
    lo = pltpu.bitcast(y[:, :HALF].astype(BF16).astype(F32), U32) >> 16
    hi = pltpu.bitcast(y[:, HALF:].astype(BF16).astype(F32), U32) & jnp.uint32(0xFFFF0000)
    return lo | hi


def _unpack_rows(w):
    lo = pltpu.bitcast(w << 16, F32)
    hi = pltpu.bitcast(w & jnp.uint32(0xFFFF0000), F32)
    return lo, hi


def _rope_tables(pos_f):
    lane = lax.broadcasted_iota(I32, (1, LANES), 1)
    half = QK_ROPE // 2
    freq = (lane % half).astype(F32)
    inv_freq = jnp.exp(freq * (-math.log(ROPE_THETA) * 2.0 / QK_ROPE))
    ang = pos_f * inv_freq
    c, s = jnp.cos(ang), jnp.sin(ang)
    cos_t = jnp.where(lane < QK_ROPE, c, 0.0)
    sin_t = jnp.where(lane < half, -s, jnp.where(lane < QK_ROPE, s, 0.0))
    return cos_t, sin_t


def _rope(piece, cos_t, sin_t):
    partner = pltpu.roll(piece, LANES - QK_ROPE // 2, axis=1) + pltpu.roll(piece, QK_ROPE // 2, axis=1)
    return piece * cos_t + partner * sin_t


def _const_spec(shape):
    return pl.BlockSpec(shape, lambda *_: (0,) * len(shape))


def _mixer_prep_kernel(x_ref, pos_ref, g_mix_ref, wu_ref, wv_ref, wq_ref, wkv_ref, wr_ref, wga_ref, wgb_ref,
                       lng_ref, lnb_ref, sw_ref, sb_ref, gqa_ref, wqb_ref, gkva_ref, wkb_ref, wvb_ref,
                       gqn_ref, gkn_ref, gqr_ref, gkr_ref,
                       ma_ref, gb_ref, q_ref, k_ref, v_ref):
    tm = x_ref.shape[1]
    xn = _rms(x_ref[0], g_mix_ref[...]).astype(BF16)

    gv = jax.nn.gelu(_dot(xn, wv_ref[...]))
    mu = jnp.mean(gv, axis=-1, keepdims=True)
    vc = gv - mu
    var = jnp.mean(vc * vc, axis=-1, keepdims=True)
    vn = (vc * lax.rsqrt(var + LN_EPS) * lng_ref[...] + lnb_ref[...]).astype(BF16)
    row = lax.broadcasted_iota(I32, (SGU_CHUNK, SGU_CHUNK), 0)
    col = lax.broadcasted_iota(I32, (SGU_CHUNK, SGU_CHUNK), 1)
    causal = col <= row
    for g in range(SGU_GROUPS):
        cs = slice(g * SGU_CHUNK, (g + 1) * SGU_CHUNK)
        wg = jnp.where(causal, sw_ref[g], 0.0).astype(BF16)
        gu = jax.nn.gelu(_dot(xn, wu_ref[:, cs]))
        ga = jax.nn.sigmoid(_dot(xn, wga_ref[:, cs]))
        for c in range(tm // SGU_CHUNK):
            rs = slice(c * SGU_CHUNK, (c + 1) * SGU_CHUNK)
            mixed = _dot(wg, vn[rs, cs]) + sb_ref[:, cs]
            ma_ref[0, rs, cs] = (ga[rs] * gu[rs] * mixed).astype(BF16)

    gb_ref[0] = jax.nn.sigmoid(_dot(xn, wgb_ref[...])).astype(BF16)

    cos_t, sin_t = _rope_tables(pos_ref[0].astype(F32))
    scale = QK_DIM ** -0.5
    qn = _rms(_dot(xn, wq_ref[...]), gqa_ref[...]).astype(BF16)
    kvn = _rms(_dot(xn, wkv_ref[...]), gkva_ref[...]).astype(BF16)
    kpe = _rope(_rms(_dot(xn, wr_ref[...]), gkr_ref[...], QK_ROPE), cos_t, sin_t).astype(BF16)
    for h in range(HEADS):
        qh = _dot(qn, wqb_ref[:, h * QK_PAD:(h + 1) * QK_PAD])
        q_nope = _rms(qh[:, :QK_NOPE], gqn_ref[...]) * scale
        q_pe = _rope(_rms(qh[:, QK_NOPE:], gqr_ref[...], QK_ROPE), cos_t, sin_t) * scale
        q_ref[0, h, :, :QK_NOPE] = q_nope.astype(BF16)
        q_ref[0, h, :, QK_NOPE:] = q_pe.astype(BF16)
        hs = slice(h * QK_NOPE, (h + 1) * QK_NOPE)
        k_ref[0, h, :, :QK_NOPE] = _rms(_dot(kvn, wkb_ref[:, hs]), gkn_ref[...]).astype(BF16)
        k_ref[0, h, :, QK_NOPE:] = kpe
        v_ref[0, h] = _dot(kvn, wvb_ref[:, hs]).astype(BF16)


def _mixer_prep(x, pos, consts, tm):
    b, s, d = x.shape
    grid = (b, s // tm)
    row_spec = lambda w: pl.BlockSpec((1, tm, w), lambda i, j: (i, j, 0))
    head_spec = lambda w: pl.BlockSpec((1, HEADS, tm, w), lambda i, j: (i, 0, j, 0))
    return pl.pallas_call(
        _mixer_prep_kernel,
        grid=grid,
        in_specs=[row_spec(d), row_spec(1)] + [_const_spec(c.shape) for c in consts],
        out_specs=[row_spec(d), row_spec(d), head_spec(QK_PAD), head_spec(QK_PAD), head_spec(V_DIM)],
        out_shape=[jax.ShapeDtypeStruct((b, s, d), BF16), jax.ShapeDtypeStruct((b, s, d), BF16),
                   jax.ShapeDtypeStruct((b, HEADS, s, QK_PAD), BF16),
                   jax.ShapeDtypeStruct((b, HEADS, s, QK_PAD), BF16),
                   jax.ShapeDtypeStruct((b, HEADS, s, V_DIM), BF16)],
        compiler_params=pltpu.CompilerParams(dimension_semantics=("arbitrary", "arbitrary"),
                                             vmem_limit_bytes=VMEM_LIMIT),
        name="mixer_prep",
    )(x, pos, *consts)


def _attn_kernel(q_ref, k_ref, v_ref, o_ref, *, tq):
    s = q_ref.shape[2]
    row = lax.broadcasted_iota(I32, (tq, tq), 0)
    col = lax.broadcasted_iota(I32, (tq, tq), 1)
    diag_mask = col <= row

    def scores(q, start):
        kt = k_ref[0, 0, pl.ds(start, tq), :]
        return lax.dot_general(q, kt, (((1,), (1,)), ((), ())), preferred_element_type=F32)

    def update(carry, sc, start):
        m, l, acc = carry
        m_new = jnp.maximum(m, jnp.max(sc, axis=-1, keepdims=True))
        alpha = jnp.exp(m - m_new)
        p = jnp.exp(sc - m_new)
        l = alpha * l + jnp.sum(p, axis=-1, keepdims=True)
        vt = v_ref[0, 0, pl.ds(start, tq), :]
        acc = alpha * acc + _dot(p.astype(BF16), vt)
        return m_new, l, acc

    for qi in range(s // tq):
        q = q_ref[0, 0, qi * tq:(qi + 1) * tq, :]
        sc = jnp.where(diag_mask, scores(q, qi * tq), -jnp.inf)
        m0 = jnp.max(sc, axis=-1, keepdims=True)
        p0 = jnp.exp(sc - m0)
        carry = (m0, jnp.sum(p0, axis=-1, keepdims=True),
                 _dot(p0.astype(BF16), v_ref[0, 0, qi * tq:(qi + 1) * tq, :]))

        def body(kj, carry, q=q):
            start = pl.multiple_of(kj * tq, tq)
            return update(carry, scores(q, start), start)

        m, l, acc = lax.fori_loop(0, qi, body, carry)
        o_ref[0, qi * tq:(qi + 1) * tq, :] = (acc / l).astype(BF16)


def _attention(q, k, v, tq):
    b, h, s, _ = q.shape
    return pl.pallas_call(
        functools.partial(_attn_kernel, tq=tq),
        grid=(b, h),
        in_specs=[pl.BlockSpec((1, 1, s, QK_PAD), lambda i, j: (i, j, 0, 0)),
                  pl.BlockSpec((1, 1, s, QK_PAD), lambda i, j: (i, j, 0, 0)),
                  pl.BlockSpec((1, 1, s, V_DIM), lambda i, j: (i, j, 0, 0))],
        out_specs=pl.BlockSpec((1, s, V_DIM), lambda i, j: (i, 0, j)),
        out_shape=jax.ShapeDtypeStruct((b, s, h * V_DIM), BF16),
        compiler_params=pltpu.CompilerParams(dimension_semantics=("arbitrary", "arbitrary"),
                                             vmem_limit_bytes=VMEM_LIMIT),
        name="mla_attention",
    )(q, k, v)


def _route(sel, scores):
    e, tm = sel.shape
    sel3 = sel.reshape(N_GROUPS, EXPERTS_PER_GROUP, tm)
    sub = lax.broadcasted_iota(I32, sel3.shape, 1)
    m1 = jnp.max(sel3, axis=1, keepdims=True)
    first = jnp.min(jnp.where(sel3 == m1, sub, EXPERTS_PER_GROUP), axis=1, keepdims=True)
    m2 = jnp.max(jnp.where(sub == first, -jnp.inf, sel3), axis=1, keepdims=True)
    gscore = (m1 + m2).reshape(N_GROUPS, tm)
    gid = lax.broadcasted_iota(I32, (N_GROUPS, tm), 0)
    grank = jnp.zeros((N_GROUPS, tm), I32)
    for g in range(N_GROUPS):
        other = gscore[g:g + 1]
        grank += ((other > gscore) | ((other == gscore) & (g < gid))).astype(I32)
    gmask = grank < TOPK_GROUPS
    emask = jnp.broadcast_to(gmask[:, None, :], sel3.shape).reshape(e, tm)
    msel = jnp.where(emask, sel, -jnp.inf)
    eid = lax.broadcasted_iota(I32, (e, tm), 0)
    erank = jnp.zeros((e, tm), I32)
    for j in range(e):
        other = msel[j:j + 1]
        erank += ((other > msel) | ((other == msel) & (j < eid))).astype(I32)
    chosen = erank < TOP_K
    w = jnp.where(chosen, scores, 0.0)
    return chosen, w / jnp.sum(w, axis=0, keepdims=True) * ROUTED_SCALE


def _post_attn_kernel(x_ref, ma_ref, gb_ref, o_ref, wo_ref, gffn_ref, wrt_ref, br_ref,
                      x1_ref, hp_ref, comb_ref, rank_ref, cnt_ref, run_ref):
    tm = x_ref.shape[0]

    @pl.when(pl.program_id(0) == 0)
    def _():
        run_ref[...] = jnp.zeros_like(run_ref)

    merged = ma_ref[...].astype(F32) + gb_ref[...].astype(F32) * o_ref[...].astype(F32)
    x1 = x_ref[...] + _dot(merged.astype(BF16), wo_ref[...])
    x1_ref[...] = x1
    h2 = _rms(x1, gffn_ref[...])
    hp_ref[...] = _pack_rows(h2)
    logits_t = lax.dot_general(wrt_ref[...], h2, (((1,), (1,)), ((), ())),
                               preferred_element_type=F32, precision=lax.Precision.HIGHEST)
    scores = jax.nn.sigmoid(logits_t)
    chosen, comb_t = _route(scores + br_ref[...], scores)
    comb_ref[...] = comb_t
    a = lax.broadcasted_iota(I32, (tm, tm), 0)
    b = lax.broadcasted_iota(I32, (tm, tm), 1)
    before = (a < b).astype(BF16)
    chosen_f = chosen.astype(F32)
    prefix = _dot(chosen_f.astype(BF16), before)
    run = run_ref[:, 0:1]
    rank_ref[...] = jnp.where(chosen, prefix + run, -1.0)
    run_ref[...] += jnp.sum(chosen_f, axis=1, keepdims=True)
    cnt_ref[...] = run_ref[...]


def _post_attn(x, ma, gb, o, consts, tm):
    t, d = x.shape
    row_spec = lambda w: pl.BlockSpec((tm, w), lambda i: (i, 0))
    col_spec = pl.BlockSpec((N_EXPERTS, tm), lambda i: (0, i))
    return pl.pallas_call(
        _post_attn_kernel,
        grid=(t // tm,),
        in_specs=[row_spec(d)] * 4 + [_const_spec(c.shape) for c in consts],
        out_specs=[row_spec(d), row_spec(HALF), col_spec, col_spec, _const_spec((N_EXPERTS, LANES))],
        out_shape=[jax.ShapeDtypeStruct((t, d), F32), jax.ShapeDtypeStruct((t, HALF), U32),
                   jax.ShapeDtypeStruct((N_EXPERTS, t), F32), jax.ShapeDtypeStruct((N_EXPERTS, t), F32),
                   jax.ShapeDtypeStruct((N_EXPERTS, LANES), F32)],
        scratch_shapes=[pltpu.VMEM((N_EXPERTS, LANES), F32)],
        compiler_params=pltpu.CompilerParams(dimension_semantics=("arbitrary",), vmem_limit_bytes=VMEM_LIMIT),
        name="post_attn_router",
    )(x, ma, gb, o, *consts)


def _route_lists_kernel(comb_ref, rank_ref, off_ref, pos_ref, w_ref):
    rank = rank_ref[...]
    chosen = rank >= 0.0
    e = rank.shape[0]
    lower = (lax.broadcasted_iota(I32, (e, e), 1) < lax.broadcasted_iota(I32, (e, e), 0)).astype(BF16)
    slot = _dot(lower, chosen.astype(BF16))
    pos_full = rank + off_ref[...]
    comb = comb_ref[...]
    pos_rows, w_rows = [], []
    for k in range(TOP_K):
        pick = chosen & (slot == float(k))
        pos_rows.append(jnp.sum(jnp.where(pick, pos_full, 0.0), axis=0, keepdims=True))
        w_rows.append(jnp.sum(jnp.where(pick, comb, 0.0), axis=0, keepdims=True))
    pos_ref[...] = jnp.concatenate(pos_rows, axis=0).astype(I32)
    w_ref[...] = jnp.concatenate(w_rows, axis=0).T


def _route_lists(comb_t, rank_t, offsets, tm):
    e, t = comb_t.shape
    col_spec = pl.BlockSpec((e, tm), lambda i: (0, i))
    return pl.pallas_call(
        _route_lists_kernel,
        grid=(t // tm,),
        in_specs=[col_spec, col_spec, _const_spec((e, 1))],
        out_specs=[pl.BlockSpec((TOP_K, tm), lambda i: (0, i)), pl.BlockSpec((tm, TOP_K), lambda i: (i, 0))],
        out_shape=[jax.ShapeDtypeStruct((TOP_K, t), I32), jax.ShapeDtypeStruct((t, TOP_K), F32)],
        compiler_params=pltpu.CompilerParams(dimension_semantics=("arbitrary",)),
        name="route_lists",
    )(comb_t, rank_t, offsets)


def _sc_mesh():
    return plsc.VectorSubcoreMesh(core_axis_name="c", subcore_axis_name="s",
                                  num_cores=SC_CORES, num_subcores=SC_SUBCORES)


def _sc_worker():
    return lax.axis_index("s") * SC_CORES + lax.axis_index("c")


def _sc_scatter_rows(src, pos_win, n_rows):
    t, width = src.shape
    n_win = t // SC_WINDOW
    per_w = n_win // (SC_CORES * SC_SUBCORES)

    @functools.partial(
        pl.kernel, mesh=_sc_mesh(), out_type=jax.ShapeDtypeStruct((n_rows, width), src.dtype),
        scratch_types=[pltpu.VMEM((TOP_K, SC_WINDOW), I32), pltpu.VMEM((SC_WINDOW, width), src.dtype),
                       pltpu.SemaphoreType.DMA],
        name="sc_scatter_rows")
    def run(src_hbm, pos_hbm, out_hbm, idx_v, rows_v, sem):
        base = _sc_worker() * per_w

        @pl.loop(0, per_w)
        def _(j):
            w = base + j
            pltpu.sync_copy(pos_hbm.at[w], idx_v)
            pltpu.sync_copy(src_hbm.at[pl.ds(w * SC_WINDOW, SC_WINDOW)], rows_v)
            copies = [pltpu.async_copy(rows_v, out_hbm.at[idx_v.at[k]], sem) for k in range(TOP_K)]
            for c in copies:
                c.wait()

    return run(src, pos_win)


def _sc_gather_rows(table, pos_win):
    n_win = pos_win.shape[0]
    width = table.shape[1]
    per_w = n_win // (SC_CORES * SC_SUBCORES)

    @functools.partial(
        pl.kernel, mesh=_sc_mesh(),
        out_type=jax.ShapeDtypeStruct((TOP_K, n_win * SC_WINDOW, width), table.dtype),
        scratch_types=[pltpu.VMEM((TOP_K, SC_WINDOW), I32), pltpu.VMEM((2, SC_WINDOW, width), table.dtype),
                       pltpu.SemaphoreType.DMA, pltpu.SemaphoreType.DMA],
        name="sc_gather_rows")
    def run(table_hbm, pos_hbm, out_hbm, idx_v, rows_v, gsem, wsem):
        base = _sc_worker() * per_w

        @pl.loop(0, per_w)
        def _(j):
            w = base + j
            pltpu.sync_copy(pos_hbm.at[w], idx_v)
            for k in range(TOP_K):
                buf = rows_v.at[k % 2]
                pltpu.async_copy(table_hbm.at[idx_v.at[k]], buf, gsem).wait()
                pltpu.async_copy(buf, out_hbm.at[k, pl.ds(w * SC_WINDOW, SC_WINDOW)], wsem).wait()

    return run(table, pos_win)


def _grouped_ffn_kernel(te_ref, nt_ref, xs_ref, wg_ref, wu_ref, wd_ref, ys_ref):
    @pl.when(pl.program_id(0) < nt_ref[0])
    def _():
        lo, hi = _unpack_rows(xs_ref[...])
        lo, hi = lo.astype(BF16), hi.astype(BF16)
        gate = _dot(lo, wg_ref[0, :HALF]) + _dot(hi, wg_ref[0, HALF:])
        up = _dot(lo, wu_ref[0, :HALF]) + _dot(hi, wu_ref[0, HALF:])
        act = (jax.nn.silu(gate) * up).astype(BF16)
        ys_ref[...] = _pack_rows(_dot(act, wd_ref[0]))


def _grouped_ffn(tile_expert, n_tiles, xs, wg, wu, wd):
    n_rows, half = xs.shape
    max_tiles = n_rows // ROW_TILE

    def row_map(i, te, nt):
        return (jnp.minimum(i, nt[0] - 1), 0)

    def exp_map(i, te, nt):
        return (te[jnp.minimum(i, nt[0] - 1)], 0, 0)

    return pl.pallas_call(
        _grouped_ffn_kernel,
        grid_spec=pltpu.PrefetchScalarGridSpec(
            num_scalar_prefetch=2,
            grid=(max_tiles,),
            in_specs=[pl.BlockSpec((ROW_TILE, half), row_map),
                      pl.BlockSpec((1,) + wg.shape[1:], exp_map),
                      pl.BlockSpec((1,) + wu.shape[1:], exp_map),
                      pl.BlockSpec((1,) + wd.shape[1:], exp_map)],
            out_specs=pl.BlockSpec((ROW_TILE, half), row_map)),
        out_shape=jax.ShapeDtypeStruct((n_rows, half), U32),
        compiler_params=pltpu.CompilerParams(dimension_semantics=("arbitrary",), vmem_limit_bytes=VMEM_LIMIT),
        name="grouped_ffn",
    )(tile_expert, n_tiles, xs, wg, wu, wd)


def _combine_kernel(yt_ref, w_ref, x1_ref, p_ref, gffn_ref, wsg_ref, wsu_ref, wsd_ref,
                    gpi_ref, wpg_ref, wpp_ref, gpo_ref, out_ref):
    x1 = x1_ref[...]
    w = w_ref[...]
    acc_lo = jnp.zeros((x1.shape[0], HALF), F32)
    acc_hi = jnp.zeros((x1.shape[0], HALF), F32)
    for k in range(TOP_K):
        lo, hi = _unpack_rows(yt_ref[k])
        wk = w[:, k:k + 1]
        acc_lo += wk * lo
        acc_hi += wk * hi
    h = _rms(x1, gffn_ref[...]).astype(BF16)
    shared = _dot((jax.nn.silu(_dot(h, wsg_ref[...])) * _dot(h, wsu_ref[...])).astype(BF16), wsd_ref[...])
    x2 = x1 + shared + jnp.concatenate([acc_lo, acc_hi], axis=-1)
    gate = jax.nn.sigmoid(_dot(_rms(x2, gpi_ref[...]).astype(BF16), wpg_ref[...]))
    proj = _rms(_dot(p_ref[...].astype(BF16), wpp_ref[...]), gpo_ref[...])
    out_ref[...] = x2 + gate * proj


def _combine(yt, w_tok, x1, p, consts, tm):
    t, d = x1.shape
    row_spec = lambda w: pl.BlockSpec((tm, w), lambda i: (i, 0))
    return pl.pallas_call(
        _combine_kernel,
        grid=(t // tm,),
        in_specs=[pl.BlockSpec((TOP_K, tm, HALF), lambda i: (0, i, 0)), row_spec(TOP_K), row_spec(d),
                  row_spec(PLE_DIM)] + [_const_spec(c.shape) for c in consts],
        out_specs=row_spec(d),
        out_shape=jax.ShapeDtypeStruct((t, d), F32),
        compiler_params=pltpu.CompilerParams(dimension_semantics=("arbitrary",), vmem_limit_bytes=VMEM_LIMIT),
        name="combine_ple",
    )(yt, w_tok, x1, p, *consts)


def _pad_lanes(a, width):
    return jnp.pad(a, ((0, 0), (0, width - a.shape[-1])))


def _layer(x, p, pos, g_mix, w_in, sgu_ln_g, sgu_ln_b, sgu_w, sgu_b, mla_g_qa, mla_w_qb, mla_g_kva, mla_w_kvb,
           qk_g_q_nope, qk_g_k_nope, qk_g_q_rope, qk_g_k_rope, w_o, g_ffn, w_router, b_router,
           w_exp_gate, w_exp_up, w_exp_down, w_sh_gate, w_sh_up, w_sh_down,
           g_ple_in, w_ple_gate, w_ple_proj, g_ple_out):
    b, s, d = x.shape
    t = b * s
    row = lambda a: a.reshape(1, -1)
    sizes = [d, d, Q_LORA, KV_LORA, QK_ROPE, d, d]
    offs = [0]
    for sz in sizes:
        offs.append(offs[-1] + sz)
    w_u, w_v, w_q, w_kv, w_r, w_ga, w_gb = [w_in[:, offs[i]:offs[i + 1]].astype(BF16) for i in range(7)]
    w_r = _pad_lanes(w_r, LANES)
    wqb = jnp.pad(mla_w_qb.reshape(Q_LORA, HEADS, QK_DIM), ((0, 0), (0, 0), (0, QK_PAD - QK_DIM)))
    wqb = wqb.reshape(Q_LORA, HEADS * QK_PAD).astype(BF16)
    wkvb = mla_w_kvb.reshape(KV_LORA, HEADS, QK_NOPE + V_DIM)
    wkb = wkvb[:, :, :QK_NOPE].reshape(KV_LORA, HEADS * QK_NOPE).astype(BF16)
    wvb = wkvb[:, :, QK_NOPE:].reshape(KV_LORA, HEADS * V_DIM).astype(BF16)
    sgu_bias = jnp.repeat(sgu_b.T, d // SGU_GROUPS, axis=1)

    consts1 = [row(g_mix), w_u, w_v, w_q, w_kv, w_r, w_ga, w_gb, row(sgu_ln_g), row(sgu_ln_b), sgu_w, sgu_bias,
               row(mla_g_qa), wqb, row(mla_g_kva), wkb, wvb, row(qk_g_q_nope), row(qk_g_k_nope),
               _pad_lanes(row(qk_g_q_rope), LANES), _pad_lanes(row(qk_g_k_rope), LANES)]
    ma, gb, q, k, v = _mixer_prep(x, pos.reshape(b, s, 1), consts1, tm=256)
    o = _attention(q, k, v, tq=512)

    consts3 = [w_o.astype(BF16), row(g_ffn), w_router.T, b_router.reshape(-1, 1)]
    x1, h_packed, comb_t, rank_t, counts = _post_attn(
        x.reshape(t, d), ma.reshape(t, d), gb.reshape(t, d), o.reshape(t, d), consts3, tm=512)

    counts = counts[:, 0].astype(I32)
    tiles_per_expert = (counts + ROW_TILE - 1) // ROW_TILE
    tile_end = jnp.cumsum(tiles_per_expert)
    offsets = ((tile_end - tiles_per_expert) * ROW_TILE).astype(F32).reshape(N_EXPERTS, 1)
    max_tiles = (t * TOP_K) // ROW_TILE + N_EXPERTS
    tile_expert = jnp.minimum(jnp.searchsorted(tile_end, jnp.arange(max_tiles, dtype=I32), side="right"),
                              N_EXPERTS - 1).astype(I32)
    n_tiles = tile_end[-1:].astype(I32)

    pos_t, w_tok = _route_lists(comb_t, rank_t, offsets, tm=512)
    pos_win = pos_t.reshape(TOP_K, t // SC_WINDOW, SC_WINDOW).transpose(1, 0, 2)

    xs = _sc_scatter_rows(h_packed, pos_win, max_tiles * ROW_TILE)
    ys = _grouped_ffn(tile_expert, n_tiles, xs, w_exp_gate.astype(BF16), w_exp_up.astype(BF16),
                      w_exp_down.astype(BF16))
    yt = _sc_gather_rows(ys, pos_win)

    consts6 = [row(g_ffn), w_sh_gate.astype(BF16), w_sh_up.astype(BF16), w_sh_down.astype(BF16),
               row(g_ple_in), w_ple_gate.astype(BF16), w_ple_proj.astype(BF16), row(g_ple_out)]
    out = _combine(yt, w_tok, x1, p.reshape(t, PLE_DIM), consts6, tm=256)
    return out.reshape(b, s, d)


def kernel(x, p, positions, g_mix, w_in, sgu_ln_g, sgu_ln_b, sgu_w, sgu_b, mla_g_qa, mla_w_qb, mla_g_kva, mla_w_kvb, qk_g_q_nope, qk_g_k_nope, qk_g_q_rope, qk_g_k_rope, w_o, g_ffn, w_router, b_router, w_exp_gate, w_exp_up, w_exp_down, w_sh_gate, w_sh_up, w_sh_down, g_ple_in, w_ple_gate, w_ple_proj, g_ple_out):
    params = (g_mix, w_in, sgu_ln_g, sgu_ln_b, sgu_w, sgu_b, mla_g_qa, mla_w_qb, mla_g_kva, mla_w_kvb,
              qk_g_q_nope, qk_g_k_nope, qk_g_q_rope, qk_g_k_rope, w_o, g_ffn, w_router, b_router,
              w_exp_gate, w_exp_up, w_exp_down, w_sh_gate, w_sh_up, w_sh_down,
              g_ple_in, w_ple_gate, w_ple_proj, g_ple_out)
    for l in range(g_mix.shape[0]):
        x = _layer(x, p[l], positions, *[a[l] for a in params])
    return x
```

```python
import functools
import math

import jax
import jax.numpy as jnp
from jax import lax
from jax.experimental import pallas as pl
from jax.experimental.pallas import tpu as pltpu
from jax.experimental.pallas import tpu_sc as plsc

D_MODEL = 1024
PLE_DIM = 256
SGU_CHUNK = 128
SGU_GROUPS = 8
V_DIM = 128
HEADS = 8
QK_NOPE = 128
QK_ROPE = 64
QK_DIM = QK_NOPE + QK_ROPE
QK_PAD = 256
Q_LORA = 384
KV_LORA = 256
ROPE_THETA = 10000.0
N_EXPERTS = 64
N_GROUPS = 8
EXPERTS_PER_GROUP = 8
TOPK_GROUPS = 4
TOP_K = 8
EXPERT_FF = 256
ROUTED_SCALE = 2.5
NORM_EPS = 1e-6
LN_EPS = 1e-5

LANES = 128
VMEM_LIMIT = 56 * 1024 * 1024
SC_CORES = 2
SC_SUBCORES = 16
SC_WINDOW = 64
ROW_TILE = 256
HALF = D_MODEL // 2

F32 = jnp.float32
BF16 = jnp.bfloat16
U32 = jnp.uint32
I32 = jnp.int32


def _dot(a, b):
    return jnp.dot(a, b, preferred_element_type=F32)


def _rms(xf, g, width=None):
    width = xf.shape[-1] if width is None else width
    ms = jnp.sum(xf * xf, axis=-1, keepdims=True) * (1.0 / width)
    return xf * lax.rsqrt(ms + NORM_EPS) * g


def _pack_rows(y):
    lo = pltpu.bitcast(y[:, :HALF].astype(BF16).astype(F32), U32) >> 16
    hi = pltpu.bitcast(y[:, HALF:].astype(BF16).astype(F32), U32) & jnp.uint32(0xFFFF0000)
    return lo | hi


def _unpack_rows(w):
    lo = pltpu.bitcast(w << 16, F32)
    hi = pltpu.bitcast(w & jnp.uint32(0xFFFF0000), F32)
    return lo, hi


def _rope_tables(pos_f):
    lane = lax.broadcasted_iota(I32, (1, LANES), 1)
    half = QK_ROPE // 2
    freq = (lane % half).astype(F32)
    inv_freq = jnp.exp(freq * (-math.log(ROPE_THETA) * 2.0 / QK_ROPE))
    ang = pos_f * inv_freq
    c, s = jnp.cos(ang), jnp.sin(ang)
    cos_t = jnp.where(lane < QK_ROPE, c, 0.0)
    sin_t = jnp.where(lane < half, -s, jnp.where(lane < QK_ROPE, s, 0.0))
    return cos_t, sin_t


def _rope(piece, cos_t, sin_t):
    partner = pltpu.roll(piece, LANES - QK_ROPE // 2, axis=1) + pltpu.roll(piece, QK_ROPE // 2, axis=1)
    return piece * cos_t + partner * sin_t


def _const_spec(shape):
    return pl.BlockSpec(shape, lambda *_: (0,) * len(shape))


def _mixer_prep_kernel(x_ref, pos_ref, g_mix_ref, wu_ref, wv_ref, wq_ref, wkv_ref, wr_ref, wga_ref, wgb_ref,
                       lng_ref, lnb_ref, sw_ref, sb_ref, gqa_ref, wqb_ref, gkva_ref, wkb_ref, wvb_ref,
                       gqn_ref, gkn_ref, gqr_ref, gkr_ref,
                       ma_ref, gb_ref, q_ref, k_ref, v_ref):
    tm = x_ref.shape[1]
    xn = _rms(x_ref[0], g_mix_ref[...]).astype(BF16)

    gv = jax.nn.gelu(_dot(xn, wv_ref[...]))
    mu = jnp.mean(gv, axis=-1, keepdims=True)
    vc = gv - mu
    var = jnp.mean(vc * vc, axis=-1, keepdims=True)
    vn = (vc * lax.rsqrt(var + LN_EPS) * lng_ref[...] + lnb_ref[...]).astype(BF16)
    row = lax.broadcasted_iota(I32, (SGU_CHUNK, SGU_CHUNK), 0)
    col = lax.broadcasted_iota(I32, (SGU_CHUNK, SGU_CHUNK), 1)
    causal = col <= row
    for g in range(SGU_GROUPS):
        cs = slice(g * SGU_CHUNK, (g + 1) * SGU_CHUNK)
        wg = jnp.where(causal, sw_ref[g], 0.0).astype(BF16)
        gu = jax.nn.gelu(_dot(xn, wu_ref[:, cs]))
        ga = jax.nn.sigmoid(_dot(xn, wga_ref[:, cs]))
        for c in range(tm // SGU_CHUNK):
            rs = slice(c * SGU_CHUNK, (c + 1) * SGU_CHUNK)
            mixed = _dot(wg, vn[rs, cs]) + sb_ref[:, cs]
            ma_ref[0, rs, cs] = (ga[rs] * gu[rs] * mixed).astype(BF16)

    gb_ref[0] = jax.nn.sigmoid(_dot(xn, wgb_ref[...])).astype(BF16)

    cos_t, sin_t = _rope_tables(pos_ref[0].astype(F32))
    scale = QK_DIM ** -0.5
    qn = _rms(_dot(xn, wq_ref[...]), gqa_ref[...]).astype(BF16)
    kvn = _rms(_dot(xn, wkv_ref[...]), gkva_ref[...]).astype(BF16)
    kpe = _rope(_rms(_dot(xn, wr_ref[...]), gkr_ref[...], QK_ROPE), cos_t, sin_t).astype(BF16)
    for h in range(HEADS):
        qh = _dot(qn, wqb_ref[:, h * QK_PAD:(h + 1) * QK_PAD])
        q_nope = _rms(qh[:, :QK_NOPE], gqn_ref[...]) * scale
        q_pe = _rope(_rms(qh[:, QK_NOPE:], gqr_ref[...], QK_ROPE), cos_t, sin_t) * scale
        q_ref[0, h, :, :QK_NOPE] = q_nope.astype(BF16)
        q_ref[0, h, :, QK_NOPE:] = q_pe.astype(BF16)
        hs = slice(h * QK_NOPE, (h + 1) * QK_NOPE)
        k_ref[0, h, :, :QK_NOPE] = _rms(_dot(kvn, wkb_ref[:, hs]), gkn_ref[...]).astype(BF16)
        k_ref[0, h, :, QK_NOPE:] = kpe
        v_ref[0, h] = _dot(kvn, wvb_ref[:, hs]).astype(BF16)


def _mixer_prep(x, pos, consts, tm):
    b, s, d = x.shape
    grid = (b, s // tm)
    row_spec = lambda w: pl.BlockSpec((1, tm, w), lambda i, j: (i, j, 0))
    head_spec = lambda w: pl.BlockSpec((1, HEADS, tm, w), lambda i, j: (i, 0, j, 0))
    return pl.pallas_call(
        _mixer_prep_kernel,
        grid=grid,
        in_specs=[row_spec(d), row_spec(1)] + [_const_spec(c.shape) for c in consts],
        out_specs=[row_spec(d), row_spec(d), head_spec(QK_PAD), head_spec(QK_PAD), head_spec(V_DIM)],
        out_shape=[jax.ShapeDtypeStruct((b, s, d), BF16), jax.ShapeDtypeStruct((b, s, d), BF16),
                   jax.ShapeDtypeStruct((b, HEADS, s, QK_PAD), BF16),
                   jax.ShapeDtypeStruct((b, HEADS, s, QK_PAD), BF16),
                   jax.ShapeDtypeStruct((b, HEADS, s, V_DIM), BF16)],
        compiler_params=pltpu.CompilerParams(dimension_semantics=("arbitrary", "arbitrary"),
                                             vmem_limit_bytes=VMEM_LIMIT),
        name="mixer_prep",
    )(x, pos, *consts)


def _attn_kernel(q_ref, k_ref, v_ref, o_ref, *, tq):
    s = q_ref.shape[2]
    row = lax.broadcasted_iota(I32, (tq, tq), 0)
    col = lax.broadcasted_iota(I32, (tq, tq), 1)
    diag_mask = col <= row

    def scores(q, start):
        kt = k_ref[0, 0, pl.ds(start, tq), :]
        return lax.dot_general(q, kt, (((1,), (1,)), ((), ())), preferred_element_type=F32)

    def update(carry, sc, start):
        m, l, acc = carry
        m_new = jnp.maximum(m, jnp.max(sc, axis=-1, keepdims=True))
        alpha = jnp.exp(m - m_new)
        p = jnp.exp(sc - m_new)
        l = alpha * l + jnp.sum(p, axis=-1, keepdims=True)
        vt = v_ref[0, 0, pl.ds(start, tq), :]
        acc = alpha * acc + _dot(p.astype(BF16), vt)
        return m_new, l, acc

    for qi in range(s // tq):
        q = q_ref[0, 0, qi * tq:(qi + 1) * tq, :]
        sc = jnp.where(diag_mask, scores(q, qi * tq), -jnp.inf)
        m0 = jnp.max(sc, axis=-1, keepdims=True)
        p0 = jnp.exp(sc - m0)
        carry = (m0, jnp.sum(p0, axis=-1, keepdims=True),
                 _dot(p0.astype(BF16), v_ref[0, 0, qi * tq:(qi + 1) * tq, :]))

        def body(kj, carry, q=q):
            start = pl.multiple_of(kj * tq, tq)
            return update(carry, scores(q, start), start)

        m, l, acc = lax.fori_loop(0, qi, body, carry)
        o_ref[0, qi * tq:(qi + 1) * tq, :] = (acc / l).astype(BF16)


def _attention(q, k, v, tq):
    b, h, s, _ = q.shape
    return pl.pallas_call(
        functools.partial(_attn_kernel, tq=tq),
        grid=(b, h),
        in_specs=[pl.BlockSpec((1, 1, s, QK_PAD), lambda i, j: (i, j, 0, 0)),
                  pl.BlockSpec((1, 1, s, QK_PAD), lambda i, j: (i, j, 0, 0)),
                  pl.BlockSpec((1, 1, s, V_DIM), lambda i, j: (i, j, 0, 0))],
        out_specs=pl.BlockSpec((1, s, V_DIM), lambda i, j: (i, 0, j)),
        out_shape=jax.ShapeDtypeStruct((b, s, h * V_DIM), BF16),
        compiler_params=pltpu.CompilerParams(dimension_semantics=("arbitrary", "arbitrary"),
                                             vmem_limit_bytes=VMEM_LIMIT),
        name="mla_attention",
    )(q, k, v)


def _route(sel, scores):
    e, tm = sel.shape
    sel3 = sel.reshape(N_GROUPS, EXPERTS_PER_GROUP, tm)
    sub = lax.broadcasted_iota(I32, sel3.shape, 1)
    m1 = jnp.max(sel3, axis=1, keepdims=True)
    first = jnp.min(jnp.where(sel3 == m1, sub, EXPERTS_PER_GROUP), axis=1, keepdims=True)
    m2 = jnp.max(jnp.where(sub == first, -jnp.inf, sel3), axis=1, keepdims=True)
    gscore = (m1 + m2).reshape(N_GROUPS, tm)
    gid = lax.broadcasted_iota(I32, (N_GROUPS, tm), 0)
    grank = jnp.zeros((N_GROUPS, tm), I32)
    for g in range(N_GROUPS):
        other = gscore[g:g + 1]
        grank += ((other > gscore) | ((other == gscore) & (g < gid))).astype(I32)
    gmask = grank < TOPK_GROUPS
    emask = jnp.broadcast_to(gmask[:, None, :], sel3.shape).reshape(e, tm)
    msel = jnp.where(emask, sel, -jnp.inf)
    eid = lax.broadcasted_iota(I32, (e, tm), 0)
    erank = jnp.zeros((e, tm), I32)
    for j in range(e):
        other = msel[j:j + 1]
        erank += ((other > msel) | ((other == msel) & (j < eid))).astype(I32)
    chosen = erank < TOP_K
    w = jnp.where(chosen, scores, 0.0)
    return chosen, w / jnp.sum(w, axis=0, keepdims=True) * ROUTED_SCALE


def _post_attn_kernel(x_ref, ma_ref, gb_ref, o_ref, wo_ref, gffn_ref, wrt_ref, br_ref,
                      x1_ref, hp_ref, comb_ref, rank_ref, cnt_ref, run_ref):
    tm = x_ref.shape[0]

    @pl.when(pl.program_id(0) == 0)
    def _():
        run_ref[...] = jnp.zeros_like(run_ref)

    merged = ma_ref[...].astype(F32) + gb_ref[...].astype(F32) * o_ref[...].astype(F32)
    x1 = x_ref[...] + _dot(merged.astype(BF16), wo_ref[...])
    x1_ref[...] = x1
    h2 = _rms(x1, gffn_ref[...])
    hp_ref[...] = _pack_rows(h2)
    logits_t = lax.dot_general(wrt_ref[...], h2, (((1,), (1,)), ((), ())),
                               preferred_element_type=F32, precision=lax.Precision.HIGHEST)
    scores = jax.nn.sigmoid(logits_t)
    chosen, comb_t = _route(scores + br_ref[...], scores)
    comb_ref[...] = comb_t
    a = lax.broadcasted_iota(I32, (tm, tm), 0)
    b = lax.broadcasted_iota(I32, (tm, tm), 1)
    before = (a < b).astype(BF16)
    chosen_f = chosen.astype(F32)
    prefix = _dot(chosen_f.astype(BF16), before)
    run = run_ref[:, 0:1]
    rank_ref[...] = jnp.where(chosen, prefix + run, -1.0)
    run_ref[...] += jnp.sum(chosen_f, axis=1, keepdims=True)
    cnt_ref[...] = run_ref[...]


def _post_attn(x, ma, gb, o, consts, tm):
    t, d = x.shape
    row_spec = lambda w: pl.BlockSpec((tm, w), lambda i: (i, 0))
    col_spec = pl.BlockSpec((N_EXPERTS, tm), lambda i: (0, i))
    return pl.pallas_call(
        _post_attn_kernel,
        grid=(t // tm,),
        in_specs=[row_spec(d)] * 4 + [_const_spec(c.shape) for c in consts],
        out_specs=[row_spec(d), row_spec(HALF), col_spec, col_spec, _const_spec((N_EXPERTS, LANES))],
        out_shape=[jax.ShapeDtypeStruct((t, d), F32), jax.ShapeDtypeStruct((t, HALF), U32),
                   jax.ShapeDtypeStruct((N_EXPERTS, t), F32), jax.ShapeDtypeStruct((N_EXPERTS, t), F32),
                   jax.ShapeDtypeStruct((N_EXPERTS, LANES), F32)],
        scratch_shapes=[pltpu.VMEM((N_EXPERTS, LANES), F32)],
        compiler_params=pltpu.CompilerParams(dimension_semantics=("arbitrary",), vmem_limit_bytes=VMEM_LIMIT),
        name="post_attn_router",
    )(x, ma, gb, o, *consts)


def _route_lists_kernel(comb_ref, rank_ref, off_ref, pos_ref, w_ref):
    rank = rank_ref[...]
    chosen = rank >= 0.0
    e = rank.shape[0]
    lower = (lax.broadcasted_iota(I32, (e, e), 1) < lax.broadcasted_iota(I32, (e, e), 0)).astype(BF16)
    slot = _dot(lower, chosen.astype(BF16))
    pos_full = rank + off_ref[...]
    comb = comb_ref[...]
    pos_rows, w_rows = [], []
    for k in range(TOP_K):
        pick = chosen & (slot == float(k))
        pos_rows.append(jnp.sum(jnp.where(pick, pos_full, 0.0), axis=0, keepdims=True))
        w_rows.append(jnp.sum(jnp.where(pick, comb, 0.0), axis=0, keepdims=True))
    pos_ref[...] = jnp.concatenate(pos_rows, axis=0).astype(I32)
    w_ref[...] = jnp.concatenate(w_rows, axis=0).T


def _route_lists(comb_t, rank_t, offsets, tm):
    e, t = comb_t.shape
    col_spec = pl.BlockSpec((e, tm), lambda i: (0, i))
    return pl.pallas_call(
        _route_lists_kernel,
        grid=(t // tm,),
        in_specs=[col_spec, col_spec, _const_spec((e, 1))],
        out_specs=[pl.BlockSpec((TOP_K, tm), lambda i: (0, i)), pl.BlockSpec((tm, TOP_K), lambda i: (i, 0))],
        out_shape=[jax.ShapeDtypeStruct((TOP_K, t), I32), jax.ShapeDtypeStruct((t, TOP_K), F32)],
        compiler_params=pltpu.CompilerParams(dimension_semantics=("arbitrary",)),
        name="route_lists",
    )(comb_t, rank_t, offsets)


def _sc_mesh():
    return plsc.VectorSubcoreMesh(core_axis_name="c", subcore_axis_name="s",
                                  num_cores=SC_CORES, num_subcores=SC_SUBCORES)


def _sc_worker():
    return lax.axis_index("s") * SC_CORES + lax.axis_index("c")


def _sc_scatter_rows(src, pos_win, n_rows):
    t, width = src.shape
    n_win = t // SC_WINDOW
    per_w = n_win // (SC_CORES * SC_SUBCORES)

    @functools.partial(
        pl.kernel, mesh=_sc_mesh(), out_type=jax.ShapeDtypeStruct((n_rows, width), src.dtype),
        scratch_types=[pltpu.VMEM((TOP_K, SC_WINDOW), I32), pltpu.VMEM((SC_WINDOW, width), src.dtype),
                       pltpu.SemaphoreType.DMA],
        name="sc_scatter_rows")
    def run(src_hbm, pos_hbm, out_hbm, idx_v, rows_v, sem):
        base = _sc_worker() * per_w

        @pl.loop(0, per_w)
        def _(j):
            w = base + j
            pltpu.sync_copy(pos_hbm.at[w], idx_v)
            pltpu.sync_copy(src_hbm.at[pl.ds(w * SC_WINDOW, SC_WINDOW)], rows_v)
            copies = [pltpu.async_copy(rows_v, out_hbm.at[idx_v.at[k]], sem) for k in range(TOP_K)]
            for c in copies:
                c.wait()

    return run(src, pos_win)


def _sc_gather_rows(table, pos_win):
    n_win = pos_win.shape[0]
    width = table.shape[1]
    per_w = n_win // (SC_CORES * SC_SUBCORES)

    @functools.partial(
        pl.kernel, mesh=_sc_mesh(),
        out_type=jax.ShapeDtypeStruct((TOP_K, n_win * SC_WINDOW, width), table.dtype),
        scratch_types=[pltpu.VMEM((TOP_K, SC_WINDOW), I32), pltpu.VMEM((2, SC_WINDOW, width), table.dtype),
                       pltpu.SemaphoreType.DMA, pltpu.SemaphoreType.DMA],
        name="sc_gather_rows")
    def run(table_hbm, pos_hbm, out_hbm, idx_v, rows_v, gsem, wsem):
        base = _sc_worker() * per_w

        @pl.loop(0, per_w)
        def _(j):
            w = base + j
            pltpu.sync_copy(pos_hbm.at[w], idx_v)
            for k in range(TOP_K):
                buf = rows_v.at[k % 2]
                pltpu.async_copy(table_hbm.at[idx_v.at[k]], buf, gsem).wait()
                pltpu.async_copy(buf, out_hbm.at[k, pl.ds(w * SC_WINDOW, SC_WINDOW)], wsem).wait()

    return run(table, pos_win)


def _grouped_ffn_kernel(te_ref, nt_ref, xs_ref, wg_ref, wu_ref, wd_ref, ys_ref):
    @pl.when(pl.program_id(0) < nt_ref[0])
    def _():
        lo, hi = _unpack_rows(xs_ref[...])
        lo, hi = lo.astype(BF16), hi.astype(BF16)
        gate = _dot(lo, wg_ref[0, :HALF]) + _dot(hi, wg_ref[0, HALF:])
        up = _dot(lo, wu_ref[0, :HALF]) + _dot(hi, wu_ref[0, HALF:])
        act = (jax.nn.silu(gate) * up).astype(BF16)
        ys_ref[...] = _pack_rows(_dot(act, wd_ref[0]))


def _grouped_ffn(tile_expert, n_tiles, xs, wg, wu, wd):
    n_rows, half = xs.shape
    max_tiles = n_rows // ROW_TILE

    def row_map(i, te, nt):
        return (jnp.minimum(i, nt[0] - 1), 0)

    def exp_map(i, te, nt):
        return (te[jnp.minimum(i, nt[0] - 1)], 0, 0)

    return pl.pallas_call(
        _grouped_ffn_kernel,
        grid_spec=pltpu.PrefetchScalarGridSpec(
            num_scalar_prefetch=2,
            grid=(max_tiles,),
            in_specs=[pl.BlockSpec((ROW_TILE, half), row_map),
                      pl.BlockSpec((1,) + wg.shape[1:], exp_map),
                      pl.BlockSpec((1,) + wu.shape[1:], exp_map),
                      pl.BlockSpec((1,) + wd.shape[1:], exp_map)],
            out_specs=pl.BlockSpec((ROW_TILE, half), row_map)),
        out_shape=jax.ShapeDtypeStruct((n_rows, half), U32),
        compiler_params=pltpu.CompilerParams(dimension_semantics=("arbitrary",), vmem_limit_bytes=VMEM_LIMIT),
        name="grouped_ffn",
    )(tile_expert, n_tiles, xs, wg, wu, wd)


def _combine_kernel(yt_ref, w_ref, x1_ref, p_ref, gffn_ref, wsg_ref, wsu_ref, wsd_ref,
                    gpi_ref, wpg_ref, wpp_ref, gpo_ref, out_ref):
    x1 = x1_ref[...]
    w = w_ref[...]
    acc_lo = jnp.zeros((x1.shape[0], HALF), F32)
    acc_hi = jnp.zeros((x1.shape[0], HALF), F32)
    for k in range(TOP_K):
        lo, hi = _unpack_rows(yt_ref[k])
        wk = w[:, k:k + 1]
        acc_lo += wk * lo
        acc_hi += wk * hi
    h = _rms(x1, gffn_ref[...]).astype(BF16)
    shared = _dot((jax.nn.silu(_dot(h, wsg_ref[...])) * _dot(h, wsu_ref[...])).astype(BF16), wsd_ref[...])
    x2 = x1 + shared + jnp.concatenate([acc_lo, acc_hi], axis=-1)
    gate = jax.nn.sigmoid(_dot(_rms(x2, gpi_ref[...]).astype(BF16), wpg_ref[...]))
    proj = _rms(_dot(p_ref[...].astype(BF16), wpp_ref[...]), gpo_ref[...])
    out_ref[...] = x2 + gate * proj


def _combine(yt, w_tok, x1, p, consts, tm):
    t, d = x1.shape
    row_spec = lambda w: pl.BlockSpec((tm, w), lambda i: (i, 0))
    return pl.pallas_call(
        _combine_kernel,
        grid=(t // tm,),
        in_specs=[pl.BlockSpec((TOP_K, tm, HALF), lambda i: (0, i, 0)), row_spec(TOP_K), row_spec(d),
                  row_spec(PLE_DIM)] + [_const_spec(c.shape) for c in consts],
        out_specs=row_spec(d),
        out_shape=jax.ShapeDtypeStruct((t, d), F32),
        compiler_params=pltpu.CompilerParams(dimension_semantics=("arbitrary",), vmem_limit_bytes=VMEM_LIMIT),
        name="combine_ple",
    )(yt, w_tok, x1, p, *consts)


def _pad_lanes(a, width):
    return jnp.pad(a, ((0, 0), (0, width - a.shape[-1])))


def _layer(x, p, pos, g_mix, w_in, sgu_ln_g, sgu_ln_b, sgu_w, sgu_b, mla_g_qa, mla_w_qb, mla_g_kva, mla_w_kvb,
           qk_g_q_nope, qk_g_k_nope, qk_g_q_rope, qk_g_k_rope, w_o, g_ffn, w_router, b_router,
           w_exp_gate, w_exp_up, w_exp_down, w_sh_gate, w_sh_up, w_sh_down,
           g_ple_in, w_ple_gate, w_ple_proj, g_ple_out):
    b, s, d = x.shape
    t = b * s
    row = lambda a: a.reshape(1, -1)
    sizes = [d, d, Q_LORA, KV_LORA, QK_ROPE, d, d]
    offs = [0]
    for sz in sizes:
        offs.append(offs[-1] + sz)
    w_u, w_v, w_q, w_kv, w_r, w_ga, w_gb = [w_in[:, offs[i]:offs[i + 1]].astype(BF16) for i in range(7)]
    w_r = _pad_lanes(w_r, LANES)
    wqb = jnp.pad(mla_w_qb.reshape(Q_LORA, HEADS, QK_DIM), ((0, 0), (0, 0), (0, QK_PAD - QK_DIM)))
    wqb = wqb.reshape(Q_LORA, HEADS * QK_PAD).astype(BF16)
    wkvb = mla_w_kvb.reshape(KV_LORA, HEADS, QK_NOPE + V_DIM)
    wkb = wkvb[:, :, :QK_NOPE].reshape(KV_LORA, HEADS * QK_NOPE).astype(BF16)
    wvb = wkvb[:, :, QK_NOPE:].reshape(KV_LORA, HEADS * V_DIM).astype(BF16)
    sgu_bias = jnp.repeat(sgu_b.T, d // SGU_GROUPS, axis=1)

    consts1 = [row(g_mix), w_u, w_v, w_q, w_kv, w_r, w_ga, w_gb, row(sgu_ln_g), row(sgu_ln_b), sgu_w, sgu_bias,
               row(mla_g_qa), wqb, row(mla_g_kva), wkb, wvb, row(qk_g_q_nope), row(qk_g_k_nope),
               _pad_lanes(row(qk_g_q_rope), LANES), _pad_lanes(row(qk_g_k_rope), LANES)]
    ma, gb, q, k, v = _mixer_prep(x, pos.reshape(b, s, 1), consts1, tm=256)
    o = _attention(q, k, v, tq=512)

    consts3 = [w_o.astype(BF16), row(g_ffn), w_router.T, b_router.reshape(-1, 1)]
    x1, h_packed, comb_t, rank_t, counts = _post_attn(
        x.reshape(t, d), ma.reshape(t, d), gb.reshape(t, d), o.reshape(t, d), consts3, tm=512)

    counts = counts[:, 0].astype(I32)
    tiles_per_expert = (counts + ROW_TILE - 1) // ROW_TILE
    tile_end = jnp.cumsum(tiles_per_expert)
    offsets = ((tile_end - tiles_per_expert) * ROW_TILE).astype(F32).reshape(N_EXPERTS, 1)
    max_tiles = (t * TOP_K) // ROW_TILE + N_EXPERTS
    tile_ids = jnp.arange(max_tiles, dtype=I32)
    tile_expert = jnp.minimum(jnp.sum((tile_end[None, :] <= tile_ids[:, None]).astype(I32), axis=1),
                              N_EXPERTS - 1)
    n_tiles = tile_end[-1:].astype(I32)

    pos_t, w_tok = _route_lists(comb_t, rank_t, offsets, tm=512)
    pos_win = pos_t.reshape(TOP_K, t // SC_WINDOW, SC_WINDOW).transpose(1, 0, 2)

    xs = _sc_scatter_rows(h_packed, pos_win, max_tiles * ROW_TILE)
    ys = _grouped_ffn(tile_expert, n_tiles, xs, w_exp_gate.astype(BF16), w_exp_up.astype(BF16),
                      w_exp_down.astype(BF16))
    yt = _sc_gather_rows(ys, pos_win)

    consts6 = [row(g_ffn), w_sh_gate.astype(BF16), w_sh_up.astype(BF16), w_sh_down.astype(BF16),
               row(g_ple_in), w_ple_gate.astype(BF16), w_ple_proj.astype(BF16), row(g_ple_out)]
    out = _combine(yt, w_tok, x1, p.reshape(t, PLE_DIM), consts6, tm=256)
    return out.reshape(b, s, d)


def kernel(x, p, positions, g_mix, w_in, sgu_ln_g, sgu_ln_b, sgu_w, sgu_b, mla_g_qa, mla_w_qb, mla_g_kva, mla_w_kvb, qk_g_q_nope, qk_g_k_nope, qk_g_q_rope, qk_g_k_rope, w_o, g_ffn, w_router, b_router, w_exp_gate, w_exp_up, w_exp_down, w_sh_gate, w_sh_up, w_sh_down, g_ple_in, w_ple_gate, w_ple_proj, g_ple_out):
    params = (g_mix, w_in, sgu_ln_g, sgu_ln_b, sgu_w, sgu_b, mla_g_qa, mla_w_qb, mla_g_kva, mla_w_kvb,
              qk_g_q_nope, qk_g_k_nope, qk_g_q_rope, qk_g_k_rope, w_o, g_ffn, w_router, b_router,
              w_exp_gate, w_exp_up, w_exp_down, w_sh_gate, w_sh_up, w_sh_down,
              g_ple_in, w_ple_gate, w_ple_proj, g_ple_out)
    for l in range(g_mix.shape[0]):
        x = _layer(x, p[l], positions, *[a[l] for a in params])
    return x
```

```python
import functools
import math

import jax
import jax.numpy as jnp
from jax import lax
from jax.experimental import pallas as pl
from jax.experimental.pallas import tpu as pltpu
from jax.experimental.pallas import tpu_sc as plsc

D_MODEL = 1024
PLE_DIM = 256
SGU_CHUNK = 128
SGU_GROUPS = 8
V_DIM = 128
HEADS = 8
QK_NOPE = 128
QK_ROPE = 64
QK_DIM = QK_NOPE + QK_ROPE
QK_PAD = 256
Q_LORA = 384
KV_LORA = 256
ROPE_THETA = 10000.0
N_EXPERTS = 64
N_GROUPS = 8
EXPERTS_PER_GROUP = 8
TOPK_GROUPS = 4
TOP_K = 8
EXPERT_FF = 256
ROUTED_SCALE = 2.5
NORM_EPS = 1e-6
LN_EPS = 1e-5

LANES = 128
MXU_COLS = 256
VMEM_LIMIT = 56 * 1024 * 1024
SC_CORES = 2
SC_SUBCORES = 16
SC_WINDOW = 64
ROW_TILE = 256
TILES_PER_STEP = 4
HALF = D_MODEL // 2

F32 = jnp.float32
BF16 = jnp.bfloat16
U32 = jnp.uint32
I32 = jnp.int32


def _dot(a, b):
    return jnp.dot(a, b, preferred_element_type=F32)


def _rms(xf, g, width=None):
    width = xf.shape[-1] if width is None else width
    ms = jnp.sum(xf * xf, axis=-1, keepdims=True) * (1.0 / width)
    return xf * lax.rsqrt(ms + NORM_EPS) * g


def _pack_rows(y):
    lo = pltpu.bitcast(y[:, :HALF].astype(BF16).astype(F32), U32) >> 16
    hi = pltpu.bitcast(y[:, HALF:].astype(BF16).astype(F32), U32) & jnp.uint32(0xFFFF0000)
    return lo | hi


def _unpack_rows(w):
    lo = pltpu.bitcast(w << 16, F32)
    hi = pltpu.bitcast(w & jnp.uint32(0xFFFF0000), F32)
    return lo, hi


def _rope_tables(pos_f):
    lane = lax.broadcasted_iota(I32, (1, LANES), 1)
    freq = (lane % (QK_ROPE // 2)).astype(F32)
    inv_freq = jnp.exp(freq * (-math.log(ROPE_THETA) * 2.0 / QK_ROPE))
    ang = pos_f * inv_freq
    return jnp.cos(ang), jnp.sin(ang)


def _norm_rope(piece, cos_g, sin_g):
    lane = lax.broadcasted_iota(I32, (1, LANES), 1)
    ssq = jnp.sum(jnp.where(lane < QK_ROPE, piece * piece, 0.0), axis=-1, keepdims=True)
    y = piece * lax.rsqrt(ssq * (1.0 / QK_ROPE) + NORM_EPS)
    return y * cos_g + pltpu.roll(y * sin_g, QK_ROPE, axis=1)


def _const_spec(shape):
    return pl.BlockSpec(shape, lambda *_: (0,) * len(shape))


def _mixer_prep_kernel(x_ref, pos_ref, g_mix_ref, wu_ref, wv_ref, wq_ref, wkv_ref, wr_ref, wga_ref, wgb_ref,
                       lng_ref, lnb_ref, sw_ref, sb_ref, gqa_ref, wqb_ref, gkva_ref, wkb_ref, wvb_ref,
                       gqn_ref, gkn_ref, gqc_ref, gqs_ref, gkc_ref, gks_ref,
                       ma_ref, gb_ref, q_ref, k_ref, v_ref):
    tm = x_ref.shape[1]
    xn = _rms(x_ref[0], g_mix_ref[...]).astype(BF16)

    cos_t, sin_t = _rope_tables(pos_ref[0].astype(F32))
    qn = _rms(_dot(xn, wq_ref[...]), gqa_ref[...]).astype(BF16)
    kvn = _rms(_dot(xn, wkv_ref[...]), gkva_ref[...]).astype(BF16)
    kpe = _norm_rope(_dot(xn, wr_ref[...]), cos_t * gkc_ref[...], sin_t * gks_ref[...]).astype(BF16)
    q_cos, q_sin = cos_t * gqc_ref[...], sin_t * gqs_ref[...]

    gv = jax.nn.gelu(_dot(xn, wv_ref[...]))
    mu = jnp.mean(gv, axis=-1, keepdims=True)
    vc = gv - mu
    var = jnp.mean(vc * vc, axis=-1, keepdims=True)
    vn = (vc * lax.rsqrt(var + LN_EPS) * lng_ref[...] + lnb_ref[...]).astype(BF16)
    row = lax.broadcasted_iota(I32, (SGU_CHUNK, SGU_CHUNK), 0)
    col = lax.broadcasted_iota(I32, (SGU_CHUNK, SGU_CHUNK), 1)
    causal = col <= row
    n_chunks = tm // SGU_CHUNK
    for pair in range(SGU_GROUPS // 2):
        ps = slice(pair * MXU_COLS, (pair + 1) * MXU_COLS)
        k2 = _dot(kvn, wkb_ref[:, ps])
        v2 = _dot(kvn, wvb_ref[:, ps]).astype(BF16)
        gu2 = jax.nn.gelu(_dot(xn, wu_ref[:, ps]))
        ga2 = jax.nn.sigmoid(_dot(xn, wga_ref[:, ps]))
        gb_ref[0, :, ps] = jax.nn.sigmoid(_dot(xn, wgb_ref[:, ps])).astype(BF16)
        for half in range(2):
            g = 2 * pair + half
            hs = slice(half * LANES, (half + 1) * LANES)
            cs = slice(g * SGU_CHUNK, (g + 1) * SGU_CHUNK)
            qh = _dot(qn, wqb_ref[:, g * QK_PAD:(g + 1) * QK_PAD])
            q_ref[0, g, :, :QK_NOPE] = _rms(qh[:, :QK_NOPE], gqn_ref[...]).astype(BF16)
            q_ref[0, g, :, QK_NOPE:] = _norm_rope(qh[:, QK_NOPE:], q_cos, q_sin).astype(BF16)
            k_ref[0, g, :, :QK_NOPE] = _rms(k2[:, hs], gkn_ref[...]).astype(BF16)
            k_ref[0, g, :, QK_NOPE:] = kpe
            v_ref[0, g] = v2[:, hs]

            wg = jnp.where(causal, sw_ref[g], 0.0).astype(BF16)
            vcat = jnp.concatenate([vn[c * SGU_CHUNK:(c + 1) * SGU_CHUNK, cs] for c in range(n_chunks)], axis=1)
            mixed = _dot(wg, vcat)
            for c in range(n_chunks):
                rs = slice(c * SGU_CHUNK, (c + 1) * SGU_CHUNK)
                m = mixed[:, c * SGU_CHUNK:(c + 1) * SGU_CHUNK] + sb_ref[:, cs]
                ma_ref[0, rs, cs] = (ga2[rs, hs] * gu2[rs, hs] * m).astype(BF16)


def _mixer_prep(x, pos, consts, tm):
    b, s, d = x.shape
    grid = (b, s // tm)
    row_spec = lambda w: pl.BlockSpec((1, tm, w), lambda i, j: (i, j, 0))
    head_spec = lambda w: pl.BlockSpec((1, HEADS, tm, w), lambda i, j: (i, 0, j, 0))
    return pl.pallas_call(
        _mixer_prep_kernel,
        grid=grid,
        in_specs=[row_spec(d), row_spec(1)] + [_const_spec(c.shape) for c in consts],
        out_specs=[row_spec(d), row_spec(d), head_spec(QK_PAD), head_spec(QK_PAD), head_spec(V_DIM)],
        out_shape=[jax.ShapeDtypeStruct((b, s, d), BF16), jax.ShapeDtypeStruct((b, s, d), BF16),
                   jax.ShapeDtypeStruct((b, HEADS, s, QK_PAD), BF16),
                   jax.ShapeDtypeStruct((b, HEADS, s, QK_PAD), BF16),
                   jax.ShapeDtypeStruct((b, HEADS, s, V_DIM), BF16)],
        compiler_params=pltpu.CompilerParams(dimension_semantics=("arbitrary", "arbitrary"),
                                             vmem_limit_bytes=VMEM_LIMIT),
        name="mixer_prep",
    )(x, pos, *consts)


def _attn_kernel(q_ref, k_ref, v_ref, o_ref, *, tq):
    s = q_ref.shape[2]
    row = lax.broadcasted_iota(I32, (tq, tq), 0)
    col = lax.broadcasted_iota(I32, (tq, tq), 1)
    diag_mask = col <= row

    def scores(q, start):
        kt = k_ref[0, 0, pl.ds(start, tq), :]
        return lax.dot_general(q, kt, (((1,), (1,)), ((), ())), preferred_element_type=F32)

    def update(carry, sc, start):
        m, l, acc = carry
        m_new = jnp.maximum(m, jnp.max(sc, axis=-1, keepdims=True))
        alpha = jnp.exp(m - m_new)
        p = jnp.exp(sc - m_new)
        l = alpha * l + jnp.sum(p, axis=-1, keepdims=True)
        vt = v_ref[0, 0, pl.ds(start, tq), :]
        acc = alpha * acc + _dot(p.astype(BF16), vt)
        return m_new, l, acc

    for qi in range(s // tq):
        q = q_ref[0, 0, qi * tq:(qi + 1) * tq, :]
        sc = jnp.where(diag_mask, scores(q, qi * tq), -jnp.inf)
        m0 = jnp.max(sc, axis=-1, keepdims=True)
        p0 = jnp.exp(sc - m0)
        carry = (m0, jnp.sum(p0, axis=-1, keepdims=True),
                 _dot(p0.astype(BF16), v_ref[0, 0, qi * tq:(qi + 1) * tq, :]))

        def body(kj, carry, q=q):
            start = pl.multiple_of(kj * tq, tq)
            return update(carry, scores(q, start), start)

        m, l, acc = lax.fori_loop(0, qi, body, carry)
        o_ref[0, qi * tq:(qi + 1) * tq, :] = (acc / l).astype(BF16)


def _attention(q, k, v, tq):
    b, h, s, _ = q.shape
    return pl.pallas_call(
        functools.partial(_attn_kernel, tq=tq),
        grid=(b, h),
        in_specs=[pl.BlockSpec((1, 1, s, QK_PAD), lambda i, j: (i, j, 0, 0)),
                  pl.BlockSpec((1, 1, s, QK_PAD), lambda i, j: (i, j, 0, 0)),
                  pl.BlockSpec((1, 1, s, V_DIM), lambda i, j: (i, j, 0, 0))],
        out_specs=pl.BlockSpec((1, s, V_DIM), lambda i, j: (i, 0, j)),
        out_shape=jax.ShapeDtypeStruct((b, s, h * V_DIM), BF16),
        compiler_params=pltpu.CompilerParams(dimension_semantics=("arbitrary", "arbitrary"),
                                             vmem_limit_bytes=VMEM_LIMIT),
        name="mla_attention",
    )(q, k, v)


def _route(sel, scores):
    e, tm = sel.shape
    sel3 = sel.reshape(N_GROUPS, EXPERTS_PER_GROUP, tm)
    sub = lax.broadcasted_iota(I32, sel3.shape, 1)
    m1 = jnp.max(sel3, axis=1, keepdims=True)
    first = jnp.min(jnp.where(sel3 == m1, sub, EXPERTS_PER_GROUP), axis=1, keepdims=True)
    m2 = jnp.max(jnp.where(sub == first, -jnp.inf, sel3), axis=1, keepdims=True)
    gscore = (m1 + m2).reshape(N_GROUPS, tm)
    gid = lax.broadcasted_iota(I32, (N_GROUPS, tm), 0)
    grank = jnp.zeros((N_GROUPS, tm), I32)
    for g in range(N_GROUPS):
        other = gscore[g:g + 1]
        grank += ((other > gscore) | ((other == gscore) & (g < gid))).astype(I32)
    gmask = grank < TOPK_GROUPS
    emask = jnp.broadcast_to(gmask[:, None, :], sel3.shape).reshape(e, tm)
    msel = jnp.where(emask, sel, -jnp.inf)
    eid = lax.broadcasted_iota(I32, (e, tm), 0)
    erank = jnp.zeros((e, tm), I32)
    for j in range(e):
        other = msel[j:j + 1]
        erank += ((other > msel) | ((other == msel) & (j < eid))).astype(I32)
    chosen = erank < TOP_K
    w = jnp.where(chosen, scores, 0.0)
    return chosen, w / jnp.sum(w, axis=0, keepdims=True) * ROUTED_SCALE


def _post_attn_kernel(x_ref, ma_ref, gb_ref, o_ref, wo_ref, gffn_ref, wrt_ref, br_ref,
                      x1_ref, hp_ref, comb_ref, rank_ref, cnt_ref, run_ref):
    tm = x_ref.shape[0]

    @pl.when(pl.program_id(0) == 0)
    def _():
        run_ref[...] = jnp.zeros_like(run_ref)

    merged = ma_ref[...].astype(F32) + gb_ref[...].astype(F32) * o_ref[...].astype(F32)
    x1 = x_ref[...] + _dot(merged.astype(BF16), wo_ref[...])
    x1_ref[...] = x1
    h2 = _rms(x1, gffn_ref[...])
    hp_ref[...] = _pack_rows(h2)
    logits_t = lax.dot_general(wrt_ref[...], h2, (((1,), (1,)), ((), ())),
                               preferred_element_type=F32, precision=lax.Precision.HIGHEST)
    scores = jax.nn.sigmoid(logits_t)
    chosen, comb_t = _route(scores + br_ref[...], scores)
    comb_ref[...] = comb_t
    a = lax.broadcasted_iota(I32, (tm, tm), 0)
    b = lax.broadcasted_iota(I32, (tm, tm), 1)
    before = (a < b).astype(BF16)
    chosen_f = chosen.astype(F32)
    prefix = _dot(chosen_f.astype(BF16), before)
    run = run_ref[:, 0:1]
    rank_ref[...] = jnp.where(chosen, prefix + run, -1.0)
    run_ref[...] += jnp.sum(chosen_f, axis=1, keepdims=True)
    cnt_ref[...] = run_ref[...]


def _post_attn(x, ma, gb, o, consts, tm):
    t, d = x.shape
    row_spec = lambda w: pl.BlockSpec((tm, w), lambda i: (i, 0))
    col_spec = pl.BlockSpec((N_EXPERTS, tm), lambda i: (0, i))
    return pl.pallas_call(
        _post_attn_kernel,
        grid=(t // tm,),
        in_specs=[row_spec(d)] * 4 + [_const_spec(c.shape) for c in consts],
        out_specs=[row_spec(d), row_spec(HALF), col_spec, col_spec, _const_spec((N_EXPERTS, LANES))],
        out_shape=[jax.ShapeDtypeStruct((t, d), F32), jax.ShapeDtypeStruct((t, HALF), U32),
                   jax.ShapeDtypeStruct((N_EXPERTS, t), F32), jax.ShapeDtypeStruct((N_EXPERTS, t), F32),
                   jax.ShapeDtypeStruct((N_EXPERTS, LANES), F32)],
        scratch_shapes=[pltpu.VMEM((N_EXPERTS, LANES), F32)],
        compiler_params=pltpu.CompilerParams(dimension_semantics=("arbitrary",), vmem_limit_bytes=VMEM_LIMIT),
        name="post_attn_router",
    )(x, ma, gb, o, *consts)


def _route_lists_kernel(comb_ref, rank_ref, off_ref, pos_ref, w_ref):
    rank = rank_ref[...]
    chosen = rank >= 0.0
    e = rank.shape[0]
    lower = (lax.broadcasted_iota(I32, (e, e), 1) < lax.broadcasted_iota(I32, (e, e), 0)).astype(BF16)
    slot = _dot(lower, chosen.astype(BF16))
    pos_full = rank + off_ref[...]
    comb = comb_ref[...]
    pos_rows, w_rows = [], []
    for k in range(TOP_K):
        pick = chosen & (slot == float(k))
        pos_rows.append(jnp.sum(jnp.where(pick, pos_full, 0.0), axis=0, keepdims=True))
        w_rows.append(jnp.sum(jnp.where(pick, comb, 0.0), axis=0, keepdims=True))
    pos_ref[...] = jnp.concatenate(pos_rows, axis=0).astype(I32)
    w_ref[...] = jnp.concatenate(w_rows, axis=0).T


def _route_lists(comb_t, rank_t, offsets, tm):
    e, t = comb_t.shape
    col_spec = pl.BlockSpec((e, tm), lambda i: (0, i))
    return pl.pallas_call(
        _route_lists_kernel,
        grid=(t // tm,),
        in_specs=[col_spec, col_spec, _const_spec((e, 1))],
        out_specs=[pl.BlockSpec((TOP_K, tm), lambda i: (0, i)), pl.BlockSpec((tm, TOP_K), lambda i: (i, 0))],
        out_shape=[jax.ShapeDtypeStruct((TOP_K, t), I32), jax.ShapeDtypeStruct((t, TOP_K), F32)],
        compiler_params=pltpu.CompilerParams(dimension_semantics=("arbitrary",)),
        name="route_lists",
    )(comb_t, rank_t, offsets)


def _sc_mesh():
    return plsc.VectorSubcoreMesh(core_axis_name="c", subcore_axis_name="s",
                                  num_cores=SC_CORES, num_subcores=SC_SUBCORES)


def _sc_worker():
    return lax.axis_index("s") * SC_CORES + lax.axis_index("c")


def _sc_scatter_rows(src, pos_win, n_rows):
    t, width = src.shape
    n_win = t // SC_WINDOW
    per_w = n_win // (SC_CORES * SC_SUBCORES)

    @functools.partial(
        pl.kernel, mesh=_sc_mesh(), out_type=jax.ShapeDtypeStruct((n_rows, width), src.dtype),
        scratch_types=[pltpu.VMEM((TOP_K, SC_WINDOW), I32), pltpu.VMEM((SC_WINDOW, width), src.dtype),
                       pltpu.SemaphoreType.DMA],
        name="sc_scatter_rows")
    def run(src_hbm, pos_hbm, out_hbm, idx_v, rows_v, sem):
        base = _sc_worker() * per_w

        @pl.loop(0, per_w)
        def _(j):
            w = base + j
            pltpu.sync_copy(pos_hbm.at[w], idx_v)
            pltpu.sync_copy(src_hbm.at[pl.ds(w * SC_WINDOW, SC_WINDOW)], rows_v)
            copies = [pltpu.async_copy(rows_v, out_hbm.at[idx_v.at[k]], sem) for k in range(TOP_K)]
            for c in copies:
                c.wait()

    return run(src, pos_win)


def _sc_gather_rows(table, pos_win):
    n_win = pos_win.shape[0]
    width = table.shape[1]
    per_w = n_win // (SC_CORES * SC_SUBCORES)

    @functools.partial(
        pl.kernel, mesh=_sc_mesh(),
        out_type=jax.ShapeDtypeStruct((TOP_K, n_win * SC_WINDOW, width), table.dtype),
        scratch_types=[pltpu.VMEM((TOP_K, SC_WINDOW), I32), pltpu.VMEM((2, SC_WINDOW, width), table.dtype),
                       pltpu.SemaphoreType.DMA, pltpu.SemaphoreType.DMA],
        name="sc_gather_rows")
    def run(table_hbm, pos_hbm, out_hbm, idx_v, rows_v, gsem, wsem):
        base = _sc_worker() * per_w

        @pl.loop(0, per_w)
        def _(j):
            w = base + j
            pltpu.sync_copy(pos_hbm.at[w], idx_v)
            for k in range(TOP_K):
                buf = rows_v.at[k % 2]
                pltpu.async_copy(table_hbm.at[idx_v.at[k]], buf, gsem).wait()
                pltpu.async_copy(buf, out_hbm.at[k, pl.ds(w * SC_WINDOW, SC_WINDOW)], wsem).wait()

    return run(table, pos_win)


def _grouped_ffn_kernel(te_ref, nt_ref, xs_ref, *refs):
    w_refs, ys_ref = refs[:-1], refs[-1]

    @pl.when(pl.program_id(0) * TILES_PER_STEP < nt_ref[0])
    def _():
        acts = []
        for j in range(TILES_PER_STEP):
            wgu_ref = w_refs[2 * j]
            lo, hi = _unpack_rows(xs_ref[j * ROW_TILE:(j + 1) * ROW_TILE])
            gu = _dot(lo.astype(BF16), wgu_ref[0, :HALF]) + _dot(hi.astype(BF16), wgu_ref[0, HALF:])
            acts.append((jax.nn.silu(gu[:, :EXPERT_FF]) * gu[:, EXPERT_FF:]).astype(BF16))
        for j in range(TILES_PER_STEP):
            ys_ref[j * ROW_TILE:(j + 1) * ROW_TILE] = _pack_rows(_dot(acts[j], w_refs[2 * j + 1][0]))


def _grouped_ffn(tile_expert, n_tiles, xs, wgu, wd):
    n_rows, half = xs.shape
    rows_per_step = TILES_PER_STEP * ROW_TILE
    n_steps = n_rows // rows_per_step

    def row_map(i, te, nt):
        return (jnp.minimum(i, (nt[0] - 1) // TILES_PER_STEP), 0)

    def exp_map(j):
        return lambda i, te, nt: (te[jnp.minimum(i * TILES_PER_STEP + j, nt[0] - 1)], 0, 0)

    w_specs = []
    for j in range(TILES_PER_STEP):
        w_specs += [pl.BlockSpec((1,) + w.shape[1:], exp_map(j)) for w in (wgu, wd)]
    return pl.pallas_call(
        _grouped_ffn_kernel,
        grid_spec=pltpu.PrefetchScalarGridSpec(
            num_scalar_prefetch=2,
            grid=(n_steps,),
            in_specs=[pl.BlockSpec((rows_per_step, half), row_map)] + w_specs,
            out_specs=pl.BlockSpec((rows_per_step, half), row_map)),
        out_shape=jax.ShapeDtypeStruct((n_rows, half), U32),
        compiler_params=pltpu.CompilerParams(dimension_semantics=("arbitrary",), vmem_limit_bytes=VMEM_LIMIT),
        name="grouped_ffn",
    )(tile_expert, n_tiles, xs, *([wgu, wd] * TILES_PER_STEP))


def _combine_kernel(yt_ref, w_ref, x1_ref, p_ref, gffn_ref, wsg_ref, wsu_ref, wsd_ref,
                    gpi_ref, wpg_ref, wpp_ref, gpo_ref, out_ref):
    x1 = x1_ref[...]
    w = w_ref[...]
    acc_lo = jnp.zeros((x1.shape[0], HALF), F32)
    acc_hi = jnp.zeros((x1.shape[0], HALF), F32)
    for k in range(TOP_K):
        lo, hi = _unpack_rows(yt_ref[k])
        wk = w[:, k:k + 1]
        acc_lo += wk * lo
        acc_hi += wk * hi
    h = _rms(x1, gffn_ref[...]).astype(BF16)
    shared = _dot((jax.nn.silu(_dot(h, wsg_ref[...])) * _dot(h, wsu_ref[...])).astype(BF16), wsd_ref[...])
    x2 = x1 + shared + jnp.concatenate([acc_lo, acc_hi], axis=-1)
    gate = jax.nn.sigmoid(_dot(_rms(x2, gpi_ref[...]).astype(BF16), wpg_ref[...]))
    proj = _rms(_dot(p_ref[...].astype(BF16), wpp_ref[...]), gpo_ref[...])
    out_ref[...] = x2 + gate * proj


def _combine(yt, w_tok, x1, p, consts, tm):
    t, d = x1.shape
    row_spec = lambda w: pl.BlockSpec((tm, w), lambda i: (i, 0))
    return pl.pallas_call(
        _combine_kernel,
        grid=(t // tm,),
        in_specs=[pl.BlockSpec((TOP_K, tm, HALF), lambda i: (0, i, 0)), row_spec(TOP_K), row_spec(d),
                  row_spec(PLE_DIM)] + [_const_spec(c.shape) for c in consts],
        out_specs=row_spec(d),
        out_shape=jax.ShapeDtypeStruct((t, d), F32),
        compiler_params=pltpu.CompilerParams(dimension_semantics=("arbitrary",), vmem_limit_bytes=VMEM_LIMIT),
        name="combine_ple",
    )(yt, w_tok, x1, p, *consts)


def _rotate_half(a):
    half = QK_ROPE // 2
    return jnp.concatenate([-a[..., half:], a[..., :half]], axis=-1)


def _with_rotate_half(w):
    return jnp.concatenate([w, _rotate_half(w[..., -QK_ROPE:])], axis=-1)


def _rope_gain_tables(g, scale):
    zeros = jnp.zeros((QK_ROPE,), F32)
    half = QK_ROPE // 2
    g_swapped = jnp.concatenate([g[half:], g[:half]])
    return (jnp.concatenate([g, zeros]) * scale).reshape(1, LANES), \
        (jnp.concatenate([zeros, g_swapped]) * scale).reshape(1, LANES)


def _layer(x, p, pos, g_mix, w_in, sgu_ln_g, sgu_ln_b, sgu_w, sgu_b, mla_g_qa, mla_w_qb, mla_g_kva, mla_w_kvb,
           qk_g_q_nope, qk_g_k_nope, qk_g_q_rope, qk_g_k_rope, w_o, g_ffn, w_router, b_router,
           w_exp_gate, w_exp_up, w_exp_down, w_sh_gate, w_sh_up, w_sh_down,
           g_ple_in, w_ple_gate, w_ple_proj, g_ple_out):
    b, s, d = x.shape
    t = b * s
    row = lambda a: a.reshape(1, -1)
    sizes = [d, d, Q_LORA, KV_LORA, QK_ROPE, d, d]
    offs = [0]
    for sz in sizes:
        offs.append(offs[-1] + sz)
    w_u, w_v, w_q, w_kv, w_r, w_ga, w_gb = [w_in[:, offs[i]:offs[i + 1]].astype(BF16) for i in range(7)]
    w_r = _with_rotate_half(w_r)
    wqb = _with_rotate_half(mla_w_qb.reshape(Q_LORA, HEADS, QK_DIM)).reshape(Q_LORA, HEADS * QK_PAD).astype(BF16)
    wkvb = mla_w_kvb.reshape(KV_LORA, HEADS, QK_NOPE + V_DIM)
    wkb = wkvb[:, :, :QK_NOPE].reshape(KV_LORA, HEADS * QK_NOPE).astype(BF16)
    wvb = wkvb[:, :, QK_NOPE:].reshape(KV_LORA, HEADS * V_DIM).astype(BF16)
    sgu_bias = jnp.repeat(sgu_b.T, d // SGU_GROUPS, axis=1)
    q_scale = QK_DIM ** -0.5

    consts1 = [row(g_mix), w_u, w_v, w_q, w_kv, w_r, w_ga, w_gb, row(sgu_ln_g), row(sgu_ln_b), sgu_w, sgu_bias,
               row(mla_g_qa), wqb, row(mla_g_kva), wkb, wvb, row(qk_g_q_nope) * q_scale, row(qk_g_k_nope),
               *_rope_gain_tables(qk_g_q_rope, q_scale), *_rope_gain_tables(qk_g_k_rope, 1.0)]
    ma, gb, q, k, v = _mixer_prep(x, pos.reshape(b, s, 1), consts1, tm=512)
    o = _attention(q, k, v, tq=512)

    consts3 = [w_o.astype(BF16), row(g_ffn), w_router.T, b_router.reshape(-1, 1)]
    x1, h_packed, comb_t, rank_t, counts = _post_attn(
        x.reshape(t, d), ma.reshape(t, d), gb.reshape(t, d), o.reshape(t, d), consts3, tm=512)

    counts = counts[:, 0].astype(I32)
    tiles_per_expert = (counts + ROW_TILE - 1) // ROW_TILE
    tile_end = jnp.cumsum(tiles_per_expert)
    offsets = ((tile_end - tiles_per_expert) * ROW_TILE).astype(F32).reshape(N_EXPERTS, 1)
    max_tiles = (t * TOP_K) // ROW_TILE + N_EXPERTS
    tile_ids = jnp.arange(max_tiles, dtype=I32)
    tile_expert = jnp.minimum(jnp.sum((tile_end[None, :] <= tile_ids[:, None]).astype(I32), axis=1),
                              N_EXPERTS - 1)
    n_tiles = tile_end[-1:].astype(I32)

    pos_t, w_tok = _route_lists(comb_t, rank_t, offsets, tm=512)
    pos_win = pos_t.reshape(TOP_K, t // SC_WINDOW, SC_WINDOW).transpose(1, 0, 2)

    xs = _sc_scatter_rows(h_packed, pos_win, max_tiles * ROW_TILE)
    w_gate_up = jnp.concatenate([w_exp_gate.astype(BF16), w_exp_up.astype(BF16)], axis=2)
    ys = _grouped_ffn(tile_expert, n_tiles, xs, w_gate_up, w_exp_down.astype(BF16))
    yt = _sc_gather_rows(ys, pos_win)

    consts6 = [row(g_ffn), w_sh_gate.astype(BF16), w_sh_up.astype(BF16), w_sh_down.astype(BF16),
               row(g_ple_in), w_ple_gate.astype(BF16), w_ple_proj.astype(BF16), row(g_ple_out)]
    out = _combine(yt, w_tok, x1, p.reshape(t, PLE_DIM), consts6, tm=256)
    return out.reshape(b, s, d)


def kernel(x, p, positions, g_mix, w_in, sgu_ln_g, sgu_ln_b, sgu_w, sgu_b, mla_g_qa, mla_w_qb, mla_g_kva, mla_w_kvb, qk_g_q_nope, qk_g_k_nope, qk_g_q_rope, qk_g_k_rope, w_o, g_ffn, w_router, b_router, w_exp_gate, w_exp_up, w_exp_down, w_sh_gate, w_sh_up, w_sh_down, g_ple_in, w_ple_gate, w_ple_proj, g_ple_out):
    params = (g_mix, w_in, sgu_ln_g, sgu_ln_b, sgu_w, sgu_b, mla_g_qa, mla_w_qb, mla_g_kva, mla_w_kvb,
              qk_g_q_nope, qk_g_k_nope, qk_g_q_rope, qk_g_k_rope, w_o, g_ffn, w_router, b_router,
              w_exp_gate, w_exp_up, w_exp_down, w_sh_gate, w_sh_up, w_sh_down,
              g_ple_in, w_ple_gate, w_ple_proj, g_ple_out)
    for l in range(g_mix.shape[0]):
        x = _layer(x, p[l], positions, *[a[l] for a in params])
    return x
```

```python
import functools
import math

import jax
import jax.numpy as jnp
from jax import lax
from jax.experimental import pallas as pl
from jax.experimental.pallas import tpu as pltpu
from jax.experimental.pallas import tpu_sc as plsc

D_MODEL = 1024
PLE_DIM = 256
SGU_CHUNK = 128
SGU_GROUPS = 8
V_DIM = 128
HEADS = 8
QK_NOPE = 128
QK_ROPE = 64
QK_DIM = QK_NOPE + QK_ROPE
QK_PAD = 256
Q_LORA = 384
KV_LORA = 256
ROPE_THETA = 10000.0
N_EXPERTS = 64
N_GROUPS = 8
EXPERTS_PER_GROUP = 8
TOPK_GROUPS = 4
TOP_K = 8
EXPERT_FF = 256
ROUTED_SCALE = 2.5
NORM_EPS = 1e-6
LN_EPS = 1e-5

LANES = 128
MXU_COLS = 256
VMEM_LIMIT = 56 * 1024 * 1024
SC_CORES = 2
SC_SUBCORES = 16
SC_WINDOW = 64
ROW_TILE = 256
TILES_PER_STEP = 4
HALF = D_MODEL // 2

F32 = jnp.float32
BF16 = jnp.bfloat16
U32 = jnp.uint32
I32 = jnp.int32


def _dot(a, b):
    return lax.dot_general(a, b, (((1,), (0,)), ((), ())), preferred_element_type=F32)


def _rms(xf, g, width=None):
    width = xf.shape[-1] if width is None else width
    ms = jnp.sum(xf * xf, axis=-1, keepdims=True) * (1.0 / width)
    return xf * lax.rsqrt(ms + NORM_EPS) * g


def _pack_rows(y):
    lo = pltpu.bitcast(y[:, :HALF].astype(BF16).astype(F32), U32) >> 16
    hi = pltpu.bitcast(y[:, HALF:].astype(BF16).astype(F32), U32) & jnp.uint32(0xFFFF0000)
    return lo | hi


def _unpack_rows(w):
    lo = pltpu.bitcast(w << 16, F32)
    hi = pltpu.bitcast(w & jnp.uint32(0xFFFF0000), F32)
    return lo, hi


def _rope_tables(pos_f):
    lane = lax.broadcasted_iota(I32, (1, LANES), 1)
    freq = (lane % (QK_ROPE // 2)).astype(F32)
    inv_freq = jnp.exp(freq * (-math.log(ROPE_THETA) * 2.0 / QK_ROPE))
    ang = pos_f * inv_freq
    return jnp.cos(ang), jnp.sin(ang)


def _norm_rope(piece, cos_g, sin_g):
    lane = lax.broadcasted_iota(I32, (1, LANES), 1)
    ssq = jnp.sum(jnp.where(lane < QK_ROPE, piece * piece, 0.0), axis=-1, keepdims=True)
    y = piece * lax.rsqrt(ssq * (1.0 / QK_ROPE) + NORM_EPS)
    return y * cos_g + pltpu.roll(y * sin_g, QK_ROPE, axis=1)


def _const_spec(shape):
    return pl.BlockSpec(shape, lambda *_: (0,) * len(shape), pipeline_mode=pl.Buffered(1))


def _mixer_prep_kernel(x_ref, pos_ref, g_mix_ref, wu_ref, wv_ref, wq_ref, wkv_ref, wr_ref, wga_ref, wgb_ref,
                       lng_ref, lnb_ref, sw_ref, sb_ref, gqa_ref, wqb_ref, gkva_ref, wkb_ref, wvb_ref,
                       gqn_ref, gkn_ref, gqc_ref, gqs_ref, gkc_ref, gks_ref,
                       ma_ref, gb_ref, q_ref, k_ref, v_ref):
    tm = x_ref.shape[1]
    xn = _rms(x_ref[0], g_mix_ref[...]).astype(BF16)

    cos_t, sin_t = _rope_tables(pos_ref[0].astype(F32))
    qn = _rms(_dot(xn, wq_ref[...]), gqa_ref[...]).astype(BF16)
    kvn = _rms(_dot(xn, wkv_ref[...]), gkva_ref[...]).astype(BF16)
    kpe = _norm_rope(_dot(xn, wr_ref[...]), cos_t * gkc_ref[...], sin_t * gks_ref[...]).astype(BF16)
    q_cos, q_sin = cos_t * gqc_ref[...], sin_t * gqs_ref[...]

    gv = jax.nn.gelu(_dot(xn, wv_ref[...]))
    mu = jnp.mean(gv, axis=-1, keepdims=True)
    vc = gv - mu
    var = jnp.mean(vc * vc, axis=-1, keepdims=True)
    vn = (vc * lax.rsqrt(var + LN_EPS) * lng_ref[...] + lnb_ref[...]).astype(BF16)
    row = lax.broadcasted_iota(I32, (SGU_CHUNK, SGU_CHUNK), 0)
    col = lax.broadcasted_iota(I32, (SGU_CHUNK, SGU_CHUNK), 1)
    causal = col <= row
    n_chunks = tm // SGU_CHUNK
    for pair in range(SGU_GROUPS // 2):
        ps = slice(pair * MXU_COLS, (pair + 1) * MXU_COLS)
        k2 = _dot(kvn, wkb_ref[:, ps])
        v2 = _dot(kvn, wvb_ref[:, ps]).astype(BF16)
        gu2 = jax.nn.gelu(_dot(xn, wu_ref[:, ps]))
        ga2 = jax.nn.sigmoid(_dot(xn, wga_ref[:, ps]))
        gb_ref[0, :, ps] = jax.nn.sigmoid(_dot(xn, wgb_ref[:, ps])).astype(BF16)
        for half in range(2):
            g = 2 * pair + half
            hs = slice(half * LANES, (half + 1) * LANES)
            cs = slice(g * SGU_CHUNK, (g + 1) * SGU_CHUNK)
            qh = _dot(qn, wqb_ref[:, g * QK_PAD:(g + 1) * QK_PAD])
            q_ref[0, g, :, :QK_NOPE] = _rms(qh[:, :QK_NOPE], gqn_ref[...]).astype(BF16)
            q_ref[0, g, :, QK_NOPE:] = _norm_rope(qh[:, QK_NOPE:], q_cos, q_sin).astype(BF16)
            k_ref[0, g, :, :QK_NOPE] = _rms(k2[:, hs], gkn_ref[...]).astype(BF16)
            k_ref[0, g, :, QK_NOPE:] = kpe
            v_ref[0, g] = v2[:, hs]

            wg = jnp.where(causal, sw_ref[g], 0.0).astype(BF16)
            vcat = jnp.concatenate([vn[c * SGU_CHUNK:(c + 1) * SGU_CHUNK, cs] for c in range(n_chunks)], axis=1)
            mixed = _dot(wg, vcat)
            for c in range(n_chunks):
                rs = slice(c * SGU_CHUNK, (c + 1) * SGU_CHUNK)
                m = mixed[:, c * SGU_CHUNK:(c + 1) * SGU_CHUNK] + sb_ref[:, cs]
                ma_ref[0, rs, cs] = (ga2[rs, hs] * gu2[rs, hs] * m).astype(BF16)


def _mixer_prep(x, pos, consts, tm):
    b, s, d = x.shape
    grid = (b, s // tm)
    row_spec = lambda w: pl.BlockSpec((1, tm, w), lambda i, j: (i, j, 0))
    head_spec = lambda w: pl.BlockSpec((1, HEADS, tm, w), lambda i, j: (i, 0, j, 0))
    return pl.pallas_call(
        _mixer_prep_kernel,
        grid=grid,
        in_specs=[row_spec(d), row_spec(1)] + [_const_spec(c.shape) for c in consts],
        out_specs=[row_spec(d), row_spec(d), head_spec(QK_PAD), head_spec(QK_PAD), head_spec(V_DIM)],
        out_shape=[jax.ShapeDtypeStruct((b, s, d), BF16), jax.ShapeDtypeStruct((b, s, d), BF16),
                   jax.ShapeDtypeStruct((b, HEADS, s, QK_PAD), BF16),
                   jax.ShapeDtypeStruct((b, HEADS, s, QK_PAD), BF16),
                   jax.ShapeDtypeStruct((b, HEADS, s, V_DIM), BF16)],
        compiler_params=pltpu.CompilerParams(dimension_semantics=("arbitrary", "arbitrary"),
                                             vmem_limit_bytes=VMEM_LIMIT),
        name="mixer_prep",
    )(x, pos, *consts)


def _attn_kernel(q_ref, k_ref, v_ref, o_ref, *, tq):
    s = q_ref.shape[2]
    row = lax.broadcasted_iota(I32, (tq, tq), 0)
    col = lax.broadcasted_iota(I32, (tq, tq), 1)
    diag_mask = col <= row

    def scores(q, start):
        kt = k_ref[0, 0, pl.ds(start, tq), :]
        return lax.dot_general(q, kt, (((1,), (1,)), ((), ())), preferred_element_type=F32)

    def update(carry, sc, start):
        m, l, acc = carry
        m_new = jnp.maximum(m, jnp.max(sc, axis=-1, keepdims=True))
        alpha = jnp.exp(m - m_new)
        p = jnp.exp(sc - m_new)
        l = alpha * l + jnp.sum(p, axis=-1, keepdims=True)
        vt = v_ref[0, 0, pl.ds(start, tq), :]
        acc = alpha * acc + _dot(p.astype(BF16), vt)
        return m_new, l, acc

    for qi in range(s // tq):
        q = q_ref[0, 0, qi * tq:(qi + 1) * tq, :]
        sc = jnp.where(diag_mask, scores(q, qi * tq), -jnp.inf)
        m0 = jnp.max(sc, axis=-1, keepdims=True)
        p0 = jnp.exp(sc - m0)
        carry = (m0, jnp.sum(p0, axis=-1, keepdims=True),
                 _dot(p0.astype(BF16), v_ref[0, 0, qi * tq:(qi + 1) * tq, :]))

        def body(kj, carry, q=q):
            start = pl.multiple_of(kj * tq, tq)
            return update(carry, scores(q, start), start)

        m, l, acc = lax.fori_loop(0, qi, body, carry)
        o_ref[0, qi * tq:(qi + 1) * tq, :] = (acc / l).astype(BF16)


def _attention(q, k, v, tq):
    b, h, s, _ = q.shape
    return pl.pallas_call(
        functools.partial(_attn_kernel, tq=tq),
        grid=(b, h),
        in_specs=[pl.BlockSpec((1, 1, s, QK_PAD), lambda i, j: (i, j, 0, 0)),
                  pl.BlockSpec((1, 1, s, QK_PAD), lambda i, j: (i, j, 0, 0)),
                  pl.BlockSpec((1, 1, s, V_DIM), lambda i, j: (i, j, 0, 0))],
        out_specs=pl.BlockSpec((1, s, V_DIM), lambda i, j: (i, 0, j)),
        out_shape=jax.ShapeDtypeStruct((b, s, h * V_DIM), BF16),
        compiler_params=pltpu.CompilerParams(dimension_semantics=("arbitrary", "arbitrary"),
                                             vmem_limit_bytes=VMEM_LIMIT),
        name="mla_attention",
    )(q, k, v)


def _route(sel, scores):
    e, tm = sel.shape
    sel3 = sel.reshape(N_GROUPS, EXPERTS_PER_GROUP, tm)
    sub = lax.broadcasted_iota(I32, sel3.shape, 1)
    m1 = jnp.max(sel3, axis=1, keepdims=True)
    first = jnp.min(jnp.where(sel3 == m1, sub, EXPERTS_PER_GROUP), axis=1, keepdims=True)
    m2 = jnp.max(jnp.where(sub == first, -jnp.inf, sel3), axis=1, keepdims=True)
    gscore = (m1 + m2).reshape(N_GROUPS, tm)
    gid = lax.broadcasted_iota(I32, (N_GROUPS, tm), 0)
    grank = jnp.zeros((N_GROUPS, tm), I32)
    for g in range(N_GROUPS):
        other = gscore[g:g + 1]
        grank += ((other > gscore) | ((other == gscore) & (g < gid))).astype(I32)
    gmask = grank < TOPK_GROUPS
    emask = jnp.broadcast_to(gmask[:, None, :], sel3.shape).reshape(e, tm)
    msel = jnp.where(emask, sel, -jnp.inf)
    eid = lax.broadcasted_iota(I32, (e, tm), 0)
    erank = jnp.zeros((e, tm), I32)
    for j in range(e):
        other = msel[j:j + 1]
        erank += ((other > msel) | ((other == msel) & (j < eid))).astype(I32)
    chosen = erank < TOP_K
    w = jnp.where(chosen, scores, 0.0)
    return chosen, w / jnp.sum(w, axis=0, keepdims=True) * ROUTED_SCALE


def _post_attn_kernel(x_ref, ma_ref, gb_ref, o_ref, wo_ref, gffn_ref, wrt_ref, br_ref,
                      x1_ref, hp_ref, comb_ref, rank_ref, cnt_ref, run_ref):
    tm = x_ref.shape[0]

    @pl.when(pl.program_id(0) == 0)
    def _():
        run_ref[...] = jnp.zeros_like(run_ref)

    merged = ma_ref[...].astype(F32) + gb_ref[...].astype(F32) * o_ref[...].astype(F32)
    x1 = x_ref[...] + _dot(merged.astype(BF16), wo_ref[...])
    x1_ref[...] = x1
    h2 = _rms(x1, gffn_ref[...])
    hp_ref[...] = _pack_rows(h2)
    logits_t = lax.dot_general(wrt_ref[...], h2, (((1,), (1,)), ((), ())),
                               preferred_element_type=F32, precision=lax.Precision.HIGHEST)
    scores = jax.nn.sigmoid(logits_t)
    chosen, comb_t = _route(scores + br_ref[...], scores)
    comb_ref[...] = comb_t
    a = lax.broadcasted_iota(I32, (tm, tm), 0)
    b = lax.broadcasted_iota(I32, (tm, tm), 1)
    before = (a < b).astype(BF16)
    chosen_f = chosen.astype(F32)
    prefix = _dot(chosen_f.astype(BF16), before)
    run = run_ref[:, 0:1]
    rank_ref[...] = jnp.where(chosen, prefix + run, -1.0)
    run_ref[...] += jnp.sum(chosen_f, axis=1, keepdims=True)
    cnt_ref[...] = run_ref[...]


def _post_attn(x, ma, gb, o, consts, tm):
    t, d = x.shape
    row_spec = lambda w: pl.BlockSpec((tm, w), lambda i: (i, 0))
    col_spec = pl.BlockSpec((N_EXPERTS, tm), lambda i: (0, i))
    return pl.pallas_call(
        _post_attn_kernel,
        grid=(t // tm,),
        in_specs=[row_spec(d)] * 4 + [_const_spec(c.shape) for c in consts],
        out_specs=[row_spec(d), row_spec(HALF), col_spec, col_spec, _const_spec((N_EXPERTS, LANES))],
        out_shape=[jax.ShapeDtypeStruct((t, d), F32), jax.ShapeDtypeStruct((t, HALF), U32),
                   jax.ShapeDtypeStruct((N_EXPERTS, t), F32), jax.ShapeDtypeStruct((N_EXPERTS, t), F32),
                   jax.ShapeDtypeStruct((N_EXPERTS, LANES), F32)],
        scratch_shapes=[pltpu.VMEM((N_EXPERTS, LANES), F32)],
        compiler_params=pltpu.CompilerParams(dimension_semantics=("arbitrary",), vmem_limit_bytes=VMEM_LIMIT),
        name="post_attn_router",
    )(x, ma, gb, o, *consts)


def _route_lists_kernel(comb_ref, rank_ref, off_ref, pos_ref, w_ref):
    rank = rank_ref[...]
    chosen = rank >= 0.0
    e = rank.shape[0]
    lower = (lax.broadcasted_iota(I32, (e, e), 1) < lax.broadcasted_iota(I32, (e, e), 0)).astype(BF16)
    slot = _dot(lower, chosen.astype(BF16))
    pos_full = rank + off_ref[...]
    comb = comb_ref[...]
    pos_rows, w_rows = [], []
    for k in range(TOP_K):
        pick = chosen & (slot == float(k))
        pos_rows.append(jnp.sum(jnp.where(pick, pos_full, 0.0), axis=0, keepdims=True))
        w_rows.append(jnp.sum(jnp.where(pick, comb, 0.0), axis=0, keepdims=True))
    pos_ref[...] = jnp.concatenate(pos_rows, axis=0).astype(I32)
    w_ref[...] = jnp.concatenate(w_rows, axis=0).T


def _route_lists(comb_t, rank_t, offsets, tm):
    e, t = comb_t.shape
    col_spec = pl.BlockSpec((e, tm), lambda i: (0, i))
    return pl.pallas_call(
        _route_lists_kernel,
        grid=(t // tm,),
        in_specs=[col_spec, col_spec, _const_spec((e, 1))],
        out_specs=[pl.BlockSpec((TOP_K, tm), lambda i: (0, i)), pl.BlockSpec((tm, TOP_K), lambda i: (i, 0))],
        out_shape=[jax.ShapeDtypeStruct((TOP_K, t), I32), jax.ShapeDtypeStruct((t, TOP_K), F32)],
        compiler_params=pltpu.CompilerParams(dimension_semantics=("arbitrary",)),
        name="route_lists",
    )(comb_t, rank_t, offsets)


def _sc_mesh():
    return plsc.VectorSubcoreMesh(core_axis_name="c", subcore_axis_name="s",
                                  num_cores=SC_CORES, num_subcores=SC_SUBCORES)


def _sc_worker():
    return lax.axis_index("s") * SC_CORES + lax.axis_index("c")


def _sc_scatter_rows(src, pos_win, n_rows):
    t, width = src.shape
    n_win = t // SC_WINDOW
    per_w = n_win // (SC_CORES * SC_SUBCORES)

    @functools.partial(
        pl.kernel, mesh=_sc_mesh(), out_type=jax.ShapeDtypeStruct((n_rows, width), src.dtype),
        scratch_types=[pltpu.VMEM((TOP_K, SC_WINDOW), I32), pltpu.VMEM((SC_WINDOW, width), src.dtype),
                       pltpu.SemaphoreType.DMA],
        name="sc_scatter_rows")
    def run(src_hbm, pos_hbm, out_hbm, idx_v, rows_v, sem):
        base = _sc_worker() * per_w

        @pl.loop(0, per_w)
        def _(j):
            w = base + j
            pltpu.sync_copy(pos_hbm.at[w], idx_v)
            pltpu.sync_copy(src_hbm.at[pl.ds(w * SC_WINDOW, SC_WINDOW)], rows_v)
            copies = [pltpu.async_copy(rows_v, out_hbm.at[idx_v.at[k]], sem) for k in range(TOP_K)]
            for c in copies:
                c.wait()

    return run(src, pos_win)


def _sc_gather_rows(table, pos_win):
    n_win = pos_win.shape[0]
    width = table.shape[1]
    per_w = n_win // (SC_CORES * SC_SUBCORES)

    @functools.partial(
        pl.kernel, mesh=_sc_mesh(),
        out_type=jax.ShapeDtypeStruct((TOP_K, n_win * SC_WINDOW, width), table.dtype),
        scratch_types=[pltpu.VMEM((TOP_K, SC_WINDOW), I32), pltpu.VMEM((2, SC_WINDOW, width), table.dtype),
                       pltpu.SemaphoreType.DMA, pltpu.SemaphoreType.DMA],
        name="sc_gather_rows")
    def run(table_hbm, pos_hbm, out_hbm, idx_v, rows_v, gsem, wsem):
        base = _sc_worker() * per_w

        @pl.loop(0, per_w)
        def _(j):
            w = base + j
            pltpu.sync_copy(pos_hbm.at[w], idx_v)
            for k in range(TOP_K):
                buf = rows_v.at[k % 2]
                pltpu.async_copy(table_hbm.at[idx_v.at[k]], buf, gsem).wait()
                pltpu.async_copy(buf, out_hbm.at[k, pl.ds(w * SC_WINDOW, SC_WINDOW)], wsem).wait()

    return run(table, pos_win)


def _grouped_ffn_kernel(te_ref, xs_ref, *refs):
    w_refs, ys_ref = refs[:-1], refs[-1]
    acts = []
    for j in range(TILES_PER_STEP):
        wg_ref, wu_ref = w_refs[3 * j:3 * j + 2]
        lo, hi = _unpack_rows(xs_ref[j])
        lo, hi = lo.astype(BF16), hi.astype(BF16)
        gate = _dot(lo, wg_ref[0, :HALF]) + _dot(hi, wg_ref[0, HALF:])
        up = _dot(lo, wu_ref[0, :HALF]) + _dot(hi, wu_ref[0, HALF:])
        acts.append((jax.nn.silu(gate) * up).astype(BF16))
    for j in range(TILES_PER_STEP):
        ys_ref[j] = _pack_rows(_dot(acts[j], w_refs[3 * j + 2][0]))


def _grouped_ffn(tile_expert, xs, wg, wu, wd):
    n_rows, half = xs.shape
    n_steps = n_rows // (TILES_PER_STEP * ROW_TILE)
    row_spec = pl.BlockSpec((TILES_PER_STEP, ROW_TILE, half), lambda i, te: (0, i, 0))

    def exp_map(j):
        return lambda i, te: (te[j * n_steps + i], 0, 0)

    w_specs = [pl.BlockSpec((1,) + w.shape[1:], exp_map(j)) for j in range(TILES_PER_STEP) for w in (wg, wu, wd)]
    ys = pl.pallas_call(
        _grouped_ffn_kernel,
        grid_spec=pltpu.PrefetchScalarGridSpec(
            num_scalar_prefetch=1,
            grid=(n_steps,),
            in_specs=[row_spec] + w_specs,
            out_specs=row_spec),
        out_shape=jax.ShapeDtypeStruct((TILES_PER_STEP, n_steps * ROW_TILE, half), U32),
        compiler_params=pltpu.CompilerParams(dimension_semantics=("arbitrary",), vmem_limit_bytes=VMEM_LIMIT),
        name="grouped_ffn",
    )(tile_expert, xs.reshape(TILES_PER_STEP, n_steps * ROW_TILE, half), *([wg, wu, wd] * TILES_PER_STEP))
    return ys.reshape(n_rows, half)


def _combine_kernel(yt_ref, w_ref, x1_ref, p_ref, gffn_ref, wsg_ref, wsu_ref, wsd_ref,
                    gpi_ref, wpg_ref, wpp_ref, gpo_ref, out_ref):
    x1 = x1_ref[...]
    w = w_ref[...]
    acc_lo = jnp.zeros((x1.shape[0], HALF), F32)
    acc_hi = jnp.zeros((x1.shape[0], HALF), F32)
    for k in range(TOP_K):
        lo, hi = _unpack_rows(yt_ref[k])
        wk = w[:, k:k + 1]
        acc_lo += wk * lo
        acc_hi += wk * hi
    h = _rms(x1, gffn_ref[...]).astype(BF16)
    shared = _dot((jax.nn.silu(_dot(h, wsg_ref[...])) * _dot(h, wsu_ref[...])).astype(BF16), wsd_ref[...])
    x2 = x1 + shared + jnp.concatenate([acc_lo, acc_hi], axis=-1)
    gate = jax.nn.sigmoid(_dot(_rms(x2, gpi_ref[...]).astype(BF16), wpg_ref[...]))
    proj = _rms(_dot(p_ref[...].astype(BF16), wpp_ref[...]), gpo_ref[...])
    out_ref[...] = x2 + gate * proj


def _combine(yt, w_tok, x1, p, consts, tm):
    t, d = x1.shape
    row_spec = lambda w: pl.BlockSpec((tm, w), lambda i: (i, 0))
    return pl.pallas_call(
        _combine_kernel,
        grid=(t // tm,),
        in_specs=[pl.BlockSpec((TOP_K, tm, HALF), lambda i: (0, i, 0)), row_spec(TOP_K), row_spec(d),
                  row_spec(PLE_DIM)] + [_const_spec(c.shape) for c in consts],
        out_specs=row_spec(d),
        out_shape=jax.ShapeDtypeStruct((t, d), F32),
        compiler_params=pltpu.CompilerParams(dimension_semantics=("arbitrary",), vmem_limit_bytes=VMEM_LIMIT),
        name="combine_ple",
    )(yt, w_tok, x1, p, *consts)


def _rotate_half(a):
    half = QK_ROPE // 2
    return jnp.concatenate([-a[..., half:], a[..., :half]], axis=-1)


def _with_rotate_half(w):
    return jnp.concatenate([w, _rotate_half(w[..., -QK_ROPE:])], axis=-1)


def _rope_gain_tables(g, scale):
    zeros = jnp.zeros((QK_ROPE,), F32)
    half = QK_ROPE // 2
    g_swapped = jnp.concatenate([g[half:], g[:half]])
    return (jnp.concatenate([g, zeros]) * scale).reshape(1, LANES), \
        (jnp.concatenate([zeros, g_swapped]) * scale).reshape(1, LANES)


def _layer(x, p, pos, g_mix, w_in, sgu_ln_g, sgu_ln_b, sgu_w, sgu_b, mla_g_qa, mla_w_qb, mla_g_kva, mla_w_kvb,
           qk_g_q_nope, qk_g_k_nope, qk_g_q_rope, qk_g_k_rope, w_o, g_ffn, w_router, b_router,
           w_exp_gate, w_exp_up, w_exp_down, w_sh_gate, w_sh_up, w_sh_down,
           g_ple_in, w_ple_gate, w_ple_proj, g_ple_out):
    b, s, d = x.shape
    t = b * s
    row = lambda a: a.reshape(1, -1)
    sizes = [d, d, Q_LORA, KV_LORA, QK_ROPE, d, d]
    offs = [0]
    for sz in sizes:
        offs.append(offs[-1] + sz)
    w_u, w_v, w_q, w_kv, w_r, w_ga, w_gb = [w_in[:, offs[i]:offs[i + 1]] for i in range(7)]
    w_r = _with_rotate_half(w_r)
    wqb = _with_rotate_half(mla_w_qb.reshape(Q_LORA, HEADS, QK_DIM)).reshape(Q_LORA, HEADS * QK_PAD)
    wkvb = mla_w_kvb.reshape(KV_LORA, HEADS, QK_NOPE + V_DIM)
    wkb = wkvb[:, :, :QK_NOPE].reshape(KV_LORA, HEADS * QK_NOPE)
    wvb = wkvb[:, :, QK_NOPE:].reshape(KV_LORA, HEADS * V_DIM)
    sgu_bias = jnp.repeat(sgu_b.T, d // SGU_GROUPS, axis=1)
    q_scale = QK_DIM ** -0.5

    consts1 = [row(g_mix), w_u, w_v, w_q, w_kv, w_r, w_ga, w_gb, row(sgu_ln_g), row(sgu_ln_b), sgu_w, sgu_bias,
               row(mla_g_qa), wqb, row(mla_g_kva), wkb, wvb, row(qk_g_q_nope) * q_scale, row(qk_g_k_nope),
               *_rope_gain_tables(qk_g_q_rope, q_scale), *_rope_gain_tables(qk_g_k_rope, 1.0)]
    ma, gb, q, k, v = _mixer_prep(x, pos.reshape(b, s, 1), consts1, tm=512)
    o = _attention(q, k, v, tq=512)

    consts3 = [w_o, row(g_ffn), w_router.T, b_router.reshape(-1, 1)]
    x1, h_packed, comb_t, rank_t, counts = _post_attn(
        x.reshape(t, d), ma.reshape(t, d), gb.reshape(t, d), o.reshape(t, d), consts3, tm=512)

    counts = counts[:, 0].astype(I32)
    tiles_per_expert = (counts + ROW_TILE - 1) // ROW_TILE
    tile_end = jnp.cumsum(tiles_per_expert)
    offsets = ((tile_end - tiles_per_expert) * ROW_TILE).astype(F32).reshape(N_EXPERTS, 1)
    max_tiles = (t * TOP_K) // ROW_TILE + N_EXPERTS
    tile_ids = jnp.arange(max_tiles, dtype=I32)
    tile_expert = jnp.minimum(jnp.sum((tile_end[None, :] <= tile_ids[:, None]).astype(I32), axis=1),
                              N_EXPERTS - 1)

    pos_t, w_tok = _route_lists(comb_t, rank_t, offsets, tm=512)
    pos_win = pos_t.reshape(TOP_K, t // SC_WINDOW, SC_WINDOW).transpose(1, 0, 2)

    xs = _sc_scatter_rows(h_packed, pos_win, max_tiles * ROW_TILE)
    ys = _grouped_ffn(tile_expert, xs, w_exp_gate, w_exp_up, w_exp_down)
    yt = _sc_gather_rows(ys, pos_win)

    consts6 = [row(g_ffn), w_sh_gate, w_sh_up, w_sh_down, row(g_ple_in), w_ple_gate, w_ple_proj, row(g_ple_out)]
    out = _combine(yt, w_tok, x1, p.reshape(t, PLE_DIM), consts6, tm=256)
    return out.reshape(b, s, d)


def kernel(x, p, positions, g_mix, w_in, sgu_ln_g, sgu_ln_b, sgu_w, sgu_b, mla_g_qa, mla_w_qb, mla_g_kva, mla_w_kvb, qk_g_q_nope, qk_g_k_nope, qk_g_q_rope, qk_g_k_rope, w_o, g_ffn, w_router, b_router, w_exp_gate, w_exp_up, w_exp_down, w_sh_gate, w_sh_up, w_sh_down, g_ple_in, w_ple_gate, w_ple_proj, g_ple_out):
    params = (g_mix, w_in, sgu_ln_g, sgu_ln_b, sgu_w, sgu_b, mla_g_qa, mla_w_qb, mla_g_kva, mla_w_kvb,
              qk_g_q_nope, qk_g_k_nope, qk_g_q_rope, qk_g_k_rope, w_o, g_ffn, w_router, b_router,
              w_exp_gate, w_exp_up, w_exp_down, w_sh_gate, w_sh_up, w_sh_down,
              g_ple_in, w_ple_gate, w_ple_proj, g_ple_out)
    for l in range(g_mix.shape[0]):
        x = _layer(x, p[l], positions, *[a[l] for a in params])
    return x
```

```python
import functools
import math

import jax
import jax.numpy as jnp
from jax import lax
from jax.experimental import pallas as pl
from jax.experimental.pallas import tpu as pltpu
from jax.experimental.pallas import tpu_sc as plsc

D_MODEL = 1024
PLE_DIM = 256
SGU_CHUNK = 128
SGU_GROUPS = 8
V_DIM = 128
HEADS = 8
QK_NOPE = 128
QK_ROPE = 64
QK_DIM = QK_NOPE + QK_ROPE
QK_PAD = 256
Q_LORA = 384
KV_LORA = 256
ROPE_THETA = 10000.0
N_EXPERTS = 64
N_GROUPS = 8
EXPERTS_PER_GROUP = 8
TOPK_GROUPS = 4
TOP_K = 8
EXPERT_FF = 256
ROUTED_SCALE = 2.5
NORM_EPS = 1e-6
LN_EPS = 1e-5

LANES = 128
MXU_COLS = 256
VMEM_LIMIT = 56 * 1024 * 1024
SC_CORES = 2
SC_SUBCORES = 16
SC_WINDOW = 64
ROW_TILE = 256
TILES_PER_STEP = 4
MOE_CHUNKS = 2
HALF = D_MODEL // 2

F32 = jnp.float32
BF16 = jnp.bfloat16
U32 = jnp.uint32
I32 = jnp.int32


def _dot(a, b):
    return lax.dot_general(a, b, (((1,), (0,)), ((), ())), preferred_element_type=F32)


def _rms(xf, g, width=None):
    width = xf.shape[-1] if width is None else width
    ms = jnp.sum(xf * xf, axis=-1, keepdims=True) * (1.0 / width)
    return xf * lax.rsqrt(ms + NORM_EPS) * g


def _pack_rows(y):
    lo = pltpu.bitcast(y[:, :HALF].astype(BF16).astype(F32), U32) >> 16
    hi = pltpu.bitcast(y[:, HALF:].astype(BF16).astype(F32), U32) & jnp.uint32(0xFFFF0000)
    return lo | hi


def _unpack_rows(w):
    lo = pltpu.bitcast(w << 16, F32)
    hi = pltpu.bitcast(w & jnp.uint32(0xFFFF0000), F32)
    return lo, hi


def _rope_tables(pos_f):
    lane = lax.broadcasted_iota(I32, (1, LANES), 1)
    freq = (lane % (QK_ROPE // 2)).astype(F32)
    inv_freq = jnp.exp(freq * (-math.log(ROPE_THETA) * 2.0 / QK_ROPE))
    ang = pos_f * inv_freq
    return jnp.cos(ang), jnp.sin(ang)


def _norm_rope(piece, cos_g, sin_g):
    lane = lax.broadcasted_iota(I32, (1, LANES), 1)
    ssq = jnp.sum(jnp.where(lane < QK_ROPE, piece * piece, 0.0), axis=-1, keepdims=True)
    y = piece * lax.rsqrt(ssq * (1.0 / QK_ROPE) + NORM_EPS)
    return y * cos_g + pltpu.roll(y * sin_g, QK_ROPE, axis=1)


def _const_spec(shape):
    return pl.BlockSpec(shape, lambda *_: (0,) * len(shape), pipeline_mode=pl.Buffered(1))


def _mixer_prep_kernel(x_ref, pos_ref, g_mix_ref, wu_ref, wv_ref, wq_ref, wkv_ref, wr_ref, wga_ref, wgb_ref,
                       lng_ref, lnb_ref, sw_ref, sb_ref, gqa_ref, wqb_ref, gkva_ref, wkb_ref, wvb_ref,
                       gqn_ref, gkn_ref, gqc_ref, gqs_ref, gkc_ref, gks_ref,
                       ma_ref, gb_ref, q_ref, k_ref, v_ref):
    tm = x_ref.shape[1]
    xn = _rms(x_ref[0], g_mix_ref[...]).astype(BF16)

    cos_t, sin_t = _rope_tables(pos_ref[0].astype(F32))
    qn = _rms(_dot(xn, wq_ref[...]), gqa_ref[...]).astype(BF16)
    kvn = _rms(_dot(xn, wkv_ref[...]), gkva_ref[...]).astype(BF16)
    kpe = _norm_rope(_dot(xn, wr_ref[...]), cos_t * gkc_ref[...], sin_t * gks_ref[...]).astype(BF16)
    q_cos, q_sin = cos_t * gqc_ref[...], sin_t * gqs_ref[...]

    gv = jax.nn.gelu(_dot(xn, wv_ref[...]))
    mu = jnp.mean(gv, axis=-1, keepdims=True)
    vc = gv - mu
    var = jnp.mean(vc * vc, axis=-1, keepdims=True)
    vn = (vc * lax.rsqrt(var + LN_EPS) * lng_ref[...] + lnb_ref[...]).astype(BF16)
    row = lax.broadcasted_iota(I32, (SGU_CHUNK, SGU_CHUNK), 0)
    col = lax.broadcasted_iota(I32, (SGU_CHUNK, SGU_CHUNK), 1)
    causal = col <= row
    n_chunks = tm // SGU_CHUNK
    for pair in range(SGU_GROUPS // 2):
        ps = slice(pair * MXU_COLS, (pair + 1) * MXU_COLS)
        k2 = _dot(kvn, wkb_ref[:, ps])
        v2 = _dot(kvn, wvb_ref[:, ps]).astype(BF16)
        gu2 = jax.nn.gelu(_dot(xn, wu_ref[:, ps]))
        ga2 = jax.nn.sigmoid(_dot(xn, wga_ref[:, ps]))
        gb_ref[0, :, ps] = jax.nn.sigmoid(_dot(xn, wgb_ref[:, ps])).astype(BF16)
        for half in range(2):
            g = 2 * pair + half
            hs = slice(half * LANES, (half + 1) * LANES)
            cs = slice(g * SGU_CHUNK, (g + 1) * SGU_CHUNK)
            qh = _dot(qn, wqb_ref[:, g * QK_PAD:(g + 1) * QK_PAD])
            q_ref[0, g, :, :QK_NOPE] = _rms(qh[:, :QK_NOPE], gqn_ref[...]).astype(BF16)
            q_ref[0, g, :, QK_NOPE:] = _norm_rope(qh[:, QK_NOPE:], q_cos, q_sin).astype(BF16)
            k_ref[0, g, :, :QK_NOPE] = _rms(k2[:, hs], gkn_ref[...]).astype(BF16)
            k_ref[0, g, :, QK_NOPE:] = kpe
            v_ref[0, g] = v2[:, hs]

            wg = jnp.where(causal, sw_ref[g], 0.0).astype(BF16)
            vcat = jnp.concatenate([vn[c * SGU_CHUNK:(c + 1) * SGU_CHUNK, cs] for c in range(n_chunks)], axis=1)
            mixed = _dot(wg, vcat)
            for c in range(n_chunks):
                rs = slice(c * SGU_CHUNK, (c + 1) * SGU_CHUNK)
                m = mixed[:, c * SGU_CHUNK:(c + 1) * SGU_CHUNK] + sb_ref[:, cs]
                ma_ref[0, rs, cs] = (ga2[rs, hs] * gu2[rs, hs] * m).astype(BF16)


def _mixer_prep(x, pos, consts, tm):
    b, s, d = x.shape
    grid = (b, s // tm)
    row_spec = lambda w: pl.BlockSpec((1, tm, w), lambda i, j: (i, j, 0))
    head_spec = lambda w: pl.BlockSpec((1, HEADS, tm, w), lambda i, j: (i, 0, j, 0))
    return pl.pallas_call(
        _mixer_prep_kernel,
        grid=grid,
        in_specs=[row_spec(d), row_spec(1)] + [_const_spec(c.shape) for c in consts],
        out_specs=[row_spec(d), row_spec(d), head_spec(QK_PAD), head_spec(QK_PAD), head_spec(V_DIM)],
        out_shape=[jax.ShapeDtypeStruct((b, s, d), BF16), jax.ShapeDtypeStruct((b, s, d), BF16),
                   jax.ShapeDtypeStruct((b, HEADS, s, QK_PAD), BF16),
                   jax.ShapeDtypeStruct((b, HEADS, s, QK_PAD), BF16),
                   jax.ShapeDtypeStruct((b, HEADS, s, V_DIM), BF16)],
        compiler_params=pltpu.CompilerParams(dimension_semantics=("arbitrary", "arbitrary"),
                                             vmem_limit_bytes=VMEM_LIMIT),
        name="mixer_prep",
    )(x, pos, *consts)


def _attn_kernel(q_ref, k_ref, v_ref, o_ref, *, tq):
    s = q_ref.shape[2]
    row = lax.broadcasted_iota(I32, (tq, tq), 0)
    col = lax.broadcasted_iota(I32, (tq, tq), 1)
    diag_mask = col <= row

    def scores(q, start):
        kt = k_ref[0, 0, pl.ds(start, tq), :]
        return lax.dot_general(q, kt, (((1,), (1,)), ((), ())), preferred_element_type=F32)

    def update(carry, sc, start):
        m, l, acc = carry
        m_new = jnp.maximum(m, jnp.max(sc, axis=-1, keepdims=True))
        alpha = jnp.exp(m - m_new)
        p = jnp.exp(sc - m_new)
        l = alpha * l + jnp.sum(p, axis=-1, keepdims=True)
        vt = v_ref[0, 0, pl.ds(start, tq), :]
        acc = alpha * acc + _dot(p.astype(BF16), vt)
        return m_new, l, acc

    for qi in range(s // tq):
        q = q_ref[0, 0, qi * tq:(qi + 1) * tq, :]
        sc = jnp.where(diag_mask, scores(q, qi * tq), -jnp.inf)
        m0 = jnp.max(sc, axis=-1, keepdims=True)
        p0 = jnp.exp(sc - m0)
        carry = (m0, jnp.sum(p0, axis=-1, keepdims=True),
                 _dot(p0.astype(BF16), v_ref[0, 0, qi * tq:(qi + 1) * tq, :]))

        def body(kj, carry, q=q):
            start = pl.multiple_of(kj * tq, tq)
            return update(carry, scores(q, start), start)

        m, l, acc = lax.fori_loop(0, qi, body, carry)
        o_ref[0, qi * tq:(qi + 1) * tq, :] = (acc / l).astype(BF16)


def _attention(q, k, v, tq):
    b, h, s, _ = q.shape
    return pl.pallas_call(
        functools.partial(_attn_kernel, tq=tq),
        grid=(b, h),
        in_specs=[pl.BlockSpec((1, 1, s, QK_PAD), lambda i, j: (i, j, 0, 0)),
                  pl.BlockSpec((1, 1, s, QK_PAD), lambda i, j: (i, j, 0, 0)),
                  pl.BlockSpec((1, 1, s, V_DIM), lambda i, j: (i, j, 0, 0))],
        out_specs=pl.BlockSpec((1, s, V_DIM), lambda i, j: (i, 0, j)),
        out_shape=jax.ShapeDtypeStruct((b, s, h * V_DIM), BF16),
        compiler_params=pltpu.CompilerParams(dimension_semantics=("arbitrary", "arbitrary"),
                                             vmem_limit_bytes=VMEM_LIMIT),
        name="mla_attention",
    )(q, k, v)


def _route(sel, scores):
    e, tm = sel.shape
    sel3 = sel.reshape(N_GROUPS, EXPERTS_PER_GROUP, tm)
    sub = lax.broadcasted_iota(I32, sel3.shape, 1)
    m1 = jnp.max(sel3, axis=1, keepdims=True)
    first = jnp.min(jnp.where(sel3 == m1, sub, EXPERTS_PER_GROUP), axis=1, keepdims=True)
    m2 = jnp.max(jnp.where(sub == first, -jnp.inf, sel3), axis=1, keepdims=True)
    gscore = (m1 + m2).reshape(N_GROUPS, tm)
    gid = lax.broadcasted_iota(I32, (N_GROUPS, tm), 0)
    grank = jnp.zeros((N_GROUPS, tm), I32)
    for g in range(N_GROUPS):
        other = gscore[g:g + 1]
        grank += ((other > gscore) | ((other == gscore) & (g < gid))).astype(I32)
    gmask = grank < TOPK_GROUPS
    emask = jnp.broadcast_to(gmask[:, None, :], sel3.shape).reshape(e, tm)
    msel = jnp.where(emask, sel, -jnp.inf)
    eid = lax.broadcasted_iota(I32, (e, tm), 0)
    erank = jnp.zeros((e, tm), I32)
    for j in range(e):
        other = msel[j:j + 1]
        erank += ((other > msel) | ((other == msel) & (j < eid))).astype(I32)
    chosen = erank < TOP_K
    w = jnp.where(chosen, scores, 0.0)
    return chosen, w / jnp.sum(w, axis=0, keepdims=True) * ROUTED_SCALE


def _post_attn_kernel(x_ref, ma_ref, gb_ref, o_ref, wo_ref, gffn_ref, wrt_ref, br_ref,
                      x1_ref, hp_ref, comb_ref, rank_ref, cnt_ref, run_ref):
    tm = x_ref.shape[0]

    @pl.when(pl.program_id(0) == 0)
    def _():
        run_ref[...] = jnp.zeros_like(run_ref)

    merged = ma_ref[...].astype(F32) + gb_ref[...].astype(F32) * o_ref[...].astype(F32)
    x1 = x_ref[...] + _dot(merged.astype(BF16), wo_ref[...])
    x1_ref[...] = x1
    h2 = _rms(x1, gffn_ref[...])
    hp_ref[...] = _pack_rows(h2)
    logits_t = lax.dot_general(wrt_ref[...], h2, (((1,), (1,)), ((), ())),
                               preferred_element_type=F32, precision=lax.Precision.HIGHEST)
    scores = jax.nn.sigmoid(logits_t)
    chosen, comb_t = _route(scores + br_ref[...], scores)
    comb_ref[...] = comb_t
    a = lax.broadcasted_iota(I32, (tm, tm), 0)
    b = lax.broadcasted_iota(I32, (tm, tm), 1)
    before = (a < b).astype(BF16)
    chosen_f = chosen.astype(F32)
    prefix = _dot(chosen_f.astype(BF16), before)
    run = run_ref[:, 0:1]
    rank_ref[...] = jnp.where(chosen, prefix + run, -1.0)
    run_ref[...] += jnp.sum(chosen_f, axis=1, keepdims=True)
    cnt_ref[...] = run_ref[...]


def _post_attn(x, ma, gb, o, consts, tm, chunk):
    t, d = x.shape[0] // MOE_CHUNKS, x.shape[1]
    first = chunk * (t // tm)
    row_spec = lambda w: pl.BlockSpec((tm, w), lambda i: (i, 0))
    col_spec = pl.BlockSpec((N_EXPERTS, tm), lambda i: (0, i))
    return pl.pallas_call(
        _post_attn_kernel,
        grid=(t // tm,),
        in_specs=[pl.BlockSpec((tm, d), lambda i: (first + i, 0))] * 4 + [_const_spec(c.shape) for c in consts],
        out_specs=[row_spec(d), row_spec(HALF), col_spec, col_spec, _const_spec((N_EXPERTS, LANES))],
        out_shape=[jax.ShapeDtypeStruct((t, d), F32), jax.ShapeDtypeStruct((t, HALF), U32),
                   jax.ShapeDtypeStruct((N_EXPERTS, t), F32), jax.ShapeDtypeStruct((N_EXPERTS, t), F32),
                   jax.ShapeDtypeStruct((N_EXPERTS, LANES), F32)],
        scratch_shapes=[pltpu.VMEM((N_EXPERTS, LANES), F32)],
        compiler_params=pltpu.CompilerParams(dimension_semantics=("arbitrary",), vmem_limit_bytes=VMEM_LIMIT),
        name="post_attn_router",
    )(x, ma, gb, o, *consts)


def _route_lists_kernel(comb_ref, rank_ref, off_ref, pos_ref, w_ref):
    rank = rank_ref[...]
    chosen = rank >= 0.0
    e = rank.shape[0]
    lower = (lax.broadcasted_iota(I32, (e, e), 1) < lax.broadcasted_iota(I32, (e, e), 0)).astype(BF16)
    slot = _dot(lower, chosen.astype(BF16))
    pos_full = rank + off_ref[...]
    comb = comb_ref[...]
    pos_rows, w_rows = [], []
    for k in range(TOP_K):
        pick = chosen & (slot == float(k))
        pos_rows.append(jnp.sum(jnp.where(pick, pos_full, 0.0), axis=0, keepdims=True))
        w_rows.append(jnp.sum(jnp.where(pick, comb, 0.0), axis=0, keepdims=True))
    pos_ref[...] = jnp.concatenate(pos_rows, axis=0).astype(I32)
    w_ref[...] = jnp.concatenate(w_rows, axis=0).T


def _route_lists(comb_t, rank_t, offsets, tm):
    e, t = comb_t.shape
    col_spec = pl.BlockSpec((e, tm), lambda i: (0, i))
    return pl.pallas_call(
        _route_lists_kernel,
        grid=(t // tm,),
        in_specs=[col_spec, col_spec, _const_spec((e, 1))],
        out_specs=[pl.BlockSpec((TOP_K, tm), lambda i: (0, i)), pl.BlockSpec((tm, TOP_K), lambda i: (i, 0))],
        out_shape=[jax.ShapeDtypeStruct((TOP_K, t), I32), jax.ShapeDtypeStruct((t, TOP_K), F32)],
        compiler_params=pltpu.CompilerParams(dimension_semantics=("arbitrary",)),
        name="route_lists",
    )(comb_t, rank_t, offsets)


def _sc_mesh():
    return plsc.VectorSubcoreMesh(core_axis_name="c", subcore_axis_name="s",
                                  num_cores=SC_CORES, num_subcores=SC_SUBCORES)


def _sc_worker():
    return lax.axis_index("s") * SC_CORES + lax.axis_index("c")


def _sc_scatter_rows(src, pos_win, n_rows):
    t, width = src.shape
    n_win = t // SC_WINDOW
    per_w = n_win // (SC_CORES * SC_SUBCORES)

    @functools.partial(
        pl.kernel, mesh=_sc_mesh(), out_type=jax.ShapeDtypeStruct((n_rows, width), src.dtype),
        scratch_types=[pltpu.VMEM((TOP_K, SC_WINDOW), I32), pltpu.VMEM((SC_WINDOW, width), src.dtype),
                       pltpu.SemaphoreType.DMA],
        name="sc_scatter_rows")
    def run(src_hbm, pos_hbm, out_hbm, idx_v, rows_v, sem):
        base = _sc_worker() * per_w

        @pl.loop(0, per_w)
        def _(j):
            w = base + j
            pltpu.sync_copy(pos_hbm.at[w], idx_v)
            pltpu.sync_copy(src_hbm.at[pl.ds(w * SC_WINDOW, SC_WINDOW)], rows_v)
            copies = [pltpu.async_copy(rows_v, out_hbm.at[idx_v.at[k]], sem) for k in range(TOP_K)]
            for c in copies:
                c.wait()

    return run(src, pos_win)


def _sc_gather_rows(table, pos_win):
    n_win = pos_win.shape[0]
    width = table.shape[1]
    per_w = n_win // (SC_CORES * SC_SUBCORES)

    @functools.partial(
        pl.kernel, mesh=_sc_mesh(),
        out_type=jax.ShapeDtypeStruct((TOP_K, n_win * SC_WINDOW, width), table.dtype),
        scratch_types=[pltpu.VMEM((TOP_K, SC_WINDOW), I32), pltpu.VMEM((2, SC_WINDOW, width), table.dtype),
                       pltpu.SemaphoreType.DMA, pltpu.SemaphoreType.DMA],
        name="sc_gather_rows")
    def run(table_hbm, pos_hbm, out_hbm, idx_v, rows_v, gsem, wsem):
        base = _sc_worker() * per_w

        @pl.loop(0, per_w)
        def _(j):
            w = base + j
            pltpu.sync_copy(pos_hbm.at[w], idx_v)
            for k in range(TOP_K):
                buf = rows_v.at[k % 2]
                pltpu.async_copy(table_hbm.at[idx_v.at[k]], buf, gsem).wait()
                pltpu.async_copy(buf, out_hbm.at[k, pl.ds(w * SC_WINDOW, SC_WINDOW)], wsem).wait()

    return run(table, pos_win)


def _grouped_ffn_kernel(te_ref, xs_ref, *refs):
    w_refs, ys_ref = refs[:-1], refs[-1]
    acts = []
    for j in range(TILES_PER_STEP):
        wg_ref, wu_ref = w_refs[3 * j:3 * j + 2]
        lo, hi = _unpack_rows(xs_ref[j])
        lo, hi = lo.astype(BF16), hi.astype(BF16)
        gate = _dot(lo, wg_ref[0, :HALF]) + _dot(hi, wg_ref[0, HALF:])
        up = _dot(lo, wu_ref[0, :HALF]) + _dot(hi, wu_ref[0, HALF:])
        acts.append((jax.nn.silu(gate) * up).astype(BF16))
    for j in range(TILES_PER_STEP):
        ys_ref[j] = _pack_rows(_dot(acts[j], w_refs[3 * j + 2][0]))


def _grouped_ffn(tile_expert, xs, wg, wu, wd):
    n_rows, half = xs.shape
    n_steps = n_rows // (TILES_PER_STEP * ROW_TILE)
    row_spec = pl.BlockSpec((TILES_PER_STEP, ROW_TILE, half), lambda i, te: (0, i, 0))

    def exp_map(j):
        return lambda i, te: (te[j * n_steps + i], 0, 0)

    w_specs = [pl.BlockSpec((1,) + w.shape[1:], exp_map(j)) for j in range(TILES_PER_STEP) for w in (wg, wu, wd)]
    ys = pl.pallas_call(
        _grouped_ffn_kernel,
        grid_spec=pltpu.PrefetchScalarGridSpec(
            num_scalar_prefetch=1,
            grid=(n_steps,),
            in_specs=[row_spec] + w_specs,
            out_specs=row_spec),
        out_shape=jax.ShapeDtypeStruct((TILES_PER_STEP, n_steps * ROW_TILE, half), U32),
        compiler_params=pltpu.CompilerParams(dimension_semantics=("arbitrary",), vmem_limit_bytes=VMEM_LIMIT),
        name="grouped_ffn",
    )(tile_expert, xs.reshape(TILES_PER_STEP, n_steps * ROW_TILE, half), *([wg, wu, wd] * TILES_PER_STEP))
    return ys.reshape(n_rows, half)


def _combine_kernel(yt_ref, w_ref, x1_ref, p_ref, gffn_ref, wsg_ref, wsu_ref, wsd_ref,
                    gpi_ref, wpg_ref, wpp_ref, gpo_ref, *rest):
    out_ref = rest[-1]
    x1 = x1_ref[...]
    w = w_ref[...]
    acc_lo = jnp.zeros((x1.shape[0], HALF), F32)
    acc_hi = jnp.zeros((x1.shape[0], HALF), F32)
    for k in range(TOP_K):
        lo, hi = _unpack_rows(yt_ref[k])
        wk = w[:, k:k + 1]
        acc_lo += wk * lo
        acc_hi += wk * hi
    h = _rms(x1, gffn_ref[...]).astype(BF16)
    shared = _dot((jax.nn.silu(_dot(h, wsg_ref[...])) * _dot(h, wsu_ref[...])).astype(BF16), wsd_ref[...])
    x2 = x1 + shared + jnp.concatenate([acc_lo, acc_hi], axis=-1)
    gate = jax.nn.sigmoid(_dot(_rms(x2, gpi_ref[...]).astype(BF16), wpg_ref[...]))
    proj = _rms(_dot(p_ref[...].astype(BF16), wpp_ref[...]), gpo_ref[...])
    out_ref[...] = x2 + gate * proj


def _combine(yt, w_tok, x1, p, consts, tm, chunk, out_so_far):
    t, d = x1.shape
    first = chunk * (t // tm)
    row_spec = lambda w: pl.BlockSpec((tm, w), lambda i: (i, 0))
    full_spec = lambda w: pl.BlockSpec((tm, w), lambda i: (first + i, 0))
    carried = [] if out_so_far is None else [out_so_far]
    return pl.pallas_call(
        _combine_kernel,
        grid=(t // tm,),
        in_specs=[pl.BlockSpec((TOP_K, tm, HALF), lambda i: (0, i, 0)), row_spec(TOP_K), row_spec(d),
                  full_spec(PLE_DIM)] + [_const_spec(c.shape) for c in consts]
                 + [pl.BlockSpec(memory_space=pl.ANY)] * len(carried),
        out_specs=full_spec(d),
        out_shape=jax.ShapeDtypeStruct((t * MOE_CHUNKS, d), F32),
        input_output_aliases={4 + len(consts): 0} if carried else {},
        compiler_params=pltpu.CompilerParams(dimension_semantics=("arbitrary",), vmem_limit_bytes=VMEM_LIMIT),
        name="combine_ple",
    )(yt, w_tok, x1, p, *consts, *carried)


def _rotate_half(a):
    half = QK_ROPE // 2
    return jnp.concatenate([-a[..., half:], a[..., :half]], axis=-1)


def _with_rotate_half(w):
    return jnp.concatenate([w, _rotate_half(w[..., -QK_ROPE:])], axis=-1)


def _rope_gain_tables(g, scale):
    zeros = jnp.zeros((QK_ROPE,), F32)
    half = QK_ROPE // 2
    g_swapped = jnp.concatenate([g[half:], g[:half]])
    return (jnp.concatenate([g, zeros]) * scale).reshape(1, LANES), \
        (jnp.concatenate([zeros, g_swapped]) * scale).reshape(1, LANES)


def _layer(x, p, pos, g_mix, w_in, sgu_ln_g, sgu_ln_b, sgu_w, sgu_b, mla_g_qa, mla_w_qb, mla_g_kva, mla_w_kvb,
           qk_g_q_nope, qk_g_k_nope, qk_g_q_rope, qk_g_k_rope, w_o, g_ffn, w_router, b_router,
           w_exp_gate, w_exp_up, w_exp_down, w_sh_gate, w_sh_up, w_sh_down,
           g_ple_in, w_ple_gate, w_ple_proj, g_ple_out):
    b, s, d = x.shape
    t = b * s
    row = lambda a: a.reshape(1, -1)
    sizes = [d, d, Q_LORA, KV_LORA, QK_ROPE, d, d]
    offs = [0]
    for sz in sizes:
        offs.append(offs[-1] + sz)
    w_u, w_v, w_q, w_kv, w_r, w_ga, w_gb = [w_in[:, offs[i]:offs[i + 1]] for i in range(7)]
    w_r = _with_rotate_half(w_r)
    wqb = _with_rotate_half(mla_w_qb.reshape(Q_LORA, HEADS, QK_DIM)).reshape(Q_LORA, HEADS * QK_PAD)
    wkvb = mla_w_kvb.reshape(KV_LORA, HEADS, QK_NOPE + V_DIM)
    wkb = wkvb[:, :, :QK_NOPE].reshape(KV_LORA, HEADS * QK_NOPE)
    wvb = wkvb[:, :, QK_NOPE:].reshape(KV_LORA, HEADS * V_DIM)
    sgu_bias = jnp.repeat(sgu_b.T, d // SGU_GROUPS, axis=1)
    q_scale = QK_DIM ** -0.5

    consts1 = [row(g_mix), w_u, w_v, w_q, w_kv, w_r, w_ga, w_gb, row(sgu_ln_g), row(sgu_ln_b), sgu_w, sgu_bias,
               row(mla_g_qa), wqb, row(mla_g_kva), wkb, wvb, row(qk_g_q_nope) * q_scale, row(qk_g_k_nope),
               *_rope_gain_tables(qk_g_q_rope, q_scale), *_rope_gain_tables(qk_g_k_rope, 1.0)]
    ma, gb, q, k, v = _mixer_prep(x, pos.reshape(b, s, 1), consts1, tm=512)
    o = _attention(q, k, v, tq=512)

    consts3 = [w_o, row(g_ffn), w_router.T, b_router.reshape(-1, 1)]
    consts6 = [row(g_ffn), w_sh_gate, w_sh_up, w_sh_down, row(g_ple_in), w_ple_gate, w_ple_proj, row(g_ple_out)]
    x2d, ma2d, gb2d, o2d, p2d = (a.reshape(t, -1) for a in (x, ma, gb, o, p))
    tc = t // MOE_CHUNKS
    max_tiles = (tc * TOP_K) // ROW_TILE + N_EXPERTS
    tile_ids = jnp.arange(max_tiles, dtype=I32)
    out = None
    for chunk in range(MOE_CHUNKS):
        x1, h_packed, comb_t, rank_t, counts = _post_attn(x2d, ma2d, gb2d, o2d, consts3, tm=512, chunk=chunk)

        counts = counts[:, 0].astype(I32)
        tiles_per_expert = (counts + ROW_TILE - 1) // ROW_TILE
        tile_end = jnp.cumsum(tiles_per_expert)
        offsets = ((tile_end - tiles_per_expert) * ROW_TILE).astype(F32).reshape(N_EXPERTS, 1)
        tile_expert = jnp.minimum(jnp.sum((tile_end[None, :] <= tile_ids[:, None]).astype(I32), axis=1),
                                  N_EXPERTS - 1)

        pos_t, w_tok = _route_lists(comb_t, rank_t, offsets, tm=512)
        pos_win = pos_t.reshape(TOP_K, tc // SC_WINDOW, SC_WINDOW).transpose(1, 0, 2)

        xs = _sc_scatter_rows(h_packed, pos_win, max_tiles * ROW_TILE)
        ys = _grouped_ffn(tile_expert, xs, w_exp_gate, w_exp_up, w_exp_down)
        yt = _sc_gather_rows(ys, pos_win)
        out = _combine(yt, w_tok, x1, p2d, consts6, tm=256, chunk=chunk, out_so_far=out)
    return out.reshape(b, s, d)


def kernel(x, p, positions, g_mix, w_in, sgu_ln_g, sgu_ln_b, sgu_w, sgu_b, mla_g_qa, mla_w_qb, mla_g_kva, mla_w_kvb, qk_g_q_nope, qk_g_k_nope, qk_g_q_rope, qk_g_k_rope, w_o, g_ffn, w_router, b_router, w_exp_gate, w_exp_up, w_exp_down, w_sh_gate, w_sh_up, w_sh_down, g_ple_in, w_ple_gate, w_ple_proj, g_ple_out):
    params = (g_mix, w_in, sgu_ln_g, sgu_ln_b, sgu_w, sgu_b, mla_g_qa, mla_w_qb, mla_g_kva, mla_w_kvb,
              qk_g_q_nope, qk_g_k_nope, qk_g_q_rope, qk_g_k_rope, w_o, g_ffn, w_router, b_router,
              w_exp_gate, w_exp_up, w_exp_down, w_sh_gate, w_sh_up, w_sh_down,
              g_ple_in, w_ple_gate, w_ple_proj, g_ple_out)
    for l in range(g_mix.shape[0]):
        x = _layer(x, p[l], positions, *[a[l] for a in params])
    return x
```

```python
import functools
import math

import jax
import jax.numpy as jnp
from jax import lax
from jax.experimental import pallas as pl
from jax.experimental.pallas import tpu as pltpu
from jax.experimental.pallas import tpu_sc as plsc

D_MODEL = 1024
PLE_DIM = 256
SGU_CHUNK = 128
SGU_GROUPS = 8
V_DIM = 128
HEADS = 8
QK_NOPE = 128
QK_ROPE = 64
QK_DIM = QK_NOPE + QK_ROPE
QK_PAD = 256
Q_LORA = 384
KV_LORA = 256
ROPE_THETA = 10000.0
N_EXPERTS = 64
N_GROUPS = 8
EXPERTS_PER_GROUP = 8
TOPK_GROUPS = 4
TOP_K = 8
EXPERT_FF = 256
ROUTED_SCALE = 2.5
NORM_EPS = 1e-6
LN_EPS = 1e-5

LANES = 128
MXU_COLS = 256
VMEM_LIMIT = 56 * 1024 * 1024
SC_CORES = 2
SC_SUBCORES = 16
SC_WINDOW = 64
ROW_TILE = 256
TILES_PER_STEP = 4
MOE_CHUNKS = 2
HALF = D_MODEL // 2

F32 = jnp.float32
BF16 = jnp.bfloat16
U32 = jnp.uint32
I32 = jnp.int32


def _dot(a, b):
    return lax.dot_general(a, b, (((1,), (0,)), ((), ())), preferred_element_type=F32)


def _rms(xf, g, width=None):
    width = xf.shape[-1] if width is None else width
    ms = jnp.sum(xf * xf, axis=-1, keepdims=True) * (1.0 / width)
    return xf * lax.rsqrt(ms + NORM_EPS) * g


def _pack_rows(y):
    lo = pltpu.bitcast(y[:, :HALF].astype(BF16).astype(F32), U32) >> 16
    hi = pltpu.bitcast(y[:, HALF:].astype(BF16).astype(F32), U32) & jnp.uint32(0xFFFF0000)
    return lo | hi


def _unpack_rows(w):
    lo = pltpu.bitcast(w << 16, F32)
    hi = pltpu.bitcast(w & jnp.uint32(0xFFFF0000), F32)
    return lo, hi


def _rope_tables(pos_f):
    lane = lax.broadcasted_iota(I32, (1, LANES), 1)
    freq = (lane % (QK_ROPE // 2)).astype(F32)
    inv_freq = jnp.exp(freq * (-math.log(ROPE_THETA) * 2.0 / QK_ROPE))
    ang = pos_f * inv_freq
    return jnp.cos(ang), jnp.sin(ang)


def _norm_rope(piece, cos_g, sin_g):
    lane = lax.broadcasted_iota(I32, (1, LANES), 1)
    ssq = jnp.sum(jnp.where(lane < QK_ROPE, piece * piece, 0.0), axis=-1, keepdims=True)
    y = piece * lax.rsqrt(ssq * (1.0 / QK_ROPE) + NORM_EPS)
    return y * cos_g + pltpu.roll(y * sin_g, QK_ROPE, axis=1)


def _const_spec(shape):
    return pl.BlockSpec(shape, lambda *_: (0,) * len(shape), pipeline_mode=pl.Buffered(1))


def _mixer_prep_kernel(x_ref, pos_ref, g_mix_ref, wu_ref, wv_ref, wq_ref, wkv_ref, wr_ref, wga_ref, wgb_ref,
                       lng_ref, lnb_ref, sw_ref, sb_ref, gqa_ref, wqb_ref, gkva_ref, wkb_ref, wvbt_ref,
                       gqn_ref, gkn_ref, gqc_ref, gqs_ref, gkc_ref, gks_ref,
                       ma_ref, gb_ref, q_ref, k_ref, vt_ref):
    tm = x_ref.shape[1]
    xn = _rms(x_ref[0], g_mix_ref[...]).astype(BF16)

    cos_t, sin_t = _rope_tables(pos_ref[0].astype(F32))
    qn = _rms(_dot(xn, wq_ref[...]), gqa_ref[...]).astype(BF16)
    kvn = _rms(_dot(xn, wkv_ref[...]), gkva_ref[...]).astype(BF16)
    kpe = _norm_rope(_dot(xn, wr_ref[...]), cos_t * gkc_ref[...], sin_t * gks_ref[...]).astype(BF16)
    q_cos, q_sin = cos_t * gqc_ref[...], sin_t * gqs_ref[...]

    gv = jax.nn.gelu(_dot(xn, wv_ref[...]))
    mu = jnp.mean(gv, axis=-1, keepdims=True)
    vc = gv - mu
    var = jnp.mean(vc * vc, axis=-1, keepdims=True)
    vn = (vc * lax.rsqrt(var + LN_EPS) * lng_ref[...] + lnb_ref[...]).astype(BF16)
    row = lax.broadcasted_iota(I32, (SGU_CHUNK, SGU_CHUNK), 0)
    col = lax.broadcasted_iota(I32, (SGU_CHUNK, SGU_CHUNK), 1)
    causal = col <= row
    n_chunks = tm // SGU_CHUNK
    for pair in range(SGU_GROUPS // 2):
        ps = slice(pair * MXU_COLS, (pair + 1) * MXU_COLS)
        k2 = _dot(kvn, wkb_ref[:, ps])
        v2t = lax.dot_general(wvbt_ref[ps, :].astype(BF16), kvn, (((1,), (1,)), ((), ())),
                              preferred_element_type=F32).astype(BF16)
        gu2 = jax.nn.gelu(_dot(xn, wu_ref[:, ps]))
        ga2 = jax.nn.sigmoid(_dot(xn, wga_ref[:, ps]))
        gb_ref[0, :, ps] = jax.nn.sigmoid(_dot(xn, wgb_ref[:, ps])).astype(BF16)
        for half in range(2):
            g = 2 * pair + half
            hs = slice(half * LANES, (half + 1) * LANES)
            cs = slice(g * SGU_CHUNK, (g + 1) * SGU_CHUNK)
            qh = _dot(qn, wqb_ref[:, g * QK_PAD:(g + 1) * QK_PAD])
            q_ref[0, g, :, :QK_NOPE] = _rms(qh[:, :QK_NOPE], gqn_ref[...]).astype(BF16)
            q_ref[0, g, :, QK_NOPE:] = _norm_rope(qh[:, QK_NOPE:], q_cos, q_sin).astype(BF16)
            k_ref[0, g, :, :QK_NOPE] = _rms(k2[:, hs], gkn_ref[...]).astype(BF16)
            k_ref[0, g, :, QK_NOPE:] = kpe
            vt_ref[0, g] = v2t[hs, :]

            wg = jnp.where(causal, sw_ref[g], 0.0).astype(BF16)
            vcat = jnp.concatenate([vn[c * SGU_CHUNK:(c + 1) * SGU_CHUNK, cs] for c in range(n_chunks)], axis=1)
            mixed = _dot(wg, vcat)
            for c in range(n_chunks):
                rs = slice(c * SGU_CHUNK, (c + 1) * SGU_CHUNK)
                m = mixed[:, c * SGU_CHUNK:(c + 1) * SGU_CHUNK] + sb_ref[:, cs]
                ma_ref[0, rs, cs] = (ga2[rs, hs] * gu2[rs, hs] * m).astype(BF16)


def _mixer_prep(x, pos, consts, tm):
    b, s, d = x.shape
    grid = (b, s // tm)
    row_spec = lambda w: pl.BlockSpec((1, tm, w), lambda i, j: (i, j, 0))
    head_spec = lambda w: pl.BlockSpec((1, HEADS, tm, w), lambda i, j: (i, 0, j, 0))
    return pl.pallas_call(
        _mixer_prep_kernel,
        grid=grid,
        in_specs=[row_spec(d), row_spec(1)] + [_const_spec(c.shape) for c in consts],
        out_specs=[row_spec(d), row_spec(d), head_spec(QK_PAD), head_spec(QK_PAD),
                   pl.BlockSpec((1, HEADS, V_DIM, tm), lambda i, j: (i, 0, 0, j))],
        out_shape=[jax.ShapeDtypeStruct((b, s, d), BF16), jax.ShapeDtypeStruct((b, s, d), BF16),
                   jax.ShapeDtypeStruct((b, HEADS, s, QK_PAD), BF16),
                   jax.ShapeDtypeStruct((b, HEADS, s, QK_PAD), BF16),
                   jax.ShapeDtypeStruct((b, HEADS, V_DIM, s), BF16)],
        compiler_params=pltpu.CompilerParams(dimension_semantics=("arbitrary", "arbitrary"),
                                             vmem_limit_bytes=VMEM_LIMIT),
        name="mixer_prep",
    )(x, pos, *consts)


def _attn_kernel(q_ref, k_ref, vt_ref, o_ref, *, tq):
    s = q_ref.shape[2]
    heads = range(q_ref.shape[1])
    key = lax.broadcasted_iota(I32, (tq, tq), 0)
    qry = lax.broadcasted_iota(I32, (tq, tq), 1)
    diag_mask = key <= qry

    for qi in range(s // tq):
        qs = slice(qi * tq, (qi + 1) * tq)
        q = [q_ref[0, h, qs, :] for h in heads]
        m, l, acc = ([None] * len(heads) for _ in range(3))
        for kj in [qi] + list(range(qi)):
            ks = slice(kj * tq, (kj + 1) * tq)
            for h in heads:
                sc = lax.dot_general(k_ref[0, h, ks, :], q[h], (((1,), (1,)), ((), ())),
                                     preferred_element_type=F32)
                if kj == qi:
                    sc = jnp.where(diag_mask, sc, -jnp.inf)
                    m[h] = jnp.max(sc, axis=0, keepdims=True)
                    p = jnp.exp(sc - m[h])
                    l[h] = jnp.sum(p, axis=0, keepdims=True)
                    acc[h] = _dot(vt_ref[0, h, :, ks], p.astype(BF16))
                else:
                    m_new = jnp.maximum(m[h], jnp.max(sc, axis=0, keepdims=True))
                    alpha = jnp.exp(m[h] - m_new)
                    p = jnp.exp(sc - m_new)
                    l[h] = alpha * l[h] + jnp.sum(p, axis=0, keepdims=True)
                    acc[h] = alpha * acc[h] + _dot(vt_ref[0, h, :, ks], p.astype(BF16))
                    m[h] = m_new
        for h in heads:
            o_ref[0, qs, h * V_DIM:(h + 1) * V_DIM] = (acc[h] / l[h]).T.astype(BF16)


def _attention(q, k, vt, tq, heads_per_step):
    b, h, s, _ = q.shape
    hp = heads_per_step
    return pl.pallas_call(
        functools.partial(_attn_kernel, tq=tq),
        grid=(b, h // hp),
        in_specs=[pl.BlockSpec((1, hp, s, QK_PAD), lambda i, j: (i, j, 0, 0)),
                  pl.BlockSpec((1, hp, s, QK_PAD), lambda i, j: (i, j, 0, 0)),
                  pl.BlockSpec((1, hp, V_DIM, s), lambda i, j: (i, j, 0, 0))],
        out_specs=pl.BlockSpec((1, s, hp * V_DIM), lambda i, j: (i, 0, j)),
        out_shape=jax.ShapeDtypeStruct((b, s, h * V_DIM), BF16),
        compiler_params=pltpu.CompilerParams(dimension_semantics=("arbitrary", "arbitrary"),
                                             vmem_limit_bytes=VMEM_LIMIT),
        name="mla_attention",
    )(q, k, vt)


def _route(sel, scores):
    e, tm = sel.shape
    sel3 = sel.reshape(N_GROUPS, EXPERTS_PER_GROUP, tm)
    sub = lax.broadcasted_iota(I32, sel3.shape, 1)
    m1 = jnp.max(sel3, axis=1, keepdims=True)
    first = jnp.min(jnp.where(sel3 == m1, sub, EXPERTS_PER_GROUP), axis=1, keepdims=True)
    m2 = jnp.max(jnp.where(sub == first, -jnp.inf, sel3), axis=1, keepdims=True)
    gscore = (m1 + m2).reshape(N_GROUPS, tm)
    gid = lax.broadcasted_iota(I32, (N_GROUPS, tm), 0)
    grank = jnp.zeros((N_GROUPS, tm), I32)
    for g in range(N_GROUPS):
        other = gscore[g:g + 1]
        grank += ((other > gscore) | ((other == gscore) & (g < gid))).astype(I32)
    gmask = grank < TOPK_GROUPS
    emask = jnp.broadcast_to(gmask[:, None, :], sel3.shape).reshape(e, tm)
    msel = jnp.where(emask, sel, -jnp.inf)
    eid = lax.broadcasted_iota(I32, (e, tm), 0)
    erank = jnp.zeros((e, tm), I32)
    for j in range(e):
        other = msel[j:j + 1]
        erank += ((other > msel) | ((other == msel) & (j < eid))).astype(I32)
    chosen = erank < TOP_K
    w = jnp.where(chosen, scores, 0.0)
    return chosen, w / jnp.sum(w, axis=0, keepdims=True) * ROUTED_SCALE


def _post_attn_kernel(x_ref, ma_ref, gb_ref, o_ref, wo_ref, gffn_ref, wrt_ref, br_ref,
                      x1_ref, hp_ref, comb_ref, rank_ref, cnt_ref, run_ref):
    tm = x_ref.shape[0]

    @pl.when(pl.program_id(0) == 0)
    def _():
        run_ref[...] = jnp.zeros_like(run_ref)

    merged = ma_ref[...].astype(F32) + gb_ref[...].astype(F32) * o_ref[...].astype(F32)
    x1 = x_ref[...] + _dot(merged.astype(BF16), wo_ref[...])
    x1_ref[...] = x1
    h2 = _rms(x1, gffn_ref[...])
    hp_ref[...] = _pack_rows(h2)
    logits_t = lax.dot_general(wrt_ref[...], h2, (((1,), (1,)), ((), ())),
                               preferred_element_type=F32, precision=lax.Precision.HIGHEST)
    scores = jax.nn.sigmoid(logits_t)
    chosen, comb_t = _route(scores + br_ref[...], scores)
    comb_ref[...] = comb_t
    a = lax.broadcasted_iota(I32, (tm, tm), 0)
    b = lax.broadcasted_iota(I32, (tm, tm), 1)
    before = (a < b).astype(BF16)
    chosen_f = chosen.astype(F32)
    prefix = _dot(chosen_f.astype(BF16), before)
    run = run_ref[:, 0:1]
    rank_ref[...] = jnp.where(chosen, prefix + run, -1.0)
    run_ref[...] += jnp.sum(chosen_f, axis=1, keepdims=True)
    cnt_ref[...] = run_ref[...]


def _post_attn(x, ma, gb, o, consts, tm, chunk):
    t, d = x.shape[0] // MOE_CHUNKS, x.shape[1]
    first = chunk * (t // tm)
    row_spec = lambda w: pl.BlockSpec((tm, w), lambda i: (i, 0))
    col_spec = pl.BlockSpec((N_EXPERTS, tm), lambda i: (0, i))
    return pl.pallas_call(
        _post_attn_kernel,
        grid=(t // tm,),
        in_specs=[pl.BlockSpec((tm, d), lambda i: (first + i, 0))] * 4 + [_const_spec(c.shape) for c in consts],
        out_specs=[row_spec(d), row_spec(HALF), col_spec, col_spec, _const_spec((N_EXPERTS, LANES))],
        out_shape=[jax.ShapeDtypeStruct((t, d), F32), jax.ShapeDtypeStruct((t, HALF), U32),
                   jax.ShapeDtypeStruct((N_EXPERTS, t), F32), jax.ShapeDtypeStruct((N_EXPERTS, t), F32),
                   jax.ShapeDtypeStruct((N_EXPERTS, LANES), F32)],
        scratch_shapes=[pltpu.VMEM((N_EXPERTS, LANES), F32)],
        compiler_params=pltpu.CompilerParams(dimension_semantics=("arbitrary",), vmem_limit_bytes=VMEM_LIMIT),
        name="post_attn_router",
    )(x, ma, gb, o, *consts)


def _route_lists_kernel(comb_ref, rank_ref, off_ref, pos_ref, w_ref):
    rank = rank_ref[...]
    chosen = rank >= 0.0
    e = rank.shape[0]
    lower = (lax.broadcasted_iota(I32, (e, e), 1) < lax.broadcasted_iota(I32, (e, e), 0)).astype(BF16)
    slot = _dot(lower, chosen.astype(BF16))
    pos_full = rank + off_ref[...]
    comb = comb_ref[...]
    pos_rows, w_rows = [], []
    for k in range(TOP_K):
        pick = chosen & (slot == float(k))
        pos_rows.append(jnp.sum(jnp.where(pick, pos_full, 0.0), axis=0, keepdims=True))
        w_rows.append(jnp.sum(jnp.where(pick, comb, 0.0), axis=0, keepdims=True))
    pos_ref[...] = jnp.concatenate(pos_rows, axis=0).astype(I32)
    w_ref[...] = jnp.concatenate(w_rows, axis=0).T


def _route_lists(comb_t, rank_t, offsets, tm):
    e, t = comb_t.shape
    col_spec = pl.BlockSpec((e, tm), lambda i: (0, i))
    return pl.pallas_call(
        _route_lists_kernel,
        grid=(t // tm,),
        in_specs=[col_spec, col_spec, _const_spec((e, 1))],
        out_specs=[pl.BlockSpec((TOP_K, tm), lambda i: (0, i)), pl.BlockSpec((tm, TOP_K), lambda i: (i, 0))],
        out_shape=[jax.ShapeDtypeStruct((TOP_K, t), I32), jax.ShapeDtypeStruct((t, TOP_K), F32)],
        compiler_params=pltpu.CompilerParams(dimension_semantics=("arbitrary",)),
        name="route_lists",
    )(comb_t, rank_t, offsets)


def _sc_mesh():
    return plsc.VectorSubcoreMesh(core_axis_name="c", subcore_axis_name="s",
                                  num_cores=SC_CORES, num_subcores=SC_SUBCORES)


def _sc_worker():
    return lax.axis_index("s") * SC_CORES + lax.axis_index("c")


def _sc_scatter_rows(src, pos_win, n_rows):
    t, width = src.shape
    n_win = t // SC_WINDOW
    per_w = n_win // (SC_CORES * SC_SUBCORES)

    @functools.partial(
        pl.kernel, mesh=_sc_mesh(), out_type=jax.ShapeDtypeStruct((n_rows, width), src.dtype),
        scratch_types=[pltpu.VMEM((TOP_K, SC_WINDOW), I32), pltpu.VMEM((SC_WINDOW, width), src.dtype),
                       pltpu.SemaphoreType.DMA],
        name="sc_scatter_rows")
    def run(src_hbm, pos_hbm, out_hbm, idx_v, rows_v, sem):
        base = _sc_worker() * per_w

        @pl.loop(0, per_w)
        def _(j):
            w = base + j
            pltpu.sync_copy(pos_hbm.at[w], idx_v)
            pltpu.sync_copy(src_hbm.at[pl.ds(w * SC_WINDOW, SC_WINDOW)], rows_v)
            copies = [pltpu.async_copy(rows_v, out_hbm.at[idx_v.at[k]], sem) for k in range(TOP_K)]
            for c in copies:
                c.wait()

    return run(src, pos_win)


def _sc_gather_rows(table, pos_win):
    n_win = pos_win.shape[0]
    width = table.shape[1]
    per_w = n_win // (SC_CORES * SC_SUBCORES)

    @functools.partial(
        pl.kernel, mesh=_sc_mesh(),
        out_type=jax.ShapeDtypeStruct((TOP_K, n_win * SC_WINDOW, width), table.dtype),
        scratch_types=[pltpu.VMEM((TOP_K, SC_WINDOW), I32), pltpu.VMEM((2, SC_WINDOW, width), table.dtype),
                       pltpu.SemaphoreType.DMA, pltpu.SemaphoreType.DMA],
        name="sc_gather_rows")
    def run(table_hbm, pos_hbm, out_hbm, idx_v, rows_v, gsem, wsem):
        base = _sc_worker() * per_w

        @pl.loop(0, per_w)
        def _(j):
            w = base + j
            pltpu.sync_copy(pos_hbm.at[w], idx_v)
            for k in range(TOP_K):
                buf = rows_v.at[k % 2]
                pltpu.async_copy(table_hbm.at[idx_v.at[k]], buf, gsem).wait()
                pltpu.async_copy(buf, out_hbm.at[k, pl.ds(w * SC_WINDOW, SC_WINDOW)], wsem).wait()

    return run(table, pos_win)


def _grouped_ffn_kernel(te_ref, xs_ref, *refs):
    w_refs, ys_ref = refs[:-1], refs[-1]
    acts = []
    for j in range(TILES_PER_STEP):
        wg_ref, wu_ref = w_refs[3 * j:3 * j + 2]
        lo, hi = _unpack_rows(xs_ref[j])
        lo, hi = lo.astype(BF16), hi.astype(BF16)
        gate = _dot(lo, wg_ref[0, :HALF]) + _dot(hi, wg_ref[0, HALF:])
        up = _dot(lo, wu_ref[0, :HALF]) + _dot(hi, wu_ref[0, HALF:])
        acts.append((jax.nn.silu(gate) * up).astype(BF16))
    for j in range(TILES_PER_STEP):
        ys_ref[j] = _pack_rows(_dot(acts[j], w_refs[3 * j + 2][0]))


def _grouped_ffn(tile_expert, xs, wg, wu, wd):
    n_rows, half = xs.shape
    n_steps = n_rows // (TILES_PER_STEP * ROW_TILE)
    row_spec = pl.BlockSpec((TILES_PER_STEP, ROW_TILE, half), lambda i, te: (0, i, 0))

    def exp_map(j):
        return lambda i, te: (te[j * n_steps + i], 0, 0)

    w_specs = [pl.BlockSpec((1,) + w.shape[1:], exp_map(j)) for j in range(TILES_PER_STEP) for w in (wg, wu, wd)]
    ys = pl.pallas_call(
        _grouped_ffn_kernel,
        grid_spec=pltpu.PrefetchScalarGridSpec(
            num_scalar_prefetch=1,
            grid=(n_steps,),
            in_specs=[row_spec] + w_specs,
            out_specs=row_spec),
        out_shape=jax.ShapeDtypeStruct((TILES_PER_STEP, n_steps * ROW_TILE, half), U32),
        compiler_params=pltpu.CompilerParams(dimension_semantics=("arbitrary",), vmem_limit_bytes=VMEM_LIMIT),
        name="grouped_ffn",
    )(tile_expert, xs.reshape(TILES_PER_STEP, n_steps * ROW_TILE, half), *([wg, wu, wd] * TILES_PER_STEP))
    return ys.reshape(n_rows, half)


def _combine_kernel(yt_ref, w_ref, x1_ref, p_ref, gffn_ref, wsg_ref, wsu_ref, wsd_ref,
                    gpi_ref, wpg_ref, wpp_ref, gpo_ref, *rest):
    out_ref = rest[-1]
    x1 = x1_ref[...]
    w = w_ref[...]
    acc_lo = jnp.zeros((x1.shape[0], HALF), F32)
    acc_hi = jnp.zeros((x1.shape[0], HALF), F32)
    for k in range(TOP_K):
        lo, hi = _unpack_rows(yt_ref[k])
        wk = w[:, k:k + 1]
        acc_lo += wk * lo
        acc_hi += wk * hi
    h = _rms(x1, gffn_ref[...]).astype(BF16)
    shared = _dot((jax.nn.silu(_dot(h, wsg_ref[...])) * _dot(h, wsu_ref[...])).astype(BF16), wsd_ref[...])
    x2 = x1 + shared + jnp.concatenate([acc_lo, acc_hi], axis=-1)
    gate = jax.nn.sigmoid(_dot(_rms(x2, gpi_ref[...]).astype(BF16), wpg_ref[...]))
    proj = _rms(_dot(p_ref[...].astype(BF16), wpp_ref[...]), gpo_ref[...])
    out_ref[...] = x2 + gate * proj


def _combine(yt, w_tok, x1, p, consts, tm, chunk, out_so_far):
    t, d = x1.shape
    first = chunk * (t // tm)
    row_spec = lambda w: pl.BlockSpec((tm, w), lambda i: (i, 0))
    full_spec = lambda w: pl.BlockSpec((tm, w), lambda i: (first + i, 0))
    carried = [] if out_so_far is None else [out_so_far]
    return pl.pallas_call(
        _combine_kernel,
        grid=(t // tm,),
        in_specs=[pl.BlockSpec((TOP_K, tm, HALF), lambda i: (0, i, 0)), row_spec(TOP_K), row_spec(d),
                  full_spec(PLE_DIM)] + [_const_spec(c.shape) for c in consts]
                 + [pl.BlockSpec(memory_space=pl.ANY)] * len(carried),
        out_specs=full_spec(d),
        out_shape=jax.ShapeDtypeStruct((t * MOE_CHUNKS, d), F32),
        input_output_aliases={4 + len(consts): 0} if carried else {},
        compiler_params=pltpu.CompilerParams(dimension_semantics=("arbitrary",), vmem_limit_bytes=VMEM_LIMIT),
        name="combine_ple",
    )(yt, w_tok, x1, p, *consts, *carried)


def _rotate_half(a):
    half = QK_ROPE // 2
    return jnp.concatenate([-a[..., half:], a[..., :half]], axis=-1)


def _with_rotate_half(w):
    return jnp.concatenate([w, _rotate_half(w[..., -QK_ROPE:])], axis=-1)


def _rope_gain_tables(g, scale):
    zeros = jnp.zeros((QK_ROPE,), F32)
    half = QK_ROPE // 2
    g_swapped = jnp.concatenate([g[half:], g[:half]])
    return (jnp.concatenate([g, zeros]) * scale).reshape(1, LANES), \
        (jnp.concatenate([zeros, g_swapped]) * scale).reshape(1, LANES)


def _layer(x, p, pos, g_mix, w_in, sgu_ln_g, sgu_ln_b, sgu_w, sgu_b, mla_g_qa, mla_w_qb, mla_g_kva, mla_w_kvb,
           qk_g_q_nope, qk_g_k_nope, qk_g_q_rope, qk_g_k_rope, w_o, g_ffn, w_router, b_router,
           w_exp_gate, w_exp_up, w_exp_down, w_sh_gate, w_sh_up, w_sh_down,
           g_ple_in, w_ple_gate, w_ple_proj, g_ple_out):
    b, s, d = x.shape
    t = b * s
    row = lambda a: a.reshape(1, -1)
    sizes = [d, d, Q_LORA, KV_LORA, QK_ROPE, d, d]
    offs = [0]
    for sz in sizes:
        offs.append(offs[-1] + sz)
    w_u, w_v, w_q, w_kv, w_r, w_ga, w_gb = [w_in[:, offs[i]:offs[i + 1]] for i in range(7)]
    w_r = _with_rotate_half(w_r)
    wqb = _with_rotate_half(mla_w_qb.reshape(Q_LORA, HEADS, QK_DIM)).reshape(Q_LORA, HEADS * QK_PAD)
    wkvb = mla_w_kvb.reshape(KV_LORA, HEADS, QK_NOPE + V_DIM)
    wkb = wkvb[:, :, :QK_NOPE].reshape(KV_LORA, HEADS * QK_NOPE)
    wvbt = wkvb[:, :, QK_NOPE:].reshape(KV_LORA, HEADS * V_DIM).T
    sgu_bias = jnp.repeat(sgu_b.T, d // SGU_GROUPS, axis=1)
    q_scale = QK_DIM ** -0.5

    consts1 = [row(g_mix), w_u, w_v, w_q, w_kv, w_r, w_ga, w_gb, row(sgu_ln_g), row(sgu_ln_b), sgu_w, sgu_bias,
               row(mla_g_qa), wqb, row(mla_g_kva), wkb, wvbt, row(qk_g_q_nope) * q_scale, row(qk_g_k_nope),
               *_rope_gain_tables(qk_g_q_rope, q_scale), *_rope_gain_tables(qk_g_k_rope, 1.0)]
    ma, gb, q, k, vt = _mixer_prep(x, pos.reshape(b, s, 1), consts1, tm=512)
    o = _attention(q, k, vt, tq=512, heads_per_step=2)

    consts3 = [w_o, row(g_ffn), w_router.T, b_router.reshape(-1, 1)]
    consts6 = [row(g_ffn), w_sh_gate, w_sh_up, w_sh_down, row(g_ple_in), w_ple_gate, w_ple_proj, row(g_ple_out)]
    x2d, ma2d, gb2d, o2d, p2d = (a.reshape(t, -1) for a in (x, ma, gb, o, p))
    tc = t // MOE_CHUNKS
    max_tiles = (tc * TOP_K) // ROW_TILE + N_EXPERTS
    tile_ids = jnp.arange(max_tiles, dtype=I32)
    out = None
    for chunk in range(MOE_CHUNKS):
        x1, h_packed, comb_t, rank_t, counts = _post_attn(x2d, ma2d, gb2d, o2d, consts3, tm=512, chunk=chunk)

        counts = counts[:, 0].astype(I32)
        tiles_per_expert = (counts + ROW_TILE - 1) // ROW_TILE
        tile_end = jnp.cumsum(tiles_per_expert)
        offsets = ((tile_end - tiles_per_expert) * ROW_TILE).astype(F32).reshape(N_EXPERTS, 1)
        tile_expert = jnp.minimum(jnp.sum((tile_end[None, :] <= tile_ids[:, None]).astype(I32), axis=1),
                                  N_EXPERTS - 1)

        pos_t, w_tok = _route_lists(comb_t, rank_t, offsets, tm=512)
        pos_win = pos_t.reshape(TOP_K, tc // SC_WINDOW, SC_WINDOW).transpose(1, 0, 2)

        xs = _sc_scatter_rows(h_packed, pos_win, max_tiles * ROW_TILE)
        ys = _grouped_ffn(tile_expert, xs, w_exp_gate, w_exp_up, w_exp_down)
        yt = _sc_gather_rows(ys, pos_win)
        out = _combine(yt, w_tok, x1, p2d, consts6, tm=256, chunk=chunk, out_so_far=out)
    return out.reshape(b, s, d)


def kernel(x, p, positions, g_mix, w_in, sgu_ln_g, sgu_ln_b, sgu_w, sgu_b, mla_g_qa, mla_w_qb, mla_g_kva, mla_w_kvb, qk_g_q_nope, qk_g_k_nope, qk_g_q_rope, qk_g_k_rope, w_o, g_ffn, w_router, b_router, w_exp_gate, w_exp_up, w_exp_down, w_sh_gate, w_sh_up, w_sh_down, g_ple_in, w_ple_gate, w_ple_proj, g_ple_out):
    params = (g_mix, w_in, sgu_ln_g, sgu_ln_b, sgu_w, sgu_b, mla_g_qa, mla_w_qb, mla_g_kva, mla_w_kvb,
              qk_g_q_nope, qk_g_k_nope, qk_g_q_rope, qk_g_k_rope, w_o, g_ffn, w_router, b_router,
              w_exp_gate, w_exp_up, w_exp_down, w_sh_gate, w_sh_up, w_sh_down,
              g_ple_in, w_ple_gate, w_ple_proj, g_ple_out)
    for l in range(g_mix.shape[0]):
        x = _layer(x, p[l], positions, *[a[l] for a in params])
    return x
```

```python
import functools
import math

import jax
import jax.numpy as jnp
from jax import lax
from jax.experimental import pallas as pl
from jax.experimental.pallas import tpu as pltpu
from jax.experimental.pallas import tpu_sc as plsc

D_MODEL = 1024
PLE_DIM = 256
SGU_CHUNK = 128
SGU_GROUPS = 8
V_DIM = 128
HEADS = 8
QK_NOPE = 128
QK_ROPE = 64
QK_DIM = QK_NOPE + QK_ROPE
QK_PAD = 256
Q_LORA = 384
KV_LORA = 256
ROPE_THETA = 10000.0
N_EXPERTS = 64
N_GROUPS = 8
EXPERTS_PER_GROUP = 8
TOPK_GROUPS = 4
TOP_K = 8
EXPERT_FF = 256
ROUTED_SCALE = 2.5
NORM_EPS = 1e-6
LN_EPS = 1e-5

LANES = 128
MXU_COLS = 256
VMEM_LIMIT = 56 * 1024 * 1024
SC_CORES = 2
SC_SUBCORES = 16
SC_WINDOW = 64
ROW_TILE = 256
TILES_PER_STEP = 4
MOE_CHUNKS = 2
HALF = D_MODEL // 2

F32 = jnp.float32
BF16 = jnp.bfloat16
U32 = jnp.uint32
I32 = jnp.int32


def _dot(a, b):
    return lax.dot_general(a, b, (((1,), (0,)), ((), ())), preferred_element_type=F32)


def _rms(xf, g, width=None):
    width = xf.shape[-1] if width is None else width
    ms = jnp.sum(xf * xf, axis=-1, keepdims=True) * (1.0 / width)
    return xf * lax.rsqrt(ms + NORM_EPS) * g


def _pack_rows(y):
    lo = pltpu.bitcast(y[:, :HALF].astype(BF16).astype(F32), U32) >> 16
    hi = pltpu.bitcast(y[:, HALF:].astype(BF16).astype(F32), U32) & jnp.uint32(0xFFFF0000)
    return lo | hi


def _unpack_rows(w):
    lo = pltpu.bitcast(w << 16, F32)
    hi = pltpu.bitcast(w & jnp.uint32(0xFFFF0000), F32)
    return lo, hi


def _rope_tables(pos_f):
    lane = lax.broadcasted_iota(I32, (1, LANES), 1)
    freq = (lane % (QK_ROPE // 2)).astype(F32)
    inv_freq = jnp.exp(freq * (-math.log(ROPE_THETA) * 2.0 / QK_ROPE))
    ang = pos_f * inv_freq
    return jnp.cos(ang), jnp.sin(ang)


def _norm_rope(piece, cos_g, sin_g):
    lane = lax.broadcasted_iota(I32, (1, LANES), 1)
    ssq = jnp.sum(jnp.where(lane < QK_ROPE, piece * piece, 0.0), axis=-1, keepdims=True)
    y = piece * lax.rsqrt(ssq * (1.0 / QK_ROPE) + NORM_EPS)
    return y * cos_g + pltpu.roll(y * sin_g, QK_ROPE, axis=1)


def _const_spec(shape):
    return pl.BlockSpec(shape, lambda *_: (0,) * len(shape), pipeline_mode=pl.Buffered(1))


def _mixer_prep_kernel(x_ref, pos_ref, g_mix_ref, wu_ref, wv_ref, wq_ref, wkv_ref, wr_ref, wga_ref, wgb_ref,
                       lng_ref, lnb_ref, sw_ref, sb_ref, gqa_ref, wqb_ref, gkva_ref, wkb_ref, wvbt_ref,
                       gqn_ref, gkn_ref, gqc_ref, gqs_ref, gkc_ref, gks_ref,
                       ma_ref, gb_ref, q_ref, k_ref, vt_ref):
    tm = x_ref.shape[1]
    xn = _rms(x_ref[0], g_mix_ref[...]).astype(BF16)

    cos_t, sin_t = _rope_tables(pos_ref[0].astype(F32))
    qn = _rms(_dot(xn, wq_ref[...]), gqa_ref[...]).astype(BF16)
    kvn = _rms(_dot(xn, wkv_ref[...]), gkva_ref[...]).astype(BF16)
    kpe = _norm_rope(_dot(xn, wr_ref[...]), cos_t * gkc_ref[...], sin_t * gks_ref[...]).astype(BF16)
    q_cos, q_sin = cos_t * gqc_ref[...], sin_t * gqs_ref[...]

    gv = jax.nn.gelu(_dot(xn, wv_ref[...]))
    mu = jnp.mean(gv, axis=-1, keepdims=True)
    vc = gv - mu
    var = jnp.mean(vc * vc, axis=-1, keepdims=True)
    vn = (vc * lax.rsqrt(var + LN_EPS) * lng_ref[...] + lnb_ref[...]).astype(BF16)
    row = lax.broadcasted_iota(I32, (SGU_CHUNK, SGU_CHUNK), 0)
    col = lax.broadcasted_iota(I32, (SGU_CHUNK, SGU_CHUNK), 1)
    causal = col <= row
    n_chunks = tm // SGU_CHUNK
    for pair in range(SGU_GROUPS // 2):
        ps = slice(pair * MXU_COLS, (pair + 1) * MXU_COLS)
        k2 = _dot(kvn, wkb_ref[:, ps])
        v2t = lax.dot_general(wvbt_ref[ps, :].astype(BF16), kvn, (((1,), (1,)), ((), ())),
                              preferred_element_type=F32).astype(BF16)
        gu2 = jax.nn.gelu(_dot(xn, wu_ref[:, ps]))
        ga2 = jax.nn.sigmoid(_dot(xn, wga_ref[:, ps]))
        gb_ref[0, :, ps] = jax.nn.sigmoid(_dot(xn, wgb_ref[:, ps])).astype(BF16)
        for half in range(2):
            g = 2 * pair + half
            hs = slice(half * LANES, (half + 1) * LANES)
            cs = slice(g * SGU_CHUNK, (g + 1) * SGU_CHUNK)
            qh = _dot(qn, wqb_ref[:, g * QK_PAD:(g + 1) * QK_PAD])
            q_ref[0, g, :, :QK_NOPE] = _rms(qh[:, :QK_NOPE], gqn_ref[...]).astype(BF16)
            q_ref[0, g, :, QK_NOPE:] = _norm_rope(qh[:, QK_NOPE:], q_cos, q_sin).astype(BF16)
            k_ref[0, g, :, :QK_NOPE] = _rms(k2[:, hs], gkn_ref[...]).astype(BF16)
            k_ref[0, g, :, QK_NOPE:] = kpe
            vt_ref[0, g] = v2t[hs, :]

            wg = jnp.where(causal, sw_ref[g], 0.0).astype(BF16)
            vcat = jnp.concatenate([vn[c * SGU_CHUNK:(c + 1) * SGU_CHUNK, cs] for c in range(n_chunks)], axis=1)
            mixed = _dot(wg, vcat)
            for c in range(n_chunks):
                rs = slice(c * SGU_CHUNK, (c + 1) * SGU_CHUNK)
                m = mixed[:, c * SGU_CHUNK:(c + 1) * SGU_CHUNK] + sb_ref[:, cs]
                ma_ref[0, rs, cs] = (ga2[rs, hs] * gu2[rs, hs] * m).astype(BF16)


def _mixer_prep(x, pos, consts, tm):
    b, s, d = x.shape
    grid = (b, s // tm)
    row_spec = lambda w: pl.BlockSpec((1, tm, w), lambda i, j: (i, j, 0))
    head_spec = lambda w: pl.BlockSpec((1, HEADS, tm, w), lambda i, j: (i, 0, j, 0))
    return pl.pallas_call(
        _mixer_prep_kernel,
        grid=grid,
        in_specs=[row_spec(d), row_spec(1)] + [_const_spec(c.shape) for c in consts],
        out_specs=[row_spec(d), row_spec(d), head_spec(QK_PAD), head_spec(QK_PAD),
                   pl.BlockSpec((1, HEADS, V_DIM, tm), lambda i, j: (i, 0, 0, j))],
        out_shape=[jax.ShapeDtypeStruct((b, s, d), BF16), jax.ShapeDtypeStruct((b, s, d), BF16),
                   jax.ShapeDtypeStruct((b, HEADS, s, QK_PAD), BF16),
                   jax.ShapeDtypeStruct((b, HEADS, s, QK_PAD), BF16),
                   jax.ShapeDtypeStruct((b, HEADS, V_DIM, s), BF16)],
        compiler_params=pltpu.CompilerParams(dimension_semantics=("arbitrary", "arbitrary"),
                                             vmem_limit_bytes=VMEM_LIMIT),
        name="mixer_prep",
    )(x, pos, *consts)


def _attn_kernel(q_ref, k_ref, vt_ref, o_ref, *, tq):
    s = q_ref.shape[2]
    key = lax.broadcasted_iota(I32, (tq, tq), 0)
    qry = lax.broadcasted_iota(I32, (tq, tq), 1)
    diag_mask = key <= qry
    for qi in range(s // tq):
        qs = slice(qi * tq, (qi + 1) * tq)
        n_keys = (qi + 1) * tq
        for h in range(q_ref.shape[1]):
            sc = lax.dot_general(k_ref[0, h, :n_keys, :], q_ref[0, h, qs, :], (((1,), (1,)), ((), ())),
                                 preferred_element_type=F32)
            last = jnp.where(diag_mask, sc[n_keys - tq:], -jnp.inf)
            sc = last if qi == 0 else jnp.concatenate([sc[:n_keys - tq], last], axis=0)
            m = jnp.max(sc, axis=0, keepdims=True)
            p = jnp.exp(sc - m)
            l = jnp.sum(p, axis=0, keepdims=True)
            acc = _dot(vt_ref[0, h, :, :n_keys], p.astype(BF16))
            o_ref[0, qs, h * V_DIM:(h + 1) * V_DIM] = (acc / l).T.astype(BF16)


def _attention(q, k, vt, tq, heads_per_step):
    b, h, s, _ = q.shape
    hp = heads_per_step
    return pl.pallas_call(
        functools.partial(_attn_kernel, tq=tq),
        grid=(b, h // hp),
        in_specs=[pl.BlockSpec((1, hp, s, QK_PAD), lambda i, j: (i, j, 0, 0)),
                  pl.BlockSpec((1, hp, s, QK_PAD), lambda i, j: (i, j, 0, 0)),
                  pl.BlockSpec((1, hp, V_DIM, s), lambda i, j: (i, j, 0, 0))],
        out_specs=pl.BlockSpec((1, s, hp * V_DIM), lambda i, j: (i, 0, j)),
        out_shape=jax.ShapeDtypeStruct((b, s, h * V_DIM), BF16),
        compiler_params=pltpu.CompilerParams(dimension_semantics=("arbitrary", "arbitrary"),
                                             vmem_limit_bytes=VMEM_LIMIT),
        name="mla_attention",
    )(q, k, vt)


def _route(sel, scores):
    e, tm = sel.shape
    sel3 = sel.reshape(N_GROUPS, EXPERTS_PER_GROUP, tm)
    sub = lax.broadcasted_iota(I32, sel3.shape, 1)
    m1 = jnp.max(sel3, axis=1, keepdims=True)
    first = jnp.min(jnp.where(sel3 == m1, sub, EXPERTS_PER_GROUP), axis=1, keepdims=True)
    m2 = jnp.max(jnp.where(sub == first, -jnp.inf, sel3), axis=1, keepdims=True)
    gscore = (m1 + m2).reshape(N_GROUPS, tm)
    gid = lax.broadcasted_iota(I32, (N_GROUPS, tm), 0)
    grank = jnp.zeros((N_GROUPS, tm), I32)
    for g in range(N_GROUPS):
        other = gscore[g:g + 1]
        grank += ((other > gscore) | ((other == gscore) & (g < gid))).astype(I32)
    gmask = grank < TOPK_GROUPS
    emask = jnp.broadcast_to(gmask[:, None, :], sel3.shape).reshape(e, tm)
    msel = jnp.where(emask, sel, -jnp.inf)
    groups = [msel[g * EXPERTS_PER_GROUP:(g + 1) * EXPERTS_PER_GROUP] for g in range(N_GROUPS)]
    ranks = [jnp.zeros((EXPERTS_PER_GROUP, tm), F32) for _ in range(N_GROUPS)]
    sub8 = lax.broadcasted_iota(I32, (EXPERTS_PER_GROUP, tm), 0)
    for j in range(e):
        jg, jr = divmod(j, EXPERTS_PER_GROUP)
        other = jnp.broadcast_to(groups[jg][jr:jr + 1], (EXPERTS_PER_GROUP, tm))
        for g in range(N_GROUPS):
            if g < jg:
                beats = other > groups[g]
            elif g > jg:
                beats = other >= groups[g]
            else:
                beats = (other > groups[g]) | ((other == groups[g]) & (jr < sub8))
            ranks[g] += jnp.where(beats, 1.0, 0.0)
    chosen = jnp.concatenate(ranks, axis=0) < float(TOP_K)
    w = jnp.where(chosen, scores, 0.0)
    return chosen, w / jnp.sum(w, axis=0, keepdims=True) * ROUTED_SCALE


def _post_attn_kernel(x_ref, ma_ref, gb_ref, o_ref, wo_ref, gffn_ref, wrt_ref, br_ref,
                      x1_ref, hp_ref, comb_ref, rank_ref, cnt_ref, run_ref):
    tm = x_ref.shape[0]

    @pl.when(pl.program_id(0) == 0)
    def _():
        run_ref[...] = jnp.zeros_like(run_ref)

    merged = ma_ref[...].astype(F32) + gb_ref[...].astype(F32) * o_ref[...].astype(F32)
    x1 = x_ref[...] + _dot(merged.astype(BF16), wo_ref[...])
    x1_ref[...] = x1
    h2 = _rms(x1, gffn_ref[...])
    hp_ref[...] = _pack_rows(h2)
    logits_t = lax.dot_general(wrt_ref[...], h2, (((1,), (1,)), ((), ())),
                               preferred_element_type=F32, precision=lax.Precision.HIGHEST)
    scores = jax.nn.sigmoid(logits_t)
    chosen, comb_t = _route(scores + br_ref[...], scores)
    comb_ref[...] = comb_t
    a = lax.broadcasted_iota(I32, (tm, tm), 0)
    b = lax.broadcasted_iota(I32, (tm, tm), 1)
    before = (a < b).astype(BF16)
    chosen_f = chosen.astype(F32)
    prefix = _dot(chosen_f.astype(BF16), before)
    run = run_ref[:, 0:1]
    rank_ref[...] = jnp.where(chosen, prefix + run, -1.0)
    run_ref[...] += jnp.sum(chosen_f, axis=1, keepdims=True)
    cnt_ref[...] = run_ref[...]


def _post_attn(x, ma, gb, o, consts, tm, chunk):
    t, d = x.shape[0] // MOE_CHUNKS, x.shape[1]
    first = chunk * (t // tm)
    row_spec = lambda w: pl.BlockSpec((tm, w), lambda i: (i, 0))
    col_spec = pl.BlockSpec((N_EXPERTS, tm), lambda i: (0, i))
    return pl.pallas_call(
        _post_attn_kernel,
        grid=(t // tm,),
        in_specs=[pl.BlockSpec((tm, d), lambda i: (first + i, 0))] * 4 + [_const_spec(c.shape) for c in consts],
        out_specs=[row_spec(d), row_spec(HALF), col_spec, col_spec, _const_spec((N_EXPERTS, LANES))],
        out_shape=[jax.ShapeDtypeStruct((t, d), F32), jax.ShapeDtypeStruct((t, HALF), U32),
                   jax.ShapeDtypeStruct((N_EXPERTS, t), F32), jax.ShapeDtypeStruct((N_EXPERTS, t), F32),
                   jax.ShapeDtypeStruct((N_EXPERTS, LANES), F32)],
        scratch_shapes=[pltpu.VMEM((N_EXPERTS, LANES), F32)],
        compiler_params=pltpu.CompilerParams(dimension_semantics=("arbitrary",), vmem_limit_bytes=VMEM_LIMIT),
        name="post_attn_router",
    )(x, ma, gb, o, *consts)


def _route_lists_kernel(comb_ref, rank_ref, off_ref, pos_ref, w_ref):
    rank = rank_ref[...]
    chosen = rank >= 0.0
    e = rank.shape[0]
    lower = (lax.broadcasted_iota(I32, (e, e), 1) < lax.broadcasted_iota(I32, (e, e), 0)).astype(BF16)
    slot = _dot(lower, chosen.astype(BF16))
    pos_full = rank + off_ref[...]
    comb = comb_ref[...]
    pos_rows, w_rows = [], []
    for k in range(TOP_K):
        pick = chosen & (slot == float(k))
        pos_rows.append(jnp.sum(jnp.where(pick, pos_full, 0.0), axis=0, keepdims=True))
        w_rows.append(jnp.sum(jnp.where(pick, comb, 0.0), axis=0, keepdims=True))
    pos_ref[...] = jnp.concatenate(pos_rows, axis=0).astype(I32)
    w_ref[...] = jnp.concatenate(w_rows, axis=0).T


def _route_lists(comb_t, rank_t, offsets, tm):
    e, t = comb_t.shape
    col_spec = pl.BlockSpec((e, tm), lambda i: (0, i))
    return pl.pallas_call(
        _route_lists_kernel,
        grid=(t // tm,),
        in_specs=[col_spec, col_spec, _const_spec((e, 1))],
        out_specs=[pl.BlockSpec((TOP_K, tm), lambda i: (0, i)), pl.BlockSpec((tm, TOP_K), lambda i: (i, 0))],
        out_shape=[jax.ShapeDtypeStruct((TOP_K, t), I32), jax.ShapeDtypeStruct((t, TOP_K), F32)],
        compiler_params=pltpu.CompilerParams(dimension_semantics=("arbitrary",)),
        name="route_lists",
    )(comb_t, rank_t, offsets)


def _sc_mesh():
    return plsc.VectorSubcoreMesh(core_axis_name="c", subcore_axis_name="s",
                                  num_cores=SC_CORES, num_subcores=SC_SUBCORES)


def _sc_worker():
    return lax.axis_index("s") * SC_CORES + lax.axis_index("c")


def _sc_scatter_rows(src, pos_win, n_rows):
    t, width = src.shape
    n_win = t // SC_WINDOW
    per_w = n_win // (SC_CORES * SC_SUBCORES)

    @functools.partial(
        pl.kernel, mesh=_sc_mesh(), out_type=jax.ShapeDtypeStruct((n_rows, width), src.dtype),
        scratch_types=[pltpu.VMEM((TOP_K, SC_WINDOW), I32), pltpu.VMEM((SC_WINDOW, width), src.dtype),
                       pltpu.SemaphoreType.DMA],
        name="sc_scatter_rows")
    def run(src_hbm, pos_hbm, out_hbm, idx_v, rows_v, sem):
        base = _sc_worker() * per_w

        @pl.loop(0, per_w)
        def _(j):
            w = base + j
            pltpu.sync_copy(pos_hbm.at[w], idx_v)
            pltpu.sync_copy(src_hbm.at[pl.ds(w * SC_WINDOW, SC_WINDOW)], rows_v)
            copies = [pltpu.async_copy(rows_v, out_hbm.at[idx_v.at[k]], sem) for k in range(TOP_K)]
            for c in copies:
                c.wait()

    return run(src, pos_win)


def _sc_gather_rows(table, pos_win):
    n_win = pos_win.shape[0]
    width = table.shape[1]
    per_w = n_win // (SC_CORES * SC_SUBCORES)

    @functools.partial(
        pl.kernel, mesh=_sc_mesh(),
        out_type=jax.ShapeDtypeStruct((TOP_K, n_win * SC_WINDOW, width), table.dtype),
        scratch_types=[pltpu.VMEM((TOP_K, SC_WINDOW), I32), pltpu.VMEM((2, SC_WINDOW, width), table.dtype),
                       pltpu.SemaphoreType.DMA, pltpu.SemaphoreType.DMA],
        name="sc_gather_rows")
    def run(table_hbm, pos_hbm, out_hbm, idx_v, rows_v, gsem, wsem):
        base = _sc_worker() * per_w

        @pl.loop(0, per_w)
        def _(j):
            w = base + j
            pltpu.sync_copy(pos_hbm.at[w], idx_v)
            for k in range(TOP_K):
                buf = rows_v.at[k % 2]
                pltpu.async_copy(table_hbm.at[idx_v.at[k]], buf, gsem).wait()
                pltpu.async_copy(buf, out_hbm.at[k, pl.ds(w * SC_WINDOW, SC_WINDOW)], wsem).wait()

    return run(table, pos_win)


def _grouped_ffn_kernel(te_ref, xs_ref, *refs):
    w_refs, ys_ref = refs[:-1], refs[-1]
    acts = []
    for j in range(TILES_PER_STEP):
        wg_ref, wu_ref = w_refs[3 * j:3 * j + 2]
        lo, hi = _unpack_rows(xs_ref[j])
        lo, hi = lo.astype(BF16), hi.astype(BF16)
        gate = _dot(lo, wg_ref[0, :HALF]) + _dot(hi, wg_ref[0, HALF:])
        up = _dot(lo, wu_ref[0, :HALF]) + _dot(hi, wu_ref[0, HALF:])
        acts.append((jax.nn.silu(gate) * up).astype(BF16))
    for j in range(TILES_PER_STEP):
        ys_ref[j] = _pack_rows(_dot(acts[j], w_refs[3 * j + 2][0]))


def _grouped_ffn(tile_expert, xs, wg, wu, wd):
    n_rows, half = xs.shape
    n_steps = n_rows // (TILES_PER_STEP * ROW_TILE)
    row_spec = pl.BlockSpec((TILES_PER_STEP, ROW_TILE, half), lambda i, te: (0, i, 0))

    def exp_map(j):
        return lambda i, te: (te[j * n_steps + i], 0, 0)

    w_specs = [pl.BlockSpec((1,) + w.shape[1:], exp_map(j)) for j in range(TILES_PER_STEP) for w in (wg, wu, wd)]
    ys = pl.pallas_call(
        _grouped_ffn_kernel,
        grid_spec=pltpu.PrefetchScalarGridSpec(
            num_scalar_prefetch=1,
            grid=(n_steps,),
            in_specs=[row_spec] + w_specs,
            out_specs=row_spec),
        out_shape=jax.ShapeDtypeStruct((TILES_PER_STEP, n_steps * ROW_TILE, half), U32),
        compiler_params=pltpu.CompilerParams(dimension_semantics=("arbitrary",), vmem_limit_bytes=VMEM_LIMIT),
        name="grouped_ffn",
    )(tile_expert, xs.reshape(TILES_PER_STEP, n_steps * ROW_TILE, half), *([wg, wu, wd] * TILES_PER_STEP))
    return ys.reshape(n_rows, half)


def _combine_kernel(yt_ref, w_ref, x1_ref, p_ref, gffn_ref, wsg_ref, wsu_ref, wsd_ref,
                    gpi_ref, wpg_ref, wpp_ref, gpo_ref, *rest):
    out_ref = rest[-1]
    x1 = x1_ref[...]
    w = w_ref[...]
    acc_lo = jnp.zeros((x1.shape[0], HALF), F32)
    acc_hi = jnp.zeros((x1.shape[0], HALF), F32)
    for k in range(TOP_K):
        lo, hi = _unpack_rows(yt_ref[k])
        wk = w[:, k:k + 1]
        acc_lo += wk * lo
        acc_hi += wk * hi
    h = _rms(x1, gffn_ref[...]).astype(BF16)
    shared = _dot((jax.nn.silu(_dot(h, wsg_ref[...])) * _dot(h, wsu_ref[...])).astype(BF16), wsd_ref[...])
    x2 = x1 + shared + jnp.concatenate([acc_lo, acc_hi], axis=-1)
    gate = jax.nn.sigmoid(_dot(_rms(x2, gpi_ref[...]).astype(BF16), wpg_ref[...]))
    proj = _rms(_dot(p_ref[...].astype(BF16), wpp_ref[...]), gpo_ref[...])
    out_ref[...] = x2 + gate * proj


def _combine(yt, w_tok, x1, p, consts, tm, chunk, out_so_far):
    t, d = x1.shape
    first = chunk * (t // tm)
    row_spec = lambda w: pl.BlockSpec((tm, w), lambda i: (i, 0))
    full_spec = lambda w: pl.BlockSpec((tm, w), lambda i: (first + i, 0))
    carried = [] if out_so_far is None else [out_so_far]
    return pl.pallas_call(
        _combine_kernel,
        grid=(t // tm,),
        in_specs=[pl.BlockSpec((TOP_K, tm, HALF), lambda i: (0, i, 0)), row_spec(TOP_K), row_spec(d),
                  full_spec(PLE_DIM)] + [_const_spec(c.shape) for c in consts]
                 + [pl.BlockSpec(memory_space=pl.ANY)] * len(carried),
        out_specs=full_spec(d),
        out_shape=jax.ShapeDtypeStruct((t * MOE_CHUNKS, d), F32),
        input_output_aliases={4 + len(consts): 0} if carried else {},
        compiler_params=pltpu.CompilerParams(dimension_semantics=("arbitrary",), vmem_limit_bytes=VMEM_LIMIT),
        name="combine_ple",
    )(yt, w_tok, x1, p, *consts, *carried)


def _rotate_half(a):
    half = QK_ROPE // 2
    return jnp.concatenate([-a[..., half:], a[..., :half]], axis=-1)


def _with_rotate_half(w):
    return jnp.concatenate([w, _rotate_half(w[..., -QK_ROPE:])], axis=-1)


def _rope_gain_tables(g, scale):
    zeros = jnp.zeros((QK_ROPE,), F32)
    half = QK_ROPE // 2
    g_swapped = jnp.concatenate([g[half:], g[:half]])
    return (jnp.concatenate([g, zeros]) * scale).reshape(1, LANES), \
        (jnp.concatenate([zeros, g_swapped]) * scale).reshape(1, LANES)


def _layer(x, p, pos, g_mix, w_in, sgu_ln_g, sgu_ln_b, sgu_w, sgu_b, mla_g_qa, mla_w_qb, mla_g_kva, mla_w_kvb,
           qk_g_q_nope, qk_g_k_nope, qk_g_q_rope, qk_g_k_rope, w_o, g_ffn, w_router, b_router,
           w_exp_gate, w_exp_up, w_exp_down, w_sh_gate, w_sh_up, w_sh_down,
           g_ple_in, w_ple_gate, w_ple_proj, g_ple_out):
    b, s, d = x.shape
    t = b * s
    row = lambda a: a.reshape(1, -1)
    sizes = [d, d, Q_LORA, KV_LORA, QK_ROPE, d, d]
    offs = [0]
    for sz in sizes:
        offs.append(offs[-1] + sz)
    w_u, w_v, w_q, w_kv, w_r, w_ga, w_gb = [w_in[:, offs[i]:offs[i + 1]] for i in range(7)]
    w_r = _with_rotate_half(w_r)
    wqb = _with_rotate_half(mla_w_qb.reshape(Q_LORA, HEADS, QK_DIM)).reshape(Q_LORA, HEADS * QK_PAD)
    wkvb = mla_w_kvb.reshape(KV_LORA, HEADS, QK_NOPE + V_DIM)
    wkb = wkvb[:, :, :QK_NOPE].reshape(KV_LORA, HEADS * QK_NOPE)
    wvbt = wkvb[:, :, QK_NOPE:].reshape(KV_LORA, HEADS * V_DIM).T
    sgu_bias = jnp.repeat(sgu_b.T, d // SGU_GROUPS, axis=1)
    q_scale = QK_DIM ** -0.5

    consts1 = [row(g_mix), w_u, w_v, w_q, w_kv, w_r, w_ga, w_gb, row(sgu_ln_g), row(sgu_ln_b), sgu_w, sgu_bias,
               row(mla_g_qa), wqb, row(mla_g_kva), wkb, wvbt, row(qk_g_q_nope) * q_scale, row(qk_g_k_nope),
               *_rope_gain_tables(qk_g_q_rope, q_scale), *_rope_gain_tables(qk_g_k_rope, 1.0)]
    ma, gb, q, k, vt = _mixer_prep(x, pos.reshape(b, s, 1), consts1, tm=512)
    o = _attention(q, k, vt, tq=512, heads_per_step=2)

    consts3 = [w_o, row(g_ffn), w_router.T, b_router.reshape(-1, 1)]
    consts6 = [row(g_ffn), w_sh_gate, w_sh_up, w_sh_down, row(g_ple_in), w_ple_gate, w_ple_proj, row(g_ple_out)]
    x2d, ma2d, gb2d, o2d, p2d = (a.reshape(t, -1) for a in (x, ma, gb, o, p))
    tc = t // MOE_CHUNKS
    max_tiles = (tc * TOP_K) // ROW_TILE + N_EXPERTS
    tile_ids = jnp.arange(max_tiles, dtype=I32)
    out = None
    for chunk in range(MOE_CHUNKS):
        x1, h_packed, comb_t, rank_t, counts = _post_attn(x2d, ma2d, gb2d, o2d, consts3, tm=512, chunk=chunk)

        counts = counts[:, 0].astype(I32)
        tiles_per_expert = (counts + ROW_TILE - 1) // ROW_TILE
        tile_end = jnp.cumsum(tiles_per_expert)
        offsets = ((tile_end - tiles_per_expert) * ROW_TILE).astype(F32).reshape(N_EXPERTS, 1)
        tile_expert = jnp.minimum(jnp.sum((tile_end[None, :] <= tile_ids[:, None]).astype(I32), axis=1),
                                  N_EXPERTS - 1)

        pos_t, w_tok = _route_lists(comb_t, rank_t, offsets, tm=512)
        pos_win = pos_t.reshape(TOP_K, tc // SC_WINDOW, SC_WINDOW).transpose(1, 0, 2)

        xs = _sc_scatter_rows(h_packed, pos_win, max_tiles * ROW_TILE)
        ys = _grouped_ffn(tile_expert, xs, w_exp_gate, w_exp_up, w_exp_down)
        yt = _sc_gather_rows(ys, pos_win)
        out = _combine(yt, w_tok, x1, p2d, consts6, tm=512, chunk=chunk, out_so_far=out)
    return out.reshape(b, s, d)


def kernel(x, p, positions, g_mix, w_in, sgu_ln_g, sgu_ln_b, sgu_w, sgu_b, mla_g_qa, mla_w_qb, mla_g_kva, mla_w_kvb, qk_g_q_nope, qk_g_k_nope, qk_g_q_rope, qk_g_k_rope, w_o, g_ffn, w_router, b_router, w_exp_gate, w_exp_up, w_exp_down, w_sh_gate, w_sh_up, w_sh_down, g_ple_in, w_ple_gate, w_ple_proj, g_ple_out):
    params = (g_mix, w_in, sgu_ln_g, sgu_ln_b, sgu_w, sgu_b, mla_g_qa, mla_w_qb, mla_g_kva, mla_w_kvb,
              qk_g_q_nope, qk_g_k_nope, qk_g_q_rope, qk_g_k_rope, w_o, g_ffn, w_router, b_router,
              w_exp_gate, w_exp_up, w_exp_down, w_sh_gate, w_sh_up, w_sh_down,
              g_ple_in, w_ple_gate, w_ple_proj, g_ple_out)
    for l in range(g_mix.shape[0]):
        x = _layer(x, p[l], positions, *[a[l] for a in params])
    return x
```

```python
import functools
import math

import jax
import jax.numpy as jnp
from jax import lax
from jax.experimental import pallas as pl
from jax.experimental.pallas import tpu as pltpu
from jax.experimental.pallas import tpu_sc as plsc

D_MODEL = 1024
PLE_DIM = 256
SGU_CHUNK = 128
SGU_GROUPS = 8
V_DIM = 128
HEADS = 8
QK_NOPE = 128
QK_ROPE = 64
QK_DIM = QK_NOPE + QK_ROPE
QK_PAD = 256
Q_LORA = 384
KV_LORA = 256
ROPE_THETA = 10000.0
N_EXPERTS = 64
N_GROUPS = 8
EXPERTS_PER_GROUP = 8
TOPK_GROUPS = 4
TOP_K = 8
EXPERT_FF = 256
ROUTED_SCALE = 2.5
NORM_EPS = 1e-6
LN_EPS = 1e-5

LANES = 128
MXU_COLS = 256
VMEM_LIMIT = 56 * 1024 * 1024
SC_CORES = 2
SC_SUBCORES = 16
SC_WINDOW = 64
ROW_TILE = 256
TILES_PER_STEP = 4
MOE_CHUNKS = 2
HALF = D_MODEL // 2

F32 = jnp.float32
BF16 = jnp.bfloat16
U32 = jnp.uint32
I32 = jnp.int32


def _dot(a, b):
    return lax.dot_general(a, b, (((1,), (0,)), ((), ())), preferred_element_type=F32)


def _rms(xf, g, width=None):
    width = xf.shape[-1] if width is None else width
    ms = jnp.sum(xf * xf, axis=-1, keepdims=True) * (1.0 / width)
    return xf * lax.rsqrt(ms + NORM_EPS) * g


def _pack_rows(y):
    lo = pltpu.bitcast(y[:, :HALF].astype(BF16).astype(F32), U32) >> 16
    hi = pltpu.bitcast(y[:, HALF:].astype(BF16).astype(F32), U32) & jnp.uint32(0xFFFF0000)
    return lo | hi


def _unpack_rows(w):
    lo = pltpu.bitcast(w << 16, F32)
    hi = pltpu.bitcast(w & jnp.uint32(0xFFFF0000), F32)
    return lo, hi


def _rope_table(pos_f):
    lane = lax.broadcasted_iota(I32, (1, LANES), 1)
    freq = (lane % (QK_ROPE // 2)).astype(F32)
    inv_freq = jnp.exp(freq * (-math.log(ROPE_THETA) * 2.0 / QK_ROPE))
    phase = jnp.where(lane < QK_ROPE, 0.0, math.pi / 2)
    return jnp.cos(pos_f * inv_freq - phase)


def _norm_rope(piece, table_g):
    lane = lax.broadcasted_iota(I32, (1, LANES), 1)
    ssq = jnp.sum(jnp.where(lane < QK_ROPE, piece * piece, 0.0), axis=-1, keepdims=True)
    z = piece * lax.rsqrt(ssq * (1.0 / QK_ROPE) + NORM_EPS) * table_g
    return z + pltpu.roll(z, QK_ROPE, axis=1)


def _const_spec(shape):
    return pl.BlockSpec(shape, lambda *_: (0,) * len(shape), pipeline_mode=pl.Buffered(1))


def _mixer_prep_kernel(x_ref, pos_ref, g_mix_ref, wu_ref, wv_ref, wq_ref, wkv_ref, wr_ref, wga_ref, wgb_ref,
                       lng_ref, lnb_ref, sw_ref, sb_ref, gqa_ref, wqb_ref, gkva_ref, wkb_ref, wvbt_ref,
                       gqn_ref, gkn_ref, gqr_ref, gkr_ref,
                       ma_ref, gb_ref, q_ref, k_ref, vt_ref):
    tm = x_ref.shape[1]
    xn = _rms(x_ref[0], g_mix_ref[...]).astype(BF16)

    rope_t = _rope_table(pos_ref[0].astype(F32))
    qn = _rms(_dot(xn, wq_ref[...]), gqa_ref[...]).astype(BF16)
    kvn = _rms(_dot(xn, wkv_ref[...]), gkva_ref[...]).astype(BF16)
    lane = lax.broadcasted_iota(I32, (1, LANES), 1)
    kpe = jnp.where(lane < QK_ROPE, _norm_rope(_dot(xn, wr_ref[...]), rope_t * gkr_ref[...]), 0.0).astype(BF16)
    q_rope_t = rope_t * gqr_ref[...]

    gv = jax.nn.gelu(_dot(xn, wv_ref[...]))
    mu = jnp.mean(gv, axis=-1, keepdims=True)
    vc = gv - mu
    var = jnp.mean(vc * vc, axis=-1, keepdims=True)
    vn = (vc * lax.rsqrt(var + LN_EPS) * lng_ref[...] + lnb_ref[...]).astype(BF16)
    row = lax.broadcasted_iota(I32, (SGU_CHUNK, SGU_CHUNK), 0)
    col = lax.broadcasted_iota(I32, (SGU_CHUNK, SGU_CHUNK), 1)
    causal = col <= row
    n_chunks = tm // SGU_CHUNK
    for pair in range(SGU_GROUPS // 2):
        ps = slice(pair * MXU_COLS, (pair + 1) * MXU_COLS)
        k2 = _dot(kvn, wkb_ref[:, ps])
        v2t = lax.dot_general(wvbt_ref[ps, :].astype(BF16), kvn, (((1,), (1,)), ((), ())),
                              preferred_element_type=F32).astype(BF16)
        gu2 = jax.nn.gelu(_dot(xn, wu_ref[:, ps]))
        ga2 = jax.nn.sigmoid(_dot(xn, wga_ref[:, ps]))
        gb_ref[0, :, ps] = jax.nn.sigmoid(_dot(xn, wgb_ref[:, ps])).astype(BF16)
        for half in range(2):
            g = 2 * pair + half
            hs = slice(half * LANES, (half + 1) * LANES)
            cs = slice(g * SGU_CHUNK, (g + 1) * SGU_CHUNK)
            qh = _dot(qn, wqb_ref[:, g * QK_PAD:(g + 1) * QK_PAD])
            q_ref[0, g, :, :QK_NOPE] = _rms(qh[:, :QK_NOPE], gqn_ref[...]).astype(BF16)
            q_ref[0, g, :, QK_NOPE:] = _norm_rope(qh[:, QK_NOPE:], q_rope_t).astype(BF16)
            k_ref[0, g, :, :QK_NOPE] = _rms(k2[:, hs], gkn_ref[...]).astype(BF16)
            k_ref[0, g, :, QK_NOPE:] = kpe
            vt_ref[0, g] = v2t[hs, :]

            wg = jnp.where(causal, sw_ref[g], 0.0).astype(BF16)
            vcat = jnp.concatenate([vn[c * SGU_CHUNK:(c + 1) * SGU_CHUNK, cs] for c in range(n_chunks)], axis=1)
            mixed = _dot(wg, vcat)
            for c in range(n_chunks):
                rs = slice(c * SGU_CHUNK, (c + 1) * SGU_CHUNK)
                m = mixed[:, c * SGU_CHUNK:(c + 1) * SGU_CHUNK] + sb_ref[:, cs]
                ma_ref[0, rs, cs] = (ga2[rs, hs] * gu2[rs, hs] * m).astype(BF16)


def _mixer_prep(x, pos, consts, tm):
    b, s, d = x.shape
    grid = (b, s // tm)
    row_spec = lambda w: pl.BlockSpec((1, tm, w), lambda i, j: (i, j, 0))
    head_spec = lambda w: pl.BlockSpec((1, HEADS, tm, w), lambda i, j: (i, 0, j, 0))
    return pl.pallas_call(
        _mixer_prep_kernel,
        grid=grid,
        in_specs=[row_spec(d), row_spec(1)] + [_const_spec(c.shape) for c in consts],
        out_specs=[row_spec(d), row_spec(d), head_spec(QK_PAD), head_spec(QK_PAD),
                   pl.BlockSpec((1, HEADS, V_DIM, tm), lambda i, j: (i, 0, 0, j))],
        out_shape=[jax.ShapeDtypeStruct((b, s, d), BF16), jax.ShapeDtypeStruct((b, s, d), BF16),
                   jax.ShapeDtypeStruct((b, HEADS, s, QK_PAD), BF16),
                   jax.ShapeDtypeStruct((b, HEADS, s, QK_PAD), BF16),
                   jax.ShapeDtypeStruct((b, HEADS, V_DIM, s), BF16)],
        compiler_params=pltpu.CompilerParams(dimension_semantics=("arbitrary", "arbitrary"),
                                             vmem_limit_bytes=VMEM_LIMIT),
        name="mixer_prep",
    )(x, pos, *consts)


def _attn_kernel(q_ref, k_ref, vt_ref, *refs, tq):
    n_w = (len(refs) - 1) // 2
    o_ref = refs[n_w]
    for w_ref, wb_ref in zip(refs[:n_w], refs[n_w + 1:]):
        wb_ref[...] = w_ref[...].astype(BF16)
    s = q_ref.shape[2]
    key = lax.broadcasted_iota(I32, (tq, tq), 0)
    qry = lax.broadcasted_iota(I32, (tq, tq), 1)
    diag_mask = key <= qry
    for qi in range(s // tq):
        qs = slice(qi * tq, (qi + 1) * tq)
        n_keys = (qi + 1) * tq
        for h in range(q_ref.shape[1]):
            sc = lax.dot_general(k_ref[0, h, :n_keys, :], q_ref[0, h, qs, :], (((1,), (1,)), ((), ())),
                                 preferred_element_type=F32)
            last = jnp.where(diag_mask, sc[n_keys - tq:], -jnp.inf)
            sc = last if qi == 0 else jnp.concatenate([sc[:n_keys - tq], last], axis=0)
            m = jnp.max(sc, axis=0, keepdims=True)
            p = jnp.exp(sc - m)
            l = jnp.sum(p, axis=0, keepdims=True)
            acc = _dot(vt_ref[0, h, :, :n_keys], p.astype(BF16))
            o_ref[0, qs, h * V_DIM:(h + 1) * V_DIM] = (acc / l).T.astype(BF16)


def _attention(q, k, vt, expert_weights, tq, heads_per_step):
    b, h, s, _ = q.shape
    hp = heads_per_step
    n_steps = b * (h // hp)
    w_specs = [pl.BlockSpec((w.shape[0] // n_steps,) + w.shape[1:], lambda i, j: (i * (h // hp) + j, 0, 0))
               for w in expert_weights]
    return pl.pallas_call(
        functools.partial(_attn_kernel, tq=tq),
        grid=(b, h // hp),
        in_specs=[pl.BlockSpec((1, hp, s, QK_PAD), lambda i, j: (i, j, 0, 0)),
                  pl.BlockSpec((1, hp, s, QK_PAD), lambda i, j: (i, j, 0, 0)),
                  pl.BlockSpec((1, hp, V_DIM, s), lambda i, j: (i, j, 0, 0))] + w_specs,
        out_specs=[pl.BlockSpec((1, s, hp * V_DIM), lambda i, j: (i, 0, j))] + w_specs,
        out_shape=[jax.ShapeDtypeStruct((b, s, h * V_DIM), BF16)]
                  + [jax.ShapeDtypeStruct(w.shape, BF16) for w in expert_weights],
        compiler_params=pltpu.CompilerParams(dimension_semantics=("arbitrary", "arbitrary"),
                                             vmem_limit_bytes=VMEM_LIMIT),
        name="mla_attention",
    )(q, k, vt, *expert_weights)


def _route(sel, scores):
    e, tm = sel.shape
    sel3 = sel.reshape(N_GROUPS, EXPERTS_PER_GROUP, tm)
    sub = lax.broadcasted_iota(I32, sel3.shape, 1)
    m1 = jnp.max(sel3, axis=1, keepdims=True)
    first = jnp.min(jnp.where(sel3 == m1, sub, EXPERTS_PER_GROUP), axis=1, keepdims=True)
    m2 = jnp.max(jnp.where(sub == first, -jnp.inf, sel3), axis=1, keepdims=True)
    gscore = (m1 + m2).reshape(N_GROUPS, tm)
    gid = lax.broadcasted_iota(I32, (N_GROUPS, tm), 0)
    grank = jnp.zeros((N_GROUPS, tm), I32)
    for g in range(N_GROUPS):
        other = gscore[g:g + 1]
        grank += ((other > gscore) | ((other == gscore) & (g < gid))).astype(I32)
    gmask = grank < TOPK_GROUPS
    emask = jnp.broadcast_to(gmask[:, None, :], sel3.shape).reshape(e, tm)
    msel = jnp.where(emask, sel, -jnp.inf)
    groups = [msel[g * EXPERTS_PER_GROUP:(g + 1) * EXPERTS_PER_GROUP] for g in range(N_GROUPS)]
    ranks = [jnp.zeros((EXPERTS_PER_GROUP, tm), F32) for _ in range(N_GROUPS)]
    sub8 = lax.broadcasted_iota(I32, (EXPERTS_PER_GROUP, tm), 0)
    for j in range(e):
        jg, jr = divmod(j, EXPERTS_PER_GROUP)
        other = jnp.broadcast_to(groups[jg][jr:jr + 1], (EXPERTS_PER_GROUP, tm))
        for g in range(N_GROUPS):
            if g < jg:
                beats = other > groups[g]
            elif g > jg:
                beats = other >= groups[g]
            else:
                beats = (other > groups[g]) | ((other == groups[g]) & (jr < sub8))
            ranks[g] += jnp.where(beats, 1.0, 0.0)
    chosen = jnp.concatenate(ranks, axis=0) < float(TOP_K)
    w = jnp.where(chosen, scores, 0.0)
    return chosen, w / jnp.sum(w, axis=0, keepdims=True) * ROUTED_SCALE


def _post_attn_kernel(x_ref, ma_ref, gb_ref, o_ref, wo_ref, gffn_ref, wrt_ref, br_ref,
                      x1_ref, hp_ref, comb_ref, rank_ref, cnt_ref, run_ref):
    tm = x_ref.shape[0]

    @pl.when(pl.program_id(0) == 0)
    def _():
        run_ref[...] = jnp.zeros_like(run_ref)

    merged = ma_ref[...].astype(F32) + gb_ref[...].astype(F32) * o_ref[...].astype(F32)
    x1 = x_ref[...] + _dot(merged.astype(BF16), wo_ref[...])
    x1_ref[...] = x1
    h2 = _rms(x1, gffn_ref[...])
    hp_ref[...] = _pack_rows(h2)
    logits_t = lax.dot_general(wrt_ref[...], h2, (((1,), (1,)), ((), ())),
                               preferred_element_type=F32, precision=lax.Precision.HIGHEST)
    scores = jax.nn.sigmoid(logits_t)
    chosen, comb_t = _route(scores + br_ref[...], scores)
    comb_ref[...] = comb_t
    a = lax.broadcasted_iota(I32, (tm, tm), 0)
    b = lax.broadcasted_iota(I32, (tm, tm), 1)
    before = (a < b).astype(BF16)
    chosen_f = chosen.astype(F32)
    prefix = _dot(chosen_f.astype(BF16), before)
    run = run_ref[:, 0:1]
    rank_ref[...] = jnp.where(chosen, prefix + run, -1.0)
    run_ref[...] += jnp.sum(chosen_f, axis=1, keepdims=True)
    cnt_ref[...] = run_ref[...]


def _post_attn(x, ma, gb, o, consts, tm, chunk):
    t, d = x.shape[0] // MOE_CHUNKS, x.shape[1]
    first = chunk * (t // tm)
    row_spec = lambda w: pl.BlockSpec((tm, w), lambda i: (i, 0))
    col_spec = pl.BlockSpec((N_EXPERTS, tm), lambda i: (0, i))
    return pl.pallas_call(
        _post_attn_kernel,
        grid=(t // tm,),
        in_specs=[pl.BlockSpec((tm, d), lambda i: (first + i, 0))] * 4 + [_const_spec(c.shape) for c in consts],
        out_specs=[row_spec(d), row_spec(HALF), col_spec, col_spec, _const_spec((N_EXPERTS, LANES))],
        out_shape=[jax.ShapeDtypeStruct((t, d), F32), jax.ShapeDtypeStruct((t, HALF), U32),
                   jax.ShapeDtypeStruct((N_EXPERTS, t), F32), jax.ShapeDtypeStruct((N_EXPERTS, t), F32),
                   jax.ShapeDtypeStruct((N_EXPERTS, LANES), F32)],
        scratch_shapes=[pltpu.VMEM((N_EXPERTS, LANES), F32)],
        compiler_params=pltpu.CompilerParams(dimension_semantics=("arbitrary",), vmem_limit_bytes=VMEM_LIMIT),
        name="post_attn_router",
    )(x, ma, gb, o, *consts)


def _route_lists_kernel(comb_ref, rank_ref, off_ref, pos_ref, w_ref):
    rank = rank_ref[...]
    chosen = rank >= 0.0
    e = rank.shape[0]
    lower = (lax.broadcasted_iota(I32, (e, e), 1) < lax.broadcasted_iota(I32, (e, e), 0)).astype(BF16)
    slot = _dot(lower, chosen.astype(BF16))
    pos_full = rank + off_ref[...]
    comb = comb_ref[...]
    pos_rows, w_rows = [], []
    for k in range(TOP_K):
        pick = chosen & (slot == float(k))
        pos_rows.append(jnp.sum(jnp.where(pick, pos_full, 0.0), axis=0, keepdims=True))
        w_rows.append(jnp.sum(jnp.where(pick, comb, 0.0), axis=0, keepdims=True))
    pos_ref[...] = jnp.concatenate(pos_rows, axis=0).astype(I32)
    w_ref[...] = jnp.concatenate(w_rows, axis=0).T


def _route_lists(comb_t, rank_t, offsets, tm):
    e, t = comb_t.shape
    col_spec = pl.BlockSpec((e, tm), lambda i: (0, i))
    return pl.pallas_call(
        _route_lists_kernel,
        grid=(t // tm,),
        in_specs=[col_spec, col_spec, _const_spec((e, 1))],
        out_specs=[pl.BlockSpec((TOP_K, tm), lambda i: (0, i)), pl.BlockSpec((tm, TOP_K), lambda i: (i, 0))],
        out_shape=[jax.ShapeDtypeStruct((TOP_K, t), I32), jax.ShapeDtypeStruct((t, TOP_K), F32)],
        compiler_params=pltpu.CompilerParams(dimension_semantics=("arbitrary",)),
        name="route_lists",
    )(comb_t, rank_t, offsets)


def _sc_mesh():
    return plsc.VectorSubcoreMesh(core_axis_name="c", subcore_axis_name="s",
                                  num_cores=SC_CORES, num_subcores=SC_SUBCORES)


def _sc_worker():
    return lax.axis_index("s") * SC_CORES + lax.axis_index("c")


def _sc_scatter_rows(src, pos_win, n_rows):
    t, width = src.shape
    n_win = t // SC_WINDOW
    per_w = n_win // (SC_CORES * SC_SUBCORES)

    @functools.partial(
        pl.kernel, mesh=_sc_mesh(), out_type=jax.ShapeDtypeStruct((n_rows, width), src.dtype),
        scratch_types=[pltpu.VMEM((TOP_K, SC_WINDOW), I32), pltpu.VMEM((SC_WINDOW, width), src.dtype),
                       pltpu.SemaphoreType.DMA],
        name="sc_scatter_rows")
    def run(src_hbm, pos_hbm, out_hbm, idx_v, rows_v, sem):
        base = _sc_worker() * per_w

        @pl.loop(0, per_w)
        def _(j):
            w = base + j
            pltpu.sync_copy(pos_hbm.at[w], idx_v)
            pltpu.sync_copy(src_hbm.at[pl.ds(w * SC_WINDOW, SC_WINDOW)], rows_v)
            copies = [pltpu.async_copy(rows_v, out_hbm.at[idx_v.at[k]], sem) for k in range(TOP_K)]
            for c in copies:
                c.wait()

    return run(src, pos_win)


def _sc_gather_rows(table, pos_win):
    n_win = pos_win.shape[0]
    width = table.shape[1]
    per_w = n_win // (SC_CORES * SC_SUBCORES)

    @functools.partial(
        pl.kernel, mesh=_sc_mesh(),
        out_type=jax.ShapeDtypeStruct((TOP_K, n_win * SC_WINDOW, width), table.dtype),
        scratch_types=[pltpu.VMEM((TOP_K, SC_WINDOW), I32), pltpu.VMEM((2, SC_WINDOW, width), table.dtype),
                       pltpu.SemaphoreType.DMA, pltpu.SemaphoreType.DMA],
        name="sc_gather_rows")
    def run(table_hbm, pos_hbm, out_hbm, idx_v, rows_v, gsem, wsem):
        base = _sc_worker() * per_w

        @pl.loop(0, per_w)
        def _(j):
            w = base + j
            pltpu.sync_copy(pos_hbm.at[w], idx_v)
            for k in range(TOP_K):
                buf = rows_v.at[k % 2]
                pltpu.async_copy(table_hbm.at[idx_v.at[k]], buf, gsem).wait()
                pltpu.async_copy(buf, out_hbm.at[k, pl.ds(w * SC_WINDOW, SC_WINDOW)], wsem).wait()

    return run(table, pos_win)


def _grouped_ffn_kernel(te_ref, xs_ref, *refs):
    w_refs, ys_ref = refs[:-1], refs[-1]
    acts = []
    for j in range(TILES_PER_STEP):
        wg_ref, wu_ref = w_refs[3 * j:3 * j + 2]
        lo, hi = _unpack_rows(xs_ref[j])
        lo, hi = lo.astype(BF16), hi.astype(BF16)
        gate = _dot(lo, wg_ref[0, :HALF]) + _dot(hi, wg_ref[0, HALF:])
        up = _dot(lo, wu_ref[0, :HALF]) + _dot(hi, wu_ref[0, HALF:])
        acts.append((jax.nn.silu(gate) * up).astype(BF16))
    for j in range(TILES_PER_STEP):
        ys_ref[j] = _pack_rows(_dot(acts[j], w_refs[3 * j + 2][0]))


def _grouped_ffn(tile_expert, xs, wg, wu, wd):
    n_rows, half = xs.shape
    n_steps = n_rows // (TILES_PER_STEP * ROW_TILE)
    row_spec = pl.BlockSpec((TILES_PER_STEP, ROW_TILE, half), lambda i, te: (0, i, 0))

    def exp_map(j):
        return lambda i, te: (te[j * n_steps + i], 0, 0)

    w_specs = [pl.BlockSpec((1,) + w.shape[1:], exp_map(j)) for j in range(TILES_PER_STEP) for w in (wg, wu, wd)]
    ys = pl.pallas_call(
        _grouped_ffn_kernel,
        grid_spec=pltpu.PrefetchScalarGridSpec(
            num_scalar_prefetch=1,
            grid=(n_steps,),
            in_specs=[row_spec] + w_specs,
            out_specs=row_spec),
        out_shape=jax.ShapeDtypeStruct((TILES_PER_STEP, n_steps * ROW_TILE, half), U32),
        compiler_params=pltpu.CompilerParams(dimension_semantics=("arbitrary",), vmem_limit_bytes=VMEM_LIMIT),
        name="grouped_ffn",
    )(tile_expert, xs.reshape(TILES_PER_STEP, n_steps * ROW_TILE, half), *([wg, wu, wd] * TILES_PER_STEP))
    return ys.reshape(n_rows, half)


def _combine_kernel(yt_ref, w_ref, x1_ref, p_ref, gffn_ref, wsg_ref, wsu_ref, wsd_ref,
                    gpi_ref, wpg_ref, wpp_ref, gpo_ref, *rest):
    out_ref = rest[-1]
    x1 = x1_ref[...]
    w = w_ref[...]
    acc_lo = jnp.zeros((x1.shape[0], HALF), F32)
    acc_hi = jnp.zeros((x1.shape[0], HALF), F32)
    for k in range(TOP_K):
        lo, hi = _unpack_rows(yt_ref[k])
        wk = w[:, k:k + 1]
        acc_lo += wk * lo
        acc_hi += wk * hi
    h = _rms(x1, gffn_ref[...]).astype(BF16)
    shared = _dot((jax.nn.silu(_dot(h, wsg_ref[...])) * _dot(h, wsu_ref[...])).astype(BF16), wsd_ref[...])
    x2 = x1 + shared + jnp.concatenate([acc_lo, acc_hi], axis=-1)
    gate = jax.nn.sigmoid(_dot(_rms(x2, gpi_ref[...]).astype(BF16), wpg_ref[...]))
    proj = _rms(_dot(p_ref[...].astype(BF16), wpp_ref[...]), gpo_ref[...])
    out_ref[...] = x2 + gate * proj


def _combine(yt, w_tok, x1, p, consts, tm, chunk, out_so_far):
    t, d = x1.shape
    first = chunk * (t // tm)
    row_spec = lambda w: pl.BlockSpec((tm, w), lambda i: (i, 0))
    full_spec = lambda w: pl.BlockSpec((tm, w), lambda i: (first + i, 0))
    carried = [] if out_so_far is None else [out_so_far]
    return pl.pallas_call(
        _combine_kernel,
        grid=(t // tm,),
        in_specs=[pl.BlockSpec((TOP_K, tm, HALF), lambda i: (0, i, 0)), row_spec(TOP_K), row_spec(d),
                  full_spec(PLE_DIM)] + [_const_spec(c.shape) for c in consts]
                 + [pl.BlockSpec(memory_space=pl.ANY)] * len(carried),
        out_specs=full_spec(d),
        out_shape=jax.ShapeDtypeStruct((t * MOE_CHUNKS, d), F32),
        input_output_aliases={4 + len(consts): 0} if carried else {},
        compiler_params=pltpu.CompilerParams(dimension_semantics=("arbitrary",), vmem_limit_bytes=VMEM_LIMIT),
        name="combine_ple",
    )(yt, w_tok, x1, p, *consts, *carried)


def _rotate_half(a):
    half = QK_ROPE // 2
    return jnp.concatenate([-a[..., half:], a[..., :half]], axis=-1)


def _with_rotate_half(w):
    return jnp.concatenate([w, _rotate_half(w[..., -QK_ROPE:])], axis=-1)


def _rope_gain_table(g, scale):
    half = QK_ROPE // 2
    return (jnp.concatenate([g, g[half:], g[:half]]) * scale).reshape(1, LANES)


def _layer(x, p, pos, g_mix, w_in, sgu_ln_g, sgu_ln_b, sgu_w, sgu_b, mla_g_qa, mla_w_qb, mla_g_kva, mla_w_kvb,
           qk_g_q_nope, qk_g_k_nope, qk_g_q_rope, qk_g_k_rope, w_o, g_ffn, w_router, b_router,
           w_exp_gate, w_exp_up, w_exp_down, w_sh_gate, w_sh_up, w_sh_down,
           g_ple_in, w_ple_gate, w_ple_proj, g_ple_out):
    b, s, d = x.shape
    t = b * s
    row = lambda a: a.reshape(1, -1)
    sizes = [d, d, Q_LORA, KV_LORA, QK_ROPE, d, d]
    offs = [0]
    for sz in sizes:
        offs.append(offs[-1] + sz)
    w_u, w_v, w_q, w_kv, w_r, w_ga, w_gb = [w_in[:, offs[i]:offs[i + 1]] for i in range(7)]
    w_r = _with_rotate_half(w_r)
    wqb = _with_rotate_half(mla_w_qb.reshape(Q_LORA, HEADS, QK_DIM)).reshape(Q_LORA, HEADS * QK_PAD)
    wkvb = mla_w_kvb.reshape(KV_LORA, HEADS, QK_NOPE + V_DIM)
    wkb = wkvb[:, :, :QK_NOPE].reshape(KV_LORA, HEADS * QK_NOPE)
    wvbt = wkvb[:, :, QK_NOPE:].reshape(KV_LORA, HEADS * V_DIM).T
    sgu_bias = jnp.repeat(sgu_b.T, d // SGU_GROUPS, axis=1)
    q_scale = QK_DIM ** -0.5

    consts1 = [row(g_mix), w_u, w_v, w_q, w_kv, w_r, w_ga, w_gb, row(sgu_ln_g), row(sgu_ln_b), sgu_w, sgu_bias,
               row(mla_g_qa), wqb, row(mla_g_kva), wkb, wvbt, row(qk_g_q_nope) * q_scale, row(qk_g_k_nope),
               _rope_gain_table(qk_g_q_rope, q_scale), _rope_gain_table(qk_g_k_rope, 1.0)]
    ma, gb, q, k, vt = _mixer_prep(x, pos.reshape(b, s, 1), consts1, tm=512)
    o, w_gate_b, w_up_b, w_down_b = _attention(q, k, vt, (w_exp_gate, w_exp_up, w_exp_down), tq=512,
                                               heads_per_step=2)

    consts3 = [w_o, row(g_ffn), w_router.T, b_router.reshape(-1, 1)]
    consts6 = [row(g_ffn), w_sh_gate, w_sh_up, w_sh_down, row(g_ple_in), w_ple_gate, w_ple_proj, row(g_ple_out)]
    x2d, ma2d, gb2d, o2d, p2d = (a.reshape(t, -1) for a in (x, ma, gb, o, p))
    tc = t // MOE_CHUNKS
    max_tiles = (tc * TOP_K) // ROW_TILE + N_EXPERTS
    tile_ids = jnp.arange(max_tiles, dtype=I32)
    out = None
    for chunk in range(MOE_CHUNKS):
        x1, h_packed, comb_t, rank_t, counts = _post_attn(x2d, ma2d, gb2d, o2d, consts3, tm=512, chunk=chunk)

        counts = counts[:, 0].astype(I32)
        tiles_per_expert = (counts + ROW_TILE - 1) // ROW_TILE
        tile_end = jnp.cumsum(tiles_per_expert)
        offsets = ((tile_end - tiles_per_expert) * ROW_TILE).astype(F32).reshape(N_EXPERTS, 1)
        tile_expert = jnp.minimum(jnp.sum((tile_end[None, :] <= tile_ids[:, None]).astype(I32), axis=1),
                                  N_EXPERTS - 1)

        pos_t, w_tok = _route_lists(comb_t, rank_t, offsets, tm=512)
        pos_win = pos_t.reshape(TOP_K, tc // SC_WINDOW, SC_WINDOW).transpose(1, 0, 2)

        xs = _sc_scatter_rows(h_packed, pos_win, max_tiles * ROW_TILE)
        ys = _grouped_ffn(tile_expert, xs, w_gate_b, w_up_b, w_down_b)
        yt = _sc_gather_rows(ys, pos_win)
        out = _combine(yt, w_tok, x1, p2d, consts6, tm=512, chunk=chunk, out_so_far=out)
    return out.reshape(b, s, d)


def kernel(x, p, positions, g_mix, w_in, sgu_ln_g, sgu_ln_b, sgu_w, sgu_b, mla_g_qa, mla_w_qb, mla_g_kva, mla_w_kvb, qk_g_q_nope, qk_g_k_nope, qk_g_q_rope, qk_g_k_rope, w_o, g_ffn, w_router, b_router, w_exp_gate, w_exp_up, w_exp_down, w_sh_gate, w_sh_up, w_sh_down, g_ple_in, w_ple_gate, w_ple_proj, g_ple_out):
    params = (g_mix, w_in, sgu_ln_g, sgu_ln_b, sgu_w, sgu_b, mla_g_qa, mla_w_qb, mla_g_kva, mla_w_kvb,
              qk_g_q_nope, qk_g_k_nope, qk_g_q_rope, qk_g_k_rope, w_o, g_ffn, w_router, b_router,
              w_exp_gate, w_exp_up, w_exp_down, w_sh_gate, w_sh_up, w_sh_down,
              g_ple_in, w_ple_gate, w_ple_proj, g_ple_out)
    for l in range(g_mix.shape[0]):
        x = _layer(x, p[l], positions, *[a[l] for a in params])
    return x
```

```python
import functools
import math

import jax
import jax.numpy as jnp
from jax import lax
from jax.experimental import pallas as pl
from jax.experimental.pallas import tpu as pltpu
from jax.experimental.pallas import tpu_sc as plsc

D_MODEL = 1024
PLE_DIM = 256
SGU_CHUNK = 128
SGU_GROUPS = 8
V_DIM = 128
HEADS = 8
QK_NOPE = 128
QK_ROPE = 64
QK_DIM = QK_NOPE + QK_ROPE
QK_PAD = 256
Q_LORA = 384
KV_LORA = 256
ROPE_THETA = 10000.0
N_EXPERTS = 64
N_GROUPS = 8
EXPERTS_PER_GROUP = 8
TOPK_GROUPS = 4
TOP_K = 8
EXPERT_FF = 256
ROUTED_SCALE = 2.5
NORM_EPS = 1e-6
LN_EPS = 1e-5

LANES = 128
MXU_COLS = 256
VMEM_LIMIT = 56 * 1024 * 1024
SC_CORES = 2
SC_SUBCORES = 16
SC_WINDOW = 64
ROW_TILE = 256
TILES_PER_STEP = 4
MOE_CHUNKS = 2
HALF = D_MODEL // 2

F32 = jnp.float32
BF16 = jnp.bfloat16
U32 = jnp.uint32
I32 = jnp.int32


def _dot(a, b):
    return lax.dot_general(a, b, (((1,), (0,)), ((), ())), preferred_element_type=F32)


def _rms(xf, g, width=None):
    width = xf.shape[-1] if width is None else width
    ms = jnp.sum(xf * xf, axis=-1, keepdims=True) * (1.0 / width)
    return xf * lax.rsqrt(ms + NORM_EPS) * g


def _pack_rows(y):
    lo = pltpu.bitcast(y[:, :HALF].astype(BF16).astype(F32), U32) >> 16
    hi = pltpu.bitcast(y[:, HALF:].astype(BF16).astype(F32), U32) & jnp.uint32(0xFFFF0000)
    return lo | hi


def _unpack_rows(w):
    lo = pltpu.bitcast(w << 16, F32)
    hi = pltpu.bitcast(w & jnp.uint32(0xFFFF0000), F32)
    return lo, hi


def _rope_table(pos_f):
    lane = lax.broadcasted_iota(I32, (1, LANES), 1)
    freq = (lane % (QK_ROPE // 2)).astype(F32)
    inv_freq = jnp.exp(freq * (-math.log(ROPE_THETA) * 2.0 / QK_ROPE))
    phase = jnp.where(lane < QK_ROPE, 0.0, math.pi / 2)
    return jnp.cos(pos_f * inv_freq - phase)


def _norm_rope(piece, table_g):
    lane = lax.broadcasted_iota(I32, (1, LANES), 1)
    ssq = jnp.sum(jnp.where(lane < QK_ROPE, piece * piece, 0.0), axis=-1, keepdims=True)
    z = piece * lax.rsqrt(ssq * (1.0 / QK_ROPE) + NORM_EPS) * table_g
    return z + pltpu.roll(z, QK_ROPE, axis=1)


def _const_spec(shape):
    return pl.BlockSpec(shape, lambda *_: (0,) * len(shape), pipeline_mode=pl.Buffered(1))


def _mixer_prep_kernel(x_ref, pos_ref, g_mix_ref, wu_ref, wv_ref, wq_ref, wkv_ref, wr_ref, wga_ref, wgb_ref,
                       lng_ref, lnb_ref, sw_ref, sb_ref, gqa_ref, wqb_ref, gkva_ref, wkb_ref, wvbt_ref,
                       gqn_ref, gkn_ref, gqr_ref, gkr_ref,
                       ma_ref, gb_ref, q_ref, k_ref, vt_ref):
    tm = x_ref.shape[1]
    xn = _rms(x_ref[0], g_mix_ref[...]).astype(BF16)

    rope_t = _rope_table(pos_ref[0])
    qn = _rms(_dot(xn, wq_ref[...]), gqa_ref[...]).astype(BF16)
    kvn = _rms(_dot(xn, wkv_ref[...]), gkva_ref[...]).astype(BF16)
    lane = lax.broadcasted_iota(I32, (1, LANES), 1)
    kpe = jnp.where(lane < QK_ROPE, _norm_rope(_dot(xn, wr_ref[...]), rope_t * gkr_ref[...]), 0.0).astype(BF16)
    q_rope_t = rope_t * gqr_ref[...]

    gv = jax.nn.gelu(_dot(xn, wv_ref[...]))
    mu = jnp.mean(gv, axis=-1, keepdims=True)
    vc = gv - mu
    var = jnp.mean(vc * vc, axis=-1, keepdims=True)
    vn = (vc * lax.rsqrt(var + LN_EPS) * lng_ref[...] + lnb_ref[...]).astype(BF16)
    row = lax.broadcasted_iota(I32, (SGU_CHUNK, SGU_CHUNK), 0)
    col = lax.broadcasted_iota(I32, (SGU_CHUNK, SGU_CHUNK), 1)
    causal = col <= row
    n_chunks = tm // SGU_CHUNK
    for pair in range(SGU_GROUPS // 2):
        ps = slice(pair * MXU_COLS, (pair + 1) * MXU_COLS)
        k2 = _dot(kvn, wkb_ref[:, ps])
        v2t = lax.dot_general(wvbt_ref[ps, :].astype(BF16), kvn, (((1,), (1,)), ((), ())),
                              preferred_element_type=F32).astype(BF16)
        gu2 = jax.nn.gelu(_dot(xn, wu_ref[:, ps]))
        ga2 = jax.nn.sigmoid(_dot(xn, wga_ref[:, ps]))
        gb_ref[0, :, ps] = jax.nn.sigmoid(_dot(xn, wgb_ref[:, ps])).astype(BF16)
        for half in range(2):
            g = 2 * pair + half
            hs = slice(half * LANES, (half + 1) * LANES)
            cs = slice(g * SGU_CHUNK, (g + 1) * SGU_CHUNK)
            qh = _dot(qn, wqb_ref[:, g * QK_PAD:(g + 1) * QK_PAD])
            q_ref[0, g, :, :QK_NOPE] = _rms(qh[:, :QK_NOPE], gqn_ref[...]).astype(BF16)
            q_ref[0, g, :, QK_NOPE:] = _norm_rope(qh[:, QK_NOPE:], q_rope_t).astype(BF16)
            k_ref[0, g, :, :QK_NOPE] = _rms(k2[:, hs], gkn_ref[...]).astype(BF16)
            k_ref[0, g, :, QK_NOPE:] = kpe
            vt_ref[0, g] = v2t[hs, :]

            wg = jnp.where(causal, sw_ref[g], 0.0).astype(BF16)
            vcat = jnp.concatenate([vn[c * SGU_CHUNK:(c + 1) * SGU_CHUNK, cs] for c in range(n_chunks)], axis=1)
            mixed = _dot(wg, vcat)
            for c in range(n_chunks):
                rs = slice(c * SGU_CHUNK, (c + 1) * SGU_CHUNK)
                m = mixed[:, c * SGU_CHUNK:(c + 1) * SGU_CHUNK] + sb_ref[:, cs]
                ma_ref[0, rs, cs] = (ga2[rs, hs] * gu2[rs, hs] * m).astype(BF16)


def _mixer_prep(x, pos, consts, tm):
    b, s, d = x.shape
    grid = (b, s // tm)
    row_spec = lambda w: pl.BlockSpec((1, tm, w), lambda i, j: (i, j, 0))
    head_spec = lambda w: pl.BlockSpec((1, HEADS, tm, w), lambda i, j: (i, 0, j, 0))
    return pl.pallas_call(
        _mixer_prep_kernel,
        grid=grid,
        in_specs=[row_spec(d), row_spec(LANES)] + [_const_spec(c.shape) for c in consts],
        out_specs=[row_spec(d), row_spec(d), head_spec(QK_PAD), head_spec(QK_PAD),
                   pl.BlockSpec((1, HEADS, V_DIM, tm), lambda i, j: (i, 0, 0, j))],
        out_shape=[jax.ShapeDtypeStruct((b, s, d), BF16), jax.ShapeDtypeStruct((b, s, d), BF16),
                   jax.ShapeDtypeStruct((b, HEADS, s, QK_PAD), BF16),
                   jax.ShapeDtypeStruct((b, HEADS, s, QK_PAD), BF16),
                   jax.ShapeDtypeStruct((b, HEADS, V_DIM, s), BF16)],
        compiler_params=pltpu.CompilerParams(dimension_semantics=("arbitrary", "arbitrary"),
                                             vmem_limit_bytes=VMEM_LIMIT),
        name="mixer_prep",
    )(x, pos, *consts)


def _attn_kernel(q_ref, k_ref, vt_ref, *refs, tq):
    n_w = (len(refs) - 1) // 2
    o_ref = refs[n_w]
    for w_ref, wb_ref in zip(refs[:n_w], refs[n_w + 1:]):
        wb_ref[...] = w_ref[...].astype(BF16)
    s = q_ref.shape[2]
    key = lax.broadcasted_iota(I32, (tq, tq), 0)
    qry = lax.broadcasted_iota(I32, (tq, tq), 1)
    diag_mask = key <= qry
    for qi in range(s // tq):
        qs = slice(qi * tq, (qi + 1) * tq)
        n_keys = (qi + 1) * tq
        for h in range(q_ref.shape[1]):
            sc = lax.dot_general(k_ref[0, h, :n_keys, :], q_ref[0, h, qs, :], (((1,), (1,)), ((), ())),
                                 preferred_element_type=F32)
            last = jnp.where(diag_mask, sc[n_keys - tq:], -jnp.inf)
            sc = last if qi == 0 else jnp.concatenate([sc[:n_keys - tq], last], axis=0)
            m = jnp.max(sc, axis=0, keepdims=True)
            p = jnp.exp(sc - m)
            l = jnp.sum(p, axis=0, keepdims=True)
            acc = _dot(vt_ref[0, h, :, :n_keys], p.astype(BF16))
            o_ref[0, qs, h * V_DIM:(h + 1) * V_DIM] = (acc / l).T.astype(BF16)


def _attention(q, k, vt, expert_weights, tq, heads_per_step):
    b, h, s, _ = q.shape
    hp = heads_per_step
    n_steps = b * (h // hp)
    w_specs = [pl.BlockSpec((w.shape[0] // n_steps,) + w.shape[1:], lambda i, j: (i * (h // hp) + j, 0, 0))
               for w in expert_weights]
    return pl.pallas_call(
        functools.partial(_attn_kernel, tq=tq),
        grid=(b, h // hp),
        in_specs=[pl.BlockSpec((1, hp, s, QK_PAD), lambda i, j: (i, j, 0, 0)),
                  pl.BlockSpec((1, hp, s, QK_PAD), lambda i, j: (i, j, 0, 0)),
                  pl.BlockSpec((1, hp, V_DIM, s), lambda i, j: (i, j, 0, 0))] + w_specs,
        out_specs=[pl.BlockSpec((1, s, hp * V_DIM), lambda i, j: (i, 0, j))] + w_specs,
        out_shape=[jax.ShapeDtypeStruct((b, s, h * V_DIM), BF16)]
                  + [jax.ShapeDtypeStruct(w.shape, BF16) for w in expert_weights],
        compiler_params=pltpu.CompilerParams(dimension_semantics=("arbitrary", "arbitrary"),
                                             vmem_limit_bytes=VMEM_LIMIT),
        name="mla_attention",
    )(q, k, vt, *expert_weights)


def _route(sel, scores):
    e, tm = sel.shape
    sel3 = sel.reshape(N_GROUPS, EXPERTS_PER_GROUP, tm)
    sub = lax.broadcasted_iota(I32, sel3.shape, 1)
    m1 = jnp.max(sel3, axis=1, keepdims=True)
    first = jnp.min(jnp.where(sel3 == m1, sub, EXPERTS_PER_GROUP), axis=1, keepdims=True)
    m2 = jnp.max(jnp.where(sub == first, -jnp.inf, sel3), axis=1, keepdims=True)
    gscore = (m1 + m2).reshape(N_GROUPS, tm)
    gid = lax.broadcasted_iota(I32, (N_GROUPS, tm), 0)
    grank = jnp.zeros((N_GROUPS, tm), I32)
    for g in range(N_GROUPS):
        other = gscore[g:g + 1]
        grank += ((other > gscore) | ((other == gscore) & (g < gid))).astype(I32)
    gmask = grank < TOPK_GROUPS
    emask = jnp.broadcast_to(gmask[:, None, :], sel3.shape).reshape(e, tm)
    msel = jnp.where(emask, sel, -jnp.inf)
    groups = [msel[g * EXPERTS_PER_GROUP:(g + 1) * EXPERTS_PER_GROUP] for g in range(N_GROUPS)]
    ranks = [jnp.zeros((EXPERTS_PER_GROUP, tm), F32) for _ in range(N_GROUPS)]
    sub8 = lax.broadcasted_iota(I32, (EXPERTS_PER_GROUP, tm), 0)
    for j in range(e):
        jg, jr = divmod(j, EXPERTS_PER_GROUP)
        other = jnp.broadcast_to(groups[jg][jr:jr + 1], (EXPERTS_PER_GROUP, tm))
        for g in range(N_GROUPS):
            if g < jg:
                beats = other > groups[g]
            elif g > jg:
                beats = other >= groups[g]
            else:
                beats = (other > groups[g]) | ((other == groups[g]) & (jr < sub8))
            ranks[g] += jnp.where(beats, 1.0, 0.0)
    chosen = jnp.concatenate(ranks, axis=0) < float(TOP_K)
    w = jnp.where(chosen, scores, 0.0)
    return chosen, w / jnp.sum(w, axis=0, keepdims=True) * ROUTED_SCALE


def _post_attn_kernel(x_ref, ma_ref, gb_ref, o_ref, wo_ref, gffn_ref, wrt_ref, br_ref,
                      x1_ref, hp_ref, comb_ref, rank_ref, cnt_ref, run_ref):
    tm = x_ref.shape[0]

    @pl.when(pl.program_id(0) == 0)
    def _():
        run_ref[...] = jnp.zeros_like(run_ref)

    merged = ma_ref[...] + gb_ref[...] * o_ref[...]
    x1 = x_ref[...] + _dot(merged, wo_ref[...])
    x1_ref[...] = x1
    h2 = _rms(x1, gffn_ref[...])
    hp_ref[...] = _pack_rows(h2)
    logits_t = lax.dot_general(wrt_ref[...], h2, (((1,), (1,)), ((), ())),
                               preferred_element_type=F32, precision=lax.Precision.HIGHEST)
    scores = jax.nn.sigmoid(logits_t)
    chosen, comb_t = _route(scores + br_ref[...], scores)
    comb_ref[...] = comb_t
    a = lax.broadcasted_iota(I32, (tm, tm), 0)
    b = lax.broadcasted_iota(I32, (tm, tm), 1)
    before = (a < b).astype(BF16)
    chosen_f = chosen.astype(F32)
    prefix = _dot(chosen_f.astype(BF16), before)
    run = run_ref[:, 0:1]
    rank_ref[...] = jnp.where(chosen, prefix + run, -1.0)
    run_ref[...] += jnp.sum(chosen_f, axis=1, keepdims=True)
    cnt_ref[...] = run_ref[...]


def _post_attn(x, ma, gb, o, consts, tm, chunk):
    t, d = x.shape[0] // MOE_CHUNKS, x.shape[1]
    first = chunk * (t // tm)
    row_spec = lambda w: pl.BlockSpec((tm, w), lambda i: (i, 0))
    col_spec = pl.BlockSpec((N_EXPERTS, tm), lambda i: (0, i))
    return pl.pallas_call(
        _post_attn_kernel,
        grid=(t // tm,),
        in_specs=[pl.BlockSpec((tm, d), lambda i: (first + i, 0))] * 4 + [_const_spec(c.shape) for c in consts],
        out_specs=[row_spec(d), row_spec(HALF), col_spec, col_spec, _const_spec((N_EXPERTS, LANES))],
        out_shape=[jax.ShapeDtypeStruct((t, d), F32), jax.ShapeDtypeStruct((t, HALF), U32),
                   jax.ShapeDtypeStruct((N_EXPERTS, t), F32), jax.ShapeDtypeStruct((N_EXPERTS, t), F32),
                   jax.ShapeDtypeStruct((N_EXPERTS, LANES), F32)],
        scratch_shapes=[pltpu.VMEM((N_EXPERTS, LANES), F32)],
        compiler_params=pltpu.CompilerParams(dimension_semantics=("arbitrary",), vmem_limit_bytes=VMEM_LIMIT),
        name="post_attn_router",
    )(x, ma, gb, o, *consts)


def _route_lists_kernel(comb_ref, rank_ref, off_ref, lay_ref, pos_ref, w_ref):
    rank = rank_ref[...]
    chosen = rank >= 0.0
    e = rank.shape[0]
    lower = (lax.broadcasted_iota(I32, (e, e), 1) < lax.broadcasted_iota(I32, (e, e), 0)).astype(BF16)
    slot = _dot(lower, chosen.astype(BF16))
    pos_sorted = rank + off_ref[...]
    tile = jnp.floor(pos_sorted * (1.0 / ROW_TILE))
    pos_full = pos_sorted + jnp.floor((tile + 0.5) / lay_ref[0:1, 0:1]) * lay_ref[0:1, 1:2]
    comb = comb_ref[...]
    pos_rows, w_rows = [], []
    for k in range(TOP_K):
        pick = chosen & (slot == float(k))
        pos_rows.append(jnp.sum(jnp.where(pick, pos_full, 0.0), axis=0, keepdims=True))
        w_rows.append(jnp.sum(jnp.where(pick, comb, 0.0), axis=0, keepdims=True))
    pos_ref[...] = jnp.concatenate(pos_rows, axis=0).astype(I32)
    w_ref[...] = jnp.concatenate(w_rows, axis=0).T


def _route_lists(comb_t, rank_t, offsets, layout, tm):
    e, t = comb_t.shape
    col_spec = pl.BlockSpec((e, tm), lambda i: (0, i))
    return pl.pallas_call(
        _route_lists_kernel,
        grid=(t // tm,),
        in_specs=[col_spec, col_spec, _const_spec((e, 1)), _const_spec(layout.shape)],
        out_specs=[pl.BlockSpec((TOP_K, tm), lambda i: (0, i)), pl.BlockSpec((tm, TOP_K), lambda i: (i, 0))],
        out_shape=[jax.ShapeDtypeStruct((TOP_K, t), I32), jax.ShapeDtypeStruct((t, TOP_K), F32)],
        compiler_params=pltpu.CompilerParams(dimension_semantics=("arbitrary",)),
        name="route_lists",
    )(comb_t, rank_t, offsets, layout)


def _sc_mesh():
    return plsc.VectorSubcoreMesh(core_axis_name="c", subcore_axis_name="s",
                                  num_cores=SC_CORES, num_subcores=SC_SUBCORES)


def _sc_worker():
    return lax.axis_index("s") * SC_CORES + lax.axis_index("c")


def _sc_scatter_rows(src, pos_win, n_rows):
    t, width = src.shape
    n_win = t // SC_WINDOW
    per_w = n_win // (SC_CORES * SC_SUBCORES)

    @functools.partial(
        pl.kernel, mesh=_sc_mesh(), out_type=jax.ShapeDtypeStruct((n_rows, width), src.dtype),
        scratch_types=[pltpu.VMEM((TOP_K, SC_WINDOW), I32), pltpu.VMEM((SC_WINDOW, width), src.dtype),
                       pltpu.SemaphoreType.DMA],
        name="sc_scatter_rows")
    def run(src_hbm, pos_hbm, out_hbm, idx_v, rows_v, sem):
        base = _sc_worker() * per_w

        @pl.loop(0, per_w)
        def _(j):
            w = base + j
            pltpu.sync_copy(pos_hbm.at[w], idx_v)
            pltpu.sync_copy(src_hbm.at[pl.ds(w * SC_WINDOW, SC_WINDOW)], rows_v)
            copies = [pltpu.async_copy(rows_v, out_hbm.at[idx_v.at[k]], sem) for k in range(TOP_K)]
            for c in copies:
                c.wait()

    return run(src, pos_win)


def _sc_gather_rows(table, pos_win):
    n_win = pos_win.shape[0]
    width = table.shape[1]
    per_w = n_win // (SC_CORES * SC_SUBCORES)

    @functools.partial(
        pl.kernel, mesh=_sc_mesh(),
        out_type=jax.ShapeDtypeStruct((TOP_K, n_win * SC_WINDOW, width), table.dtype),
        scratch_types=[pltpu.VMEM((TOP_K, SC_WINDOW), I32), pltpu.VMEM((2, SC_WINDOW, width), table.dtype),
                       pltpu.SemaphoreType.DMA, pltpu.SemaphoreType.DMA],
        name="sc_gather_rows")
    def run(table_hbm, pos_hbm, out_hbm, idx_v, rows_v, gsem, wsem):
        base = _sc_worker() * per_w

        @pl.loop(0, per_w)
        def _(j):
            w = base + j
            pltpu.sync_copy(pos_hbm.at[w], idx_v)
            for k in range(TOP_K):
                buf = rows_v.at[k % 2]
                pltpu.async_copy(table_hbm.at[idx_v.at[k]], buf, gsem).wait()
                pltpu.async_copy(buf, out_hbm.at[k, pl.ds(w * SC_WINDOW, SC_WINDOW)], wsem).wait()

    return run(table, pos_win)


def _grouped_ffn_kernel(te_ref, used_ref, xs_ref, *refs):
    w_refs, ys_ref = refs[:-1], refs[-1]

    @pl.when(pl.program_id(0) < used_ref[0])
    def _():
        acts = []
        for j in range(TILES_PER_STEP):
            wg_ref, wu_ref = w_refs[3 * j:3 * j + 2]
            lo, hi = _unpack_rows(xs_ref[j])
            lo, hi = lo.astype(BF16), hi.astype(BF16)
            gate = _dot(lo, wg_ref[0, :HALF]) + _dot(hi, wg_ref[0, HALF:])
            up = _dot(lo, wu_ref[0, :HALF]) + _dot(hi, wu_ref[0, HALF:])
            acts.append((jax.nn.silu(gate) * up).astype(BF16))
        for j in range(TILES_PER_STEP):
            ys_ref[j] = _pack_rows(_dot(acts[j], w_refs[3 * j + 2][0]))


def _grouped_ffn(tile_expert, steps_used, xs, wg, wu, wd):
    n_rows, half = xs.shape
    n_steps = n_rows // (TILES_PER_STEP * ROW_TILE)

    def step_of(i, used):
        return jnp.minimum(i, used[0] - 1)

    row_spec = pl.BlockSpec((TILES_PER_STEP, ROW_TILE, half), lambda i, te, used: (0, step_of(i, used), 0))

    def exp_map(j):
        return lambda i, te, used: (te[j * n_steps + step_of(i, used)], 0, 0)

    w_specs = [pl.BlockSpec((1,) + w.shape[1:], exp_map(j)) for j in range(TILES_PER_STEP) for w in (wg, wu, wd)]
    ys = pl.pallas_call(
        _grouped_ffn_kernel,
        grid_spec=pltpu.PrefetchScalarGridSpec(
            num_scalar_prefetch=2,
            grid=(n_steps,),
            in_specs=[row_spec] + w_specs,
            out_specs=row_spec),
        out_shape=jax.ShapeDtypeStruct((TILES_PER_STEP, n_steps * ROW_TILE, half), U32),
        compiler_params=pltpu.CompilerParams(dimension_semantics=("arbitrary",), vmem_limit_bytes=VMEM_LIMIT),
        name="grouped_ffn",
    )(tile_expert, steps_used, xs.reshape(TILES_PER_STEP, n_steps * ROW_TILE, half),
      *([wg, wu, wd] * TILES_PER_STEP))
    return ys.reshape(n_rows, half)


def _combine_kernel(yt_ref, w_ref, x1_ref, p_ref, gffn_ref, wsg_ref, wsu_ref, wsd_ref,
                    gpi_ref, wpg_ref, wpp_ref, gpo_ref, *rest):
    out_ref = rest[-1]
    x1 = x1_ref[...]
    w = w_ref[...]
    acc_lo = jnp.zeros((x1.shape[0], HALF), F32)
    acc_hi = jnp.zeros((x1.shape[0], HALF), F32)
    for k in range(TOP_K):
        lo, hi = _unpack_rows(yt_ref[k])
        wk = w[:, k:k + 1]
        acc_lo += wk * lo
        acc_hi += wk * hi
    h = _rms(x1, gffn_ref[...]).astype(BF16)
    shared = _dot((jax.nn.silu(_dot(h, wsg_ref[...])) * _dot(h, wsu_ref[...])).astype(BF16), wsd_ref[...])
    x2 = x1 + shared + jnp.concatenate([acc_lo, acc_hi], axis=-1)
    gate = jax.nn.sigmoid(_dot(_rms(x2, gpi_ref[...]).astype(BF16), wpg_ref[...]))
    proj = _rms(_dot(p_ref[...].astype(BF16), wpp_ref[...]), gpo_ref[...])
    out_ref[...] = x2 + gate * proj


def _combine(yt, w_tok, x1, p, consts, tm, chunk, out_so_far):
    t, d = x1.shape
    first = chunk * (t // tm)
    row_spec = lambda w: pl.BlockSpec((tm, w), lambda i: (i, 0))
    full_spec = lambda w: pl.BlockSpec((tm, w), lambda i: (first + i, 0))
    carried = [] if out_so_far is None else [out_so_far]
    return pl.pallas_call(
        _combine_kernel,
        grid=(t // tm,),
        in_specs=[pl.BlockSpec((TOP_K, tm, HALF), lambda i: (0, i, 0)), row_spec(TOP_K), row_spec(d),
                  full_spec(PLE_DIM)] + [_const_spec(c.shape) for c in consts]
                 + [pl.BlockSpec(memory_space=pl.ANY)] * len(carried),
        out_specs=full_spec(d),
        out_shape=jax.ShapeDtypeStruct((t * MOE_CHUNKS, d), F32),
        input_output_aliases={4 + len(consts): 0} if carried else {},
        compiler_params=pltpu.CompilerParams(dimension_semantics=("arbitrary",), vmem_limit_bytes=VMEM_LIMIT),
        name="combine_ple",
    )(yt, w_tok, x1, p, *consts, *carried)


def _rotate_half(a):
    half = QK_ROPE // 2
    return jnp.concatenate([-a[..., half:], a[..., :half]], axis=-1)


def _with_rotate_half(w):
    return jnp.concatenate([w, _rotate_half(w[..., -QK_ROPE:])], axis=-1)


def _rope_gain_table(g, scale):
    half = QK_ROPE // 2
    return (jnp.concatenate([g, g[half:], g[:half]]) * scale).reshape(1, LANES)


def _layer(x, p, pos, g_mix, w_in, sgu_ln_g, sgu_ln_b, sgu_w, sgu_b, mla_g_qa, mla_w_qb, mla_g_kva, mla_w_kvb,
           qk_g_q_nope, qk_g_k_nope, qk_g_q_rope, qk_g_k_rope, w_o, g_ffn, w_router, b_router,
           w_exp_gate, w_exp_up, w_exp_down, w_sh_gate, w_sh_up, w_sh_down,
           g_ple_in, w_ple_gate, w_ple_proj, g_ple_out):
    b, s, d = x.shape
    t = b * s
    row = lambda a: a.reshape(1, -1)
    sizes = [d, d, Q_LORA, KV_LORA, QK_ROPE, d, d]
    offs = [0]
    for sz in sizes:
        offs.append(offs[-1] + sz)
    w_u, w_v, w_q, w_kv, w_r, w_ga, w_gb = [w_in[:, offs[i]:offs[i + 1]] for i in range(7)]
    w_r = _with_rotate_half(w_r)
    wqb = _with_rotate_half(mla_w_qb.reshape(Q_LORA, HEADS, QK_DIM)).reshape(Q_LORA, HEADS * QK_PAD)
    wkvb = mla_w_kvb.reshape(KV_LORA, HEADS, QK_NOPE + V_DIM)
    wkb = wkvb[:, :, :QK_NOPE].reshape(KV_LORA, HEADS * QK_NOPE)
    wvbt = wkvb[:, :, QK_NOPE:].reshape(KV_LORA, HEADS * V_DIM).T
    sgu_bias = jnp.repeat(sgu_b.T, d // SGU_GROUPS, axis=1)
    q_scale = QK_DIM ** -0.5

    consts1 = [row(g_mix), w_u, w_v, w_q, w_kv, w_r, w_ga, w_gb, row(sgu_ln_g), row(sgu_ln_b), sgu_w, sgu_bias,
               row(mla_g_qa), wqb, row(mla_g_kva), wkb, wvbt, row(qk_g_q_nope) * q_scale, row(qk_g_k_nope),
               _rope_gain_table(qk_g_q_rope, q_scale), _rope_gain_table(qk_g_k_rope, 1.0)]
    pos_lanes = jnp.broadcast_to(pos.astype(F32)[..., None], (b, s, LANES))
    ma, gb, q, k, vt = _mixer_prep(x, pos_lanes, consts1, tm=512)
    o, w_gate_b, w_up_b, w_down_b = _attention(q, k, vt, (w_exp_gate, w_exp_up, w_exp_down), tq=512,
                                               heads_per_step=2)

    consts3 = [w_o, row(g_ffn), w_router.T, b_router.reshape(-1, 1)]
    consts6 = [row(g_ffn), w_sh_gate, w_sh_up, w_sh_down, row(g_ple_in), w_ple_gate, w_ple_proj, row(g_ple_out)]
    x2d, ma2d, gb2d, o2d, p2d = (a.reshape(t, -1) for a in (x, ma, gb, o, p))
    tc = t // MOE_CHUNKS
    max_tiles = (tc * TOP_K) // ROW_TILE + N_EXPERTS
    tile_ids = jnp.arange(max_tiles, dtype=I32)
    n_steps = max_tiles // TILES_PER_STEP
    out = None
    for chunk in range(MOE_CHUNKS):
        x1, h_packed, comb_t, rank_t, counts = _post_attn(x2d, ma2d, gb2d, o2d, consts3, tm=512, chunk=chunk)

        counts = counts[:, 0].astype(I32)
        tiles_per_expert = (counts + ROW_TILE - 1) // ROW_TILE
        tile_end = jnp.cumsum(tiles_per_expert)
        offsets = ((tile_end - tiles_per_expert) * ROW_TILE).astype(F32).reshape(N_EXPERTS, 1)
        steps_used = (tile_end[-1:] + TILES_PER_STEP - 1) // TILES_PER_STEP
        sorted_tile = (tile_ids // n_steps) * steps_used + jnp.minimum(tile_ids % n_steps, steps_used - 1)
        tile_expert = jnp.minimum(jnp.sum((tile_end[None, :] <= sorted_tile[:, None]).astype(I32), axis=1),
                                  N_EXPERTS - 1)
        layout = jnp.concatenate([steps_used, (n_steps - steps_used) * ROW_TILE]).astype(F32).reshape(1, 2)

        pos_t, w_tok = _route_lists(comb_t, rank_t, offsets, layout, tm=512)
        pos_win = pos_t.reshape(TOP_K, tc // SC_WINDOW, SC_WINDOW).transpose(1, 0, 2)

        xs = _sc_scatter_rows(h_packed, pos_win, max_tiles * ROW_TILE)
        ys = _grouped_ffn(tile_expert, steps_used, xs, w_gate_b, w_up_b, w_down_b)
        yt = _sc_gather_rows(ys, pos_win)
        out = _combine(yt, w_tok, x1, p2d, consts6, tm=512, chunk=chunk, out_so_far=out)
    return out.reshape(b, s, d)


def kernel(x, p, positions, g_mix, w_in, sgu_ln_g, sgu_ln_b, sgu_w, sgu_b, mla_g_qa, mla_w_qb, mla_g_kva, mla_w_kvb, qk_g_q_nope, qk_g_k_nope, qk_g_q_rope, qk_g_k_rope, w_o, g_ffn, w_router, b_router, w_exp_gate, w_exp_up, w_exp_down, w_sh_gate, w_sh_up, w_sh_down, g_ple_in, w_ple_gate, w_ple_proj, g_ple_out):
    params = (g_mix, w_in, sgu_ln_g, sgu_ln_b, sgu_w, sgu_b, mla_g_qa, mla_w_qb, mla_g_kva, mla_w_kvb,
              qk_g_q_nope, qk_g_k_nope, qk_g_q_rope, qk_g_k_rope, w_o, g_ffn, w_router, b_router,
              w_exp_gate, w_exp_up, w_exp_down, w_sh_gate, w_sh_up, w_sh_down,
              g_ple_in, w_ple_gate, w_ple_proj, g_ple_out)
    for l in range(g_mix.shape[0]):
        x = _layer(x, p[l], positions, *[a[l] for a in params])
    return x
```

```python
import functools
import math

import jax
import jax.numpy as jnp
from jax import lax
from jax.experimental import pallas as pl
from jax.experimental.pallas import tpu as pltpu
from jax.experimental.pallas import tpu_sc as plsc

D_MODEL = 1024
PLE_DIM = 256
SGU_CHUNK = 128
SGU_GROUPS = 8
V_DIM = 128
HEADS = 8
QK_NOPE = 128
QK_ROPE = 64
QK_DIM = QK_NOPE + QK_ROPE
QK_PAD = 256
Q_LORA = 384
KV_LORA = 256
ROPE_THETA = 10000.0
N_EXPERTS = 64
N_GROUPS = 8
EXPERTS_PER_GROUP = 8
TOPK_GROUPS = 4
TOP_K = 8
EXPERT_FF = 256
ROUTED_SCALE = 2.5
NORM_EPS = 1e-6
LN_EPS = 1e-5

LANES = 128
MXU_COLS = 256
VMEM_LIMIT = 56 * 1024 * 1024
SC_CORES = 2
SC_SUBCORES = 16
SC_WINDOW = 64
ROW_TILE = 256
TILES_PER_STEP = 4
MOE_CHUNKS = 2
HALF = D_MODEL // 2

F32 = jnp.float32
BF16 = jnp.bfloat16
U32 = jnp.uint32
I32 = jnp.int32


def _dot(a, b):
    return lax.dot_general(a, b, (((1,), (0,)), ((), ())), preferred_element_type=F32)


def _rms(xf, g, width=None):
    width = xf.shape[-1] if width is None else width
    ms = jnp.sum(xf * xf, axis=-1, keepdims=True) * (1.0 / width)
    return xf * lax.rsqrt(ms + NORM_EPS) * g


def _pack_rows(y):
    lo = pltpu.bitcast(y[:, :HALF].astype(BF16).astype(F32), U32) >> 16
    hi = pltpu.bitcast(y[:, HALF:].astype(BF16).astype(F32), U32) & jnp.uint32(0xFFFF0000)
    return lo | hi


def _unpack_rows(w):
    lo = pltpu.bitcast(w << 16, F32)
    hi = pltpu.bitcast(w & jnp.uint32(0xFFFF0000), F32)
    return lo, hi


def _rope_table(pos_f):
    lane = lax.broadcasted_iota(I32, (1, LANES), 1)
    freq = (lane % (QK_ROPE // 2)).astype(F32)
    inv_freq = jnp.exp(freq * (-math.log(ROPE_THETA) * 2.0 / QK_ROPE))
    phase = jnp.where(lane < QK_ROPE, 0.0, math.pi / 2)
    return jnp.cos(pos_f * inv_freq - phase)


def _norm_rope(piece, table_g):
    lane = lax.broadcasted_iota(I32, (1, LANES), 1)
    ssq = jnp.sum(jnp.where(lane < QK_ROPE, piece * piece, 0.0), axis=-1, keepdims=True)
    z = piece * lax.rsqrt(ssq * (1.0 / QK_ROPE) + NORM_EPS) * table_g
    return z + pltpu.roll(z, QK_ROPE, axis=1)


def _const_spec(shape):
    return pl.BlockSpec(shape, lambda *_: (0,) * len(shape), pipeline_mode=pl.Buffered(1))


def _mixer_prep_kernel(x_ref, pos_ref, g_mix_ref, wm_ref, wr_ref, wga_ref, wgb_ref,
                       lng_ref, lnb_ref, sw_ref, sb_ref, gqa_ref, wqb_ref, gkva_ref, wkb_ref, wvbt_ref,
                       gqn_ref, gkn_ref, gqr_ref, gkr_ref,
                       ma_ref, gb_ref, q_ref, k_ref, vt_ref):
    tm = x_ref.shape[1]
    xn = _rms(x_ref[0], g_mix_ref[...]).astype(BF16)

    rope_t = _rope_table(pos_ref[0])
    d = x_ref.shape[2]
    qn = _rms(_dot(xn, wm_ref[:, 2 * d:2 * d + Q_LORA]), gqa_ref[...]).astype(BF16)
    kvn = _rms(_dot(xn, wm_ref[:, 2 * d + Q_LORA:]), gkva_ref[...]).astype(BF16)
    lane = lax.broadcasted_iota(I32, (1, LANES), 1)
    kpe = jnp.where(lane < QK_ROPE, _norm_rope(_dot(xn, wr_ref[...]), rope_t * gkr_ref[...]), 0.0).astype(BF16)
    q_rope_t = rope_t * gqr_ref[...]

    def head_pair(pair):
        ps = slice(pair * MXU_COLS, (pair + 1) * MXU_COLS)
        k2 = _dot(kvn, wkb_ref[:, ps])
        v2t = lax.dot_general(wvbt_ref[ps, :].astype(BF16), kvn, (((1,), (1,)), ((), ())),
                              preferred_element_type=F32).astype(BF16)
        for half in range(2):
            g = 2 * pair + half
            hs = slice(half * LANES, (half + 1) * LANES)
            qh = _dot(qn, wqb_ref[:, g * QK_PAD:(g + 1) * QK_PAD])
            q_ref[0, g, :, :QK_NOPE] = _rms(qh[:, :QK_NOPE], gqn_ref[...]).astype(BF16)
            q_ref[0, g, :, QK_NOPE:] = _norm_rope(qh[:, QK_NOPE:], q_rope_t).astype(BF16)
            k_ref[0, g, :, :QK_NOPE] = _rms(k2[:, hs], gkn_ref[...]).astype(BF16)
            k_ref[0, g, :, QK_NOPE:] = kpe
            vt_ref[0, g] = v2t[hs, :]

    head_pair(0)

    gv = jax.nn.gelu(_dot(xn, wm_ref[:, d:2 * d]))
    mu = jnp.mean(gv, axis=-1, keepdims=True)
    vc = gv - mu
    var = jnp.mean(vc * vc, axis=-1, keepdims=True)
    vn = (vc * lax.rsqrt(var + LN_EPS) * lng_ref[...] + lnb_ref[...]).astype(BF16)
    row = lax.broadcasted_iota(I32, (SGU_CHUNK, SGU_CHUNK), 0)
    col = lax.broadcasted_iota(I32, (SGU_CHUNK, SGU_CHUNK), 1)
    causal = col <= row
    n_chunks = tm // SGU_CHUNK

    def sgu_pair(pair):
        ps = slice(pair * MXU_COLS, (pair + 1) * MXU_COLS)
        gu2 = jax.nn.gelu(_dot(xn, wm_ref[:, ps]))
        ga2 = jax.nn.sigmoid(_dot(xn, wga_ref[:, ps]))
        gb_ref[0, :, ps] = jax.nn.sigmoid(_dot(xn, wgb_ref[:, ps])).astype(BF16)
        for half in range(2):
            g = 2 * pair + half
            hs = slice(half * LANES, (half + 1) * LANES)
            cs = slice(g * SGU_CHUNK, (g + 1) * SGU_CHUNK)
            wg = jnp.where(causal, sw_ref[g], 0.0).astype(BF16)
            vcat = jnp.concatenate([vn[c * SGU_CHUNK:(c + 1) * SGU_CHUNK, cs] for c in range(n_chunks)], axis=1)
            mixed = _dot(wg, vcat)
            for c in range(n_chunks):
                rs = slice(c * SGU_CHUNK, (c + 1) * SGU_CHUNK)
                m = mixed[:, c * SGU_CHUNK:(c + 1) * SGU_CHUNK] + sb_ref[:, cs]
                ma_ref[0, rs, cs] = (ga2[rs, hs] * gu2[rs, hs] * m).astype(BF16)

    n_pairs = SGU_GROUPS // 2
    for pair in range(n_pairs):
        if pair + 1 < n_pairs:
            head_pair(pair + 1)
        sgu_pair(pair)


def _mixer_prep(x, pos, consts, tm):
    b, s, d = x.shape
    grid = (b, s // tm)
    row_spec = lambda w: pl.BlockSpec((1, tm, w), lambda i, j: (i, j, 0))
    head_spec = lambda w: pl.BlockSpec((1, HEADS, tm, w), lambda i, j: (i, 0, j, 0))
    return pl.pallas_call(
        _mixer_prep_kernel,
        grid=grid,
        in_specs=[row_spec(d), row_spec(LANES)] + [_const_spec(c.shape) for c in consts],
        out_specs=[row_spec(d), row_spec(d), head_spec(QK_PAD), head_spec(QK_PAD),
                   pl.BlockSpec((1, HEADS, V_DIM, tm), lambda i, j: (i, 0, 0, j))],
        out_shape=[jax.ShapeDtypeStruct((b, s, d), BF16), jax.ShapeDtypeStruct((b, s, d), BF16),
                   jax.ShapeDtypeStruct((b, HEADS, s, QK_PAD), BF16),
                   jax.ShapeDtypeStruct((b, HEADS, s, QK_PAD), BF16),
                   jax.ShapeDtypeStruct((b, HEADS, V_DIM, s), BF16)],
        compiler_params=pltpu.CompilerParams(dimension_semantics=("arbitrary", "arbitrary"),
                                             vmem_limit_bytes=VMEM_LIMIT),
        name="mixer_prep",
    )(x, pos, *consts)


def _attn_kernel(q_ref, k_ref, vt_ref, *refs, tq):
    n_w = (len(refs) - 1) // 2
    o_ref = refs[n_w]
    for w_ref, wb_ref in zip(refs[:n_w], refs[n_w + 1:]):
        wb_ref[...] = w_ref[...].astype(BF16)
    s = q_ref.shape[2]
    key = lax.broadcasted_iota(I32, (tq, tq), 0)
    qry = lax.broadcasted_iota(I32, (tq, tq), 1)
    diag_mask = key <= qry
    for qi in range(s // tq):
        qs = slice(qi * tq, (qi + 1) * tq)
        n_keys = (qi + 1) * tq
        for h in range(q_ref.shape[1]):
            sc = lax.dot_general(k_ref[0, h, :n_keys, :], q_ref[0, h, qs, :], (((1,), (1,)), ((), ())),
                                 preferred_element_type=F32)
            last = jnp.where(diag_mask, sc[n_keys - tq:], -jnp.inf)
            sc = last if qi == 0 else jnp.concatenate([sc[:n_keys - tq], last], axis=0)
            m = jnp.max(sc, axis=0, keepdims=True)
            p = jnp.exp(sc - m)
            l = jnp.sum(p, axis=0, keepdims=True)
            acc = _dot(vt_ref[0, h, :, :n_keys], p.astype(BF16))
            o_ref[0, qs, h * V_DIM:(h + 1) * V_DIM] = (acc / l).T.astype(BF16)


def _attention(q, k, vt, expert_weights, tq, heads_per_step):
    b, h, s, _ = q.shape
    hp = heads_per_step
    n_steps = b * (h // hp)
    w_specs = [pl.BlockSpec((w.shape[0] // n_steps,) + w.shape[1:], lambda i, j: (i * (h // hp) + j, 0, 0))
               for w in expert_weights]
    return pl.pallas_call(
        functools.partial(_attn_kernel, tq=tq),
        grid=(b, h // hp),
        in_specs=[pl.BlockSpec((1, hp, s, QK_PAD), lambda i, j: (i, j, 0, 0)),
                  pl.BlockSpec((1, hp, s, QK_PAD), lambda i, j: (i, j, 0, 0)),
                  pl.BlockSpec((1, hp, V_DIM, s), lambda i, j: (i, j, 0, 0))] + w_specs,
        out_specs=[pl.BlockSpec((1, s, hp * V_DIM), lambda i, j: (i, 0, j))] + w_specs,
        out_shape=[jax.ShapeDtypeStruct((b, s, h * V_DIM), BF16)]
                  + [jax.ShapeDtypeStruct(w.shape, BF16) for w in expert_weights],
        compiler_params=pltpu.CompilerParams(dimension_semantics=("arbitrary", "arbitrary"),
                                             vmem_limit_bytes=VMEM_LIMIT),
        name="mla_attention",
    )(q, k, vt, *expert_weights)


def _route(sel, scores):
    e, tm = sel.shape
    sel3 = sel.reshape(N_GROUPS, EXPERTS_PER_GROUP, tm)
    sub = lax.broadcasted_iota(I32, sel3.shape, 1)
    m1 = jnp.max(sel3, axis=1, keepdims=True)
    first = jnp.min(jnp.where(sel3 == m1, sub, EXPERTS_PER_GROUP), axis=1, keepdims=True)
    m2 = jnp.max(jnp.where(sub == first, -jnp.inf, sel3), axis=1, keepdims=True)
    gscore = (m1 + m2).reshape(N_GROUPS, tm)
    gid = lax.broadcasted_iota(I32, (N_GROUPS, tm), 0)
    grank = jnp.zeros((N_GROUPS, tm), I32)
    for g in range(N_GROUPS):
        other = gscore[g:g + 1]
        grank += ((other > gscore) | ((other == gscore) & (g < gid))).astype(I32)
    gmask = grank < TOPK_GROUPS
    emask = jnp.broadcast_to(gmask[:, None, :], sel3.shape).reshape(e, tm)
    msel = jnp.where(emask, sel, -jnp.inf)
    groups = [msel[g * EXPERTS_PER_GROUP:(g + 1) * EXPERTS_PER_GROUP] for g in range(N_GROUPS)]
    ranks = [jnp.zeros((EXPERTS_PER_GROUP, tm), F32) for _ in range(N_GROUPS)]
    sub8 = lax.broadcasted_iota(I32, (EXPERTS_PER_GROUP, tm), 0)
    for j in range(e):
        jg, jr = divmod(j, EXPERTS_PER_GROUP)
        other = jnp.broadcast_to(groups[jg][jr:jr + 1], (EXPERTS_PER_GROUP, tm))
        for g in range(N_GROUPS):
            if g < jg:
                beats = other > groups[g]
            elif g > jg:
                beats = other >= groups[g]
            else:
                beats = (other > groups[g]) | ((other == groups[g]) & (jr < sub8))
            ranks[g] += jnp.where(beats, 1.0, 0.0)
    chosen = jnp.concatenate(ranks, axis=0) < float(TOP_K)
    w = jnp.where(chosen, scores, 0.0)
    return chosen, w / jnp.sum(w, axis=0, keepdims=True) * ROUTED_SCALE


def _post_attn_kernel(x_ref, ma_ref, gb_ref, o_ref, wo_ref, gffn_ref, wrt_ref, br_ref,
                      x1_ref, hp_ref, comb_ref, rank_ref, cnt_ref, run_ref):
    tm = x_ref.shape[0]

    @pl.when(pl.program_id(0) == 0)
    def _():
        run_ref[...] = jnp.zeros_like(run_ref)

    merged = ma_ref[...] + gb_ref[...] * o_ref[...]
    x1 = x_ref[...] + _dot(merged, wo_ref[...])
    x1_ref[...] = x1
    h2 = _rms(x1, gffn_ref[...])
    hp_ref[...] = _pack_rows(h2)
    logits_t = lax.dot_general(wrt_ref[...], h2, (((1,), (1,)), ((), ())),
                               preferred_element_type=F32, precision=lax.Precision.HIGHEST)
    scores = jax.nn.sigmoid(logits_t)
    chosen, comb_t = _route(scores + br_ref[...], scores)
    comb_ref[...] = comb_t
    a = lax.broadcasted_iota(I32, (tm, tm), 0)
    b = lax.broadcasted_iota(I32, (tm, tm), 1)
    before = (a < b).astype(BF16)
    chosen_f = chosen.astype(F32)
    prefix = _dot(chosen_f.astype(BF16), before)
    run = run_ref[:, 0:1]
    rank_ref[...] = jnp.where(chosen, prefix + run, -1.0)
    run_ref[...] += jnp.sum(chosen_f, axis=1, keepdims=True)
    cnt_ref[...] = run_ref[...]


def _post_attn(x, ma, gb, o, consts, tm, chunk):
    t, d = x.shape[0] // MOE_CHUNKS, x.shape[1]
    first = chunk * (t // tm)
    row_spec = lambda w: pl.BlockSpec((tm, w), lambda i: (i, 0))
    col_spec = pl.BlockSpec((N_EXPERTS, tm), lambda i: (0, i))
    return pl.pallas_call(
        _post_attn_kernel,
        grid=(t // tm,),
        in_specs=[pl.BlockSpec((tm, d), lambda i: (first + i, 0))] * 4 + [_const_spec(c.shape) for c in consts],
        out_specs=[row_spec(d), row_spec(HALF), col_spec, col_spec, _const_spec((N_EXPERTS, LANES))],
        out_shape=[jax.ShapeDtypeStruct((t, d), F32), jax.ShapeDtypeStruct((t, HALF), U32),
                   jax.ShapeDtypeStruct((N_EXPERTS, t), F32), jax.ShapeDtypeStruct((N_EXPERTS, t), F32),
                   jax.ShapeDtypeStruct((N_EXPERTS, LANES), F32)],
        scratch_shapes=[pltpu.VMEM((N_EXPERTS, LANES), F32)],
        compiler_params=pltpu.CompilerParams(dimension_semantics=("arbitrary",), vmem_limit_bytes=VMEM_LIMIT),
        name="post_attn_router",
    )(x, ma, gb, o, *consts)


def _route_lists_kernel(comb_ref, rank_ref, off_ref, lay_ref, pos_ref, w_ref):
    rank = rank_ref[...]
    chosen = rank >= 0.0
    e = rank.shape[0]
    lower = (lax.broadcasted_iota(I32, (e, e), 1) < lax.broadcasted_iota(I32, (e, e), 0)).astype(BF16)
    slot = _dot(lower, chosen.astype(BF16))
    pos_sorted = rank + off_ref[...]
    tile = jnp.floor(pos_sorted * (1.0 / ROW_TILE))
    pos_full = pos_sorted + jnp.floor((tile + 0.5) / lay_ref[0:1, 0:1]) * lay_ref[0:1, 1:2]
    comb = comb_ref[...]
    pos_rows, w_rows = [], []
    for k in range(TOP_K):
        pick = chosen & (slot == float(k))
        pos_rows.append(jnp.sum(jnp.where(pick, pos_full, 0.0), axis=0, keepdims=True))
        w_rows.append(jnp.sum(jnp.where(pick, comb, 0.0), axis=0, keepdims=True))
    pos_ref[...] = jnp.concatenate(pos_rows, axis=0).astype(I32)
    w_ref[...] = jnp.concatenate(w_rows, axis=0).T


def _route_lists(comb_t, rank_t, offsets, layout, tm):
    e, t = comb_t.shape
    col_spec = pl.BlockSpec((e, tm), lambda i: (0, i))
    return pl.pallas_call(
        _route_lists_kernel,
        grid=(t // tm,),
        in_specs=[col_spec, col_spec, _const_spec((e, 1)), _const_spec(layout.shape)],
        out_specs=[pl.BlockSpec((TOP_K, tm), lambda i: (0, i)), pl.BlockSpec((tm, TOP_K), lambda i: (i, 0))],
        out_shape=[jax.ShapeDtypeStruct((TOP_K, t), I32), jax.ShapeDtypeStruct((t, TOP_K), F32)],
        compiler_params=pltpu.CompilerParams(dimension_semantics=("arbitrary",)),
        name="route_lists",
    )(comb_t, rank_t, offsets, layout)


def _sc_mesh():
    return plsc.VectorSubcoreMesh(core_axis_name="c", subcore_axis_name="s",
                                  num_cores=SC_CORES, num_subcores=SC_SUBCORES)


def _sc_worker():
    return lax.axis_index("s") * SC_CORES + lax.axis_index("c")


def _sc_scatter_rows(src, pos_win, n_rows):
    t, width = src.shape
    n_win = t // SC_WINDOW
    per_w = n_win // (SC_CORES * SC_SUBCORES)

    @functools.partial(
        pl.kernel, mesh=_sc_mesh(), out_type=jax.ShapeDtypeStruct((n_rows, width), src.dtype),
        scratch_types=[pltpu.VMEM((TOP_K, SC_WINDOW), I32), pltpu.VMEM((SC_WINDOW, width), src.dtype),
                       pltpu.SemaphoreType.DMA],
        name="sc_scatter_rows")
    def run(src_hbm, pos_hbm, out_hbm, idx_v, rows_v, sem):
        base = _sc_worker() * per_w

        @pl.loop(0, per_w)
        def _(j):
            w = base + j
            pltpu.sync_copy(pos_hbm.at[w], idx_v)
            pltpu.sync_copy(src_hbm.at[pl.ds(w * SC_WINDOW, SC_WINDOW)], rows_v)
            copies = [pltpu.async_copy(rows_v, out_hbm.at[idx_v.at[k]], sem) for k in range(TOP_K)]
            for c in copies:
                c.wait()

    return run(src, pos_win)


def _sc_gather_rows(table, pos_win):
    n_win = pos_win.shape[0]
    width = table.shape[1]
    per_w = n_win // (SC_CORES * SC_SUBCORES)

    @functools.partial(
        pl.kernel, mesh=_sc_mesh(),
        out_type=jax.ShapeDtypeStruct((TOP_K, n_win * SC_WINDOW, width), table.dtype),
        scratch_types=[pltpu.VMEM((TOP_K, SC_WINDOW), I32), pltpu.VMEM((2, SC_WINDOW, width), table.dtype),
                       pltpu.SemaphoreType.DMA, pltpu.SemaphoreType.DMA],
        name="sc_gather_rows")
    def run(table_hbm, pos_hbm, out_hbm, idx_v, rows_v, gsem, wsem):
        base = _sc_worker() * per_w

        @pl.loop(0, per_w)
        def _(j):
            w = base + j
            pltpu.sync_copy(pos_hbm.at[w], idx_v)
            for k in range(TOP_K):
                buf = rows_v.at[k % 2]
                pltpu.async_copy(table_hbm.at[idx_v.at[k]], buf, gsem).wait()
                pltpu.async_copy(buf, out_hbm.at[k, pl.ds(w * SC_WINDOW, SC_WINDOW)], wsem).wait()

    return run(table, pos_win)


def _grouped_ffn_kernel(te_ref, used_ref, xs_ref, *refs):
    w_refs, ys_ref = refs[:-1], refs[-1]

    @pl.when(pl.program_id(0) < used_ref[0])
    def _():
        acts = []
        for j in range(TILES_PER_STEP):
            wg_ref, wu_ref = w_refs[3 * j:3 * j + 2]
            lo, hi = _unpack_rows(xs_ref[j])
            lo, hi = lo.astype(BF16), hi.astype(BF16)
            gate = _dot(lo, wg_ref[0, :HALF]) + _dot(hi, wg_ref[0, HALF:])
            up = _dot(lo, wu_ref[0, :HALF]) + _dot(hi, wu_ref[0, HALF:])
            acts.append((jax.nn.silu(gate) * up).astype(BF16))
        for j in range(TILES_PER_STEP):
            ys_ref[j] = _pack_rows(_dot(acts[j], w_refs[3 * j + 2][0]))


def _grouped_ffn(tile_expert, steps_used, xs, wg, wu, wd):
    n_rows, half = xs.shape
    n_steps = n_rows // (TILES_PER_STEP * ROW_TILE)

    def step_of(i, used):
        return jnp.minimum(i, used[0] - 1)

    row_spec = pl.BlockSpec((TILES_PER_STEP, ROW_TILE, half), lambda i, te, used: (0, step_of(i, used), 0))

    def exp_map(j):
        return lambda i, te, used: (te[j * n_steps + step_of(i, used)], 0, 0)

    w_specs = [pl.BlockSpec((1,) + w.shape[1:], exp_map(j)) for j in range(TILES_PER_STEP) for w in (wg, wu, wd)]
    ys = pl.pallas_call(
        _grouped_ffn_kernel,
        grid_spec=pltpu.PrefetchScalarGridSpec(
            num_scalar_prefetch=2,
            grid=(n_steps,),
            in_specs=[row_spec] + w_specs,
            out_specs=row_spec),
        out_shape=jax.ShapeDtypeStruct((TILES_PER_STEP, n_steps * ROW_TILE, half), U32),
        compiler_params=pltpu.CompilerParams(dimension_semantics=("arbitrary",), vmem_limit_bytes=VMEM_LIMIT),
        name="grouped_ffn",
    )(tile_expert, steps_used, xs.reshape(TILES_PER_STEP, n_steps * ROW_TILE, half),
      *([wg, wu, wd] * TILES_PER_STEP))
    return ys.reshape(n_rows, half)


def _combine_kernel(yt_ref, w_ref, x1_ref, p_ref, gffn_ref, wsg_ref, wsu_ref, wsd_ref,
                    gpi_ref, wpg_ref, wpp_ref, gpo_ref, *rest):
    out_ref = rest[-1]
    x1 = x1_ref[...]
    w = w_ref[...]
    acc_lo = jnp.zeros((x1.shape[0], HALF), F32)
    acc_hi = jnp.zeros((x1.shape[0], HALF), F32)
    for k in range(TOP_K):
        lo, hi = _unpack_rows(yt_ref[k])
        wk = w[:, k:k + 1]
        acc_lo += wk * lo
        acc_hi += wk * hi
    h = _rms(x1, gffn_ref[...]).astype(BF16)
    shared = _dot((jax.nn.silu(_dot(h, wsg_ref[...])) * _dot(h, wsu_ref[...])).astype(BF16), wsd_ref[...])
    x2 = x1 + shared + jnp.concatenate([acc_lo, acc_hi], axis=-1)
    gate = jax.nn.sigmoid(_dot(_rms(x2, gpi_ref[...]).astype(BF16), wpg_ref[...]))
    proj = _rms(_dot(p_ref[...].astype(BF16), wpp_ref[...]), gpo_ref[...])
    out_ref[...] = x2 + gate * proj


def _combine(yt, w_tok, x1, p, consts, tm, chunk, out_so_far):
    t, d = x1.shape
    first = chunk * (t // tm)
    row_spec = lambda w: pl.BlockSpec((tm, w), lambda i: (i, 0))
    full_spec = lambda w: pl.BlockSpec((tm, w), lambda i: (first + i, 0))
    carried = [] if out_so_far is None else [out_so_far]
    return pl.pallas_call(
        _combine_kernel,
        grid=(t // tm,),
        in_specs=[pl.BlockSpec((TOP_K, tm, HALF), lambda i: (0, i, 0)), row_spec(TOP_K), row_spec(d),
                  full_spec(PLE_DIM)] + [_const_spec(c.shape) for c in consts]
                 + [pl.BlockSpec(memory_space=pl.ANY)] * len(carried),
        out_specs=full_spec(d),
        out_shape=jax.ShapeDtypeStruct((t * MOE_CHUNKS, d), F32),
        input_output_aliases={4 + len(consts): 0} if carried else {},
        compiler_params=pltpu.CompilerParams(dimension_semantics=("arbitrary",), vmem_limit_bytes=VMEM_LIMIT),
        name="combine_ple",
    )(yt, w_tok, x1, p, *consts, *carried)


def _rotate_half(a):
    half = QK_ROPE // 2
    return jnp.concatenate([-a[..., half:], a[..., :half]], axis=-1)


def _with_rotate_half(w):
    return jnp.concatenate([w, _rotate_half(w[..., -QK_ROPE:])], axis=-1)


def _rope_gain_table(g, scale):
    half = QK_ROPE // 2
    return (jnp.concatenate([g, g[half:], g[:half]]) * scale).reshape(1, LANES)


def _layer(x, p, pos, g_mix, w_in, sgu_ln_g, sgu_ln_b, sgu_w, sgu_b, mla_g_qa, mla_w_qb, mla_g_kva, mla_w_kvb,
           qk_g_q_nope, qk_g_k_nope, qk_g_q_rope, qk_g_k_rope, w_o, g_ffn, w_router, b_router,
           w_exp_gate, w_exp_up, w_exp_down, w_sh_gate, w_sh_up, w_sh_down,
           g_ple_in, w_ple_gate, w_ple_proj, g_ple_out):
    b, s, d = x.shape
    t = b * s
    row = lambda a: a.reshape(1, -1)
    sizes = [d, d, Q_LORA, KV_LORA, QK_ROPE, d, d]
    offs = [0]
    for sz in sizes:
        offs.append(offs[-1] + sz)
    w_main = w_in[:, :offs[4]]
    w_r, w_ga, w_gb = [w_in[:, offs[i]:offs[i + 1]] for i in (4, 5, 6)]
    w_r = _with_rotate_half(w_r)
    wqb = _with_rotate_half(mla_w_qb.reshape(Q_LORA, HEADS, QK_DIM)).reshape(Q_LORA, HEADS * QK_PAD)
    wkvb = mla_w_kvb.reshape(KV_LORA, HEADS, QK_NOPE + V_DIM)
    wkb = wkvb[:, :, :QK_NOPE].reshape(KV_LORA, HEADS * QK_NOPE)
    wvbt = wkvb[:, :, QK_NOPE:].reshape(KV_LORA, HEADS * V_DIM).T
    sgu_bias = jnp.repeat(sgu_b.T, d // SGU_GROUPS, axis=1)
    q_scale = QK_DIM ** -0.5

    consts1 = [row(g_mix), w_main, w_r, w_ga, w_gb, row(sgu_ln_g), row(sgu_ln_b), sgu_w, sgu_bias,
               row(mla_g_qa), wqb, row(mla_g_kva), wkb, wvbt, row(qk_g_q_nope) * q_scale, row(qk_g_k_nope),
               _rope_gain_table(qk_g_q_rope, q_scale), _rope_gain_table(qk_g_k_rope, 1.0)]
    pos_lanes = jnp.broadcast_to(pos.astype(F32)[..., None], (b, s, LANES))
    ma, gb, q, k, vt = _mixer_prep(x, pos_lanes, consts1, tm=512)
    o, w_gate_b, w_up_b, w_down_b = _attention(q, k, vt, (w_exp_gate, w_exp_up, w_exp_down), tq=512,
                                               heads_per_step=2)

    consts3 = [w_o, row(g_ffn), w_router.T, b_router.reshape(-1, 1)]
    consts6 = [row(g_ffn), w_sh_gate, w_sh_up, w_sh_down, row(g_ple_in), w_ple_gate, w_ple_proj, row(g_ple_out)]
    x2d, ma2d, gb2d, o2d, p2d = (a.reshape(t, -1) for a in (x, ma, gb, o, p))
    tc = t // MOE_CHUNKS
    max_tiles = (tc * TOP_K) // ROW_TILE + N_EXPERTS
    tile_ids = jnp.arange(max_tiles, dtype=I32)
    n_steps = max_tiles // TILES_PER_STEP
    out = None
    for chunk in range(MOE_CHUNKS):
        x1, h_packed, comb_t, rank_t, counts = _post_attn(x2d, ma2d, gb2d, o2d, consts3, tm=512, chunk=chunk)

        counts = counts[:, 0].astype(I32)
        tiles_per_expert = (counts + ROW_TILE - 1) // ROW_TILE
        tile_end = jnp.cumsum(tiles_per_expert)
        offsets = ((tile_end - tiles_per_expert) * ROW_TILE).astype(F32).reshape(N_EXPERTS, 1)
        steps_used = (tile_end[-1:] + TILES_PER_STEP - 1) // TILES_PER_STEP
        sorted_tile = (tile_ids // n_steps) * steps_used + jnp.minimum(tile_ids % n_steps, steps_used - 1)
        tile_expert = jnp.minimum(jnp.sum((tile_end[None, :] <= sorted_tile[:, None]).astype(I32), axis=1),
                                  N_EXPERTS - 1)
        layout = jnp.concatenate([steps_used, (n_steps - steps_used) * ROW_TILE]).astype(F32).reshape(1, 2)

        pos_t, w_tok = _route_lists(comb_t, rank_t, offsets, layout, tm=512)
        pos_win = pos_t.reshape(TOP_K, tc // SC_WINDOW, SC_WINDOW).transpose(1, 0, 2)

        xs = _sc_scatter_rows(h_packed, pos_win, max_tiles * ROW_TILE)
        ys = _grouped_ffn(tile_expert, steps_used, xs, w_gate_b, w_up_b, w_down_b)
        yt = _sc_gather_rows(ys, pos_win)
        out = _combine(yt, w_tok, x1, p2d, consts6, tm=512, chunk=chunk, out_so_far=out)
    return out.reshape(b, s, d)


def kernel(x, p, positions, g_mix, w_in, sgu_ln_g, sgu_ln_b, sgu_w, sgu_b, mla_g_qa, mla_w_qb, mla_g_kva, mla_w_kvb, qk_g_q_nope, qk_g_k_nope, qk_g_q_rope, qk_g_k_rope, w_o, g_ffn, w_router, b_router, w_exp_gate, w_exp_up, w_exp_down, w_sh_gate, w_sh_up, w_sh_down, g_ple_in, w_ple_gate, w_ple_proj, g_ple_out):
    params = (g_mix, w_in, sgu_ln_g, sgu_ln_b, sgu_w, sgu_b, mla_g_qa, mla_w_qb, mla_g_kva, mla_w_kvb,
              qk_g_q_nope, qk_g_k_nope, qk_g_q_rope, qk_g_k_rope, w_o, g_ffn, w_router, b_router,
              w_exp_gate, w_exp_up, w_exp_down, w_sh_gate, w_sh_up, w_sh_down,
              g_ple_in, w_ple_gate, w_ple_proj, g_ple_out)
    for l in range(g_mix.shape[0]):
        x = _layer(x, p[l], positions, *[a[l] for a in params])
    return x
```

```python
import functools
import math

import jax
import jax.numpy as jnp
from jax import lax
from jax.experimental import pallas as pl
from jax.experimental.pallas import tpu as pltpu
from jax.experimental.pallas import tpu_sc as plsc

D_MODEL = 1024
PLE_DIM = 256
SGU_CHUNK = 128
SGU_GROUPS = 8
V_DIM = 128
HEADS = 8
QK_NOPE = 128
QK_ROPE = 64
QK_DIM = QK_NOPE + QK_ROPE
QK_PAD = 256
Q_LORA = 384
KV_LORA = 256
ROPE_THETA = 10000.0
N_EXPERTS = 64
N_GROUPS = 8
EXPERTS_PER_GROUP = 8
TOPK_GROUPS = 4
TOP_K = 8
EXPERT_FF = 256
ROUTED_SCALE = 2.5
NORM_EPS = 1e-6
LN_EPS = 1e-5

LANES = 128
MXU_COLS = 256
VMEM_LIMIT = 56 * 1024 * 1024
SC_CORES = 2
SC_SUBCORES = 16
SC_WINDOW = 64
ROW_TILE = 256
TILES_PER_STEP = 4
MOE_CHUNKS = 2
HALF = D_MODEL // 2

F32 = jnp.float32
BF16 = jnp.bfloat16
U32 = jnp.uint32
I32 = jnp.int32


def _dot(a, b):
    return lax.dot_general(a, b, (((1,), (0,)), ((), ())), preferred_element_type=F32)


def _rms(xf, g, width=None):
    width = xf.shape[-1] if width is None else width
    ms = jnp.sum(xf * xf, axis=-1, keepdims=True) * (1.0 / width)
    return xf * lax.rsqrt(ms + NORM_EPS) * g


def _pack_rows(y):
    lo = pltpu.bitcast(y[:, :HALF].astype(BF16).astype(F32), U32) >> 16
    hi = pltpu.bitcast(y[:, HALF:].astype(BF16).astype(F32), U32) & jnp.uint32(0xFFFF0000)
    return lo | hi


def _unpack_rows(w):
    lo = pltpu.bitcast(w << 16, F32)
    hi = pltpu.bitcast(w & jnp.uint32(0xFFFF0000), F32)
    return lo, hi


def _rope_table(pos_f):
    lane = lax.broadcasted_iota(I32, (1, LANES), 1)
    freq = (lane % (QK_ROPE // 2)).astype(F32)
    inv_freq = jnp.exp(freq * (-math.log(ROPE_THETA) * 2.0 / QK_ROPE))
    phase = jnp.where(lane < QK_ROPE, 0.0, math.pi / 2)
    return jnp.cos(pos_f * inv_freq - phase)


def _norm_rope(piece, table_g):
    lane = lax.broadcasted_iota(I32, (1, LANES), 1)
    ssq = jnp.sum(jnp.where(lane < QK_ROPE, piece * piece, 0.0), axis=-1, keepdims=True)
    z = piece * lax.rsqrt(ssq * (1.0 / QK_ROPE) + NORM_EPS) * table_g
    return z + pltpu.roll(z, QK_ROPE, axis=1)


def _const_spec(shape):
    return pl.BlockSpec(shape, lambda *_: (0,) * len(shape), pipeline_mode=pl.Buffered(1))


def _mixer_prep_kernel(x_ref, pos_ref, g_mix_ref, wm_ref, wr_ref, wga_ref, wgb_ref,
                       lng_ref, lnb_ref, sw_ref, sb_ref, gqa_ref, wqb_ref, gkva_ref, wkb_ref, wvbt_ref,
                       gqn_ref, gkn_ref, gqr_ref, gkr_ref,
                       ma_ref, gb_ref, q_ref, k_ref, vt_ref):
    tm = x_ref.shape[1]
    xn = _rms(x_ref[0], g_mix_ref[...]).astype(BF16)

    rope_t = _rope_table(pos_ref[0])
    d = x_ref.shape[2]
    qn = _rms(_dot(xn, wm_ref[:, 2 * d:2 * d + Q_LORA]), gqa_ref[...]).astype(BF16)
    kvn = _rms(_dot(xn, wm_ref[:, 2 * d + Q_LORA:]), gkva_ref[...]).astype(BF16)
    lane = lax.broadcasted_iota(I32, (1, LANES), 1)
    kpe = jnp.where(lane < QK_ROPE, _norm_rope(_dot(xn, wr_ref[...]), rope_t * gkr_ref[...]), 0.0).astype(BF16)
    q_rope_t = rope_t * gqr_ref[...]

    def head_pair(pair):
        ps = slice(pair * MXU_COLS, (pair + 1) * MXU_COLS)
        k2 = _dot(kvn, wkb_ref[:, ps])
        v2t = lax.dot_general(wvbt_ref[ps, :].astype(BF16), kvn, (((1,), (1,)), ((), ())),
                              preferred_element_type=F32).astype(BF16)
        for half in range(2):
            g = 2 * pair + half
            hs = slice(half * LANES, (half + 1) * LANES)
            qh = _dot(qn, wqb_ref[:, g * QK_PAD:(g + 1) * QK_PAD])
            q_ref[0, g, :, :QK_NOPE] = _rms(qh[:, :QK_NOPE], gqn_ref[...]).astype(BF16)
            q_ref[0, g, :, QK_NOPE:] = _norm_rope(qh[:, QK_NOPE:], q_rope_t).astype(BF16)
            k_ref[0, g, :, :QK_NOPE] = _rms(k2[:, hs], gkn_ref[...]).astype(BF16)
            k_ref[0, g, :, QK_NOPE:] = kpe
            vt_ref[0, g] = v2t[hs, :]

    head_pair(0)

    gv = jax.nn.gelu(_dot(xn, wm_ref[:, d:2 * d]))
    mu = jnp.mean(gv, axis=-1, keepdims=True)
    vc = gv - mu
    var = jnp.mean(vc * vc, axis=-1, keepdims=True)
    vn = (vc * lax.rsqrt(var + LN_EPS) * lng_ref[...] + lnb_ref[...]).astype(BF16)
    row = lax.broadcasted_iota(I32, (SGU_CHUNK, SGU_CHUNK), 0)
    col = lax.broadcasted_iota(I32, (SGU_CHUNK, SGU_CHUNK), 1)
    causal = col <= row
    n_chunks = tm // SGU_CHUNK

    def sgu_pair(pair):
        ps = slice(pair * MXU_COLS, (pair + 1) * MXU_COLS)
        gu2 = jax.nn.gelu(_dot(xn, wm_ref[:, ps]))
        ga2 = jax.nn.sigmoid(_dot(xn, wga_ref[:, ps]))
        gb_ref[0, :, ps] = jax.nn.sigmoid(_dot(xn, wgb_ref[:, ps])).astype(BF16)
        for half in range(2):
            g = 2 * pair + half
            hs = slice(half * LANES, (half + 1) * LANES)
            cs = slice(g * SGU_CHUNK, (g + 1) * SGU_CHUNK)
            wg = jnp.where(causal, sw_ref[g], 0.0).astype(BF16)
            vcat = jnp.concatenate([vn[c * SGU_CHUNK:(c + 1) * SGU_CHUNK, cs] for c in range(n_chunks)], axis=1)
            mixed = _dot(wg, vcat)
            for c in range(n_chunks):
                rs = slice(c * SGU_CHUNK, (c + 1) * SGU_CHUNK)
                m = mixed[:, c * SGU_CHUNK:(c + 1) * SGU_CHUNK] + sb_ref[:, cs]
                ma_ref[0, rs, cs] = (ga2[rs, hs] * gu2[rs, hs] * m).astype(BF16)

    n_pairs = SGU_GROUPS // 2
    for pair in range(n_pairs):
        if pair + 1 < n_pairs:
            head_pair(pair + 1)
        sgu_pair(pair)


def _mixer_prep(x, pos, consts, tm):
    b, s, d = x.shape
    grid = (b, s // tm)
    row_spec = lambda w: pl.BlockSpec((1, tm, w), lambda i, j: (i, j, 0))
    head_spec = lambda w: pl.BlockSpec((1, HEADS, tm, w), lambda i, j: (i, 0, j, 0))
    return pl.pallas_call(
        _mixer_prep_kernel,
        grid=grid,
        in_specs=[row_spec(d), row_spec(LANES)] + [_const_spec(c.shape) for c in consts],
        out_specs=[row_spec(d), row_spec(d), head_spec(QK_PAD), head_spec(QK_PAD),
                   pl.BlockSpec((1, HEADS, V_DIM, tm), lambda i, j: (i, 0, 0, j))],
        out_shape=[jax.ShapeDtypeStruct((b, s, d), BF16), jax.ShapeDtypeStruct((b, s, d), BF16),
                   jax.ShapeDtypeStruct((b, HEADS, s, QK_PAD), BF16),
                   jax.ShapeDtypeStruct((b, HEADS, s, QK_PAD), BF16),
                   jax.ShapeDtypeStruct((b, HEADS, V_DIM, s), BF16)],
        compiler_params=pltpu.CompilerParams(dimension_semantics=("arbitrary", "arbitrary"),
                                             vmem_limit_bytes=VMEM_LIMIT),
        name="mixer_prep",
    )(x, pos, *consts)


def _attn_kernel(q_ref, k_ref, vt_ref, ma_ref, gb_ref, *refs, tq):
    n_w = (len(refs) - 1) // 2
    o_ref = refs[n_w]
    for w_ref, wb_ref in zip(refs[:n_w], refs[n_w + 1:]):
        wb_ref[...] = w_ref[...].astype(BF16)
    s = q_ref.shape[2]
    key = lax.broadcasted_iota(I32, (tq, tq), 0)
    qry = lax.broadcasted_iota(I32, (tq, tq), 1)
    diag_mask = key <= qry
    for qi in range(s // tq):
        qs = slice(qi * tq, (qi + 1) * tq)
        n_keys = (qi + 1) * tq
        for h in range(q_ref.shape[1]):
            sc = lax.dot_general(k_ref[0, h, :n_keys, :], q_ref[0, h, qs, :], (((1,), (1,)), ((), ())),
                                 preferred_element_type=F32)
            last = jnp.where(diag_mask, sc[n_keys - tq:], -jnp.inf)
            sc = last if qi == 0 else jnp.concatenate([sc[:n_keys - tq], last], axis=0)
            m = jnp.max(sc, axis=0, keepdims=True)
            p = jnp.exp(sc - m)
            l = jnp.sum(p, axis=0, keepdims=True)
            acc = _dot(vt_ref[0, h, :, :n_keys], p.astype(BF16))
            cs = slice(h * V_DIM, (h + 1) * V_DIM)
            o = (acc / l).T
            o_ref[0, qs, cs] = (ma_ref[0, qs, cs].astype(F32) + gb_ref[0, qs, cs].astype(F32) * o).astype(BF16)


def _attention(q, k, vt, ma, gb, expert_weights, tq, heads_per_step):
    b, h, s, _ = q.shape
    hp = heads_per_step
    n_steps = b * (h // hp)
    w_specs = [pl.BlockSpec((w.shape[0] // n_steps,) + w.shape[1:], lambda i, j: (i * (h // hp) + j, 0, 0))
               for w in expert_weights]
    col_spec = pl.BlockSpec((1, s, hp * V_DIM), lambda i, j: (i, 0, j))
    return pl.pallas_call(
        functools.partial(_attn_kernel, tq=tq),
        grid=(b, h // hp),
        in_specs=[pl.BlockSpec((1, hp, s, QK_PAD), lambda i, j: (i, j, 0, 0)),
                  pl.BlockSpec((1, hp, s, QK_PAD), lambda i, j: (i, j, 0, 0)),
                  pl.BlockSpec((1, hp, V_DIM, s), lambda i, j: (i, j, 0, 0)), col_spec, col_spec] + w_specs,
        out_specs=[col_spec] + w_specs,
        out_shape=[jax.ShapeDtypeStruct((b, s, h * V_DIM), BF16)]
                  + [jax.ShapeDtypeStruct(w.shape, BF16) for w in expert_weights],
        compiler_params=pltpu.CompilerParams(dimension_semantics=("arbitrary", "arbitrary"),
                                             vmem_limit_bytes=VMEM_LIMIT),
        name="mla_attention",
    )(q, k, vt, ma, gb, *expert_weights)


def _route(sel, scores):
    e, tm = sel.shape
    sel3 = sel.reshape(N_GROUPS, EXPERTS_PER_GROUP, tm)
    sub = lax.broadcasted_iota(I32, sel3.shape, 1)
    m1 = jnp.max(sel3, axis=1, keepdims=True)
    first = jnp.min(jnp.where(sel3 == m1, sub, EXPERTS_PER_GROUP), axis=1, keepdims=True)
    m2 = jnp.max(jnp.where(sub == first, -jnp.inf, sel3), axis=1, keepdims=True)
    gscore = (m1 + m2).reshape(N_GROUPS, tm)
    gid = lax.broadcasted_iota(I32, (N_GROUPS, tm), 0)
    grank = jnp.zeros((N_GROUPS, tm), I32)
    for g in range(N_GROUPS):
        other = gscore[g:g + 1]
        grank += ((other > gscore) | ((other == gscore) & (g < gid))).astype(I32)
    gmask = grank < TOPK_GROUPS
    emask = jnp.broadcast_to(gmask[:, None, :], sel3.shape).reshape(e, tm)
    msel = jnp.where(emask, sel, -jnp.inf)
    groups = [msel[g * EXPERTS_PER_GROUP:(g + 1) * EXPERTS_PER_GROUP] for g in range(N_GROUPS)]
    ranks = [jnp.zeros((EXPERTS_PER_GROUP, tm), F32) for _ in range(N_GROUPS)]
    sub8 = lax.broadcasted_iota(I32, (EXPERTS_PER_GROUP, tm), 0)
    for j in range(e):
        jg, jr = divmod(j, EXPERTS_PER_GROUP)
        other = jnp.broadcast_to(groups[jg][jr:jr + 1], (EXPERTS_PER_GROUP, tm))
        for g in range(N_GROUPS):
            if g < jg:
                beats = other > groups[g]
            elif g > jg:
                beats = other >= groups[g]
            else:
                beats = (other > groups[g]) | ((other == groups[g]) & (jr < sub8))
            ranks[g] += jnp.where(beats, 1.0, 0.0)
    chosen = jnp.concatenate(ranks, axis=0) < float(TOP_K)
    w = jnp.where(chosen, scores, 0.0)
    return chosen, w / jnp.sum(w, axis=0, keepdims=True) * ROUTED_SCALE


def _post_attn_kernel(x_ref, merged_ref, wo_ref, gffn_ref, wrt_ref, br_ref,
                      x1_ref, hp_ref, comb_ref, rank_ref, cnt_ref, run_ref):
    tm = x_ref.shape[0]

    @pl.when(pl.program_id(0) == 0)
    def _():
        run_ref[...] = jnp.zeros_like(run_ref)

    x1 = x_ref[...] + _dot(merged_ref[...], wo_ref[...])
    x1_ref[...] = x1
    h2 = _rms(x1, gffn_ref[...])
    hp_ref[...] = _pack_rows(h2)
    logits_t = lax.dot_general(wrt_ref[...], h2, (((1,), (1,)), ((), ())),
                               preferred_element_type=F32, precision=lax.Precision.HIGHEST)
    scores = jax.nn.sigmoid(logits_t)
    chosen, comb_t = _route(scores + br_ref[...], scores)
    comb_ref[...] = comb_t
    a = lax.broadcasted_iota(I32, (tm, tm), 0)
    b = lax.broadcasted_iota(I32, (tm, tm), 1)
    before = (a < b).astype(BF16)
    chosen_f = chosen.astype(F32)
    prefix = _dot(chosen_f.astype(BF16), before)
    run = run_ref[:, 0:1]
    rank_ref[...] = jnp.where(chosen, prefix + run, -1.0)
    run_ref[...] += jnp.sum(chosen_f, axis=1, keepdims=True)
    cnt_ref[...] = run_ref[...]


def _post_attn(x, merged, consts, tm, chunk):
    t, d = x.shape[0] // MOE_CHUNKS, x.shape[1]
    first = chunk * (t // tm)
    row_spec = lambda w: pl.BlockSpec((tm, w), lambda i: (i, 0))
    col_spec = pl.BlockSpec((N_EXPERTS, tm), lambda i: (0, i))
    return pl.pallas_call(
        _post_attn_kernel,
        grid=(t // tm,),
        in_specs=[pl.BlockSpec((tm, d), lambda i: (first + i, 0))] * 2 + [_const_spec(c.shape) for c in consts],
        out_specs=[row_spec(d), row_spec(HALF), col_spec, col_spec, _const_spec((N_EXPERTS, LANES))],
        out_shape=[jax.ShapeDtypeStruct((t, d), F32), jax.ShapeDtypeStruct((t, HALF), U32),
                   jax.ShapeDtypeStruct((N_EXPERTS, t), F32), jax.ShapeDtypeStruct((N_EXPERTS, t), F32),
                   jax.ShapeDtypeStruct((N_EXPERTS, LANES), F32)],
        scratch_shapes=[pltpu.VMEM((N_EXPERTS, LANES), F32)],
        compiler_params=pltpu.CompilerParams(dimension_semantics=("arbitrary",), vmem_limit_bytes=VMEM_LIMIT),
        name="post_attn_router",
    )(x, merged, *consts)


def _route_lists_kernel(comb_ref, rank_ref, off_ref, lay_ref, pos_ref, w_ref):
    rank = rank_ref[...]
    chosen = rank >= 0.0
    e = rank.shape[0]
    lower = (lax.broadcasted_iota(I32, (e, e), 1) < lax.broadcasted_iota(I32, (e, e), 0)).astype(BF16)
    slot = _dot(lower, chosen.astype(BF16))
    pos_sorted = rank + off_ref[...]
    tile = jnp.floor(pos_sorted * (1.0 / ROW_TILE))
    pos_full = pos_sorted + jnp.floor((tile + 0.5) / lay_ref[0:1, 0:1]) * lay_ref[0:1, 1:2]
    comb = comb_ref[...]
    pos_rows, w_rows = [], []
    for k in range(TOP_K):
        pick = chosen & (slot == float(k))
        pos_rows.append(jnp.sum(jnp.where(pick, pos_full, 0.0), axis=0, keepdims=True))
        w_rows.append(jnp.sum(jnp.where(pick, comb, 0.0), axis=0, keepdims=True))
    pos_ref[...] = jnp.concatenate(pos_rows, axis=0).astype(I32)
    w_ref[...] = jnp.concatenate(w_rows, axis=0).T


def _route_lists(comb_t, rank_t, offsets, layout, tm):
    e, t = comb_t.shape
    col_spec = pl.BlockSpec((e, tm), lambda i: (0, i))
    return pl.pallas_call(
        _route_lists_kernel,
        grid=(t // tm,),
        in_specs=[col_spec, col_spec, _const_spec((e, 1)), _const_spec(layout.shape)],
        out_specs=[pl.BlockSpec((TOP_K, tm), lambda i: (0, i)), pl.BlockSpec((tm, TOP_K), lambda i: (i, 0))],
        out_shape=[jax.ShapeDtypeStruct((TOP_K, t), I32), jax.ShapeDtypeStruct((t, TOP_K), F32)],
        compiler_params=pltpu.CompilerParams(dimension_semantics=("arbitrary",)),
        name="route_lists",
    )(comb_t, rank_t, offsets, layout)


def _sc_mesh():
    return plsc.VectorSubcoreMesh(core_axis_name="c", subcore_axis_name="s",
                                  num_cores=SC_CORES, num_subcores=SC_SUBCORES)


def _sc_worker():
    return lax.axis_index("s") * SC_CORES + lax.axis_index("c")


def _sc_scatter_rows(src, pos_win, n_rows):
    t, width = src.shape
    n_win = t // SC_WINDOW
    per_w = n_win // (SC_CORES * SC_SUBCORES)

    @functools.partial(
        pl.kernel, mesh=_sc_mesh(), out_type=jax.ShapeDtypeStruct((n_rows, width), src.dtype),
        scratch_types=[pltpu.VMEM((TOP_K, SC_WINDOW), I32), pltpu.VMEM((SC_WINDOW, width), src.dtype),
                       pltpu.SemaphoreType.DMA],
        name="sc_scatter_rows")
    def run(src_hbm, pos_hbm, out_hbm, idx_v, rows_v, sem):
        base = _sc_worker() * per_w

        @pl.loop(0, per_w)
        def _(j):
            w = base + j
            pltpu.sync_copy(pos_hbm.at[w], idx_v)
            pltpu.sync_copy(src_hbm.at[pl.ds(w * SC_WINDOW, SC_WINDOW)], rows_v)
            copies = [pltpu.async_copy(rows_v, out_hbm.at[idx_v.at[k]], sem) for k in range(TOP_K)]
            for c in copies:
                c.wait()

    return run(src, pos_win)


def _sc_gather_rows(table, pos_win):
    n_win = pos_win.shape[0]
    width = table.shape[1]
    per_w = n_win // (SC_CORES * SC_SUBCORES)

    @functools.partial(
        pl.kernel, mesh=_sc_mesh(),
        out_type=jax.ShapeDtypeStruct((TOP_K, n_win * SC_WINDOW, width), table.dtype),
        scratch_types=[pltpu.VMEM((TOP_K, SC_WINDOW), I32), pltpu.VMEM((2, SC_WINDOW, width), table.dtype),
                       pltpu.SemaphoreType.DMA, pltpu.SemaphoreType.DMA],
        name="sc_gather_rows")
    def run(table_hbm, pos_hbm, out_hbm, idx_v, rows_v, gsem, wsem):
        base = _sc_worker() * per_w

        @pl.loop(0, per_w)
        def _(j):
            w = base + j
            pltpu.sync_copy(pos_hbm.at[w], idx_v)
            for k in range(TOP_K):
                buf = rows_v.at[k % 2]
                pltpu.async_copy(table_hbm.at[idx_v.at[k]], buf, gsem).wait()
                pltpu.async_copy(buf, out_hbm.at[k, pl.ds(w * SC_WINDOW, SC_WINDOW)], wsem).wait()

    return run(table, pos_win)


def _grouped_ffn_kernel(te_ref, used_ref, xs_ref, *refs):
    w_refs, ys_ref = refs[:-1], refs[-1]

    @pl.when(pl.program_id(0) < used_ref[0])
    def _():
        acts = []
        for j in range(TILES_PER_STEP):
            wg_ref, wu_ref = w_refs[3 * j:3 * j + 2]
            lo, hi = _unpack_rows(xs_ref[j])
            lo, hi = lo.astype(BF16), hi.astype(BF16)
            gate = _dot(lo, wg_ref[0, :HALF]) + _dot(hi, wg_ref[0, HALF:])
            up = _dot(lo, wu_ref[0, :HALF]) + _dot(hi, wu_ref[0, HALF:])
            acts.append((jax.nn.silu(gate) * up).astype(BF16))
        for j in range(TILES_PER_STEP):
            ys_ref[j] = _pack_rows(_dot(acts[j], w_refs[3 * j + 2][0]))


def _grouped_ffn(tile_expert, steps_used, xs, wg, wu, wd):
    n_rows, half = xs.shape
    n_steps = n_rows // (TILES_PER_STEP * ROW_TILE)

    def step_of(i, used):
        return jnp.minimum(i, used[0] - 1)

    row_spec = pl.BlockSpec((TILES_PER_STEP, ROW_TILE, half), lambda i, te, used: (0, step_of(i, used), 0))

    def exp_map(j):
        return lambda i, te, used: (te[j * n_steps + step_of(i, used)], 0, 0)

    w_specs = [pl.BlockSpec((1,) + w.shape[1:], exp_map(j)) for j in range(TILES_PER_STEP) for w in (wg, wu, wd)]
    ys = pl.pallas_call(
        _grouped_ffn_kernel,
        grid_spec=pltpu.PrefetchScalarGridSpec(
            num_scalar_prefetch=2,
            grid=(n_steps,),
            in_specs=[row_spec] + w_specs,
            out_specs=row_spec),
        out_shape=jax.ShapeDtypeStruct((TILES_PER_STEP, n_steps * ROW_TILE, half), U32),
        compiler_params=pltpu.CompilerParams(dimension_semantics=("arbitrary",), vmem_limit_bytes=VMEM_LIMIT),
        name="grouped_ffn",
    )(tile_expert, steps_used, xs.reshape(TILES_PER_STEP, n_steps * ROW_TILE, half),
      *([wg, wu, wd] * TILES_PER_STEP))
    return ys.reshape(n_rows, half)


def _combine_kernel(yt_ref, w_ref, x1_ref, p_ref, gffn_ref, wsg_ref, wsu_ref, wsd_ref,
                    gpi_ref, wpg_ref, wpp_ref, gpo_ref, *rest):
    out_ref = rest[-1]
    x1 = x1_ref[...]
    w = w_ref[...]
    acc_lo = jnp.zeros((x1.shape[0], HALF), F32)
    acc_hi = jnp.zeros((x1.shape[0], HALF), F32)
    for k in range(TOP_K):
        lo, hi = _unpack_rows(yt_ref[k])
        wk = w[:, k:k + 1]
        acc_lo += wk * lo
        acc_hi += wk * hi
    h = _rms(x1, gffn_ref[...]).astype(BF16)
    shared = _dot((jax.nn.silu(_dot(h, wsg_ref[...])) * _dot(h, wsu_ref[...])).astype(BF16), wsd_ref[...])
    x2 = x1 + shared + jnp.concatenate([acc_lo, acc_hi], axis=-1)
    gate = jax.nn.sigmoid(_dot(_rms(x2, gpi_ref[...]).astype(BF16), wpg_ref[...]))
    proj = _rms(_dot(p_ref[...].astype(BF16), wpp_ref[...]), gpo_ref[...])
    out_ref[...] = x2 + gate * proj


def _combine(yt, w_tok, x1, p, consts, tm, chunk, out_so_far):
    t, d = x1.shape
    first = chunk * (t // tm)
    row_spec = lambda w: pl.BlockSpec((tm, w), lambda i: (i, 0))
    full_spec = lambda w: pl.BlockSpec((tm, w), lambda i: (first + i, 0))
    carried = [] if out_so_far is None else [out_so_far]
    return pl.pallas_call(
        _combine_kernel,
        grid=(t // tm,),
        in_specs=[pl.BlockSpec((TOP_K, tm, HALF), lambda i: (0, i, 0)), row_spec(TOP_K), row_spec(d),
                  full_spec(PLE_DIM)] + [_const_spec(c.shape) for c in consts]
                 + [pl.BlockSpec(memory_space=pl.ANY)] * len(carried),
        out_specs=full_spec(d),
        out_shape=jax.ShapeDtypeStruct((t * MOE_CHUNKS, d), F32),
        input_output_aliases={4 + len(consts): 0} if carried else {},
        compiler_params=pltpu.CompilerParams(dimension_semantics=("arbitrary",), vmem_limit_bytes=VMEM_LIMIT),
        name="combine_ple",
    )(yt, w_tok, x1, p, *consts, *carried)


def _rotate_half(a):
    half = QK_ROPE // 2
    return jnp.concatenate([-a[..., half:], a[..., :half]], axis=-1)


def _with_rotate_half(w):
    return jnp.concatenate([w, _rotate_half(w[..., -QK_ROPE:])], axis=-1)


def _rope_gain_table(g, scale):
    half = QK_ROPE // 2
    return (jnp.concatenate([g, g[half:], g[:half]]) * scale).reshape(1, LANES)


def _layer(x, p, pos, g_mix, w_in, sgu_ln_g, sgu_ln_b, sgu_w, sgu_b, mla_g_qa, mla_w_qb, mla_g_kva, mla_w_kvb,
           qk_g_q_nope, qk_g_k_nope, qk_g_q_rope, qk_g_k_rope, w_o, g_ffn, w_router, b_router,
           w_exp_gate, w_exp_up, w_exp_down, w_sh_gate, w_sh_up, w_sh_down,
           g_ple_in, w_ple_gate, w_ple_proj, g_ple_out):
    b, s, d = x.shape
    t = b * s
    row = lambda a: a.reshape(1, -1)
    sizes = [d, d, Q_LORA, KV_LORA, QK_ROPE, d, d]
    offs = [0]
    for sz in sizes:
        offs.append(offs[-1] + sz)
    w_main = w_in[:, :offs[4]]
    w_r, w_ga, w_gb = [w_in[:, offs[i]:offs[i + 1]] for i in (4, 5, 6)]
    w_r = _with_rotate_half(w_r)
    wqb = _with_rotate_half(mla_w_qb.reshape(Q_LORA, HEADS, QK_DIM)).reshape(Q_LORA, HEADS * QK_PAD)
    wkvb = mla_w_kvb.reshape(KV_LORA, HEADS, QK_NOPE + V_DIM)
    wkb = wkvb[:, :, :QK_NOPE].reshape(KV_LORA, HEADS * QK_NOPE)
    wvbt = wkvb[:, :, QK_NOPE:].reshape(KV_LORA, HEADS * V_DIM).T
    sgu_bias = jnp.repeat(sgu_b.T, d // SGU_GROUPS, axis=1)
    q_scale = QK_DIM ** -0.5

    consts1 = [row(g_mix), w_main, w_r, w_ga, w_gb, row(sgu_ln_g), row(sgu_ln_b), sgu_w, sgu_bias,
               row(mla_g_qa), wqb, row(mla_g_kva), wkb, wvbt, row(qk_g_q_nope) * q_scale, row(qk_g_k_nope),
               _rope_gain_table(qk_g_q_rope, q_scale), _rope_gain_table(qk_g_k_rope, 1.0)]
    pos_lanes = jnp.broadcast_to(pos.astype(F32)[..., None], (b, s, LANES))
    ma, gb, q, k, vt = _mixer_prep(x, pos_lanes, consts1, tm=512)
    merged, w_gate_b, w_up_b, w_down_b = _attention(q, k, vt, ma, gb, (w_exp_gate, w_exp_up, w_exp_down),
                                                    tq=512, heads_per_step=2)

    consts3 = [w_o, row(g_ffn), w_router.T, b_router.reshape(-1, 1)]
    consts6 = [row(g_ffn), w_sh_gate, w_sh_up, w_sh_down, row(g_ple_in), w_ple_gate, w_ple_proj, row(g_ple_out)]
    x2d, merged2d, p2d = (a.reshape(t, -1) for a in (x, merged, p))
    tc = t // MOE_CHUNKS
    max_tiles = (tc * TOP_K) // ROW_TILE + N_EXPERTS
    tile_ids = jnp.arange(max_tiles, dtype=I32)
    n_steps = max_tiles // TILES_PER_STEP
    out = None
    for chunk in range(MOE_CHUNKS):
        x1, h_packed, comb_t, rank_t, counts = _post_attn(x2d, merged2d, consts3, tm=512, chunk=chunk)

        counts = counts[:, 0].astype(I32)
        tiles_per_expert = (counts + ROW_TILE - 1) // ROW_TILE
        tile_end = jnp.cumsum(tiles_per_expert)
        offsets = ((tile_end - tiles_per_expert) * ROW_TILE).astype(F32).reshape(N_EXPERTS, 1)
        steps_used = (tile_end[-1:] + TILES_PER_STEP - 1) // TILES_PER_STEP
        sorted_tile = (tile_ids // n_steps) * steps_used + jnp.minimum(tile_ids % n_steps, steps_used - 1)
        tile_expert = jnp.minimum(jnp.sum((tile_end[None, :] <= sorted_tile[:, None]).astype(I32), axis=1),
                                  N_EXPERTS - 1)
        layout = jnp.concatenate([steps_used, (n_steps - steps_used) * ROW_TILE]).astype(F32).reshape(1, 2)

        pos_t, w_tok = _route_lists(comb_t, rank_t, offsets, layout, tm=512)
        pos_win = pos_t.reshape(TOP_K, tc // SC_WINDOW, SC_WINDOW).transpose(1, 0, 2)

        xs = _sc_scatter_rows(h_packed, pos_win, max_tiles * ROW_TILE)
        ys = _grouped_ffn(tile_expert, steps_used, xs, w_gate_b, w_up_b, w_down_b)
        yt = _sc_gather_rows(ys, pos_win)
        out = _combine(yt, w_tok, x1, p2d, consts6, tm=512, chunk=chunk, out_so_far=out)
    return out.reshape(b, s, d)


def kernel(x, p, positions, g_mix, w_in, sgu_ln_g, sgu_ln_b, sgu_w, sgu_b, mla_g_qa, mla_w_qb, mla_g_kva, mla_w_kvb, qk_g_q_nope, qk_g_k_nope, qk_g_q_rope, qk_g_k_rope, w_o, g_ffn, w_router, b_router, w_exp_gate, w_exp_up, w_exp_down, w_sh_gate, w_sh_up, w_sh_down, g_ple_in, w_ple_gate, w_ple_proj, g_ple_out):
    params = (g_mix, w_in, sgu_ln_g, sgu_ln_b, sgu_w, sgu_b, mla_g_qa, mla_w_qb, mla_g_kva, mla_w_kvb,
              qk_g_q_nope, qk_g_k_nope, qk_g_q_rope, qk_g_k_rope, w_o, g_ffn, w_router, b_router,
              w_exp_gate, w_exp_up, w_exp_down, w_sh_gate, w_sh_up, w_sh_down,
              g_ple_in, w_ple_gate, w_ple_proj, g_ple_out)
    for l in range(g_mix.shape[0]):
        x = _layer(x, p[l], positions, *[a[l] for a in params])
    return x
```

```python
import functools
import math

import jax
import jax.numpy as jnp
from jax import lax
from jax.experimental import pallas as pl
from jax.experimental.pallas import tpu as pltpu
from jax.experimental.pallas import tpu_sc as plsc

D_MODEL = 1024
PLE_DIM = 256
SGU_CHUNK = 128
SGU_GROUPS = 8
V_DIM = 128
HEADS = 8
QK_NOPE = 128
QK_ROPE = 64
QK_DIM = QK_NOPE + QK_ROPE
QK_PAD = 256
Q_LORA = 384
KV_LORA = 256
ROPE_THETA = 10000.0
N_EXPERTS = 64
N_GROUPS = 8
EXPERTS_PER_GROUP = 8
TOPK_GROUPS = 4
TOP_K = 8
EXPERT_FF = 256
ROUTED_SCALE = 2.5
NORM_EPS = 1e-6
LN_EPS = 1e-5

LANES = 128
MXU_COLS = 256
VMEM_LIMIT = 56 * 1024 * 1024
SC_CORES = 2
SC_SUBCORES = 16
SC_WINDOW = 64
ROW_TILE = 256
TILES_PER_STEP = 4
MOE_CHUNKS = 2
HALF = D_MODEL // 2

F32 = jnp.float32
BF16 = jnp.bfloat16
U32 = jnp.uint32
I32 = jnp.int32


def _dot(a, b):
    return lax.dot_general(a, b, (((1,), (0,)), ((), ())), preferred_element_type=F32)


def _rms(xf, g, width=None):
    width = xf.shape[-1] if width is None else width
    ms = jnp.sum(xf * xf, axis=-1, keepdims=True) * (1.0 / width)
    return xf * lax.rsqrt(ms + NORM_EPS) * g


def _pack_rows(y):
    lo = pltpu.bitcast(y[:, :HALF].astype(BF16).astype(F32), U32) >> 16
    hi = pltpu.bitcast(y[:, HALF:].astype(BF16).astype(F32), U32) & jnp.uint32(0xFFFF0000)
    return lo | hi


def _unpack_rows(w):
    lo = pltpu.bitcast(w << 16, F32)
    hi = pltpu.bitcast(w & jnp.uint32(0xFFFF0000), F32)
    return lo, hi


def _rope_table(pos_f):
    lane = lax.broadcasted_iota(I32, (1, LANES), 1)
    freq = (lane % (QK_ROPE // 2)).astype(F32)
    inv_freq = jnp.exp(freq * (-math.log(ROPE_THETA) * 2.0 / QK_ROPE))
    phase = jnp.where(lane < QK_ROPE, 0.0, math.pi / 2)
    return jnp.cos(pos_f * inv_freq - phase)


def _norm_rope(piece, table_g):
    lane = lax.broadcasted_iota(I32, (1, LANES), 1)
    ssq = jnp.sum(jnp.where(lane < QK_ROPE, piece * piece, 0.0), axis=-1, keepdims=True)
    z = piece * lax.rsqrt(ssq * (1.0 / QK_ROPE) + NORM_EPS) * table_g
    return z + pltpu.roll(z, QK_ROPE, axis=1)


def _const_spec(shape):
    return pl.BlockSpec(shape, lambda *_: (0,) * len(shape), pipeline_mode=pl.Buffered(1))


def _mixer_prep_kernel(x_ref, pos_ref, g_mix_ref, wm_ref, wr_ref, wga_ref, wgb_ref,
                       lng_ref, lnb_ref, sw_ref, sb_ref, gqa_ref, wqb_ref, gkva_ref, wkb_ref, wvbt_ref,
                       gqn_ref, gkn_ref, gqr_ref, gkr_ref,
                       ma_ref, gb_ref, q_ref, k_ref, vt_ref):
    tm = x_ref.shape[1]
    xn = _rms(x_ref[0], g_mix_ref[...]).astype(BF16)

    rope_t = _rope_table(pos_ref[0])
    d = x_ref.shape[2]
    qn = _rms(_dot(xn, wm_ref[:, 2 * d:2 * d + Q_LORA]), gqa_ref[...]).astype(BF16)
    kvn = _rms(_dot(xn, wm_ref[:, 2 * d + Q_LORA:]), gkva_ref[...]).astype(BF16)
    lane = lax.broadcasted_iota(I32, (1, LANES), 1)
    kpe = jnp.where(lane < QK_ROPE, _norm_rope(_dot(xn, wr_ref[...]), rope_t * gkr_ref[...]), 0.0).astype(BF16)
    q_rope_t = rope_t * gqr_ref[...]

    def head_pair(pair):
        ps = slice(pair * MXU_COLS, (pair + 1) * MXU_COLS)
        k2 = _dot(kvn, wkb_ref[:, ps])
        v2t = lax.dot_general(wvbt_ref[ps, :].astype(BF16), kvn, (((1,), (1,)), ((), ())),
                              preferred_element_type=F32).astype(BF16)
        for half in range(2):
            g = 2 * pair + half
            hs = slice(half * LANES, (half + 1) * LANES)
            qh = _dot(qn, wqb_ref[:, g * QK_PAD:(g + 1) * QK_PAD])
            q_ref[0, g, :, :QK_NOPE] = _rms(qh[:, :QK_NOPE], gqn_ref[...]).astype(BF16)
            q_ref[0, g, :, QK_NOPE:] = _norm_rope(qh[:, QK_NOPE:], q_rope_t).astype(BF16)
            k_ref[0, g, :, :QK_NOPE] = _rms(k2[:, hs], gkn_ref[...]).astype(BF16)
            k_ref[0, g, :, QK_NOPE:] = kpe
            vt_ref[0, g] = v2t[hs, :]

    head_pair(0)

    gv = jax.nn.gelu(_dot(xn, wm_ref[:, d:2 * d]))
    mu = jnp.mean(gv, axis=-1, keepdims=True)
    vc = gv - mu
    var = jnp.mean(vc * vc, axis=-1, keepdims=True)
    vn = (vc * lax.rsqrt(var + LN_EPS) * lng_ref[...] + lnb_ref[...]).astype(BF16)
    row = lax.broadcasted_iota(I32, (SGU_CHUNK, SGU_CHUNK), 0)
    col = lax.broadcasted_iota(I32, (SGU_CHUNK, SGU_CHUNK), 1)
    causal = col <= row
    n_chunks = tm // SGU_CHUNK

    def sgu_pair(pair):
        ps = slice(pair * MXU_COLS, (pair + 1) * MXU_COLS)
        gu2 = jax.nn.gelu(_dot(xn, wm_ref[:, ps]))
        ga2 = jax.nn.sigmoid(_dot(xn, wga_ref[:, ps]))
        gb_ref[0, :, ps] = jax.nn.sigmoid(_dot(xn, wgb_ref[:, ps])).astype(BF16)
        for half in range(2):
            g = 2 * pair + half
            hs = slice(half * LANES, (half + 1) * LANES)
            cs = slice(g * SGU_CHUNK, (g + 1) * SGU_CHUNK)
            wg = jnp.where(causal, sw_ref[g], 0.0).astype(BF16)
            vcat = jnp.concatenate([vn[c * SGU_CHUNK:(c + 1) * SGU_CHUNK, cs] for c in range(n_chunks)], axis=1)
            mixed = _dot(wg, vcat)
            for c in range(n_chunks):
                rs = slice(c * SGU_CHUNK, (c + 1) * SGU_CHUNK)
                m = mixed[:, c * SGU_CHUNK:(c + 1) * SGU_CHUNK] + sb_ref[:, cs]
                ma_ref[0, rs, cs] = (ga2[rs, hs] * gu2[rs, hs] * m).astype(BF16)

    n_pairs = SGU_GROUPS // 2
    for pair in range(n_pairs):
        if pair + 1 < n_pairs:
            head_pair(pair + 1)
        sgu_pair(pair)


def _mixer_prep(x, pos, consts, tm):
    b, s, d = x.shape
    grid = (b, s // tm)
    row_spec = lambda w: pl.BlockSpec((1, tm, w), lambda i, j: (i, j, 0))
    head_spec = lambda w: pl.BlockSpec((1, HEADS, tm, w), lambda i, j: (i, 0, j, 0))
    return pl.pallas_call(
        _mixer_prep_kernel,
        grid=grid,
        in_specs=[row_spec(d), row_spec(LANES)] + [_const_spec(c.shape) for c in consts],
        out_specs=[row_spec(d), row_spec(d), head_spec(QK_PAD), head_spec(QK_PAD),
                   pl.BlockSpec((1, HEADS, V_DIM, tm), lambda i, j: (i, 0, 0, j))],
        out_shape=[jax.ShapeDtypeStruct((b, s, d), BF16), jax.ShapeDtypeStruct((b, s, d), BF16),
                   jax.ShapeDtypeStruct((b, HEADS, s, QK_PAD), BF16),
                   jax.ShapeDtypeStruct((b, HEADS, s, QK_PAD), BF16),
                   jax.ShapeDtypeStruct((b, HEADS, V_DIM, s), BF16)],
        compiler_params=pltpu.CompilerParams(dimension_semantics=("arbitrary", "arbitrary"),
                                             vmem_limit_bytes=VMEM_LIMIT),
        name="mixer_prep",
    )(x, pos, *consts)


def _attn_kernel(q_ref, k_ref, vt_ref, ma_ref, gb_ref, *refs, tq):
    n_w = (len(refs) - 1) // 2
    o_ref = refs[n_w]
    for w_ref, wb_ref in zip(refs[:n_w], refs[n_w + 1:]):
        wb_ref[...] = w_ref[...].astype(BF16)
    s = q_ref.shape[2]
    key = lax.broadcasted_iota(I32, (tq, tq), 0)
    qry = lax.broadcasted_iota(I32, (tq, tq), 1)
    diag_mask = key <= qry
    heads = range(q_ref.shape[1])
    n_tiles = s // tq

    def score_dots(qi):
        return [lax.dot_general(k_ref[0, h, :(qi + 1) * tq, :], q_ref[0, h, qi * tq:(qi + 1) * tq, :],
                                (((1,), (1,)), ((), ())), preferred_element_type=F32) for h in heads]

    scores = score_dots(0)
    for qi in range(n_tiles):
        qs = slice(qi * tq, (qi + 1) * tq)
        n_keys = (qi + 1) * tq
        next_scores = score_dots(qi + 1) if qi + 1 < n_tiles else None
        probs, sums = [], []
        for h in heads:
            sc = scores[h]
            last = jnp.where(diag_mask, sc[n_keys - tq:], -jnp.inf)
            sc = last if qi == 0 else jnp.concatenate([sc[:n_keys - tq], last], axis=0)
            p = jnp.exp(sc - jnp.max(sc, axis=0, keepdims=True))
            sums.append(jnp.sum(p, axis=0, keepdims=True))
            probs.append(p.astype(BF16))
        for h in heads:
            acc = _dot(vt_ref[0, h, :, :n_keys], probs[h])
            cs = slice(h * V_DIM, (h + 1) * V_DIM)
            o = (acc / sums[h]).T
            o_ref[0, qs, cs] = (ma_ref[0, qs, cs].astype(F32) + gb_ref[0, qs, cs].astype(F32) * o).astype(BF16)
        scores = next_scores


def _attention(q, k, vt, ma, gb, expert_weights, tq, heads_per_step):
    b, h, s, _ = q.shape
    hp = heads_per_step
    n_steps = b * (h // hp)
    w_specs = [pl.BlockSpec((w.shape[0] // n_steps,) + w.shape[1:], lambda i, j: (i * (h // hp) + j, 0, 0))
               for w in expert_weights]
    col_spec = pl.BlockSpec((1, s, hp * V_DIM), lambda i, j: (i, 0, j))
    return pl.pallas_call(
        functools.partial(_attn_kernel, tq=tq),
        grid=(b, h // hp),
        in_specs=[pl.BlockSpec((1, hp, s, QK_PAD), lambda i, j: (i, j, 0, 0)),
                  pl.BlockSpec((1, hp, s, QK_PAD), lambda i, j: (i, j, 0, 0)),
                  pl.BlockSpec((1, hp, V_DIM, s), lambda i, j: (i, j, 0, 0)), col_spec, col_spec] + w_specs,
        out_specs=[col_spec] + w_specs,
        out_shape=[jax.ShapeDtypeStruct((b, s, h * V_DIM), BF16)]
                  + [jax.ShapeDtypeStruct(w.shape, BF16) for w in expert_weights],
        compiler_params=pltpu.CompilerParams(dimension_semantics=("arbitrary", "arbitrary"),
                                             vmem_limit_bytes=VMEM_LIMIT),
        name="mla_attention",
    )(q, k, vt, ma, gb, *expert_weights)


def _route(sel, scores):
    e, tm = sel.shape
    sel3 = sel.reshape(N_GROUPS, EXPERTS_PER_GROUP, tm)
    sub = lax.broadcasted_iota(I32, sel3.shape, 1)
    m1 = jnp.max(sel3, axis=1, keepdims=True)
    first = jnp.min(jnp.where(sel3 == m1, sub, EXPERTS_PER_GROUP), axis=1, keepdims=True)
    m2 = jnp.max(jnp.where(sub == first, -jnp.inf, sel3), axis=1, keepdims=True)
    gscore = (m1 + m2).reshape(N_GROUPS, tm)
    gid = lax.broadcasted_iota(I32, (N_GROUPS, tm), 0)
    grank = jnp.zeros((N_GROUPS, tm), I32)
    for g in range(N_GROUPS):
        other = gscore[g:g + 1]
        grank += ((other > gscore) | ((other == gscore) & (g < gid))).astype(I32)
    gmask = grank < TOPK_GROUPS
    emask = jnp.broadcast_to(gmask[:, None, :], sel3.shape).reshape(e, tm)
    remaining = jnp.where(emask, sel, -jnp.inf)
    eid = lax.broadcasted_iota(I32, (e, tm), 0)
    chosen = jnp.zeros((e, tm), jnp.bool_)
    for _ in range(TOP_K):
        best = jnp.max(remaining, axis=0, keepdims=True)
        pick = eid == jnp.min(jnp.where(remaining == best, eid, e), axis=0, keepdims=True)
        chosen = chosen | pick
        remaining = jnp.where(pick, -jnp.inf, remaining)
    w = jnp.where(chosen, scores, 0.0)
    return chosen, w / jnp.sum(w, axis=0, keepdims=True) * ROUTED_SCALE


def _post_attn_kernel(x_ref, merged_ref, wo_ref, gffn_ref, wrt_ref, br_ref,
                      x1_ref, hp_ref, comb_ref, rank_ref, cnt_ref, run_ref):
    tm = x_ref.shape[0]

    @pl.when(pl.program_id(0) == 0)
    def _():
        run_ref[...] = jnp.zeros_like(run_ref)

    x1 = x_ref[...] + _dot(merged_ref[...], wo_ref[...])
    x1_ref[...] = x1
    h2 = _rms(x1, gffn_ref[...])
    hp_ref[...] = _pack_rows(h2)
    logits_t = lax.dot_general(wrt_ref[...], h2, (((1,), (1,)), ((), ())),
                               preferred_element_type=F32, precision=lax.Precision.HIGHEST)
    scores = jax.nn.sigmoid(logits_t)
    chosen, comb_t = _route(scores + br_ref[...], scores)
    comb_ref[...] = comb_t
    a = lax.broadcasted_iota(I32, (tm, tm), 0)
    b = lax.broadcasted_iota(I32, (tm, tm), 1)
    before = (a < b).astype(BF16)
    chosen_f = chosen.astype(F32)
    prefix = _dot(chosen_f.astype(BF16), before)
    run = run_ref[:, 0:1]
    rank_ref[...] = jnp.where(chosen, prefix + run, -1.0)
    run_ref[...] += jnp.sum(chosen_f, axis=1, keepdims=True)
    cnt_ref[...] = run_ref[...]


def _post_attn(x, merged, consts, tm, chunk):
    t, d = x.shape[0] // MOE_CHUNKS, x.shape[1]
    first = chunk * (t // tm)
    row_spec = lambda w: pl.BlockSpec((tm, w), lambda i: (i, 0))
    col_spec = pl.BlockSpec((N_EXPERTS, tm), lambda i: (0, i))
    return pl.pallas_call(
        _post_attn_kernel,
        grid=(t // tm,),
        in_specs=[pl.BlockSpec((tm, d), lambda i: (first + i, 0))] * 2 + [_const_spec(c.shape) for c in consts],
        out_specs=[row_spec(d), row_spec(HALF), col_spec, col_spec, _const_spec((N_EXPERTS, LANES))],
        out_shape=[jax.ShapeDtypeStruct((t, d), F32), jax.ShapeDtypeStruct((t, HALF), U32),
                   jax.ShapeDtypeStruct((N_EXPERTS, t), F32), jax.ShapeDtypeStruct((N_EXPERTS, t), F32),
                   jax.ShapeDtypeStruct((N_EXPERTS, LANES), F32)],
        scratch_shapes=[pltpu.VMEM((N_EXPERTS, LANES), F32)],
        compiler_params=pltpu.CompilerParams(dimension_semantics=("arbitrary",), vmem_limit_bytes=VMEM_LIMIT),
        name="post_attn_router",
    )(x, merged, *consts)


def _route_lists_kernel(comb_ref, rank_ref, off_ref, lay_ref, pos_ref, w_ref):
    rank = rank_ref[...]
    chosen = rank >= 0.0
    e = rank.shape[0]
    lower = (lax.broadcasted_iota(I32, (e, e), 1) < lax.broadcasted_iota(I32, (e, e), 0)).astype(BF16)
    slot = _dot(lower, chosen.astype(BF16))
    pos_sorted = rank + off_ref[...]
    tile = jnp.floor(pos_sorted * (1.0 / ROW_TILE))
    pos_full = pos_sorted + jnp.floor((tile + 0.5) / lay_ref[0:1, 0:1]) * lay_ref[0:1, 1:2]
    comb = comb_ref[...]
    pos_rows, w_rows = [], []
    for k in range(TOP_K):
        pick = chosen & (slot == float(k))
        pos_rows.append(jnp.sum(jnp.where(pick, pos_full, 0.0), axis=0, keepdims=True))
        w_rows.append(jnp.sum(jnp.where(pick, comb, 0.0), axis=0, keepdims=True))
    pos_ref[...] = jnp.concatenate(pos_rows, axis=0).astype(I32)
    w_ref[...] = jnp.concatenate(w_rows, axis=0).T


def _route_lists(comb_t, rank_t, offsets, layout, tm):
    e, t = comb_t.shape
    col_spec = pl.BlockSpec((e, tm), lambda i: (0, i))
    return pl.pallas_call(
        _route_lists_kernel,
        grid=(t // tm,),
        in_specs=[col_spec, col_spec, _const_spec((e, 1)), _const_spec(layout.shape)],
        out_specs=[pl.BlockSpec((TOP_K, tm), lambda i: (0, i)), pl.BlockSpec((tm, TOP_K), lambda i: (i, 0))],
        out_shape=[jax.ShapeDtypeStruct((TOP_K, t), I32), jax.ShapeDtypeStruct((t, TOP_K), F32)],
        compiler_params=pltpu.CompilerParams(dimension_semantics=("arbitrary",)),
        name="route_lists",
    )(comb_t, rank_t, offsets, layout)


def _sc_mesh():
    return plsc.VectorSubcoreMesh(core_axis_name="c", subcore_axis_name="s",
                                  num_cores=SC_CORES, num_subcores=SC_SUBCORES)


def _sc_worker():
    return lax.axis_index("s") * SC_CORES + lax.axis_index("c")


def _sc_scatter_rows(src, pos_win, n_rows):
    t, width = src.shape
    n_win = t // SC_WINDOW
    per_w = n_win // (SC_CORES * SC_SUBCORES)

    @functools.partial(
        pl.kernel, mesh=_sc_mesh(), out_type=jax.ShapeDtypeStruct((n_rows, width), src.dtype),
        scratch_types=[pltpu.VMEM((TOP_K, SC_WINDOW), I32), pltpu.VMEM((SC_WINDOW, width), src.dtype),
                       pltpu.SemaphoreType.DMA],
        name="sc_scatter_rows")
    def run(src_hbm, pos_hbm, out_hbm, idx_v, rows_v, sem):
        base = _sc_worker() * per_w

        @pl.loop(0, per_w)
        def _(j):
            w = base + j
            pltpu.sync_copy(pos_hbm.at[w], idx_v)
            pltpu.sync_copy(src_hbm.at[pl.ds(w * SC_WINDOW, SC_WINDOW)], rows_v)
            copies = [pltpu.async_copy(rows_v, out_hbm.at[idx_v.at[k]], sem) for k in range(TOP_K)]
            for c in copies:
                c.wait()

    return run(src, pos_win)


def _sc_gather_rows(table, pos_win):
    n_win = pos_win.shape[0]
    width = table.shape[1]
    per_w = n_win // (SC_CORES * SC_SUBCORES)

    @functools.partial(
        pl.kernel, mesh=_sc_mesh(),
        out_type=jax.ShapeDtypeStruct((TOP_K, n_win * SC_WINDOW, width), table.dtype),
        scratch_types=[pltpu.VMEM((TOP_K, SC_WINDOW), I32), pltpu.VMEM((2, SC_WINDOW, width), table.dtype),
                       pltpu.SemaphoreType.DMA, pltpu.SemaphoreType.DMA],
        name="sc_gather_rows")
    def run(table_hbm, pos_hbm, out_hbm, idx_v, rows_v, gsem, wsem):
        base = _sc_worker() * per_w

        @pl.loop(0, per_w)
        def _(j):
            w = base + j
            pltpu.sync_copy(pos_hbm.at[w], idx_v)
            for k in range(TOP_K):
                buf = rows_v.at[k % 2]
                pltpu.async_copy(table_hbm.at[idx_v.at[k]], buf, gsem).wait()
                pltpu.async_copy(buf, out_hbm.at[k, pl.ds(w * SC_WINDOW, SC_WINDOW)], wsem).wait()

    return run(table, pos_win)


def _grouped_ffn_kernel(te_ref, used_ref, xs_ref, *refs):
    w_refs, ys_ref = refs[:-1], refs[-1]

    @pl.when(pl.program_id(0) < used_ref[0])
    def _():
        def gate_up(j):
            wg_ref, wu_ref = w_refs[3 * j:3 * j + 2]
            lo, hi = _unpack_rows(xs_ref[j])
            lo, hi = lo.astype(BF16), hi.astype(BF16)
            return (_dot(lo, wg_ref[0, :HALF]) + _dot(hi, wg_ref[0, HALF:]),
                    _dot(lo, wu_ref[0, :HALF]) + _dot(hi, wu_ref[0, HALF:]))

        pre = gate_up(0)
        for j in range(TILES_PER_STEP):
            nxt = gate_up(j + 1) if j + 1 < TILES_PER_STEP else None
            act = (jax.nn.silu(pre[0]) * pre[1]).astype(BF16)
            ys_ref[j] = _pack_rows(_dot(act, w_refs[3 * j + 2][0]))
            pre = nxt


def _grouped_ffn(tile_expert, steps_used, xs, wg, wu, wd):
    n_rows, half = xs.shape
    n_steps = n_rows // (TILES_PER_STEP * ROW_TILE)

    def step_of(i, used):
        return jnp.minimum(i, used[0] - 1)

    row_spec = pl.BlockSpec((TILES_PER_STEP, ROW_TILE, half), lambda i, te, used: (0, step_of(i, used), 0))

    def exp_map(j):
        return lambda i, te, used: (te[j * n_steps + step_of(i, used)], 0, 0)

    w_specs = [pl.BlockSpec((1,) + w.shape[1:], exp_map(j)) for j in range(TILES_PER_STEP) for w in (wg, wu, wd)]
    ys = pl.pallas_call(
        _grouped_ffn_kernel,
        grid_spec=pltpu.PrefetchScalarGridSpec(
            num_scalar_prefetch=2,
            grid=(n_steps,),
            in_specs=[row_spec] + w_specs,
            out_specs=row_spec),
        out_shape=jax.ShapeDtypeStruct((TILES_PER_STEP, n_steps * ROW_TILE, half), U32),
        compiler_params=pltpu.CompilerParams(dimension_semantics=("arbitrary",), vmem_limit_bytes=VMEM_LIMIT),
        name="grouped_ffn",
    )(tile_expert, steps_used, xs.reshape(TILES_PER_STEP, n_steps * ROW_TILE, half),
      *([wg, wu, wd] * TILES_PER_STEP))
    return ys.reshape(n_rows, half)


def _combine_kernel(yt_ref, w_ref, x1_ref, p_ref, gffn_ref, wsg_ref, wsu_ref, wsd_ref,
                    gpi_ref, wpg_ref, wpp_ref, gpo_ref, *rest):
    out_ref = rest[-1]
    x1 = x1_ref[...]
    w = w_ref[...]
    h = _rms(x1, gffn_ref[...]).astype(BF16)
    shared = _dot((jax.nn.silu(_dot(h, wsg_ref[...])) * _dot(h, wsu_ref[...])).astype(BF16), wsd_ref[...])
    proj = _rms(_dot(p_ref[...].astype(BF16), wpp_ref[...]), gpo_ref[...])
    acc_lo = jnp.zeros((x1.shape[0], HALF), F32)
    acc_hi = jnp.zeros((x1.shape[0], HALF), F32)
    for k in range(TOP_K):
        lo, hi = _unpack_rows(yt_ref[k])
        wk = w[:, k:k + 1]
        acc_lo += wk * lo
        acc_hi += wk * hi
    x2 = x1 + shared + jnp.concatenate([acc_lo, acc_hi], axis=-1)
    gate = jax.nn.sigmoid(_dot(_rms(x2, gpi_ref[...]).astype(BF16), wpg_ref[...]))
    out_ref[...] = x2 + gate * proj


def _combine(yt, w_tok, x1, p, consts, tm, chunk, out_so_far):
    t, d = x1.shape
    first = chunk * (t // tm)
    row_spec = lambda w: pl.BlockSpec((tm, w), lambda i: (i, 0))
    full_spec = lambda w: pl.BlockSpec((tm, w), lambda i: (first + i, 0))
    carried = [] if out_so_far is None else [out_so_far]
    return pl.pallas_call(
        _combine_kernel,
        grid=(t // tm,),
        in_specs=[pl.BlockSpec((TOP_K, tm, HALF), lambda i: (0, i, 0)), row_spec(TOP_K), row_spec(d),
                  full_spec(PLE_DIM)] + [_const_spec(c.shape) for c in consts]
                 + [pl.BlockSpec(memory_space=pl.ANY)] * len(carried),
        out_specs=full_spec(d),
        out_shape=jax.ShapeDtypeStruct((t * MOE_CHUNKS, d), F32),
        input_output_aliases={4 + len(consts): 0} if carried else {},
        compiler_params=pltpu.CompilerParams(dimension_semantics=("arbitrary",), vmem_limit_bytes=VMEM_LIMIT),
        name="combine_ple",
    )(yt, w_tok, x1, p, *consts, *carried)


def _rotate_half(a):
    half = QK_ROPE // 2
    return jnp.concatenate([-a[..., half:], a[..., :half]], axis=-1)


def _with_rotate_half(w):
    return jnp.concatenate([w, _rotate_half(w[..., -QK_ROPE:])], axis=-1)


def _rope_gain_table(g, scale):
    half = QK_ROPE // 2
    return (jnp.concatenate([g, g[half:], g[:half]]) * scale).reshape(1, LANES)


def _layer(x, p, pos, g_mix, w_in, sgu_ln_g, sgu_ln_b, sgu_w, sgu_b, mla_g_qa, mla_w_qb, mla_g_kva, mla_w_kvb,
           qk_g_q_nope, qk_g_k_nope, qk_g_q_rope, qk_g_k_rope, w_o, g_ffn, w_router, b_router,
           w_exp_gate, w_exp_up, w_exp_down, w_sh_gate, w_sh_up, w_sh_down,
           g_ple_in, w_ple_gate, w_ple_proj, g_ple_out):
    b, s, d = x.shape
    t = b * s
    row = lambda a: a.reshape(1, -1)
    sizes = [d, d, Q_LORA, KV_LORA, QK_ROPE, d, d]
    offs = [0]
    for sz in sizes:
        offs.append(offs[-1] + sz)
    w_main = w_in[:, :offs[4]]
    w_r, w_ga, w_gb = [w_in[:, offs[i]:offs[i + 1]] for i in (4, 5, 6)]
    w_r = _with_rotate_half(w_r)
    wqb = _with_rotate_half(mla_w_qb.reshape(Q_LORA, HEADS, QK_DIM)).reshape(Q_LORA, HEADS * QK_PAD)
    wkvb = mla_w_kvb.reshape(KV_LORA, HEADS, QK_NOPE + V_DIM)
    wkb = wkvb[:, :, :QK_NOPE].reshape(KV_LORA, HEADS * QK_NOPE)
    wvbt = wkvb[:, :, QK_NOPE:].reshape(KV_LORA, HEADS * V_DIM).T
    sgu_bias = jnp.repeat(sgu_b.T, d // SGU_GROUPS, axis=1)
    q_scale = QK_DIM ** -0.5

    consts1 = [row(g_mix), w_main, w_r, w_ga, w_gb, row(sgu_ln_g), row(sgu_ln_b), sgu_w, sgu_bias,
               row(mla_g_qa), wqb, row(mla_g_kva), wkb, wvbt, row(qk_g_q_nope) * q_scale, row(qk_g_k_nope),
               _rope_gain_table(qk_g_q_rope, q_scale), _rope_gain_table(qk_g_k_rope, 1.0)]
    pos_lanes = jnp.broadcast_to(pos.astype(F32)[..., None], (b, s, LANES))
    ma, gb, q, k, vt = _mixer_prep(x, pos_lanes, consts1, tm=512)
    merged, w_gate_b, w_up_b, w_down_b = _attention(q, k, vt, ma, gb, (w_exp_gate, w_exp_up, w_exp_down),
                                                    tq=512, heads_per_step=2)

    consts3 = [w_o, row(g_ffn), w_router.T, b_router.reshape(-1, 1)]
    consts6 = [row(g_ffn), w_sh_gate, w_sh_up, w_sh_down, row(g_ple_in), w_ple_gate, w_ple_proj, row(g_ple_out)]
    x2d, merged2d, p2d = (a.reshape(t, -1) for a in (x, merged, p))
    tc = t // MOE_CHUNKS
    max_tiles = (tc * TOP_K) // ROW_TILE + N_EXPERTS
    tile_ids = jnp.arange(max_tiles, dtype=I32)
    n_steps = max_tiles // TILES_PER_STEP
    out = None
    for chunk in range(MOE_CHUNKS):
        x1, h_packed, comb_t, rank_t, counts = _post_attn(x2d, merged2d, consts3, tm=512, chunk=chunk)

        counts = counts[:, 0].astype(I32)
        tiles_per_expert = (counts + ROW_TILE - 1) // ROW_TILE
        tile_end = jnp.cumsum(tiles_per_expert)
        offsets = ((tile_end - tiles_per_expert) * ROW_TILE).astype(F32).reshape(N_EXPERTS, 1)
        steps_used = (tile_end[-1:] + TILES_PER_STEP - 1) // TILES_PER_STEP
        sorted_tile = (tile_ids // n_steps) * steps_used + jnp.minimum(tile_ids % n_steps, steps_used - 1)
        tile_expert = jnp.minimum(jnp.sum((tile_end[None, :] <= sorted_tile[:, None]).astype(I32), axis=1),
                                  N_EXPERTS - 1)
        layout = jnp.concatenate([steps_used, (n_steps - steps_used) * ROW_TILE]).astype(F32).reshape(1, 2)

        pos_t, w_tok = _route_lists(comb_t, rank_t, offsets, layout, tm=512)
        pos_win = pos_t.reshape(TOP_K, tc // SC_WINDOW, SC_WINDOW).transpose(1, 0, 2)

        xs = _sc_scatter_rows(h_packed, pos_win, max_tiles * ROW_TILE)
        ys = _grouped_ffn(tile_expert, steps_used, xs, w_gate_b, w_up_b, w_down_b)
        yt = _sc_gather_rows(ys, pos_win)
        out = _combine(yt, w_tok, x1, p2d, consts6, tm=512, chunk=chunk, out_so_far=out)
    return out.reshape(b, s, d)


def kernel(x, p, positions, g_mix, w_in, sgu_ln_g, sgu_ln_b, sgu_w, sgu_b, mla_g_qa, mla_w_qb, mla_g_kva, mla_w_kvb, qk_g_q_nope, qk_g_k_nope, qk_g_q_rope, qk_g_k_rope, w_o, g_ffn, w_router, b_router, w_exp_gate, w_exp_up, w_exp_down, w_sh_gate, w_sh_up, w_sh_down, g_ple_in, w_ple_gate, w_ple_proj, g_ple_out):
    params = (g_mix, w_in, sgu_ln_g, sgu_ln_b, sgu_w, sgu_b, mla_g_qa, mla_w_qb, mla_g_kva, mla_w_kvb,
              qk_g_q_nope, qk_g_k_nope, qk_g_q_rope, qk_g_k_rope, w_o, g_ffn, w_router, b_router,
              w_exp_gate, w_exp_up, w_exp_down, w_sh_gate, w_sh_up, w_sh_down,
              g_ple_in, w_ple_gate, w_ple_proj, g_ple_out)
    for l in range(g_mix.shape[0]):
        x = _layer(x, p[l], positions, *[a[l] for a in params])
    return x
```

```python
import functools
import math

import jax
import jax.numpy as jnp
from jax import lax
from jax.experimental import pallas as pl
from jax.experimental.pallas import tpu as pltpu
from jax.experimental.pallas import tpu_sc as plsc

D_MODEL = 1024
PLE_DIM = 256
SGU_CHUNK = 128
SGU_GROUPS = 8
V_DIM = 128
HEADS = 8
QK_NOPE = 128
QK_ROPE = 64
QK_DIM = QK_NOPE + QK_ROPE
QK_PAD = 256
Q_LORA = 384
KV_LORA = 256
ROPE_THETA = 10000.0
N_EXPERTS = 64
N_GROUPS = 8
EXPERTS_PER_GROUP = 8
TOPK_GROUPS = 4
TOP_K = 8
EXPERT_FF = 256
ROUTED_SCALE = 2.5
NORM_EPS = 1e-6
LN_EPS = 1e-5

LANES = 128
MXU_COLS = 256
VMEM_LIMIT = 56 * 1024 * 1024
SC_CORES = 2
SC_SUBCORES = 16
SC_WINDOW = 64
ROW_TILE = 256
TILES_PER_STEP = 8
MOE_CHUNKS = 2
HALF = D_MODEL // 2

F32 = jnp.float32
BF16 = jnp.bfloat16
U32 = jnp.uint32
I32 = jnp.int32


def _dot(a, b):
    return lax.dot_general(a, b, (((1,), (0,)), ((), ())), preferred_element_type=F32)


def _rms(xf, g, width=None):
    width = xf.shape[-1] if width is None else width
    ms = jnp.sum(xf * xf, axis=-1, keepdims=True) * (1.0 / width)
    return xf * lax.rsqrt(ms + NORM_EPS) * g


def _pack_rows(y):
    lo = pltpu.bitcast(y[:, :HALF].astype(BF16).astype(F32), U32) >> 16
    hi = pltpu.bitcast(y[:, HALF:].astype(BF16).astype(F32), U32) & jnp.uint32(0xFFFF0000)
    return lo | hi


def _unpack_rows(w):
    lo = pltpu.bitcast(w << 16, F32)
    hi = pltpu.bitcast(w & jnp.uint32(0xFFFF0000), F32)
    return lo, hi


def _rope_table(pos_f):
    lane = lax.broadcasted_iota(I32, (1, LANES), 1)
    freq = (lane % (QK_ROPE // 2)).astype(F32)
    inv_freq = jnp.exp(freq * (-math.log(ROPE_THETA) * 2.0 / QK_ROPE))
    phase = jnp.where(lane < QK_ROPE, 0.0, math.pi / 2)
    return jnp.cos(pos_f * inv_freq - phase)


def _norm_rope(piece, table_g):
    lane = lax.broadcasted_iota(I32, (1, LANES), 1)
    ssq = jnp.sum(jnp.where(lane < QK_ROPE, piece * piece, 0.0), axis=-1, keepdims=True)
    z = piece * lax.rsqrt(ssq * (1.0 / QK_ROPE) + NORM_EPS) * table_g
    return z + pltpu.roll(z, QK_ROPE, axis=1)


def _const_spec(shape):
    return pl.BlockSpec(shape, lambda *_: (0,) * len(shape), pipeline_mode=pl.Buffered(1))


def _mixer_prep_kernel(x_ref, pos_ref, g_mix_ref, wm_ref, wr_ref, wga_ref, wgb_ref,
                       lng_ref, lnb_ref, sw_ref, sb_ref, gqa_ref, wqb_ref, gkva_ref, wkb_ref, wvbt_ref,
                       gqn_ref, gkn_ref, gqr_ref, gkr_ref,
                       ma_ref, gb_ref, q_ref, k_ref, vt_ref):
    tm = x_ref.shape[1]
    xn = _rms(x_ref[0], g_mix_ref[...]).astype(BF16)

    rope_t = _rope_table(pos_ref[0])
    d = x_ref.shape[2]
    qn = _rms(_dot(xn, wm_ref[:, 2 * d:2 * d + Q_LORA]), gqa_ref[...]).astype(BF16)
    kvn = _rms(_dot(xn, wm_ref[:, 2 * d + Q_LORA:]), gkva_ref[...]).astype(BF16)
    lane = lax.broadcasted_iota(I32, (1, LANES), 1)
    kpe = jnp.where(lane < QK_ROPE, _norm_rope(_dot(xn, wr_ref[...]), rope_t * gkr_ref[...]), 0.0).astype(BF16)
    q_rope_t = rope_t * gqr_ref[...]

    def head_pair(pair):
        ps = slice(pair * MXU_COLS, (pair + 1) * MXU_COLS)
        k2 = _dot(kvn, wkb_ref[:, ps])
        v2t = lax.dot_general(wvbt_ref[ps, :].astype(BF16), kvn, (((1,), (1,)), ((), ())),
                              preferred_element_type=F32).astype(BF16)
        for half in range(2):
            g = 2 * pair + half
            hs = slice(half * LANES, (half + 1) * LANES)
            qh = _dot(qn, wqb_ref[:, g * QK_PAD:(g + 1) * QK_PAD])
            q_ref[0, g, :, :QK_NOPE] = _rms(qh[:, :QK_NOPE], gqn_ref[...]).astype(BF16)
            q_ref[0, g, :, QK_NOPE:] = _norm_rope(qh[:, QK_NOPE:], q_rope_t).astype(BF16)
            k_ref[0, g, :, :QK_NOPE] = _rms(k2[:, hs], gkn_ref[...]).astype(BF16)
            k_ref[0, g, :, QK_NOPE:] = kpe
            vt_ref[0, g] = v2t[hs, :]

    head_pair(0)

    gv = jax.nn.gelu(_dot(xn, wm_ref[:, d:2 * d]))
    mu = jnp.mean(gv, axis=-1, keepdims=True)
    vc = gv - mu
    var = jnp.mean(vc * vc, axis=-1, keepdims=True)
    vn = (vc * lax.rsqrt(var + LN_EPS) * lng_ref[...] + lnb_ref[...]).astype(BF16)
    row = lax.broadcasted_iota(I32, (SGU_CHUNK, SGU_CHUNK), 0)
    col = lax.broadcasted_iota(I32, (SGU_CHUNK, SGU_CHUNK), 1)
    causal = col <= row
    n_chunks = tm // SGU_CHUNK

    def sgu_pair(pair):
        ps = slice(pair * MXU_COLS, (pair + 1) * MXU_COLS)
        gu2 = jax.nn.gelu(_dot(xn, wm_ref[:, ps]))
        ga2 = jax.nn.sigmoid(_dot(xn, wga_ref[:, ps]))
        gb_ref[0, :, ps] = jax.nn.sigmoid(_dot(xn, wgb_ref[:, ps])).astype(BF16)
        for half in range(2):
            g = 2 * pair + half
            hs = slice(half * LANES, (half + 1) * LANES)
            cs = slice(g * SGU_CHUNK, (g + 1) * SGU_CHUNK)
            wg = jnp.where(causal, sw_ref[g], 0.0).astype(BF16)
            vcat = jnp.concatenate([vn[c * SGU_CHUNK:(c + 1) * SGU_CHUNK, cs] for c in range(n_chunks)], axis=1)
            mixed = _dot(wg, vcat)
            for c in range(n_chunks):
                rs = slice(c * SGU_CHUNK, (c + 1) * SGU_CHUNK)
                m = mixed[:, c * SGU_CHUNK:(c + 1) * SGU_CHUNK] + sb_ref[:, cs]
                ma_ref[0, rs, cs] = (ga2[rs, hs] * gu2[rs, hs] * m).astype(BF16)

    n_pairs = SGU_GROUPS // 2
    for pair in range(n_pairs):
        if pair + 1 < n_pairs:
            head_pair(pair + 1)
        sgu_pair(pair)


def _mixer_prep(x, pos, consts, tm):
    b, s, d = x.shape
    grid = (b, s // tm)
    row_spec = lambda w: pl.BlockSpec((1, tm, w), lambda i, j: (i, j, 0))
    head_spec = lambda w: pl.BlockSpec((1, HEADS, tm, w), lambda i, j: (i, 0, j, 0))
    return pl.pallas_call(
        _mixer_prep_kernel,
        grid=grid,
        in_specs=[row_spec(d), row_spec(LANES)] + [_const_spec(c.shape) for c in consts],
        out_specs=[row_spec(d), row_spec(d), head_spec(QK_PAD), head_spec(QK_PAD),
                   pl.BlockSpec((1, HEADS, V_DIM, tm), lambda i, j: (i, 0, 0, j))],
        out_shape=[jax.ShapeDtypeStruct((b, s, d), BF16), jax.ShapeDtypeStruct((b, s, d), BF16),
                   jax.ShapeDtypeStruct((b, HEADS, s, QK_PAD), BF16),
                   jax.ShapeDtypeStruct((b, HEADS, s, QK_PAD), BF16),
                   jax.ShapeDtypeStruct((b, HEADS, V_DIM, s), BF16)],
        compiler_params=pltpu.CompilerParams(dimension_semantics=("arbitrary", "arbitrary"),
                                             vmem_limit_bytes=VMEM_LIMIT),
        name="mixer_prep",
    )(x, pos, *consts)


def _attn_kernel(q_ref, k_ref, vt_ref, ma_ref, gb_ref, *refs, tq):
    n_w = (len(refs) - 1) // 2
    o_ref = refs[n_w]
    for w_ref, wb_ref in zip(refs[:n_w], refs[n_w + 1:]):
        wb_ref[...] = w_ref[...].astype(BF16)
    s = q_ref.shape[2]
    key = lax.broadcasted_iota(I32, (tq, tq), 0)
    qry = lax.broadcasted_iota(I32, (tq, tq), 1)
    diag_mask = key <= qry
    heads = range(q_ref.shape[1])
    n_tiles = s // tq

    def score_dots(qi):
        return [lax.dot_general(k_ref[0, h, :(qi + 1) * tq, :], q_ref[0, h, qi * tq:(qi + 1) * tq, :],
                                (((1,), (1,)), ((), ())), preferred_element_type=F32) for h in heads]

    scores = score_dots(0)
    for qi in range(n_tiles):
        qs = slice(qi * tq, (qi + 1) * tq)
        n_keys = (qi + 1) * tq
        next_scores = score_dots(qi + 1) if qi + 1 < n_tiles else None
        probs, sums = [], []
        for h in heads:
            sc = scores[h]
            last = jnp.where(diag_mask, sc[n_keys - tq:], -jnp.inf)
            sc = last if qi == 0 else jnp.concatenate([sc[:n_keys - tq], last], axis=0)
            p = jnp.exp(sc - jnp.max(sc, axis=0, keepdims=True))
            sums.append(jnp.sum(p, axis=0, keepdims=True))
            probs.append(p.astype(BF16))
        for h in heads:
            acc = _dot(vt_ref[0, h, :, :n_keys], probs[h])
            cs = slice(h * V_DIM, (h + 1) * V_DIM)
            o = (acc / sums[h]).T
            o_ref[0, qs, cs] = (ma_ref[0, qs, cs].astype(F32) + gb_ref[0, qs, cs].astype(F32) * o).astype(BF16)
        scores = next_scores


def _attention(q, k, vt, ma, gb, expert_weights, tq, heads_per_step):
    b, h, s, _ = q.shape
    hp = heads_per_step
    n_steps = b * (h // hp)
    w_specs = [pl.BlockSpec((w.shape[0] // n_steps,) + w.shape[1:], lambda i, j: (i * (h // hp) + j, 0, 0))
               for w in expert_weights]
    col_spec = pl.BlockSpec((1, s, hp * V_DIM), lambda i, j: (i, 0, j))
    return pl.pallas_call(
        functools.partial(_attn_kernel, tq=tq),
        grid=(b, h // hp),
        in_specs=[pl.BlockSpec((1, hp, s, QK_PAD), lambda i, j: (i, j, 0, 0)),
                  pl.BlockSpec((1, hp, s, QK_PAD), lambda i, j: (i, j, 0, 0)),
                  pl.BlockSpec((1, hp, V_DIM, s), lambda i, j: (i, j, 0, 0)), col_spec, col_spec] + w_specs,
        out_specs=[col_spec] + w_specs,
        out_shape=[jax.ShapeDtypeStruct((b, s, h * V_DIM), BF16)]
                  + [jax.ShapeDtypeStruct(w.shape, BF16) for w in expert_weights],
        compiler_params=pltpu.CompilerParams(dimension_semantics=("arbitrary", "arbitrary"),
                                             vmem_limit_bytes=VMEM_LIMIT),
        name="mla_attention",
    )(q, k, vt, ma, gb, *expert_weights)


def _route(sel, scores):
    e, tm = sel.shape
    sel3 = sel.reshape(N_GROUPS, EXPERTS_PER_GROUP, tm)
    sub = lax.broadcasted_iota(I32, sel3.shape, 1)
    m1 = jnp.max(sel3, axis=1, keepdims=True)
    first = jnp.min(jnp.where(sel3 == m1, sub, EXPERTS_PER_GROUP), axis=1, keepdims=True)
    m2 = jnp.max(jnp.where(sub == first, -jnp.inf, sel3), axis=1, keepdims=True)
    gscore = (m1 + m2).reshape(N_GROUPS, tm)
    gid = lax.broadcasted_iota(I32, (N_GROUPS, tm), 0)
    grank = jnp.zeros((N_GROUPS, tm), I32)
    for g in range(N_GROUPS):
        other = gscore[g:g + 1]
        grank += ((other > gscore) | ((other == gscore) & (g < gid))).astype(I32)
    gmask = grank < TOPK_GROUPS
    emask = jnp.broadcast_to(gmask[:, None, :], sel3.shape).reshape(e, tm)
    remaining = jnp.where(emask, sel, -jnp.inf)
    eid = lax.broadcasted_iota(I32, (e, tm), 0)
    chosen = jnp.zeros((e, tm), jnp.bool_)
    for _ in range(TOP_K):
        best = jnp.max(remaining, axis=0, keepdims=True)
        pick = eid == jnp.min(jnp.where(remaining == best, eid, e), axis=0, keepdims=True)
        chosen = chosen | pick
        remaining = jnp.where(pick, -jnp.inf, remaining)
    w = jnp.where(chosen, scores, 0.0)
    return chosen, w / jnp.sum(w, axis=0, keepdims=True) * ROUTED_SCALE


def _post_attn_kernel(x_ref, merged_ref, wo_ref, gffn_ref, wrt_ref, br_ref,
                      x1_ref, hp_ref, comb_ref, rank_ref, cnt_ref, run_ref):
    tm = x_ref.shape[0]

    @pl.when(pl.program_id(0) == 0)
    def _():
        run_ref[...] = jnp.zeros_like(run_ref)

    x1 = x_ref[...] + _dot(merged_ref[...], wo_ref[...])
    x1_ref[...] = x1
    h2 = _rms(x1, gffn_ref[...])
    hp_ref[...] = _pack_rows(h2)
    logits_t = lax.dot_general(wrt_ref[...], h2, (((1,), (1,)), ((), ())),
                               preferred_element_type=F32, precision=lax.Precision.HIGHEST)
    scores = jax.nn.sigmoid(logits_t)
    chosen, comb_t = _route(scores + br_ref[...], scores)
    comb_ref[...] = comb_t
    a = lax.broadcasted_iota(I32, (tm, tm), 0)
    b = lax.broadcasted_iota(I32, (tm, tm), 1)
    before = (a < b).astype(BF16)
    chosen_f = chosen.astype(F32)
    prefix = _dot(chosen_f.astype(BF16), before)
    run = run_ref[:, 0:1]
    rank_ref[...] = jnp.where(chosen, prefix + run, -1.0)
    run_ref[...] += jnp.sum(chosen_f, axis=1, keepdims=True)
    cnt_ref[...] = run_ref[...]


def _post_attn(x, merged, consts, tm, chunk):
    t, d = x.shape[0] // MOE_CHUNKS, x.shape[1]
    first = chunk * (t // tm)
    row_spec = lambda w: pl.BlockSpec((tm, w), lambda i: (i, 0))
    col_spec = pl.BlockSpec((N_EXPERTS, tm), lambda i: (0, i))
    return pl.pallas_call(
        _post_attn_kernel,
        grid=(t // tm,),
        in_specs=[pl.BlockSpec((tm, d), lambda i: (first + i, 0))] * 2 + [_const_spec(c.shape) for c in consts],
        out_specs=[row_spec(d), row_spec(HALF), col_spec, col_spec, _const_spec((N_EXPERTS, LANES))],
        out_shape=[jax.ShapeDtypeStruct((t, d), F32), jax.ShapeDtypeStruct((t, HALF), U32),
                   jax.ShapeDtypeStruct((N_EXPERTS, t), F32), jax.ShapeDtypeStruct((N_EXPERTS, t), F32),
                   jax.ShapeDtypeStruct((N_EXPERTS, LANES), F32)],
        scratch_shapes=[pltpu.VMEM((N_EXPERTS, LANES), F32)],
        compiler_params=pltpu.CompilerParams(dimension_semantics=("arbitrary",), vmem_limit_bytes=VMEM_LIMIT),
        name="post_attn_router",
    )(x, merged, *consts)


def _route_lists_kernel(comb_ref, rank_ref, off_ref, lay_ref, pos_ref, w_ref):
    rank = rank_ref[...]
    chosen = rank >= 0.0
    e = rank.shape[0]
    lower = (lax.broadcasted_iota(I32, (e, e), 1) < lax.broadcasted_iota(I32, (e, e), 0)).astype(BF16)
    slot = _dot(lower, chosen.astype(BF16))
    pos_sorted = rank + off_ref[...]
    tile = jnp.floor(pos_sorted * (1.0 / ROW_TILE))
    pos_full = pos_sorted + jnp.floor((tile + 0.5) / lay_ref[0:1, 0:1]) * lay_ref[0:1, 1:2]
    comb = comb_ref[...]
    pos_rows, w_rows = [], []
    for k in range(TOP_K):
        pick = chosen & (slot == float(k))
        pos_rows.append(jnp.sum(jnp.where(pick, pos_full, 0.0), axis=0, keepdims=True))
        w_rows.append(jnp.sum(jnp.where(pick, comb, 0.0), axis=0, keepdims=True))
    pos_ref[...] = jnp.concatenate(pos_rows, axis=0).astype(I32)
    w_ref[...] = jnp.concatenate(w_rows, axis=0).T


def _route_lists(comb_t, rank_t, offsets, layout, tm):
    e, t = comb_t.shape
    col_spec = pl.BlockSpec((e, tm), lambda i: (0, i))
    return pl.pallas_call(
        _route_lists_kernel,
        grid=(t // tm,),
        in_specs=[col_spec, col_spec, _const_spec((e, 1)), _const_spec(layout.shape)],
        out_specs=[pl.BlockSpec((TOP_K, tm), lambda i: (0, i)), pl.BlockSpec((tm, TOP_K), lambda i: (i, 0))],
        out_shape=[jax.ShapeDtypeStruct((TOP_K, t), I32), jax.ShapeDtypeStruct((t, TOP_K), F32)],
        compiler_params=pltpu.CompilerParams(dimension_semantics=("arbitrary",)),
        name="route_lists",
    )(comb_t, rank_t, offsets, layout)


def _sc_mesh():
    return plsc.VectorSubcoreMesh(core_axis_name="c", subcore_axis_name="s",
                                  num_cores=SC_CORES, num_subcores=SC_SUBCORES)


def _sc_worker():
    return lax.axis_index("s") * SC_CORES + lax.axis_index("c")


def _sc_scatter_rows(src, pos_win, n_rows):
    t, width = src.shape
    n_win = t // SC_WINDOW
    per_w = n_win // (SC_CORES * SC_SUBCORES)

    @functools.partial(
        pl.kernel, mesh=_sc_mesh(), out_type=jax.ShapeDtypeStruct((n_rows, width), src.dtype),
        scratch_types=[pltpu.VMEM((TOP_K, SC_WINDOW), I32), pltpu.VMEM((SC_WINDOW, width), src.dtype),
                       pltpu.SemaphoreType.DMA],
        name="sc_scatter_rows")
    def run(src_hbm, pos_hbm, out_hbm, idx_v, rows_v, sem):
        base = _sc_worker() * per_w

        @pl.loop(0, per_w)
        def _(j):
            w = base + j
            pltpu.sync_copy(pos_hbm.at[w], idx_v)
            pltpu.sync_copy(src_hbm.at[pl.ds(w * SC_WINDOW, SC_WINDOW)], rows_v)
            copies = [pltpu.async_copy(rows_v, out_hbm.at[idx_v.at[k]], sem) for k in range(TOP_K)]
            for c in copies:
                c.wait()

    return run(src, pos_win)


def _sc_gather_rows(table, pos_win):
    n_win = pos_win.shape[0]
    width = table.shape[1]
    per_w = n_win // (SC_CORES * SC_SUBCORES)

    @functools.partial(
        pl.kernel, mesh=_sc_mesh(),
        out_type=jax.ShapeDtypeStruct((TOP_K, n_win * SC_WINDOW, width), table.dtype),
        scratch_types=[pltpu.VMEM((TOP_K, SC_WINDOW), I32), pltpu.VMEM((2, SC_WINDOW, width), table.dtype),
                       pltpu.SemaphoreType.DMA, pltpu.SemaphoreType.DMA],
        name="sc_gather_rows")
    def run(table_hbm, pos_hbm, out_hbm, idx_v, rows_v, gsem, wsem):
        base = _sc_worker() * per_w

        @pl.loop(0, per_w)
        def _(j):
            w = base + j
            pltpu.sync_copy(pos_hbm.at[w], idx_v)
            for k in range(TOP_K):
                buf = rows_v.at[k % 2]
                pltpu.async_copy(table_hbm.at[idx_v.at[k]], buf, gsem).wait()
                pltpu.async_copy(buf, out_hbm.at[k, pl.ds(w * SC_WINDOW, SC_WINDOW)], wsem).wait()

    return run(table, pos_win)


def _grouped_ffn_kernel(te_ref, used_ref, xs_ref, *refs):
    w_refs, ys_ref = refs[:-1], refs[-1]

    @pl.when(pl.program_id(0) < used_ref[0])
    def _():
        def gate_up(j):
            wg_ref, wu_ref = w_refs[3 * j:3 * j + 2]
            lo, hi = _unpack_rows(xs_ref[j])
            lo, hi = lo.astype(BF16), hi.astype(BF16)
            return (_dot(lo, wg_ref[0, :HALF]) + _dot(hi, wg_ref[0, HALF:]),
                    _dot(lo, wu_ref[0, :HALF]) + _dot(hi, wu_ref[0, HALF:]))

        pre = gate_up(0)
        for j in range(TILES_PER_STEP):
            nxt = gate_up(j + 1) if j + 1 < TILES_PER_STEP else None
            act = (jax.nn.silu(pre[0]) * pre[1]).astype(BF16)
            ys_ref[j] = _pack_rows(_dot(act, w_refs[3 * j + 2][0]))
            pre = nxt


def _grouped_ffn(tile_expert, steps_used, xs, wg, wu, wd):
    n_rows, half = xs.shape
    n_steps = n_rows // (TILES_PER_STEP * ROW_TILE)

    def step_of(i, used):
        return jnp.minimum(i, used[0] - 1)

    row_spec = pl.BlockSpec((TILES_PER_STEP, ROW_TILE, half), lambda i, te, used: (0, step_of(i, used), 0))

    def exp_map(j):
        return lambda i, te, used: (te[j * n_steps + step_of(i, used)], 0, 0)

    w_specs = [pl.BlockSpec((1,) + w.shape[1:], exp_map(j)) for j in range(TILES_PER_STEP) for w in (wg, wu, wd)]
    ys = pl.pallas_call(
        _grouped_ffn_kernel,
        grid_spec=pltpu.PrefetchScalarGridSpec(
            num_scalar_prefetch=2,
            grid=(n_steps,),
            in_specs=[row_spec] + w_specs,
            out_specs=row_spec),
        out_shape=jax.ShapeDtypeStruct((TILES_PER_STEP, n_steps * ROW_TILE, half), U32),
        compiler_params=pltpu.CompilerParams(dimension_semantics=("arbitrary",), vmem_limit_bytes=VMEM_LIMIT),
        name="grouped_ffn",
    )(tile_expert, steps_used, xs.reshape(TILES_PER_STEP, n_steps * ROW_TILE, half),
      *([wg, wu, wd] * TILES_PER_STEP))
    return ys.reshape(n_rows, half)


def _combine_kernel(yt_ref, w_ref, x1_ref, p_ref, gffn_ref, wsg_ref, wsu_ref, wsd_ref,
                    gpi_ref, wpg_ref, wpp_ref, gpo_ref, *rest):
    out_ref = rest[-1]
    x1 = x1_ref[...]
    w = w_ref[...]
    h = _rms(x1, gffn_ref[...]).astype(BF16)
    shared = _dot((jax.nn.silu(_dot(h, wsg_ref[...])) * _dot(h, wsu_ref[...])).astype(BF16), wsd_ref[...])
    proj = _rms(_dot(p_ref[...].astype(BF16), wpp_ref[...]), gpo_ref[...])
    acc_lo = jnp.zeros((x1.shape[0], HALF), F32)
    acc_hi = jnp.zeros((x1.shape[0], HALF), F32)
    for k in range(TOP_K):
        lo, hi = _unpack_rows(yt_ref[k])
        wk = w[:, k:k + 1]
        acc_lo += wk * lo
        acc_hi += wk * hi
    x2 = x1 + shared + jnp.concatenate([acc_lo, acc_hi], axis=-1)
    gate = jax.nn.sigmoid(_dot(_rms(x2, gpi_ref[...]).astype(BF16), wpg_ref[...]))
    out_ref[...] = x2 + gate * proj


def _combine(yt, w_tok, x1, p, consts, tm, chunk, out_so_far):
    t, d = x1.shape
    first = chunk * (t // tm)
    row_spec = lambda w: pl.BlockSpec((tm, w), lambda i: (i, 0))
    full_spec = lambda w: pl.BlockSpec((tm, w), lambda i: (first + i, 0))
    carried = [] if out_so_far is None else [out_so_far]
    return pl.pallas_call(
        _combine_kernel,
        grid=(t // tm,),
        in_specs=[pl.BlockSpec((TOP_K, tm, HALF), lambda i: (0, i, 0)), row_spec(TOP_K), row_spec(d),
                  full_spec(PLE_DIM)] + [_const_spec(c.shape) for c in consts]
                 + [pl.BlockSpec(memory_space=pl.ANY)] * len(carried),
        out_specs=full_spec(d),
        out_shape=jax.ShapeDtypeStruct((t * MOE_CHUNKS, d), F32),
        input_output_aliases={4 + len(consts): 0} if carried else {},
        compiler_params=pltpu.CompilerParams(dimension_semantics=("arbitrary",), vmem_limit_bytes=VMEM_LIMIT),
        name="combine_ple",
    )(yt, w_tok, x1, p, *consts, *carried)


def _rotate_half(a):
    half = QK_ROPE // 2
    return jnp.concatenate([-a[..., half:], a[..., :half]], axis=-1)


def _with_rotate_half(w):
    return jnp.concatenate([w, _rotate_half(w[..., -QK_ROPE:])], axis=-1)


def _rope_gain_table(g, scale):
    half = QK_ROPE // 2
    return (jnp.concatenate([g, g[half:], g[:half]]) * scale).reshape(1, LANES)


def _layer(x, p, pos, g_mix, w_in, sgu_ln_g, sgu_ln_b, sgu_w, sgu_b, mla_g_qa, mla_w_qb, mla_g_kva, mla_w_kvb,
           qk_g_q_nope, qk_g_k_nope, qk_g_q_rope, qk_g_k_rope, w_o, g_ffn, w_router, b_router,
           w_exp_gate, w_exp_up, w_exp_down, w_sh_gate, w_sh_up, w_sh_down,
           g_ple_in, w_ple_gate, w_ple_proj, g_ple_out):
    b, s, d = x.shape
    t = b * s
    row = lambda a: a.reshape(1, -1)
    sizes = [d, d, Q_LORA, KV_LORA, QK_ROPE, d, d]
    offs = [0]
    for sz in sizes:
        offs.append(offs[-1] + sz)
    w_main = w_in[:, :offs[4]]
    w_r, w_ga, w_gb = [w_in[:, offs[i]:offs[i + 1]] for i in (4, 5, 6)]
    w_r = _with_rotate_half(w_r)
    wqb = _with_rotate_half(mla_w_qb.reshape(Q_LORA, HEADS, QK_DIM)).reshape(Q_LORA, HEADS * QK_PAD)
    wkvb = mla_w_kvb.reshape(KV_LORA, HEADS, QK_NOPE + V_DIM)
    wkb = wkvb[:, :, :QK_NOPE].reshape(KV_LORA, HEADS * QK_NOPE)
    wvbt = wkvb[:, :, QK_NOPE:].reshape(KV_LORA, HEADS * V_DIM).T
    sgu_bias = jnp.repeat(sgu_b.T, d // SGU_GROUPS, axis=1)
    q_scale = QK_DIM ** -0.5

    consts1 = [row(g_mix), w_main, w_r, w_ga, w_gb, row(sgu_ln_g), row(sgu_ln_b), sgu_w, sgu_bias,
               row(mla_g_qa), wqb, row(mla_g_kva), wkb, wvbt, row(qk_g_q_nope) * q_scale, row(qk_g_k_nope),
               _rope_gain_table(qk_g_q_rope, q_scale), _rope_gain_table(qk_g_k_rope, 1.0)]
    pos_lanes = jnp.broadcast_to(pos.astype(F32)[..., None], (b, s, LANES))
    ma, gb, q, k, vt = _mixer_prep(x, pos_lanes, consts1, tm=512)
    merged, w_gate_b, w_up_b, w_down_b = _attention(q, k, vt, ma, gb, (w_exp_gate, w_exp_up, w_exp_down),
                                                    tq=512, heads_per_step=2)

    consts3 = [w_o, row(g_ffn), w_router.T, b_router.reshape(-1, 1)]
    consts6 = [row(g_ffn), w_sh_gate, w_sh_up, w_sh_down, row(g_ple_in), w_ple_gate, w_ple_proj, row(g_ple_out)]
    x2d, merged2d, p2d = (a.reshape(t, -1) for a in (x, merged, p))
    tc = t // MOE_CHUNKS
    max_tiles = (tc * TOP_K) // ROW_TILE + N_EXPERTS
    tile_ids = jnp.arange(max_tiles, dtype=I32)
    n_steps = max_tiles // TILES_PER_STEP
    out = None
    for chunk in range(MOE_CHUNKS):
        x1, h_packed, comb_t, rank_t, counts = _post_attn(x2d, merged2d, consts3, tm=512, chunk=chunk)

        counts = counts[:, 0].astype(I32)
        tiles_per_expert = (counts + ROW_TILE - 1) // ROW_TILE
        tile_end = jnp.cumsum(tiles_per_expert)
        offsets = ((tile_end - tiles_per_expert) * ROW_TILE).astype(F32).reshape(N_EXPERTS, 1)
        steps_used = (tile_end[-1:] + TILES_PER_STEP - 1) // TILES_PER_STEP
        sorted_tile = (tile_ids // n_steps) * steps_used + jnp.minimum(tile_ids % n_steps, steps_used - 1)
        tile_expert = jnp.minimum(jnp.sum((tile_end[None, :] <= sorted_tile[:, None]).astype(I32), axis=1),
                                  N_EXPERTS - 1)
        layout = jnp.concatenate([steps_used, (n_steps - steps_used) * ROW_TILE]).astype(F32).reshape(1, 2)

        pos_t, w_tok = _route_lists(comb_t, rank_t, offsets, layout, tm=512)
        pos_win = pos_t.reshape(TOP_K, tc // SC_WINDOW, SC_WINDOW).transpose(1, 0, 2)

        xs = _sc_scatter_rows(h_packed, pos_win, max_tiles * ROW_TILE)
        ys = _grouped_ffn(tile_expert, steps_used, xs, w_gate_b, w_up_b, w_down_b)
        yt = _sc_gather_rows(ys, pos_win)
        out = _combine(yt, w_tok, x1, p2d, consts6, tm=512, chunk=chunk, out_so_far=out)
    return out.reshape(b, s, d)


def kernel(x, p, positions, g_mix, w_in, sgu_ln_g, sgu_ln_b, sgu_w, sgu_b, mla_g_qa, mla_w_qb, mla_g_kva, mla_w_kvb, qk_g_q_nope, qk_g_k_nope, qk_g_q_rope, qk_g_k_rope, w_o, g_ffn, w_router, b_router, w_exp_gate, w_exp_up, w_exp_down, w_sh_gate, w_sh_up, w_sh_down, g_ple_in, w_ple_gate, w_ple_proj, g_ple_out):
    params = (g_mix, w_in, sgu_ln_g, sgu_ln_b, sgu_w, sgu_b, mla_g_qa, mla_w_qb, mla_g_kva, mla_w_kvb,
              qk_g_q_nope, qk_g_k_nope, qk_g_q_rope, qk_g_k_rope, w_o, g_ffn, w_router, b_router,
              w_exp_gate, w_exp_up, w_exp_down, w_sh_gate, w_sh_up, w_sh_down,
              g_ple_in, w_ple_gate, w_ple_proj, g_ple_out)
    for l in range(g_mix.shape[0]):
        x = _layer(x, p[l], positions, *[a[l] for a in params])
    return x
```

```python
import functools
import math

import jax
import jax.numpy as jnp
from jax import lax
from jax.experimental import pallas as pl
from jax.experimental.pallas import tpu as pltpu
from jax.experimental.pallas import tpu_sc as plsc

D_MODEL = 1024
PLE_DIM = 256
SGU_CHUNK = 128
SGU_GROUPS = 8
V_DIM = 128
HEADS = 8
QK_NOPE = 128
QK_ROPE = 64
QK_DIM = QK_NOPE + QK_ROPE
QK_PAD = 256
Q_LORA = 384
KV_LORA = 256
ROPE_THETA = 10000.0
N_EXPERTS = 64
N_GROUPS = 8
EXPERTS_PER_GROUP = 8
TOPK_GROUPS = 4
TOP_K = 8
EXPERT_FF = 256
ROUTED_SCALE = 2.5
NORM_EPS = 1e-6
LN_EPS = 1e-5

LANES = 128
MXU_COLS = 256
VMEM_LIMIT = 56 * 1024 * 1024
SC_CORES = 2
SC_SUBCORES = 16
SC_WINDOW = 64
ROW_TILE = 256
TILES_PER_STEP = 8
MOE_CHUNKS = 2
HALF = D_MODEL // 2

F32 = jnp.float32
BF16 = jnp.bfloat16
U32 = jnp.uint32
I32 = jnp.int32


def _dot(a, b):
    return lax.dot_general(a, b, (((1,), (0,)), ((), ())), preferred_element_type=F32)


def _rms(xf, g, width=None):
    width = xf.shape[-1] if width is None else width
    ms = jnp.sum(xf * xf, axis=-1, keepdims=True) * (1.0 / width)
    return xf * lax.rsqrt(ms + NORM_EPS) * g


def _pack_rows(y):
    lo = pltpu.bitcast(y[:, :HALF].astype(BF16).astype(F32), U32) >> 16
    hi = pltpu.bitcast(y[:, HALF:].astype(BF16).astype(F32), U32) & jnp.uint32(0xFFFF0000)
    return lo | hi


def _unpack_rows(w):
    lo = pltpu.bitcast(w << 16, F32)
    hi = pltpu.bitcast(w & jnp.uint32(0xFFFF0000), F32)
    return lo, hi


def _rope_table(pos_f):
    lane = lax.broadcasted_iota(I32, (1, LANES), 1)
    freq = (lane % (QK_ROPE // 2)).astype(F32)
    inv_freq = jnp.exp(freq * (-math.log(ROPE_THETA) * 2.0 / QK_ROPE))
    phase = jnp.where(lane < QK_ROPE, 0.0, math.pi / 2)
    return jnp.cos(pos_f * inv_freq - phase)


def _norm_rope(piece, table_g):
    lane = lax.broadcasted_iota(I32, (1, LANES), 1)
    ssq = jnp.sum(jnp.where(lane < QK_ROPE, piece * piece, 0.0), axis=-1, keepdims=True)
    z = piece * lax.rsqrt(ssq * (1.0 / QK_ROPE) + NORM_EPS) * table_g
    return z + pltpu.roll(z, QK_ROPE, axis=1)


def _const_spec(shape):
    return pl.BlockSpec(shape, lambda *_: (0,) * len(shape), pipeline_mode=pl.Buffered(1))


def _mixer_prep_kernel(x_ref, pos_ref, g_mix_ref, wm_ref, wr_ref, wga_ref, wgb_ref,
                       lng_ref, lnb_ref, sw_ref, sb_ref, gqa_ref, wqb_ref, gkva_ref, wkb_ref, wvbt_ref,
                       gqn_ref, gkn_ref, gqr_ref, gkr_ref, wx_ref,
                       ma_ref, gb_ref, q_ref, k_ref, vt_ref, wxb_ref):
    tm = x_ref.shape[1]
    wxb_ref[...] = wx_ref[...].astype(BF16)
    xn = _rms(x_ref[0], g_mix_ref[...]).astype(BF16)

    rope_t = _rope_table(pos_ref[0])
    d = x_ref.shape[2]
    qn = _rms(_dot(xn, wm_ref[:, 2 * d:2 * d + Q_LORA]), gqa_ref[...]).astype(BF16)
    kvn = _rms(_dot(xn, wm_ref[:, 2 * d + Q_LORA:]), gkva_ref[...]).astype(BF16)
    lane = lax.broadcasted_iota(I32, (1, LANES), 1)
    kpe = jnp.where(lane < QK_ROPE, _norm_rope(_dot(xn, wr_ref[...]), rope_t * gkr_ref[...]), 0.0).astype(BF16)
    q_rope_t = rope_t * gqr_ref[...]

    def head_pair(pair):
        ps = slice(pair * MXU_COLS, (pair + 1) * MXU_COLS)
        k2 = _dot(kvn, wkb_ref[:, ps])
        v2t = lax.dot_general(wvbt_ref[ps, :].astype(BF16), kvn, (((1,), (1,)), ((), ())),
                              preferred_element_type=F32).astype(BF16)
        for half in range(2):
            g = 2 * pair + half
            hs = slice(half * LANES, (half + 1) * LANES)
            qh = _dot(qn, wqb_ref[:, g * QK_PAD:(g + 1) * QK_PAD])
            q_ref[0, g, :, :QK_NOPE] = _rms(qh[:, :QK_NOPE], gqn_ref[...]).astype(BF16)
            q_ref[0, g, :, QK_NOPE:] = _norm_rope(qh[:, QK_NOPE:], q_rope_t).astype(BF16)
            k_ref[0, g, :, :QK_NOPE] = _rms(k2[:, hs], gkn_ref[...]).astype(BF16)
            k_ref[0, g, :, QK_NOPE:] = kpe
            vt_ref[0, g] = v2t[hs, :]

    head_pair(0)

    gv = jax.nn.gelu(_dot(xn, wm_ref[:, d:2 * d]))
    mu = jnp.mean(gv, axis=-1, keepdims=True)
    vc = gv - mu
    var = jnp.mean(vc * vc, axis=-1, keepdims=True)
    vn = (vc * lax.rsqrt(var + LN_EPS) * lng_ref[...] + lnb_ref[...]).astype(BF16)
    row = lax.broadcasted_iota(I32, (SGU_CHUNK, SGU_CHUNK), 0)
    col = lax.broadcasted_iota(I32, (SGU_CHUNK, SGU_CHUNK), 1)
    causal = col <= row
    n_chunks = tm // SGU_CHUNK

    def sgu_pair(pair):
        ps = slice(pair * MXU_COLS, (pair + 1) * MXU_COLS)
        gu2 = jax.nn.gelu(_dot(xn, wm_ref[:, ps]))
        ga2 = jax.nn.sigmoid(_dot(xn, wga_ref[:, ps]))
        gb_ref[0, :, ps] = jax.nn.sigmoid(_dot(xn, wgb_ref[:, ps])).astype(BF16)
        for half in range(2):
            g = 2 * pair + half
            hs = slice(half * LANES, (half + 1) * LANES)
            cs = slice(g * SGU_CHUNK, (g + 1) * SGU_CHUNK)
            wg = jnp.where(causal, sw_ref[g], 0.0).astype(BF16)
            vcat = jnp.concatenate([vn[c * SGU_CHUNK:(c + 1) * SGU_CHUNK, cs] for c in range(n_chunks)], axis=1)
            mixed = _dot(wg, vcat)
            for c in range(n_chunks):
                rs = slice(c * SGU_CHUNK, (c + 1) * SGU_CHUNK)
                m = mixed[:, c * SGU_CHUNK:(c + 1) * SGU_CHUNK] + sb_ref[:, cs]
                ma_ref[0, rs, cs] = (ga2[rs, hs] * gu2[rs, hs] * m).astype(BF16)

    n_pairs = SGU_GROUPS // 2
    for pair in range(n_pairs):
        if pair + 1 < n_pairs:
            head_pair(pair + 1)
        sgu_pair(pair)


def _mixer_prep(x, pos, consts, expert_weight, tm):
    b, s, d = x.shape
    grid = (b, s // tm)
    w_spec = pl.BlockSpec((expert_weight.shape[0] // (grid[0] * grid[1]),) + expert_weight.shape[1:],
                          lambda i, j: (i * (s // tm) + j, 0, 0))
    row_spec = lambda w: pl.BlockSpec((1, tm, w), lambda i, j: (i, j, 0))
    head_spec = lambda w: pl.BlockSpec((1, HEADS, tm, w), lambda i, j: (i, 0, j, 0))
    return pl.pallas_call(
        _mixer_prep_kernel,
        grid=grid,
        in_specs=[row_spec(d), row_spec(LANES)] + [_const_spec(c.shape) for c in consts] + [w_spec],
        out_specs=[row_spec(d), row_spec(d), head_spec(QK_PAD), head_spec(QK_PAD),
                   pl.BlockSpec((1, HEADS, V_DIM, tm), lambda i, j: (i, 0, 0, j)), w_spec],
        out_shape=[jax.ShapeDtypeStruct((b, s, d), BF16), jax.ShapeDtypeStruct((b, s, d), BF16),
                   jax.ShapeDtypeStruct((b, HEADS, s, QK_PAD), BF16),
                   jax.ShapeDtypeStruct((b, HEADS, s, QK_PAD), BF16),
                   jax.ShapeDtypeStruct((b, HEADS, V_DIM, s), BF16),
                   jax.ShapeDtypeStruct(expert_weight.shape, BF16)],
        compiler_params=pltpu.CompilerParams(dimension_semantics=("arbitrary", "arbitrary"),
                                             vmem_limit_bytes=VMEM_LIMIT),
        name="mixer_prep",
    )(x, pos, *consts, expert_weight)


def _attn_kernel(q_ref, k_ref, vt_ref, ma_ref, gb_ref, *refs, tq):
    n_w = (len(refs) - 1) // 2
    o_ref = refs[n_w]
    for w_ref, wb_ref in zip(refs[:n_w], refs[n_w + 1:]):
        wb_ref[...] = w_ref[...].astype(BF16)
    s = q_ref.shape[2]
    key = lax.broadcasted_iota(I32, (tq, tq), 0)
    qry = lax.broadcasted_iota(I32, (tq, tq), 1)
    diag_mask = key <= qry
    heads = range(q_ref.shape[1])
    n_tiles = s // tq

    def score_dots(qi):
        return [lax.dot_general(k_ref[0, h, :(qi + 1) * tq, :], q_ref[0, h, qi * tq:(qi + 1) * tq, :],
                                (((1,), (1,)), ((), ())), preferred_element_type=F32) for h in heads]

    scores = score_dots(0)
    for qi in range(n_tiles):
        qs = slice(qi * tq, (qi + 1) * tq)
        n_keys = (qi + 1) * tq
        next_scores = score_dots(qi + 1) if qi + 1 < n_tiles else None
        probs, sums = [], []
        for h in heads:
            sc = scores[h]
            last = jnp.where(diag_mask, sc[n_keys - tq:], -jnp.inf)
            sc = last if qi == 0 else jnp.concatenate([sc[:n_keys - tq], last], axis=0)
            p = jnp.exp(sc - jnp.max(sc, axis=0, keepdims=True))
            sums.append(jnp.sum(p, axis=0, keepdims=True))
            probs.append(p.astype(BF16))
        for h in heads:
            acc = _dot(vt_ref[0, h, :, :n_keys], probs[h])
            cs = slice(h * V_DIM, (h + 1) * V_DIM)
            o = (acc / sums[h]).T
            o_ref[0, qs, cs] = (ma_ref[0, qs, cs].astype(F32) + gb_ref[0, qs, cs].astype(F32) * o).astype(BF16)
        scores = next_scores


def _attention(q, k, vt, ma, gb, expert_weights, tq, heads_per_step):
    b, h, s, _ = q.shape
    hp = heads_per_step
    n_steps = b * (h // hp)
    w_specs = [pl.BlockSpec((w.shape[0] // n_steps,) + w.shape[1:], lambda i, j: (i * (h // hp) + j, 0, 0))
               for w in expert_weights]
    col_spec = pl.BlockSpec((1, s, hp * V_DIM), lambda i, j: (i, 0, j))
    return pl.pallas_call(
        functools.partial(_attn_kernel, tq=tq),
        grid=(b, h // hp),
        in_specs=[pl.BlockSpec((1, hp, s, QK_PAD), lambda i, j: (i, j, 0, 0)),
                  pl.BlockSpec((1, hp, s, QK_PAD), lambda i, j: (i, j, 0, 0)),
                  pl.BlockSpec((1, hp, V_DIM, s), lambda i, j: (i, j, 0, 0)), col_spec, col_spec] + w_specs,
        out_specs=[col_spec] + w_specs,
        out_shape=[jax.ShapeDtypeStruct((b, s, h * V_DIM), BF16)]
                  + [jax.ShapeDtypeStruct(w.shape, BF16) for w in expert_weights],
        compiler_params=pltpu.CompilerParams(dimension_semantics=("arbitrary", "arbitrary"),
                                             vmem_limit_bytes=VMEM_LIMIT),
        name="mla_attention",
    )(q, k, vt, ma, gb, *expert_weights)


def _route(sel, scores):
    e, tm = sel.shape
    sel3 = sel.reshape(N_GROUPS, EXPERTS_PER_GROUP, tm)
    sub = lax.broadcasted_iota(I32, sel3.shape, 1)
    m1 = jnp.max(sel3, axis=1, keepdims=True)
    first = jnp.min(jnp.where(sel3 == m1, sub, EXPERTS_PER_GROUP), axis=1, keepdims=True)
    m2 = jnp.max(jnp.where(sub == first, -jnp.inf, sel3), axis=1, keepdims=True)
    gscore = (m1 + m2).reshape(N_GROUPS, tm)
    gid = lax.broadcasted_iota(I32, (N_GROUPS, tm), 0)
    grank = jnp.zeros((N_GROUPS, tm), I32)
    for g in range(N_GROUPS):
        other = gscore[g:g + 1]
        grank += ((other > gscore) | ((other == gscore) & (g < gid))).astype(I32)
    gmask = grank < TOPK_GROUPS
    emask = jnp.broadcast_to(gmask[:, None, :], sel3.shape).reshape(e, tm)
    remaining = jnp.where(emask, sel, -jnp.inf)
    eid = lax.broadcasted_iota(I32, (e, tm), 0)
    chosen = jnp.zeros((e, tm), jnp.bool_)
    for _ in range(TOP_K):
        best = jnp.max(remaining, axis=0, keepdims=True)
        pick = eid == jnp.min(jnp.where(remaining == best, eid, e), axis=0, keepdims=True)
        chosen = chosen | pick
        remaining = jnp.where(pick, -jnp.inf, remaining)
    w = jnp.where(chosen, scores, 0.0)
    return chosen, w / jnp.sum(w, axis=0, keepdims=True) * ROUTED_SCALE


def _post_attn_kernel(x_ref, merged_ref, wo_ref, gffn_ref, wrt_ref, br_ref,
                      x1_ref, hp_ref, comb_ref, rank_ref, cnt_ref, run_ref):
    tm = x_ref.shape[0]

    @pl.when(pl.program_id(0) == 0)
    def _():
        run_ref[...] = jnp.zeros_like(run_ref)

    x1 = x_ref[...] + _dot(merged_ref[...], wo_ref[...])
    x1_ref[...] = x1
    h2 = _rms(x1, gffn_ref[...])
    hp_ref[...] = _pack_rows(h2)
    logits_t = lax.dot_general(wrt_ref[...], h2, (((1,), (1,)), ((), ())),
                               preferred_element_type=F32, precision=lax.Precision.HIGHEST)
    scores = jax.nn.sigmoid(logits_t)
    chosen, comb_t = _route(scores + br_ref[...], scores)
    comb_ref[...] = comb_t
    a = lax.broadcasted_iota(I32, (tm, tm), 0)
    b = lax.broadcasted_iota(I32, (tm, tm), 1)
    before = (a < b).astype(BF16)
    chosen_f = chosen.astype(F32)
    prefix = _dot(chosen_f.astype(BF16), before)
    run = run_ref[:, 0:1]
    rank_ref[...] = jnp.where(chosen, prefix + run, -1.0)
    run_ref[...] += jnp.sum(chosen_f, axis=1, keepdims=True)
    cnt_ref[...] = run_ref[...]


def _post_attn(x, merged, consts, tm, chunk):
    t, d = x.shape[0] // MOE_CHUNKS, x.shape[1]
    first = chunk * (t // tm)
    row_spec = lambda w: pl.BlockSpec((tm, w), lambda i: (i, 0))
    col_spec = pl.BlockSpec((N_EXPERTS, tm), lambda i: (0, i))
    return pl.pallas_call(
        _post_attn_kernel,
        grid=(t // tm,),
        in_specs=[pl.BlockSpec((tm, d), lambda i: (first + i, 0))] * 2 + [_const_spec(c.shape) for c in consts],
        out_specs=[row_spec(d), row_spec(HALF), col_spec, col_spec, _const_spec((N_EXPERTS, LANES))],
        out_shape=[jax.ShapeDtypeStruct((t, d), F32), jax.ShapeDtypeStruct((t, HALF), U32),
                   jax.ShapeDtypeStruct((N_EXPERTS, t), F32), jax.ShapeDtypeStruct((N_EXPERTS, t), F32),
                   jax.ShapeDtypeStruct((N_EXPERTS, LANES), F32)],
        scratch_shapes=[pltpu.VMEM((N_EXPERTS, LANES), F32)],
        compiler_params=pltpu.CompilerParams(dimension_semantics=("arbitrary",), vmem_limit_bytes=VMEM_LIMIT),
        name="post_attn_router",
    )(x, merged, *consts)


def _route_lists_kernel(comb_ref, rank_ref, off_ref, lay_ref, pos_ref, w_ref):
    rank = rank_ref[...]
    chosen = rank >= 0.0
    e = rank.shape[0]
    lower = (lax.broadcasted_iota(I32, (e, e), 1) < lax.broadcasted_iota(I32, (e, e), 0)).astype(BF16)
    slot = _dot(lower, chosen.astype(BF16))
    pos_sorted = rank + off_ref[...]
    tile = jnp.floor(pos_sorted * (1.0 / ROW_TILE))
    pos_full = pos_sorted + jnp.floor((tile + 0.5) / lay_ref[0:1, 0:1]) * lay_ref[0:1, 1:2]
    comb = comb_ref[...]
    pos_rows, w_rows = [], []
    for k in range(TOP_K):
        pick = chosen & (slot == float(k))
        pos_rows.append(jnp.sum(jnp.where(pick, pos_full, 0.0), axis=0, keepdims=True))
        w_rows.append(jnp.sum(jnp.where(pick, comb, 0.0), axis=0, keepdims=True))
    pos_ref[...] = jnp.concatenate(pos_rows, axis=0).astype(I32)
    w_ref[...] = jnp.concatenate(w_rows, axis=0).T


def _route_lists(comb_t, rank_t, offsets, layout, tm):
    e, t = comb_t.shape
    col_spec = pl.BlockSpec((e, tm), lambda i: (0, i))
    return pl.pallas_call(
        _route_lists_kernel,
        grid=(t // tm,),
        in_specs=[col_spec, col_spec, _const_spec((e, 1)), _const_spec(layout.shape)],
        out_specs=[pl.BlockSpec((TOP_K, tm), lambda i: (0, i)), pl.BlockSpec((tm, TOP_K), lambda i: (i, 0))],
        out_shape=[jax.ShapeDtypeStruct((TOP_K, t), I32), jax.ShapeDtypeStruct((t, TOP_K), F32)],
        compiler_params=pltpu.CompilerParams(dimension_semantics=("arbitrary",)),
        name="route_lists",
    )(comb_t, rank_t, offsets, layout)


def _sc_mesh():
    return plsc.VectorSubcoreMesh(core_axis_name="c", subcore_axis_name="s",
                                  num_cores=SC_CORES, num_subcores=SC_SUBCORES)


def _sc_worker():
    return lax.axis_index("s") * SC_CORES + lax.axis_index("c")


def _sc_scatter_rows(src, pos_win, n_rows):
    t, width = src.shape
    n_win = t // SC_WINDOW
    per_w = n_win // (SC_CORES * SC_SUBCORES)

    @functools.partial(
        pl.kernel, mesh=_sc_mesh(), out_type=jax.ShapeDtypeStruct((n_rows, width), src.dtype),
        scratch_types=[pltpu.VMEM((TOP_K, SC_WINDOW), I32), pltpu.VMEM((SC_WINDOW, width), src.dtype),
                       pltpu.SemaphoreType.DMA],
        name="sc_scatter_rows")
    def run(src_hbm, pos_hbm, out_hbm, idx_v, rows_v, sem):
        base = _sc_worker() * per_w

        @pl.loop(0, per_w)
        def _(j):
            w = base + j
            pltpu.sync_copy(pos_hbm.at[w], idx_v)
            pltpu.sync_copy(src_hbm.at[pl.ds(w * SC_WINDOW, SC_WINDOW)], rows_v)
            copies = [pltpu.async_copy(rows_v, out_hbm.at[idx_v.at[k]], sem) for k in range(TOP_K)]
            for c in copies:
                c.wait()

    return run(src, pos_win)


def _sc_gather_rows(table, pos_win):
    n_win = pos_win.shape[0]
    width = table.shape[1]
    per_w = n_win // (SC_CORES * SC_SUBCORES)

    @functools.partial(
        pl.kernel, mesh=_sc_mesh(),
        out_type=jax.ShapeDtypeStruct((TOP_K, n_win * SC_WINDOW, width), table.dtype),
        scratch_types=[pltpu.VMEM((TOP_K, SC_WINDOW), I32), pltpu.VMEM((2, SC_WINDOW, width), table.dtype),
                       pltpu.SemaphoreType.DMA, pltpu.SemaphoreType.DMA],
        name="sc_gather_rows")
    def run(table_hbm, pos_hbm, out_hbm, idx_v, rows_v, gsem, wsem):
        base = _sc_worker() * per_w

        @pl.loop(0, per_w)
        def _(j):
            w = base + j
            pltpu.sync_copy(pos_hbm.at[w], idx_v)
            for k in range(TOP_K):
                buf = rows_v.at[k % 2]
                pltpu.async_copy(table_hbm.at[idx_v.at[k]], buf, gsem).wait()
                pltpu.async_copy(buf, out_hbm.at[k, pl.ds(w * SC_WINDOW, SC_WINDOW)], wsem).wait()

    return run(table, pos_win)


def _grouped_ffn_kernel(te_ref, used_ref, xs_ref, *refs):
    w_refs, ys_ref = refs[:-1], refs[-1]

    @pl.when(pl.program_id(0) < used_ref[0])
    def _():
        def gate_up(j):
            wg_ref, wu_ref = w_refs[3 * j:3 * j + 2]
            lo, hi = _unpack_rows(xs_ref[j])
            lo, hi = lo.astype(BF16), hi.astype(BF16)
            return (_dot(lo, wg_ref[0, :HALF]) + _dot(hi, wg_ref[0, HALF:]),
                    _dot(lo, wu_ref[0, :HALF]) + _dot(hi, wu_ref[0, HALF:]))

        pre = gate_up(0)
        for j in range(TILES_PER_STEP):
            nxt = gate_up(j + 1) if j + 1 < TILES_PER_STEP else None
            act = (jax.nn.silu(pre[0]) * pre[1]).astype(BF16)
            ys_ref[j] = _pack_rows(_dot(act, w_refs[3 * j + 2][0]))
            pre = nxt


def _grouped_ffn(tile_expert, steps_used, xs, wg, wu, wd):
    n_rows, half = xs.shape
    n_steps = n_rows // (TILES_PER_STEP * ROW_TILE)

    def step_of(i, used):
        return jnp.minimum(i, used[0] - 1)

    row_spec = pl.BlockSpec((TILES_PER_STEP, ROW_TILE, half), lambda i, te, used: (0, step_of(i, used), 0))

    def exp_map(j):
        return lambda i, te, used: (te[j * n_steps + step_of(i, used)], 0, 0)

    w_specs = [pl.BlockSpec((1,) + w.shape[1:], exp_map(j)) for j in range(TILES_PER_STEP) for w in (wg, wu, wd)]
    ys = pl.pallas_call(
        _grouped_ffn_kernel,
        grid_spec=pltpu.PrefetchScalarGridSpec(
            num_scalar_prefetch=2,
            grid=(n_steps,),
            in_specs=[row_spec] + w_specs,
            out_specs=row_spec),
        out_shape=jax.ShapeDtypeStruct((TILES_PER_STEP, n_steps * ROW_TILE, half), U32),
        compiler_params=pltpu.CompilerParams(dimension_semantics=("arbitrary",), vmem_limit_bytes=VMEM_LIMIT),
        name="grouped_ffn",
    )(tile_expert, steps_used, xs.reshape(TILES_PER_STEP, n_steps * ROW_TILE, half),
      *([wg, wu, wd] * TILES_PER_STEP))
    return ys.reshape(n_rows, half)


def _combine_kernel(yt_ref, w_ref, x1_ref, p_ref, gffn_ref, wsg_ref, wsu_ref, wsd_ref,
                    gpi_ref, wpg_ref, wpp_ref, gpo_ref, *rest):
    out_ref = rest[-1]
    x1 = x1_ref[...]
    w = w_ref[...]
    h = _rms(x1, gffn_ref[...]).astype(BF16)
    shared = _dot((jax.nn.silu(_dot(h, wsg_ref[...])) * _dot(h, wsu_ref[...])).astype(BF16), wsd_ref[...])
    proj = _rms(_dot(p_ref[...].astype(BF16), wpp_ref[...]), gpo_ref[...])
    acc_lo = jnp.zeros((x1.shape[0], HALF), F32)
    acc_hi = jnp.zeros((x1.shape[0], HALF), F32)
    for k in range(TOP_K):
        lo, hi = _unpack_rows(yt_ref[k])
        wk = w[:, k:k + 1]
        acc_lo += wk * lo
        acc_hi += wk * hi
    x2 = x1 + shared + jnp.concatenate([acc_lo, acc_hi], axis=-1)
    gate = jax.nn.sigmoid(_dot(_rms(x2, gpi_ref[...]).astype(BF16), wpg_ref[...]))
    out_ref[...] = x2 + gate * proj


def _combine(yt, w_tok, x1, p, consts, tm, chunk, out_so_far):
    t, d = x1.shape
    first = chunk * (t // tm)
    row_spec = lambda w: pl.BlockSpec((tm, w), lambda i: (i, 0))
    full_spec = lambda w: pl.BlockSpec((tm, w), lambda i: (first + i, 0))
    carried = [] if out_so_far is None else [out_so_far]
    return pl.pallas_call(
        _combine_kernel,
        grid=(t // tm,),
        in_specs=[pl.BlockSpec((TOP_K, tm, HALF), lambda i: (0, i, 0)), row_spec(TOP_K), row_spec(d),
                  full_spec(PLE_DIM)] + [_const_spec(c.shape) for c in consts]
                 + [pl.BlockSpec(memory_space=pl.ANY)] * len(carried),
        out_specs=full_spec(d),
        out_shape=jax.ShapeDtypeStruct((t * MOE_CHUNKS, d), F32),
        input_output_aliases={4 + len(consts): 0} if carried else {},
        compiler_params=pltpu.CompilerParams(dimension_semantics=("arbitrary",), vmem_limit_bytes=VMEM_LIMIT),
        name="combine_ple",
    )(yt, w_tok, x1, p, *consts, *carried)


def _rotate_half(a):
    half = QK_ROPE // 2
    return jnp.concatenate([-a[..., half:], a[..., :half]], axis=-1)


def _with_rotate_half(w):
    return jnp.concatenate([w, _rotate_half(w[..., -QK_ROPE:])], axis=-1)


def _rope_gain_table(g, scale):
    half = QK_ROPE // 2
    return (jnp.concatenate([g, g[half:], g[:half]]) * scale).reshape(1, LANES)


def _layer(x, p, pos, g_mix, w_in, sgu_ln_g, sgu_ln_b, sgu_w, sgu_b, mla_g_qa, mla_w_qb, mla_g_kva, mla_w_kvb,
           qk_g_q_nope, qk_g_k_nope, qk_g_q_rope, qk_g_k_rope, w_o, g_ffn, w_router, b_router,
           w_exp_gate, w_exp_up, w_exp_down, w_sh_gate, w_sh_up, w_sh_down,
           g_ple_in, w_ple_gate, w_ple_proj, g_ple_out):
    b, s, d = x.shape
    t = b * s
    row = lambda a: a.reshape(1, -1)
    sizes = [d, d, Q_LORA, KV_LORA, QK_ROPE, d, d]
    offs = [0]
    for sz in sizes:
        offs.append(offs[-1] + sz)
    w_main = w_in[:, :offs[4]]
    w_r, w_ga, w_gb = [w_in[:, offs[i]:offs[i + 1]] for i in (4, 5, 6)]
    w_r = _with_rotate_half(w_r)
    wqb = _with_rotate_half(mla_w_qb.reshape(Q_LORA, HEADS, QK_DIM)).reshape(Q_LORA, HEADS * QK_PAD)
    wkvb = mla_w_kvb.reshape(KV_LORA, HEADS, QK_NOPE + V_DIM)
    wkb = wkvb[:, :, :QK_NOPE].reshape(KV_LORA, HEADS * QK_NOPE)
    wvbt = wkvb[:, :, QK_NOPE:].reshape(KV_LORA, HEADS * V_DIM).T
    sgu_bias = jnp.repeat(sgu_b.T, d // SGU_GROUPS, axis=1)
    q_scale = QK_DIM ** -0.5

    consts1 = [row(g_mix), w_main, w_r, w_ga, w_gb, row(sgu_ln_g), row(sgu_ln_b), sgu_w, sgu_bias,
               row(mla_g_qa), wqb, row(mla_g_kva), wkb, wvbt, row(qk_g_q_nope) * q_scale, row(qk_g_k_nope),
               _rope_gain_table(qk_g_q_rope, q_scale), _rope_gain_table(qk_g_k_rope, 1.0)]
    pos_lanes = jnp.broadcast_to(pos.astype(F32)[..., None], (b, s, LANES))
    ma, gb, q, k, vt, w_gate_b = _mixer_prep(x, pos_lanes, consts1, w_exp_gate, tm=512)
    merged, w_up_b, w_down_b = _attention(q, k, vt, ma, gb, (w_exp_up, w_exp_down),
                                                    tq=512, heads_per_step=2)

    consts3 = [w_o, row(g_ffn), w_router.T, b_router.reshape(-1, 1)]
    consts6 = [row(g_ffn), w_sh_gate, w_sh_up, w_sh_down, row(g_ple_in), w_ple_gate, w_ple_proj, row(g_ple_out)]
    x2d, merged2d, p2d = (a.reshape(t, -1) for a in (x, merged, p))
    tc = t // MOE_CHUNKS
    max_tiles = (tc * TOP_K) // ROW_TILE + N_EXPERTS
    tile_ids = jnp.arange(max_tiles, dtype=I32)
    n_steps = max_tiles // TILES_PER_STEP
    out = None
    for chunk in range(MOE_CHUNKS):
        x1, h_packed, comb_t, rank_t, counts = _post_attn(x2d, merged2d, consts3, tm=512, chunk=chunk)

        counts = counts[:, 0].astype(I32)
        tiles_per_expert = (counts + ROW_TILE - 1) // ROW_TILE
        tile_end = jnp.cumsum(tiles_per_expert)
        offsets = ((tile_end - tiles_per_expert) * ROW_TILE).astype(F32).reshape(N_EXPERTS, 1)
        steps_used = (tile_end[-1:] + TILES_PER_STEP - 1) // TILES_PER_STEP
        sorted_tile = (tile_ids // n_steps) * steps_used + jnp.minimum(tile_ids % n_steps, steps_used - 1)
        tile_expert = jnp.minimum(jnp.sum((tile_end[None, :] <= sorted_tile[:, None]).astype(I32), axis=1),
                                  N_EXPERTS - 1)
        layout = jnp.concatenate([steps_used, (n_steps - steps_used) * ROW_TILE]).astype(F32).reshape(1, 2)

        pos_t, w_tok = _route_lists(comb_t, rank_t, offsets, layout, tm=512)
        pos_win = pos_t.reshape(TOP_K, tc // SC_WINDOW, SC_WINDOW).transpose(1, 0, 2)

        xs = _sc_scatter_rows(h_packed, pos_win, max_tiles * ROW_TILE)
        ys = _grouped_ffn(tile_expert, steps_used, xs, w_gate_b, w_up_b, w_down_b)
        yt = _sc_gather_rows(ys, pos_win)
        out = _combine(yt, w_tok, x1, p2d, consts6, tm=512, chunk=chunk, out_so_far=out)
    return out.reshape(b, s, d)


def kernel(x, p, positions, g_mix, w_in, sgu_ln_g, sgu_ln_b, sgu_w, sgu_b, mla_g_qa, mla_w_qb, mla_g_kva, mla_w_kvb, qk_g_q_nope, qk_g_k_nope, qk_g_q_rope, qk_g_k_rope, w_o, g_ffn, w_router, b_router, w_exp_gate, w_exp_up, w_exp_down, w_sh_gate, w_sh_up, w_sh_down, g_ple_in, w_ple_gate, w_ple_proj, g_ple_out):
    params = (g_mix, w_in, sgu_ln_g, sgu_ln_b, sgu_w, sgu_b, mla_g_qa, mla_w_qb, mla_g_kva, mla_w_kvb,
              qk_g_q_nope, qk_g_k_nope, qk_g_q_rope, qk_g_k_rope, w_o, g_ffn, w_router, b_router,
              w_exp_gate, w_exp_up, w_exp_down, w_sh_gate, w_sh_up, w_sh_down,
              g_ple_in, w_ple_gate, w_ple_proj, g_ple_out)
    for l in range(g_mix.shape[0]):
        x = _layer(x, p[l], positions, *[a[l] for a in params])
    return x
```

```python
import functools
import math

import jax
import jax.numpy as jnp
from jax import lax
from jax.experimental import pallas as pl
from jax.experimental.pallas import tpu as pltpu
from jax.experimental.pallas import tpu_sc as plsc

D_MODEL = 1024
PLE_DIM = 256
SGU_CHUNK = 128
SGU_GROUPS = 8
V_DIM = 128
HEADS = 8
QK_NOPE = 128
QK_ROPE = 64
QK_DIM = QK_NOPE + QK_ROPE
QK_PAD = 256
Q_LORA = 384
KV_LORA = 256
ROPE_THETA = 10000.0
N_EXPERTS = 64
N_GROUPS = 8
EXPERTS_PER_GROUP = 8
TOPK_GROUPS = 4
TOP_K = 8
EXPERT_FF = 256
ROUTED_SCALE = 2.5
NORM_EPS = 1e-6
LN_EPS = 1e-5

LANES = 128
MXU_COLS = 256
VMEM_LIMIT = 56 * 1024 * 1024
SC_CORES = 2
SC_SUBCORES = 16
SC_WINDOW = 64
ROW_TILE = 256
TILES_PER_STEP = 8
MOE_CHUNKS = 2
HALF = D_MODEL // 2

F32 = jnp.float32
BF16 = jnp.bfloat16
U32 = jnp.uint32
I32 = jnp.int32


def _dot(a, b):
    return lax.dot_general(a, b, (((1,), (0,)), ((), ())), preferred_element_type=F32)


def _rms(xf, g, width=None):
    width = xf.shape[-1] if width is None else width
    ms = jnp.sum(xf * xf, axis=-1, keepdims=True) * (1.0 / width)
    return xf * lax.rsqrt(ms + NORM_EPS) * g


def _pack_rows(y):
    lo = pltpu.bitcast(y[:, :HALF].astype(BF16).astype(F32), U32) >> 16
    hi = pltpu.bitcast(y[:, HALF:].astype(BF16).astype(F32), U32) & jnp.uint32(0xFFFF0000)
    return lo | hi


def _unpack_rows(w):
    lo = pltpu.bitcast(w << 16, F32)
    hi = pltpu.bitcast(w & jnp.uint32(0xFFFF0000), F32)
    return lo, hi


def _rope_table(pos_f):
    lane = lax.broadcasted_iota(I32, (1, LANES), 1)
    freq = (lane % (QK_ROPE // 2)).astype(F32)
    inv_freq = jnp.exp(freq * (-math.log(ROPE_THETA) * 2.0 / QK_ROPE))
    phase = jnp.where(lane < QK_ROPE, 0.0, math.pi / 2)
    return jnp.cos(pos_f * inv_freq - phase)


def _norm_rope(piece, table_g):
    lane = lax.broadcasted_iota(I32, (1, LANES), 1)
    ssq = jnp.sum(jnp.where(lane < QK_ROPE, piece * piece, 0.0), axis=-1, keepdims=True)
    z = piece * lax.rsqrt(ssq * (1.0 / QK_ROPE) + NORM_EPS) * table_g
    return z + pltpu.roll(z, QK_ROPE, axis=1)


def _const_spec(shape):
    return pl.BlockSpec(shape, lambda *_: (0,) * len(shape), pipeline_mode=pl.Buffered(1))


def _mixer_prep_kernel(x_ref, pos_ref, g_mix_ref, wm_ref, wr_ref,
                       lng_ref, lnb_ref, sw_ref, sb_ref, gqa_ref, wqb_ref, gkva_ref, wkb_ref, wvbt_ref,
                       gqn_ref, gkn_ref, gqr_ref, gkr_ref,
                       ma_ref, gb_ref, q_ref, k_ref, vt_ref, gates_ref):
    tm = x_ref.shape[1]
    d = x_ref.shape[2]

    @pl.when((pl.program_id(0) == 0) & (pl.program_id(1) == 0))
    def _():
        gates_ref[...] = wm_ref[:, wm_ref.shape[1] - 2 * d:].astype(BF16)

    xn = _rms(x_ref[0], g_mix_ref[...]).astype(BF16)

    rope_t = _rope_table(pos_ref[0])
    qn = _rms(_dot(xn, wm_ref[:, 2 * d:2 * d + Q_LORA]), gqa_ref[...]).astype(BF16)
    kvn = _rms(_dot(xn, wm_ref[:, 2 * d + Q_LORA:2 * d + Q_LORA + KV_LORA]), gkva_ref[...]).astype(BF16)
    lane = lax.broadcasted_iota(I32, (1, LANES), 1)
    kpe = jnp.where(lane < QK_ROPE, _norm_rope(_dot(xn, wr_ref[...]), rope_t * gkr_ref[...]), 0.0).astype(BF16)
    q_rope_t = rope_t * gqr_ref[...]

    def head_pair(pair):
        ps = slice(pair * MXU_COLS, (pair + 1) * MXU_COLS)
        k2 = _dot(kvn, wkb_ref[:, ps])
        v2t = lax.dot_general(wvbt_ref[ps, :].astype(BF16), kvn, (((1,), (1,)), ((), ())),
                              preferred_element_type=F32).astype(BF16)
        for half in range(2):
            g = 2 * pair + half
            hs = slice(half * LANES, (half + 1) * LANES)
            qh = _dot(qn, wqb_ref[:, g * QK_PAD:(g + 1) * QK_PAD])
            q_ref[0, g, :, :QK_NOPE] = _rms(qh[:, :QK_NOPE], gqn_ref[...]).astype(BF16)
            q_ref[0, g, :, QK_NOPE:] = _norm_rope(qh[:, QK_NOPE:], q_rope_t).astype(BF16)
            k_ref[0, g, :, :QK_NOPE] = _rms(k2[:, hs], gkn_ref[...]).astype(BF16)
            k_ref[0, g, :, QK_NOPE:] = kpe
            vt_ref[0, g] = v2t[hs, :]

    head_pair(0)

    gv = jax.nn.gelu(_dot(xn, wm_ref[:, d:2 * d]))
    mu = jnp.mean(gv, axis=-1, keepdims=True)
    vc = gv - mu
    var = jnp.mean(vc * vc, axis=-1, keepdims=True)
    vn = (vc * lax.rsqrt(var + LN_EPS) * lng_ref[...] + lnb_ref[...]).astype(BF16)
    row = lax.broadcasted_iota(I32, (SGU_CHUNK, SGU_CHUNK), 0)
    col = lax.broadcasted_iota(I32, (SGU_CHUNK, SGU_CHUNK), 1)
    causal = col <= row
    n_chunks = tm // SGU_CHUNK

    def sgu_pair(pair):
        ps = slice(pair * MXU_COLS, (pair + 1) * MXU_COLS)
        gu2 = jax.nn.gelu(_dot(xn, wm_ref[:, ps]))
        ga2 = jax.nn.sigmoid(_dot(xn, gates_ref[:, ps]))
        gb_ref[0, :, ps] = jax.nn.sigmoid(
            _dot(xn, gates_ref[:, d + pair * MXU_COLS:d + (pair + 1) * MXU_COLS])).astype(BF16)
        for half in range(2):
            g = 2 * pair + half
            hs = slice(half * LANES, (half + 1) * LANES)
            cs = slice(g * SGU_CHUNK, (g + 1) * SGU_CHUNK)
            wg = jnp.where(causal, sw_ref[g], 0.0).astype(BF16)
            vcat = jnp.concatenate([vn[c * SGU_CHUNK:(c + 1) * SGU_CHUNK, cs] for c in range(n_chunks)], axis=1)
            mixed = _dot(wg, vcat)
            for c in range(n_chunks):
                rs = slice(c * SGU_CHUNK, (c + 1) * SGU_CHUNK)
                m = mixed[:, c * SGU_CHUNK:(c + 1) * SGU_CHUNK] + sb_ref[:, cs]
                ma_ref[0, rs, cs] = (ga2[rs, hs] * gu2[rs, hs] * m).astype(BF16)

    n_pairs = SGU_GROUPS // 2
    for pair in range(n_pairs):
        if pair + 1 < n_pairs:
            head_pair(pair + 1)
        sgu_pair(pair)


def _mixer_prep(x, pos, consts, tm):
    b, s, d = x.shape
    grid = (b, s // tm)
    row_spec = lambda w: pl.BlockSpec((1, tm, w), lambda i, j: (i, j, 0))
    head_spec = lambda w: pl.BlockSpec((1, HEADS, tm, w), lambda i, j: (i, 0, j, 0))
    return pl.pallas_call(
        _mixer_prep_kernel,
        grid=grid,
        in_specs=[row_spec(d), row_spec(LANES)] + [_const_spec(c.shape) for c in consts],
        out_specs=[row_spec(d), row_spec(d), head_spec(QK_PAD), head_spec(QK_PAD),
                   pl.BlockSpec((1, HEADS, V_DIM, tm), lambda i, j: (i, 0, 0, j))],
        out_shape=[jax.ShapeDtypeStruct((b, s, d), BF16), jax.ShapeDtypeStruct((b, s, d), BF16),
                   jax.ShapeDtypeStruct((b, HEADS, s, QK_PAD), BF16),
                   jax.ShapeDtypeStruct((b, HEADS, s, QK_PAD), BF16),
                   jax.ShapeDtypeStruct((b, HEADS, V_DIM, s), BF16)],
        scratch_shapes=[pltpu.VMEM((d, 2 * d), BF16)],
        compiler_params=pltpu.CompilerParams(dimension_semantics=("arbitrary", "arbitrary"),
                                             vmem_limit_bytes=VMEM_LIMIT),
        name="mixer_prep",
    )(x, pos, *consts)


def _attn_kernel(q_ref, k_ref, vt_ref, ma_ref, gb_ref, *refs, tq):
    n_w = (len(refs) - 1) // 2
    o_ref = refs[n_w]
    for w_ref, wb_ref in zip(refs[:n_w], refs[n_w + 1:]):
        wb_ref[...] = w_ref[...].astype(BF16)
    s = q_ref.shape[2]
    key = lax.broadcasted_iota(I32, (tq, tq), 0)
    qry = lax.broadcasted_iota(I32, (tq, tq), 1)
    diag_mask = key <= qry
    heads = range(q_ref.shape[1])
    n_tiles = s // tq

    def score_dots(qi):
        return [lax.dot_general(k_ref[0, h, :(qi + 1) * tq, :], q_ref[0, h, qi * tq:(qi + 1) * tq, :],
                                (((1,), (1,)), ((), ())), preferred_element_type=F32) for h in heads]

    scores = score_dots(0)
    for qi in range(n_tiles):
        qs = slice(qi * tq, (qi + 1) * tq)
        n_keys = (qi + 1) * tq
        next_scores = score_dots(qi + 1) if qi + 1 < n_tiles else None
        probs, sums = [], []
        for h in heads:
            sc = scores[h]
            last = jnp.where(diag_mask, sc[n_keys - tq:], -jnp.inf)
            sc = last if qi == 0 else jnp.concatenate([sc[:n_keys - tq], last], axis=0)
            p = jnp.exp(sc - jnp.max(sc, axis=0, keepdims=True))
            sums.append(jnp.sum(p, axis=0, keepdims=True))
            probs.append(p.astype(BF16))
        for h in heads:
            acc = _dot(vt_ref[0, h, :, :n_keys], probs[h])
            cs = slice(h * V_DIM, (h + 1) * V_DIM)
            o = (acc / sums[h]).T
            o_ref[0, qs, cs] = (ma_ref[0, qs, cs].astype(F32) + gb_ref[0, qs, cs].astype(F32) * o).astype(BF16)
        scores = next_scores


def _attention(q, k, vt, ma, gb, expert_weights, tq, heads_per_step):
    b, h, s, _ = q.shape
    hp = heads_per_step
    n_steps = b * (h // hp)
    w_specs = [pl.BlockSpec((w.shape[0] // n_steps,) + w.shape[1:], lambda i, j: (i * (h // hp) + j, 0, 0))
               for w in expert_weights]
    col_spec = pl.BlockSpec((1, s, hp * V_DIM), lambda i, j: (i, 0, j))
    return pl.pallas_call(
        functools.partial(_attn_kernel, tq=tq),
        grid=(b, h // hp),
        in_specs=[pl.BlockSpec((1, hp, s, QK_PAD), lambda i, j: (i, j, 0, 0)),
                  pl.BlockSpec((1, hp, s, QK_PAD), lambda i, j: (i, j, 0, 0)),
                  pl.BlockSpec((1, hp, V_DIM, s), lambda i, j: (i, j, 0, 0)), col_spec, col_spec] + w_specs,
        out_specs=[col_spec] + w_specs,
        out_shape=[jax.ShapeDtypeStruct((b, s, h * V_DIM), BF16)]
                  + [jax.ShapeDtypeStruct(w.shape, BF16) for w in expert_weights],
        compiler_params=pltpu.CompilerParams(dimension_semantics=("arbitrary", "arbitrary"),
                                             vmem_limit_bytes=VMEM_LIMIT),
        name="mla_attention",
    )(q, k, vt, ma, gb, *expert_weights)


def _route(sel, scores):
    e, tm = sel.shape
    sel3 = sel.reshape(N_GROUPS, EXPERTS_PER_GROUP, tm)
    sub = lax.broadcasted_iota(I32, sel3.shape, 1)
    m1 = jnp.max(sel3, axis=1, keepdims=True)
    first = jnp.min(jnp.where(sel3 == m1, sub, EXPERTS_PER_GROUP), axis=1, keepdims=True)
    m2 = jnp.max(jnp.where(sub == first, -jnp.inf, sel3), axis=1, keepdims=True)
    gscore = (m1 + m2).reshape(N_GROUPS, tm)
    gid = lax.broadcasted_iota(I32, (N_GROUPS, tm), 0)
    grank = jnp.zeros((N_GROUPS, tm), I32)
    for g in range(N_GROUPS):
        other = gscore[g:g + 1]
        grank += ((other > gscore) | ((other == gscore) & (g < gid))).astype(I32)
    gmask = grank < TOPK_GROUPS
    emask = jnp.broadcast_to(gmask[:, None, :], sel3.shape).reshape(e, tm)
    remaining = jnp.where(emask, sel, -jnp.inf)
    eid = lax.broadcasted_iota(I32, (e, tm), 0)
    chosen = jnp.zeros((e, tm), jnp.bool_)
    for _ in range(TOP_K):
        best = jnp.max(remaining, axis=0, keepdims=True)
        pick = eid == jnp.min(jnp.where(remaining == best, eid, e), axis=0, keepdims=True)
        chosen = chosen | pick
        remaining = jnp.where(pick, -jnp.inf, remaining)
    w = jnp.where(chosen, scores, 0.0)
    return chosen, w / jnp.sum(w, axis=0, keepdims=True) * ROUTED_SCALE


def _post_attn_kernel(x_ref, merged_ref, wo_ref, gffn_ref, wrt_ref, br_ref,
                      x1_ref, hp_ref, comb_ref, rank_ref, cnt_ref, run_ref):
    tm = x_ref.shape[0]

    @pl.when(pl.program_id(0) == 0)
    def _():
        run_ref[...] = jnp.zeros_like(run_ref)

    x1 = x_ref[...] + _dot(merged_ref[...], wo_ref[...])
    x1_ref[...] = x1
    h2 = _rms(x1, gffn_ref[...])
    hp_ref[...] = _pack_rows(h2)
    logits_t = lax.dot_general(wrt_ref[...], h2, (((1,), (1,)), ((), ())),
                               preferred_element_type=F32, precision=lax.Precision.HIGHEST)
    scores = jax.nn.sigmoid(logits_t)
    chosen, comb_t = _route(scores + br_ref[...], scores)
    comb_ref[...] = comb_t
    a = lax.broadcasted_iota(I32, (tm, tm), 0)
    b = lax.broadcasted_iota(I32, (tm, tm), 1)
    before = (a < b).astype(BF16)
    chosen_f = chosen.astype(F32)
    prefix = _dot(chosen_f.astype(BF16), before)
    run = run_ref[:, 0:1]
    rank_ref[...] = jnp.where(chosen, prefix + run, -1.0)
    run_ref[...] += jnp.sum(chosen_f, axis=1, keepdims=True)
    cnt_ref[...] = run_ref[...]


def _post_attn(x, merged, consts, tm, chunk):
    t, d = x.shape[0] // MOE_CHUNKS, x.shape[1]
    first = chunk * (t // tm)
    row_spec = lambda w: pl.BlockSpec((tm, w), lambda i: (i, 0))
    col_spec = pl.BlockSpec((N_EXPERTS, tm), lambda i: (0, i))
    return pl.pallas_call(
        _post_attn_kernel,
        grid=(t // tm,),
        in_specs=[pl.BlockSpec((tm, d), lambda i: (first + i, 0))] * 2 + [_const_spec(c.shape) for c in consts],
        out_specs=[row_spec(d), row_spec(HALF), col_spec, col_spec, _const_spec((N_EXPERTS, LANES))],
        out_shape=[jax.ShapeDtypeStruct((t, d), F32), jax.ShapeDtypeStruct((t, HALF), U32),
                   jax.ShapeDtypeStruct((N_EXPERTS, t), F32), jax.ShapeDtypeStruct((N_EXPERTS, t), F32),
                   jax.ShapeDtypeStruct((N_EXPERTS, LANES), F32)],
        scratch_shapes=[pltpu.VMEM((N_EXPERTS, LANES), F32)],
        compiler_params=pltpu.CompilerParams(dimension_semantics=("arbitrary",), vmem_limit_bytes=VMEM_LIMIT),
        name="post_attn_router",
    )(x, merged, *consts)


def _route_lists_kernel(comb_ref, rank_ref, off_ref, lay_ref, pos_ref, w_ref):
    rank = rank_ref[...]
    chosen = rank >= 0.0
    e = rank.shape[0]
    lower = (lax.broadcasted_iota(I32, (e, e), 1) < lax.broadcasted_iota(I32, (e, e), 0)).astype(BF16)
    slot = _dot(lower, chosen.astype(BF16))
    pos_sorted = rank + off_ref[...]
    tile = jnp.floor(pos_sorted * (1.0 / ROW_TILE))
    pos_full = pos_sorted + jnp.floor((tile + 0.5) / lay_ref[0:1, 0:1]) * lay_ref[0:1, 1:2]
    comb = comb_ref[...]
    pos_rows, w_rows = [], []
    for k in range(TOP_K):
        pick = chosen & (slot == float(k))
        pos_rows.append(jnp.sum(jnp.where(pick, pos_full, 0.0), axis=0, keepdims=True))
        w_rows.append(jnp.sum(jnp.where(pick, comb, 0.0), axis=0, keepdims=True))
    pos_ref[...] = jnp.concatenate(pos_rows, axis=0).astype(I32)
    w_ref[...] = jnp.concatenate(w_rows, axis=0).T


def _route_lists(comb_t, rank_t, offsets, layout, tm):
    e, t = comb_t.shape
    col_spec = pl.BlockSpec((e, tm), lambda i: (0, i))
    return pl.pallas_call(
        _route_lists_kernel,
        grid=(t // tm,),
        in_specs=[col_spec, col_spec, _const_spec((e, 1)), _const_spec(layout.shape)],
        out_specs=[pl.BlockSpec((TOP_K, tm), lambda i: (0, i)), pl.BlockSpec((tm, TOP_K), lambda i: (i, 0))],
        out_shape=[jax.ShapeDtypeStruct((TOP_K, t), I32), jax.ShapeDtypeStruct((t, TOP_K), F32)],
        compiler_params=pltpu.CompilerParams(dimension_semantics=("arbitrary",)),
        name="route_lists",
    )(comb_t, rank_t, offsets, layout)


def _sc_mesh():
    return plsc.VectorSubcoreMesh(core_axis_name="c", subcore_axis_name="s",
                                  num_cores=SC_CORES, num_subcores=SC_SUBCORES)


def _sc_worker():
    return lax.axis_index("s") * SC_CORES + lax.axis_index("c")


def _sc_scatter_rows(src, pos_win, n_rows):
    t, width = src.shape
    n_win = t // SC_WINDOW
    per_w = n_win // (SC_CORES * SC_SUBCORES)

    @functools.partial(
        pl.kernel, mesh=_sc_mesh(), out_type=jax.ShapeDtypeStruct((n_rows, width), src.dtype),
        scratch_types=[pltpu.VMEM((TOP_K, SC_WINDOW), I32), pltpu.VMEM((SC_WINDOW, width), src.dtype),
                       pltpu.SemaphoreType.DMA],
        name="sc_scatter_rows")
    def run(src_hbm, pos_hbm, out_hbm, idx_v, rows_v, sem):
        base = _sc_worker() * per_w

        @pl.loop(0, per_w)
        def _(j):
            w = base + j
            pltpu.sync_copy(pos_hbm.at[w], idx_v)
            pltpu.sync_copy(src_hbm.at[pl.ds(w * SC_WINDOW, SC_WINDOW)], rows_v)
            copies = [pltpu.async_copy(rows_v, out_hbm.at[idx_v.at[k]], sem) for k in range(TOP_K)]
            for c in copies:
                c.wait()

    return run(src, pos_win)


def _sc_gather_rows(table, pos_win):
    n_win = pos_win.shape[0]
    width = table.shape[1]
    per_w = n_win // (SC_CORES * SC_SUBCORES)

    @functools.partial(
        pl.kernel, mesh=_sc_mesh(),
        out_type=jax.ShapeDtypeStruct((TOP_K, n_win * SC_WINDOW, width), table.dtype),
        scratch_types=[pltpu.VMEM((TOP_K, SC_WINDOW), I32), pltpu.VMEM((2, SC_WINDOW, width), table.dtype),
                       pltpu.SemaphoreType.DMA, pltpu.SemaphoreType.DMA],
        name="sc_gather_rows")
    def run(table_hbm, pos_hbm, out_hbm, idx_v, rows_v, gsem, wsem):
        base = _sc_worker() * per_w

        @pl.loop(0, per_w)
        def _(j):
            w = base + j
            pltpu.sync_copy(pos_hbm.at[w], idx_v)
            for k in range(TOP_K):
                buf = rows_v.at[k % 2]
                pltpu.async_copy(table_hbm.at[idx_v.at[k]], buf, gsem).wait()
                pltpu.async_copy(buf, out_hbm.at[k, pl.ds(w * SC_WINDOW, SC_WINDOW)], wsem).wait()

    return run(table, pos_win)


def _grouped_ffn_kernel(te_ref, used_ref, xs_ref, *refs):
    w_refs, ys_ref = refs[:-1], refs[-1]

    @pl.when(pl.program_id(0) < used_ref[0])
    def _():
        def gate_up(j):
            wg_ref, wu_ref = w_refs[3 * j:3 * j + 2]
            lo, hi = _unpack_rows(xs_ref[j])
            lo, hi = lo.astype(BF16), hi.astype(BF16)
            return (_dot(lo, wg_ref[0, :HALF]) + _dot(hi, wg_ref[0, HALF:]),
                    _dot(lo, wu_ref[0, :HALF]) + _dot(hi, wu_ref[0, HALF:]))

        pre = gate_up(0)
        for j in range(TILES_PER_STEP):
            nxt = gate_up(j + 1) if j + 1 < TILES_PER_STEP else None
            act = (jax.nn.silu(pre[0]) * pre[1]).astype(BF16)
            ys_ref[j] = _pack_rows(_dot(act, w_refs[3 * j + 2][0]))
            pre = nxt


def _grouped_ffn(tile_expert, steps_used, xs, wg, wu, wd):
    n_rows, half = xs.shape
    n_steps = n_rows // (TILES_PER_STEP * ROW_TILE)

    def step_of(i, used):
        return jnp.minimum(i, used[0] - 1)

    row_spec = pl.BlockSpec((TILES_PER_STEP, ROW_TILE, half), lambda i, te, used: (0, step_of(i, used), 0))

    def exp_map(j):
        return lambda i, te, used: (te[j * n_steps + step_of(i, used)], 0, 0)

    w_specs = [pl.BlockSpec((1,) + w.shape[1:], exp_map(j)) for j in range(TILES_PER_STEP) for w in (wg, wu, wd)]
    ys = pl.pallas_call(
        _grouped_ffn_kernel,
        grid_spec=pltpu.PrefetchScalarGridSpec(
            num_scalar_prefetch=2,
            grid=(n_steps,),
            in_specs=[row_spec] + w_specs,
            out_specs=row_spec),
        out_shape=jax.ShapeDtypeStruct((TILES_PER_STEP, n_steps * ROW_TILE, half), U32),
        compiler_params=pltpu.CompilerParams(dimension_semantics=("arbitrary",), vmem_limit_bytes=VMEM_LIMIT),
        name="grouped_ffn",
    )(tile_expert, steps_used, xs.reshape(TILES_PER_STEP, n_steps * ROW_TILE, half),
      *([wg, wu, wd] * TILES_PER_STEP))
    return ys.reshape(n_rows, half)


def _combine_kernel(yt_ref, w_ref, x1_ref, p_ref, gffn_ref, wsg_ref, wsu_ref, wsd_ref,
                    gpi_ref, wpg_ref, wpp_ref, gpo_ref, *rest):
    out_ref = rest[-1]
    x1 = x1_ref[...]
    w = w_ref[...]
    h = _rms(x1, gffn_ref[...]).astype(BF16)
    shared = _dot((jax.nn.silu(_dot(h, wsg_ref[...])) * _dot(h, wsu_ref[...])).astype(BF16), wsd_ref[...])
    proj = _rms(_dot(p_ref[...].astype(BF16), wpp_ref[...]), gpo_ref[...])
    acc_lo = jnp.zeros((x1.shape[0], HALF), F32)
    acc_hi = jnp.zeros((x1.shape[0], HALF), F32)
    for k in range(TOP_K):
        lo, hi = _unpack_rows(yt_ref[k])
        wk = w[:, k:k + 1]
        acc_lo += wk * lo
        acc_hi += wk * hi
    x2 = x1 + shared + jnp.concatenate([acc_lo, acc_hi], axis=-1)
    gate = jax.nn.sigmoid(_dot(_rms(x2, gpi_ref[...]).astype(BF16), wpg_ref[...]))
    out_ref[...] = x2 + gate * proj


def _combine(yt, w_tok, x1, p, consts, tm, chunk, out_so_far):
    t, d = x1.shape
    first = chunk * (t // tm)
    row_spec = lambda w: pl.BlockSpec((tm, w), lambda i: (i, 0))
    full_spec = lambda w: pl.BlockSpec((tm, w), lambda i: (first + i, 0))
    carried = [] if out_so_far is None else [out_so_far]
    return pl.pallas_call(
        _combine_kernel,
        grid=(t // tm,),
        in_specs=[pl.BlockSpec((TOP_K, tm, HALF), lambda i: (0, i, 0)), row_spec(TOP_K), row_spec(d),
                  full_spec(PLE_DIM)] + [_const_spec(c.shape) for c in consts]
                 + [pl.BlockSpec(memory_space=pl.ANY)] * len(carried),
        out_specs=full_spec(d),
        out_shape=jax.ShapeDtypeStruct((t * MOE_CHUNKS, d), F32),
        input_output_aliases={4 + len(consts): 0} if carried else {},
        compiler_params=pltpu.CompilerParams(dimension_semantics=("arbitrary",), vmem_limit_bytes=VMEM_LIMIT),
        name="combine_ple",
    )(yt, w_tok, x1, p, *consts, *carried)


def _rotate_half(a):
    half = QK_ROPE // 2
    return jnp.concatenate([-a[..., half:], a[..., :half]], axis=-1)


def _with_rotate_half(w):
    return jnp.concatenate([w, _rotate_half(w[..., -QK_ROPE:])], axis=-1)


def _rope_gain_table(g, scale):
    half = QK_ROPE // 2
    return (jnp.concatenate([g, g[half:], g[:half]]) * scale).reshape(1, LANES)


def _layer(x, p, pos, g_mix, w_in, sgu_ln_g, sgu_ln_b, sgu_w, sgu_b, mla_g_qa, mla_w_qb, mla_g_kva, mla_w_kvb,
           qk_g_q_nope, qk_g_k_nope, qk_g_q_rope, qk_g_k_rope, w_o, g_ffn, w_router, b_router,
           w_exp_gate, w_exp_up, w_exp_down, w_sh_gate, w_sh_up, w_sh_down,
           g_ple_in, w_ple_gate, w_ple_proj, g_ple_out):
    b, s, d = x.shape
    t = b * s
    row = lambda a: a.reshape(1, -1)
    sizes = [d, d, Q_LORA, KV_LORA, QK_ROPE, d, d]
    offs = [0]
    for sz in sizes:
        offs.append(offs[-1] + sz)
    w_r = _with_rotate_half(w_in[:, offs[4]:offs[5]])
    wqb = _with_rotate_half(mla_w_qb.reshape(Q_LORA, HEADS, QK_DIM)).reshape(Q_LORA, HEADS * QK_PAD)
    wkvb = mla_w_kvb.reshape(KV_LORA, HEADS, QK_NOPE + V_DIM)
    wkb = wkvb[:, :, :QK_NOPE].reshape(KV_LORA, HEADS * QK_NOPE)
    wvbt = wkvb[:, :, QK_NOPE:].reshape(KV_LORA, HEADS * V_DIM).T
    sgu_bias = jnp.repeat(sgu_b.T, d // SGU_GROUPS, axis=1)
    q_scale = QK_DIM ** -0.5

    consts1 = [row(g_mix), w_in, w_r, row(sgu_ln_g), row(sgu_ln_b), sgu_w, sgu_bias,
               row(mla_g_qa), wqb, row(mla_g_kva), wkb, wvbt, row(qk_g_q_nope) * q_scale, row(qk_g_k_nope),
               _rope_gain_table(qk_g_q_rope, q_scale), _rope_gain_table(qk_g_k_rope, 1.0)]
    pos_lanes = jnp.broadcast_to(pos.astype(F32)[..., None], (b, s, LANES))
    ma, gb, q, k, vt = _mixer_prep(x, pos_lanes, consts1, tm=512)
    merged, w_gate_b, w_up_b, w_down_b = _attention(q, k, vt, ma, gb, (w_exp_gate, w_exp_up, w_exp_down),
                                                    tq=256, heads_per_step=2)

    consts3 = [w_o, row(g_ffn), w_router.T, b_router.reshape(-1, 1)]
    consts6 = [row(g_ffn), w_sh_gate, w_sh_up, w_sh_down, row(g_ple_in), w_ple_gate, w_ple_proj, row(g_ple_out)]
    x2d, merged2d, p2d = (a.reshape(t, -1) for a in (x, merged, p))
    tc = t // MOE_CHUNKS
    max_tiles = (tc * TOP_K) // ROW_TILE + N_EXPERTS
    tile_ids = jnp.arange(max_tiles, dtype=I32)
    n_steps = max_tiles // TILES_PER_STEP
    out = None
    for chunk in range(MOE_CHUNKS):
        x1, h_packed, comb_t, rank_t, counts = _post_attn(x2d, merged2d, consts3, tm=512, chunk=chunk)

        counts = counts[:, 0].astype(I32)
        tiles_per_expert = (counts + ROW_TILE - 1) // ROW_TILE
        tile_end = jnp.cumsum(tiles_per_expert)
        offsets = ((tile_end - tiles_per_expert) * ROW_TILE).astype(F32).reshape(N_EXPERTS, 1)
        steps_used = (tile_end[-1:] + TILES_PER_STEP - 1) // TILES_PER_STEP
        sorted_tile = (tile_ids // n_steps) * steps_used + jnp.minimum(tile_ids % n_steps, steps_used - 1)
        tile_expert = jnp.minimum(jnp.sum((tile_end[None, :] <= sorted_tile[:, None]).astype(I32), axis=1),
                                  N_EXPERTS - 1)
        layout = jnp.concatenate([steps_used, (n_steps - steps_used) * ROW_TILE]).astype(F32).reshape(1, 2)

        pos_t, w_tok = _route_lists(comb_t, rank_t, offsets, layout, tm=512)
        pos_win = pos_t.reshape(TOP_K, tc // SC_WINDOW, SC_WINDOW).transpose(1, 0, 2)

        xs = _sc_scatter_rows(h_packed, pos_win, max_tiles * ROW_TILE)
        ys = _grouped_ffn(tile_expert, steps_used, xs, w_gate_b, w_up_b, w_down_b)
        yt = _sc_gather_rows(ys, pos_win)
        out = _combine(yt, w_tok, x1, p2d, consts6, tm=512, chunk=chunk, out_so_far=out)
    return out.reshape(b, s, d)


def kernel(x, p, positions, g_mix, w_in, sgu_ln_g, sgu_ln_b, sgu_w, sgu_b, mla_g_qa, mla_w_qb, mla_g_kva, mla_w_kvb, qk_g_q_nope, qk_g_k_nope, qk_g_q_rope, qk_g_k_rope, w_o, g_ffn, w_router, b_router, w_exp_gate, w_exp_up, w_exp_down, w_sh_gate, w_sh_up, w_sh_down, g_ple_in, w_ple_gate, w_ple_proj, g_ple_out):
    params = (g_mix, w_in, sgu_ln_g, sgu_ln_b, sgu_w, sgu_b, mla_g_qa, mla_w_qb, mla_g_kva, mla_w_kvb,
              qk_g_q_nope, qk_g_k_nope, qk_g_q_rope, qk_g_k_rope, w_o, g_ffn, w_router, b_router,
              w_exp_gate, w_exp_up, w_exp_down, w_sh_gate, w_sh_up, w_sh_down,
              g_ple_in, w_ple_gate, w_ple_proj, g_ple_out)
    for l in range(g_mix.shape[0]):
        x = _layer(x, p[l], positions, *[a[l] for a in params])
    return x
```

```python
import functools
import math

import jax
import jax.numpy as jnp
from jax import lax
from jax.experimental import pallas as pl
from jax.experimental.pallas import tpu as pltpu
from jax.experimental.pallas import tpu_sc as plsc

D_MODEL = 1024
PLE_DIM = 256
SGU_CHUNK = 128
SGU_GROUPS = 8
V_DIM = 128
HEADS = 8
QK_NOPE = 128
QK_ROPE = 64
QK_DIM = QK_NOPE + QK_ROPE
QK_PAD = 256
Q_LORA = 384
KV_LORA = 256
ROPE_THETA = 10000.0
N_EXPERTS = 64
N_GROUPS = 8
EXPERTS_PER_GROUP = 8
TOPK_GROUPS = 4
TOP_K = 8
EXPERT_FF = 256
ROUTED_SCALE = 2.5
NORM_EPS = 1e-6
LN_EPS = 1e-5

LANES = 128
MXU_COLS = 256
VMEM_LIMIT = 56 * 1024 * 1024
SC_CORES = 2
SC_SUBCORES = 16
SC_WINDOW = 64
ROW_TILE = 256
TILES_PER_STEP = 8
MOE_CHUNKS = 2
HALF = D_MODEL // 2

F32 = jnp.float32
BF16 = jnp.bfloat16
U32 = jnp.uint32
I32 = jnp.int32


def _dot(a, b):
    return lax.dot_general(a, b, (((1,), (0,)), ((), ())), preferred_element_type=F32)


def _dot_t(a, bt):
    return lax.dot_general(a, bt, (((1,), (1,)), ((), ())), preferred_element_type=F32)


def _rms(xf, g, width=None):
    width = xf.shape[-1] if width is None else width
    ms = jnp.sum(xf * xf, axis=-1, keepdims=True) * (1.0 / width)
    return xf * lax.rsqrt(ms + NORM_EPS) * g


def _pack_rows(y):
    lo = pltpu.bitcast(y[:, :HALF].astype(BF16).astype(F32), U32) >> 16
    hi = pltpu.bitcast(y[:, HALF:].astype(BF16).astype(F32), U32) & jnp.uint32(0xFFFF0000)
    return lo | hi


def _unpack_rows(w):
    lo = pltpu.bitcast(w << 16, F32)
    hi = pltpu.bitcast(w & jnp.uint32(0xFFFF0000), F32)
    return lo, hi


def _rope_table(pos_f):
    lane = lax.broadcasted_iota(I32, (1, LANES), 1)
    freq = (lane % (QK_ROPE // 2)).astype(F32)
    inv_freq = jnp.exp(freq * (-math.log(ROPE_THETA) * 2.0 / QK_ROPE))
    phase = jnp.where(lane < QK_ROPE, 0.0, math.pi / 2)
    return jnp.cos(pos_f * inv_freq - phase)


def _norm_rope(piece, table_g):
    lane = lax.broadcasted_iota(I32, (1, LANES), 1)
    ssq = jnp.sum(jnp.where(lane < QK_ROPE, piece * piece, 0.0), axis=-1, keepdims=True)
    z = piece * lax.rsqrt(ssq * (1.0 / QK_ROPE) + NORM_EPS) * table_g
    return z + pltpu.roll(z, QK_ROPE, axis=1)


def _const_spec(shape):
    return pl.BlockSpec(shape, lambda *_: (0,) * len(shape), pipeline_mode=pl.Buffered(1))


def _mixer_prep_kernel(x_ref, pos_ref, g_mix_ref, wt_ref, wrt_ref,
                       lng_ref, lnb_ref, sw_ref, sb_ref, gqa_ref, wqb_ref, gkva_ref, wkb_ref, wvbt_ref,
                       gqn_ref, gkn_ref, gqr_ref, gkr_ref,
                       ma_ref, gb_ref, q_ref, k_ref, vt_ref):
    tm = x_ref.shape[1]
    d = x_ref.shape[2]
    v0, q0, kv0, ga0 = d, 2 * d, 2 * d + Q_LORA, 2 * d + Q_LORA + KV_LORA + QK_ROPE
    gb0 = ga0 + d
    xn = _rms(x_ref[0], g_mix_ref[...]).astype(BF16)

    rope_t = _rope_table(pos_ref[0])
    qn = _rms(_dot_t(xn, wt_ref[q0:q0 + Q_LORA]), gqa_ref[...]).astype(BF16)
    kvn = _rms(_dot_t(xn, wt_ref[kv0:kv0 + KV_LORA]), gkva_ref[...]).astype(BF16)
    lane = lax.broadcasted_iota(I32, (1, LANES), 1)
    kpe = jnp.where(lane < QK_ROPE, _norm_rope(_dot_t(xn, wrt_ref[...]), rope_t * gkr_ref[...]), 0.0).astype(BF16)
    q_rope_t = rope_t * gqr_ref[...]

    def head_pair(pair):
        ps = slice(pair * MXU_COLS, (pair + 1) * MXU_COLS)
        k2 = _dot(kvn, wkb_ref[:, ps])
        v2t = lax.dot_general(wvbt_ref[ps, :].astype(BF16), kvn, (((1,), (1,)), ((), ())),
                              preferred_element_type=F32).astype(BF16)
        for half in range(2):
            g = 2 * pair + half
            hs = slice(half * LANES, (half + 1) * LANES)
            qh = _dot(qn, wqb_ref[:, g * QK_PAD:(g + 1) * QK_PAD])
            q_ref[0, g, :, :QK_NOPE] = _rms(qh[:, :QK_NOPE], gqn_ref[...]).astype(BF16)
            q_ref[0, g, :, QK_NOPE:] = _norm_rope(qh[:, QK_NOPE:], q_rope_t).astype(BF16)
            k_ref[0, g, :, :QK_NOPE] = _rms(k2[:, hs], gkn_ref[...]).astype(BF16)
            k_ref[0, g, :, QK_NOPE:] = kpe
            vt_ref[0, g] = v2t[hs, :]

    head_pair(0)

    gv = jax.nn.gelu(_dot_t(xn, wt_ref[v0:v0 + d]))
    mu = jnp.mean(gv, axis=-1, keepdims=True)
    vc = gv - mu
    var = jnp.mean(vc * vc, axis=-1, keepdims=True)
    vn = (vc * lax.rsqrt(var + LN_EPS) * lng_ref[...] + lnb_ref[...]).astype(BF16)
    row = lax.broadcasted_iota(I32, (SGU_CHUNK, SGU_CHUNK), 0)
    col = lax.broadcasted_iota(I32, (SGU_CHUNK, SGU_CHUNK), 1)
    causal = col <= row
    n_chunks = tm // SGU_CHUNK

    def sgu_pair(pair):
        ps = slice(pair * MXU_COLS, (pair + 1) * MXU_COLS)
        p0 = pair * MXU_COLS
        gu2 = jax.nn.gelu(_dot_t(xn, wt_ref[p0:p0 + MXU_COLS]))
        ga2 = jax.nn.sigmoid(_dot_t(xn, wt_ref[ga0 + p0:ga0 + p0 + MXU_COLS]))
        gb_ref[0, :, ps] = jax.nn.sigmoid(_dot_t(xn, wt_ref[gb0 + p0:gb0 + p0 + MXU_COLS])).astype(BF16)
        for half in range(2):
            g = 2 * pair + half
            hs = slice(half * LANES, (half + 1) * LANES)
            cs = slice(g * SGU_CHUNK, (g + 1) * SGU_CHUNK)
            wg = jnp.where(causal, sw_ref[g], 0.0).astype(BF16)
            vcat = jnp.concatenate([vn[c * SGU_CHUNK:(c + 1) * SGU_CHUNK, cs] for c in range(n_chunks)], axis=1)
            mixed = _dot(wg, vcat)
            for c in range(n_chunks):
                rs = slice(c * SGU_CHUNK, (c + 1) * SGU_CHUNK)
                m = mixed[:, c * SGU_CHUNK:(c + 1) * SGU_CHUNK] + sb_ref[:, cs]
                ma_ref[0, rs, cs] = (ga2[rs, hs] * gu2[rs, hs] * m).astype(BF16)

    n_pairs = SGU_GROUPS // 2
    for pair in range(n_pairs):
        if pair + 1 < n_pairs:
            head_pair(pair + 1)
        sgu_pair(pair)


def _mixer_prep(x, pos, consts, tm):
    b, s, d = x.shape
    grid = (b, s // tm)
    row_spec = lambda w: pl.BlockSpec((1, tm, w), lambda i, j: (i, j, 0))
    head_spec = lambda w: pl.BlockSpec((1, HEADS, tm, w), lambda i, j: (i, 0, j, 0))
    return pl.pallas_call(
        _mixer_prep_kernel,
        grid=grid,
        in_specs=[row_spec(d), row_spec(LANES)] + [_const_spec(c.shape) for c in consts],
        out_specs=[row_spec(d), row_spec(d), head_spec(QK_PAD), head_spec(QK_PAD),
                   pl.BlockSpec((1, HEADS, V_DIM, tm), lambda i, j: (i, 0, 0, j))],
        out_shape=[jax.ShapeDtypeStruct((b, s, d), BF16), jax.ShapeDtypeStruct((b, s, d), BF16),
                   jax.ShapeDtypeStruct((b, HEADS, s, QK_PAD), BF16),
                   jax.ShapeDtypeStruct((b, HEADS, s, QK_PAD), BF16),
                   jax.ShapeDtypeStruct((b, HEADS, V_DIM, s), BF16)],
        compiler_params=pltpu.CompilerParams(dimension_semantics=("arbitrary", "arbitrary"),
                                             vmem_limit_bytes=VMEM_LIMIT),
        name="mixer_prep",
    )(x, pos, *consts)


def _attn_kernel(q_ref, k_ref, vt_ref, ma_ref, gb_ref, *refs, tq):
    n_w = (len(refs) - 1) // 2
    o_ref = refs[n_w]
    for w_ref, wb_ref in zip(refs[:n_w], refs[n_w + 1:]):
        wb_ref[...] = w_ref[...].astype(BF16)
    s = q_ref.shape[2]
    key = lax.broadcasted_iota(I32, (tq, tq), 0)
    qry = lax.broadcasted_iota(I32, (tq, tq), 1)
    diag_mask = key <= qry
    heads = range(q_ref.shape[1])
    n_tiles = s // tq

    def score_dots(qi):
        return [lax.dot_general(k_ref[0, h, :(qi + 1) * tq, :], q_ref[0, h, qi * tq:(qi + 1) * tq, :],
                                (((1,), (1,)), ((), ())), preferred_element_type=F32) for h in heads]

    scores = score_dots(0)
    for qi in range(n_tiles):
        qs = slice(qi * tq, (qi + 1) * tq)
        n_keys = (qi + 1) * tq
        next_scores = score_dots(qi + 1) if qi + 1 < n_tiles else None
        probs, sums = [], []
        for h in heads:
            sc = scores[h]
            last = jnp.where(diag_mask, sc[n_keys - tq:], -jnp.inf)
            sc = last if qi == 0 else jnp.concatenate([sc[:n_keys - tq], last], axis=0)
            p = jnp.exp(sc - jnp.max(sc, axis=0, keepdims=True))
            sums.append(jnp.sum(p, axis=0, keepdims=True))
            probs.append(p.astype(BF16))
        for h in heads:
            acc = _dot(vt_ref[0, h, :, :n_keys], probs[h])
            cs = slice(h * V_DIM, (h + 1) * V_DIM)
            o = (acc / sums[h]).T
            o_ref[0, qs, cs] = (ma_ref[0, qs, cs].astype(F32) + gb_ref[0, qs, cs].astype(F32) * o).astype(BF16)
        scores = next_scores


def _attention(q, k, vt, ma, gb, expert_weights, tq, heads_per_step):
    b, h, s, _ = q.shape
    hp = heads_per_step
    n_steps = b * (h // hp)
    w_specs = [pl.BlockSpec((w.shape[0] // n_steps,) + w.shape[1:], lambda i, j: (i * (h // hp) + j, 0, 0))
               for w in expert_weights]
    col_spec = pl.BlockSpec((1, s, hp * V_DIM), lambda i, j: (i, 0, j))
    return pl.pallas_call(
        functools.partial(_attn_kernel, tq=tq),
        grid=(b, h // hp),
        in_specs=[pl.BlockSpec((1, hp, s, QK_PAD), lambda i, j: (i, j, 0, 0)),
                  pl.BlockSpec((1, hp, s, QK_PAD), lambda i, j: (i, j, 0, 0)),
                  pl.BlockSpec((1, hp, V_DIM, s), lambda i, j: (i, j, 0, 0)), col_spec, col_spec] + w_specs,
        out_specs=[col_spec] + w_specs,
        out_shape=[jax.ShapeDtypeStruct((b, s, h * V_DIM), BF16)]
                  + [jax.ShapeDtypeStruct(w.shape, BF16) for w in expert_weights],
        compiler_params=pltpu.CompilerParams(dimension_semantics=("arbitrary", "arbitrary"),
                                             vmem_limit_bytes=VMEM_LIMIT),
        name="mla_attention",
    )(q, k, vt, ma, gb, *expert_weights)


def _route(sel, scores):
    e, tm = sel.shape
    sel3 = sel.reshape(N_GROUPS, EXPERTS_PER_GROUP, tm)
    sub = lax.broadcasted_iota(I32, sel3.shape, 1)
    m1 = jnp.max(sel3, axis=1, keepdims=True)
    first = jnp.min(jnp.where(sel3 == m1, sub, EXPERTS_PER_GROUP), axis=1, keepdims=True)
    m2 = jnp.max(jnp.where(sub == first, -jnp.inf, sel3), axis=1, keepdims=True)
    gscore = (m1 + m2).reshape(N_GROUPS, tm)
    gid = lax.broadcasted_iota(I32, (N_GROUPS, tm), 0)
    grank = jnp.zeros((N_GROUPS, tm), I32)
    for g in range(N_GROUPS):
        other = gscore[g:g + 1]
        grank += ((other > gscore) | ((other == gscore) & (g < gid))).astype(I32)
    gmask = grank < TOPK_GROUPS
    emask = jnp.broadcast_to(gmask[:, None, :], sel3.shape).reshape(e, tm)
    remaining = jnp.where(emask, sel, -jnp.inf)
    eid = lax.broadcasted_iota(I32, (e, tm), 0)
    chosen = jnp.zeros((e, tm), jnp.bool_)
    for _ in range(TOP_K):
        best = jnp.max(remaining, axis=0, keepdims=True)
        pick = eid == jnp.min(jnp.where(remaining == best, eid, e), axis=0, keepdims=True)
        chosen = chosen | pick
        remaining = jnp.where(pick, -jnp.inf, remaining)
    w = jnp.where(chosen, scores, 0.0)
    return chosen, w / jnp.sum(w, axis=0, keepdims=True) * ROUTED_SCALE


def _post_attn_kernel(x_ref, merged_ref, wo_ref, gffn_ref, wrt_ref, br_ref,
                      x1_ref, hp_ref, comb_ref, rank_ref, cnt_ref, run_ref):
    tm = x_ref.shape[0]

    @pl.when(pl.program_id(0) == 0)
    def _():
        run_ref[...] = jnp.zeros_like(run_ref)

    x1 = x_ref[...] + _dot(merged_ref[...], wo_ref[...])
    x1_ref[...] = x1
    h2 = _rms(x1, gffn_ref[...])
    hp_ref[...] = _pack_rows(h2)
    logits_t = lax.dot_general(wrt_ref[...], h2, (((1,), (1,)), ((), ())),
                               preferred_element_type=F32, precision=lax.Precision.HIGHEST)
    scores = jax.nn.sigmoid(logits_t)
    chosen, comb_t = _route(scores + br_ref[...], scores)
    comb_ref[...] = comb_t
    a = lax.broadcasted_iota(I32, (tm, tm), 0)
    b = lax.broadcasted_iota(I32, (tm, tm), 1)
    before = (a < b).astype(BF16)
    chosen_f = chosen.astype(F32)
    prefix = _dot(chosen_f.astype(BF16), before)
    run = run_ref[:, 0:1]
    rank_ref[...] = jnp.where(chosen, prefix + run, -1.0)
    run_ref[...] += jnp.sum(chosen_f, axis=1, keepdims=True)
    cnt_ref[...] = run_ref[...]


def _post_attn(x, merged, consts, tm, chunk):
    t, d = x.shape[0] // MOE_CHUNKS, x.shape[1]
    first = chunk * (t // tm)
    row_spec = lambda w: pl.BlockSpec((tm, w), lambda i: (i, 0))
    col_spec = pl.BlockSpec((N_EXPERTS, tm), lambda i: (0, i))
    return pl.pallas_call(
        _post_attn_kernel,
        grid=(t // tm,),
        in_specs=[pl.BlockSpec((tm, d), lambda i: (first + i, 0))] * 2 + [_const_spec(c.shape) for c in consts],
        out_specs=[row_spec(d), row_spec(HALF), col_spec, col_spec, _const_spec((N_EXPERTS, LANES))],
        out_shape=[jax.ShapeDtypeStruct((t, d), F32), jax.ShapeDtypeStruct((t, HALF), U32),
                   jax.ShapeDtypeStruct((N_EXPERTS, t), F32), jax.ShapeDtypeStruct((N_EXPERTS, t), F32),
                   jax.ShapeDtypeStruct((N_EXPERTS, LANES), F32)],
        scratch_shapes=[pltpu.VMEM((N_EXPERTS, LANES), F32)],
        compiler_params=pltpu.CompilerParams(dimension_semantics=("arbitrary",), vmem_limit_bytes=VMEM_LIMIT),
        name="post_attn_router",
    )(x, merged, *consts)


def _route_lists_kernel(comb_ref, rank_ref, off_ref, lay_ref, pos_ref, w_ref):
    rank = rank_ref[...]
    chosen = rank >= 0.0
    e = rank.shape[0]
    lower = (lax.broadcasted_iota(I32, (e, e), 1) < lax.broadcasted_iota(I32, (e, e), 0)).astype(BF16)
    slot = _dot(lower, chosen.astype(BF16))
    pos_sorted = rank + off_ref[...]
    tile = jnp.floor(pos_sorted * (1.0 / ROW_TILE))
    pos_full = pos_sorted + jnp.floor((tile + 0.5) / lay_ref[0:1, 0:1]) * lay_ref[0:1, 1:2]
    comb = comb_ref[...]
    pos_rows, w_rows = [], []
    for k in range(TOP_K):
        pick = chosen & (slot == float(k))
        pos_rows.append(jnp.sum(jnp.where(pick, pos_full, 0.0), axis=0, keepdims=True))
        w_rows.append(jnp.sum(jnp.where(pick, comb, 0.0), axis=0, keepdims=True))
    pos_ref[...] = jnp.concatenate(pos_rows, axis=0).astype(I32)
    w_ref[...] = jnp.concatenate(w_rows, axis=0).T


def _route_lists(comb_t, rank_t, offsets, layout, tm):
    e, t = comb_t.shape
    col_spec = pl.BlockSpec((e, tm), lambda i: (0, i))
    return pl.pallas_call(
        _route_lists_kernel,
        grid=(t // tm,),
        in_specs=[col_spec, col_spec, _const_spec((e, 1)), _const_spec(layout.shape)],
        out_specs=[pl.BlockSpec((TOP_K, tm), lambda i: (0, i)), pl.BlockSpec((tm, TOP_K), lambda i: (i, 0))],
        out_shape=[jax.ShapeDtypeStruct((TOP_K, t), I32), jax.ShapeDtypeStruct((t, TOP_K), F32)],
        compiler_params=pltpu.CompilerParams(dimension_semantics=("arbitrary",)),
        name="route_lists",
    )(comb_t, rank_t, offsets, layout)


def _sc_mesh():
    return plsc.VectorSubcoreMesh(core_axis_name="c", subcore_axis_name="s",
                                  num_cores=SC_CORES, num_subcores=SC_SUBCORES)


def _sc_worker():
    return lax.axis_index("s") * SC_CORES + lax.axis_index("c")


def _sc_scatter_rows(src, pos_win, n_rows):
    t, width = src.shape
    n_win = t // SC_WINDOW
    per_w = n_win // (SC_CORES * SC_SUBCORES)

    @functools.partial(
        pl.kernel, mesh=_sc_mesh(), out_type=jax.ShapeDtypeStruct((n_rows, width), src.dtype),
        scratch_types=[pltpu.VMEM((TOP_K, SC_WINDOW), I32), pltpu.VMEM((SC_WINDOW, width), src.dtype),
                       pltpu.SemaphoreType.DMA],
        name="sc_scatter_rows")
    def run(src_hbm, pos_hbm, out_hbm, idx_v, rows_v, sem):
        base = _sc_worker() * per_w

        @pl.loop(0, per_w)
        def _(j):
            w = base + j
            pltpu.sync_copy(pos_hbm.at[w], idx_v)
            pltpu.sync_copy(src_hbm.at[pl.ds(w * SC_WINDOW, SC_WINDOW)], rows_v)
            copies = [pltpu.async_copy(rows_v, out_hbm.at[idx_v.at[k]], sem) for k in range(TOP_K)]
            for c in copies:
                c.wait()

    return run(src, pos_win)


def _sc_gather_rows(table, pos_win):
    n_win = pos_win.shape[0]
    width = table.shape[1]
    per_w = n_win // (SC_CORES * SC_SUBCORES)

    @functools.partial(
        pl.kernel, mesh=_sc_mesh(),
        out_type=jax.ShapeDtypeStruct((TOP_K, n_win * SC_WINDOW, width), table.dtype),
        scratch_types=[pltpu.VMEM((TOP_K, SC_WINDOW), I32), pltpu.VMEM((2, SC_WINDOW, width), table.dtype),
                       pltpu.SemaphoreType.DMA, pltpu.SemaphoreType.DMA],
        name="sc_gather_rows")
    def run(table_hbm, pos_hbm, out_hbm, idx_v, rows_v, gsem, wsem):
        base = _sc_worker() * per_w

        @pl.loop(0, per_w)
        def _(j):
            w = base + j
            pltpu.sync_copy(pos_hbm.at[w], idx_v)
            for k in range(TOP_K):
                buf = rows_v.at[k % 2]
                pltpu.async_copy(table_hbm.at[idx_v.at[k]], buf, gsem).wait()
                pltpu.async_copy(buf, out_hbm.at[k, pl.ds(w * SC_WINDOW, SC_WINDOW)], wsem).wait()

    return run(table, pos_win)


def _grouped_ffn_kernel(te_ref, used_ref, xs_ref, *refs):
    w_refs, ys_ref = refs[:-1], refs[-1]

    @pl.when(pl.program_id(0) < used_ref[0])
    def _():
        def gate_up(j):
            wg_ref, wu_ref = w_refs[3 * j:3 * j + 2]
            lo, hi = _unpack_rows(xs_ref[j])
            lo, hi = lo.astype(BF16), hi.astype(BF16)
            return (_dot(lo, wg_ref[0, :HALF]) + _dot(hi, wg_ref[0, HALF:]),
                    _dot(lo, wu_ref[0, :HALF]) + _dot(hi, wu_ref[0, HALF:]))

        pre = gate_up(0)
        for j in range(TILES_PER_STEP):
            nxt = gate_up(j + 1) if j + 1 < TILES_PER_STEP else None
            act = (jax.nn.silu(pre[0]) * pre[1]).astype(BF16)
            ys_ref[j] = _pack_rows(_dot(act, w_refs[3 * j + 2][0]))
            pre = nxt


def _grouped_ffn(tile_expert, steps_used, xs, wg, wu, wd):
    n_rows, half = xs.shape
    n_steps = n_rows // (TILES_PER_STEP * ROW_TILE)

    def step_of(i, used):
        return jnp.minimum(i, used[0] - 1)

    row_spec = pl.BlockSpec((TILES_PER_STEP, ROW_TILE, half), lambda i, te, used: (0, step_of(i, used), 0))

    def exp_map(j):
        return lambda i, te, used: (te[j * n_steps + step_of(i, used)], 0, 0)

    w_specs = [pl.BlockSpec((1,) + w.shape[1:], exp_map(j)) for j in range(TILES_PER_STEP) for w in (wg, wu, wd)]
    ys = pl.pallas_call(
        _grouped_ffn_kernel,
        grid_spec=pltpu.PrefetchScalarGridSpec(
            num_scalar_prefetch=2,
            grid=(n_steps,),
            in_specs=[row_spec] + w_specs,
            out_specs=row_spec),
        out_shape=jax.ShapeDtypeStruct((TILES_PER_STEP, n_steps * ROW_TILE, half), U32),
        compiler_params=pltpu.CompilerParams(dimension_semantics=("arbitrary",), vmem_limit_bytes=VMEM_LIMIT),
        name="grouped_ffn",
    )(tile_expert, steps_used, xs.reshape(TILES_PER_STEP, n_steps * ROW_TILE, half),
      *([wg, wu, wd] * TILES_PER_STEP))
    return ys.reshape(n_rows, half)


def _combine_kernel(yt_ref, w_ref, x1_ref, p_ref, gffn_ref, wsg_ref, wsu_ref, wsd_ref,
                    gpi_ref, wpg_ref, wpp_ref, gpo_ref, *rest):
    out_ref = rest[-1]
    x1 = x1_ref[...]
    w = w_ref[...]
    h = _rms(x1, gffn_ref[...]).astype(BF16)
    shared = _dot((jax.nn.silu(_dot(h, wsg_ref[...])) * _dot(h, wsu_ref[...])).astype(BF16), wsd_ref[...])
    proj = _rms(_dot(p_ref[...].astype(BF16), wpp_ref[...]), gpo_ref[...])
    acc_lo = jnp.zeros((x1.shape[0], HALF), F32)
    acc_hi = jnp.zeros((x1.shape[0], HALF), F32)
    for k in range(TOP_K):
        lo, hi = _unpack_rows(yt_ref[k])
        wk = w[:, k:k + 1]
        acc_lo += wk * lo
        acc_hi += wk * hi
    x2 = x1 + shared + jnp.concatenate([acc_lo, acc_hi], axis=-1)
    gate = jax.nn.sigmoid(_dot(_rms(x2, gpi_ref[...]).astype(BF16), wpg_ref[...]))
    out_ref[...] = x2 + gate * proj


def _combine(yt, w_tok, x1, p, consts, tm, chunk, out_so_far):
    t, d = x1.shape
    first = chunk * (t // tm)
    row_spec = lambda w: pl.BlockSpec((tm, w), lambda i: (i, 0))
    full_spec = lambda w: pl.BlockSpec((tm, w), lambda i: (first + i, 0))
    carried = [] if out_so_far is None else [out_so_far]
    return pl.pallas_call(
        _combine_kernel,
        grid=(t // tm,),
        in_specs=[pl.BlockSpec((TOP_K, tm, HALF), lambda i: (0, i, 0)), row_spec(TOP_K), row_spec(d),
                  full_spec(PLE_DIM)] + [_const_spec(c.shape) for c in consts]
                 + [pl.BlockSpec(memory_space=pl.ANY)] * len(carried),
        out_specs=full_spec(d),
        out_shape=jax.ShapeDtypeStruct((t * MOE_CHUNKS, d), F32),
        input_output_aliases={4 + len(consts): 0} if carried else {},
        compiler_params=pltpu.CompilerParams(dimension_semantics=("arbitrary",), vmem_limit_bytes=VMEM_LIMIT),
        name="combine_ple",
    )(yt, w_tok, x1, p, *consts, *carried)


def _rotate_half(a):
    half = QK_ROPE // 2
    return jnp.concatenate([-a[..., half:], a[..., :half]], axis=-1)


def _with_rotate_half(w):
    return jnp.concatenate([w, _rotate_half(w[..., -QK_ROPE:])], axis=-1)


def _rope_gain_table(g, scale):
    half = QK_ROPE // 2
    return (jnp.concatenate([g, g[half:], g[:half]]) * scale).reshape(1, LANES)


def _layer(x, p, pos, g_mix, w_in, sgu_ln_g, sgu_ln_b, sgu_w, sgu_b, mla_g_qa, mla_w_qb, mla_g_kva, mla_w_kvb,
           qk_g_q_nope, qk_g_k_nope, qk_g_q_rope, qk_g_k_rope, w_o, g_ffn, w_router, b_router,
           w_exp_gate, w_exp_up, w_exp_down, w_sh_gate, w_sh_up, w_sh_down,
           g_ple_in, w_ple_gate, w_ple_proj, g_ple_out):
    b, s, d = x.shape
    t = b * s
    row = lambda a: a.reshape(1, -1)
    sizes = [d, d, Q_LORA, KV_LORA, QK_ROPE, d, d]
    offs = [0]
    for sz in sizes:
        offs.append(offs[-1] + sz)
    w_in_t = w_in.T
    w_r_t = _with_rotate_half(w_in[:, offs[4]:offs[5]]).T
    wqb = _with_rotate_half(mla_w_qb.reshape(Q_LORA, HEADS, QK_DIM)).reshape(Q_LORA, HEADS * QK_PAD)
    wkvb = mla_w_kvb.reshape(KV_LORA, HEADS, QK_NOPE + V_DIM)
    wkb = wkvb[:, :, :QK_NOPE].reshape(KV_LORA, HEADS * QK_NOPE)
    wvbt = wkvb[:, :, QK_NOPE:].reshape(KV_LORA, HEADS * V_DIM).T
    sgu_bias = jnp.repeat(sgu_b.T, d // SGU_GROUPS, axis=1)
    q_scale = QK_DIM ** -0.5

    consts1 = [row(g_mix), w_in_t, w_r_t, row(sgu_ln_g), row(sgu_ln_b), sgu_w, sgu_bias,
               row(mla_g_qa), wqb, row(mla_g_kva), wkb, wvbt, row(qk_g_q_nope) * q_scale, row(qk_g_k_nope),
               _rope_gain_table(qk_g_q_rope, q_scale), _rope_gain_table(qk_g_k_rope, 1.0)]
    pos_lanes = jnp.broadcast_to(pos.astype(F32)[..., None], (b, s, LANES))
    ma, gb, q, k, vt = _mixer_prep(x, pos_lanes, consts1, tm=512)
    merged, w_gate_b, w_up_b, w_down_b = _attention(q, k, vt, ma, gb, (w_exp_gate, w_exp_up, w_exp_down),
                                                    tq=256, heads_per_step=2)

    consts3 = [w_o, row(g_ffn), w_router.T, b_router.reshape(-1, 1)]
    consts6 = [row(g_ffn), w_sh_gate, w_sh_up, w_sh_down, row(g_ple_in), w_ple_gate, w_ple_proj, row(g_ple_out)]
    x2d, merged2d, p2d = (a.reshape(t, -1) for a in (x, merged, p))
    tc = t // MOE_CHUNKS
    max_tiles = (tc * TOP_K) // ROW_TILE + N_EXPERTS
    tile_ids = jnp.arange(max_tiles, dtype=I32)
    n_steps = max_tiles // TILES_PER_STEP
    out = None
    for chunk in range(MOE_CHUNKS):
        x1, h_packed, comb_t, rank_t, counts = _post_attn(x2d, merged2d, consts3, tm=512, chunk=chunk)

        counts = counts[:, 0].astype(I32)
        tiles_per_expert = (counts + ROW_TILE - 1) // ROW_TILE
        tile_end = jnp.cumsum(tiles_per_expert)
        offsets = ((tile_end - tiles_per_expert) * ROW_TILE).astype(F32).reshape(N_EXPERTS, 1)
        steps_used = (tile_end[-1:] + TILES_PER_STEP - 1) // TILES_PER_STEP
        sorted_tile = (tile_ids // n_steps) * steps_used + jnp.minimum(tile_ids % n_steps, steps_used - 1)
        tile_expert = jnp.minimum(jnp.sum((tile_end[None, :] <= sorted_tile[:, None]).astype(I32), axis=1),
                                  N_EXPERTS - 1)
        layout = jnp.concatenate([steps_used, (n_steps - steps_used) * ROW_TILE]).astype(F32).reshape(1, 2)

        pos_t, w_tok = _route_lists(comb_t, rank_t, offsets, layout, tm=512)
        pos_win = pos_t.reshape(TOP_K, tc // SC_WINDOW, SC_WINDOW).transpose(1, 0, 2)

        xs = _sc_scatter_rows(h_packed, pos_win, max_tiles * ROW_TILE)
        ys = _grouped_ffn(tile_expert, steps_used, xs, w_gate_b, w_up_b, w_down_b)
        yt = _sc_gather_rows(ys, pos_win)
        out = _combine(yt, w_tok, x1, p2d, consts6, tm=512, chunk=chunk, out_so_far=out)
    return out.reshape(b, s, d)


def kernel(x, p, positions, g_mix, w_in, sgu_ln_g, sgu_ln_b, sgu_w, sgu_b, mla_g_qa, mla_w_qb, mla_g_kva, mla_w_kvb, qk_g_q_nope, qk_g_k_nope, qk_g_q_rope, qk_g_k_rope, w_o, g_ffn, w_router, b_router, w_exp_gate, w_exp_up, w_exp_down, w_sh_gate, w_sh_up, w_sh_down, g_ple_in, w_ple_gate, w_ple_proj, g_ple_out):
    params = (g_mix, w_in, sgu_ln_g, sgu_ln_b, sgu_w, sgu_b, mla_g_qa, mla_w_qb, mla_g_kva, mla_w_kvb,
              qk_g_q_nope, qk_g_k_nope, qk_g_q_rope, qk_g_k_rope, w_o, g_ffn, w_router, b_router,
              w_exp_gate, w_exp_up, w_exp_down, w_sh_gate, w_sh_up, w_sh_down,
              g_ple_in, w_ple_gate, w_ple_proj, g_ple_out)
    for l in range(g_mix.shape[0]):
        x = _layer(x, p[l], positions, *[a[l] for a in params])
    return x
```

```python
import functools
import math

import jax
import jax.numpy as jnp
from jax import lax
from jax.experimental import pallas as pl
from jax.experimental.pallas import tpu as pltpu
from jax.experimental.pallas import tpu_sc as plsc

D_MODEL = 1024
PLE_DIM = 256
SGU_CHUNK = 128
SGU_GROUPS = 8
V_DIM = 128
HEADS = 8
QK_NOPE = 128
QK_ROPE = 64
QK_DIM = QK_NOPE + QK_ROPE
QK_PAD = 256
Q_LORA = 384
KV_LORA = 256
ROPE_THETA = 10000.0
N_EXPERTS = 64
N_GROUPS = 8
EXPERTS_PER_GROUP = 8
TOPK_GROUPS = 4
TOP_K = 8
EXPERT_FF = 256
ROUTED_SCALE = 2.5
NORM_EPS = 1e-6
LN_EPS = 1e-5

LANES = 128
MXU_COLS = 256
VMEM_LIMIT = 56 * 1024 * 1024
SC_CORES = 2
SC_SUBCORES = 16
SC_WINDOW = 64
ROW_TILE = 256
TILES_PER_STEP = 8
MOE_CHUNKS = 2
HALF = D_MODEL // 2

F32 = jnp.float32
BF16 = jnp.bfloat16
U32 = jnp.uint32
I32 = jnp.int32


def _dot(a, b):
    return lax.dot_general(a, b, (((1,), (0,)), ((), ())), preferred_element_type=F32)


def _dot_t(a, bt):
    return lax.dot_general(a, bt, (((1,), (1,)), ((), ())), preferred_element_type=F32)


def _rms(xf, g, width=None):
    width = xf.shape[-1] if width is None else width
    ms = jnp.sum(xf * xf, axis=-1, keepdims=True) * (1.0 / width)
    return xf * lax.rsqrt(ms + NORM_EPS) * g


def _pack_rows(y):
    lo = pltpu.bitcast(y[:, :HALF].astype(BF16).astype(F32), U32) >> 16
    hi = pltpu.bitcast(y[:, HALF:].astype(BF16).astype(F32), U32) & jnp.uint32(0xFFFF0000)
    return lo | hi


def _unpack_rows(w):
    lo = pltpu.bitcast(w << 16, F32)
    hi = pltpu.bitcast(w & jnp.uint32(0xFFFF0000), F32)
    return lo, hi


def _rope_table(pos_f):
    lane = lax.broadcasted_iota(I32, (1, LANES), 1)
    freq = (lane % (QK_ROPE // 2)).astype(F32)
    inv_freq = jnp.exp(freq * (-math.log(ROPE_THETA) * 2.0 / QK_ROPE))
    phase = jnp.where(lane < QK_ROPE, 0.0, math.pi / 2)
    return jnp.cos(pos_f * inv_freq - phase)


def _norm_rope(piece, table_g):
    lane = lax.broadcasted_iota(I32, (1, LANES), 1)
    ssq = jnp.sum(jnp.where(lane < QK_ROPE, piece * piece, 0.0), axis=-1, keepdims=True)
    z = piece * lax.rsqrt(ssq * (1.0 / QK_ROPE) + NORM_EPS) * table_g
    return z + pltpu.roll(z, QK_ROPE, axis=1)


def _const_spec(shape):
    return pl.BlockSpec(shape, lambda *_: (0,) * len(shape), pipeline_mode=pl.Buffered(1))


def _mixer_prep_kernel(x_ref, pos_ref, g_mix_ref, wt_ref, wrt_ref,
                       lng_ref, lnb_ref, sw_ref, sb_ref, gqa_ref, wqb_ref, gkva_ref, wkb_ref, wvbt_ref,
                       gqn_ref, gkn_ref, gqr_ref, gkr_ref,
                       ma_ref, gb_ref, q_ref, k_ref, vt_ref):
    tm = x_ref.shape[1]
    d = x_ref.shape[2]
    v0, q0, kv0, ga0 = d, 2 * d, 2 * d + Q_LORA, 2 * d + Q_LORA + KV_LORA + QK_ROPE
    gb0 = ga0 + d
    xn = _rms(x_ref[0], g_mix_ref[...]).astype(BF16)

    rope_t = _rope_table(pos_ref[0])
    qn = _rms(_dot_t(xn, wt_ref[q0:q0 + Q_LORA]), gqa_ref[...]).astype(BF16)
    kvn = _rms(_dot_t(xn, wt_ref[kv0:kv0 + KV_LORA]), gkva_ref[...]).astype(BF16)
    lane = lax.broadcasted_iota(I32, (1, LANES), 1)
    kpe = jnp.where(lane < QK_ROPE, _norm_rope(_dot_t(xn, wrt_ref[...]), rope_t * gkr_ref[...]), 0.0).astype(BF16)
    q_rope_t = rope_t * gqr_ref[...]

    def head_pair(pair):
        ps = slice(pair * MXU_COLS, (pair + 1) * MXU_COLS)
        k2 = _dot(kvn, wkb_ref[:, ps])
        v2t = lax.dot_general(wvbt_ref[ps, :].astype(BF16), kvn, (((1,), (1,)), ((), ())),
                              preferred_element_type=F32).astype(BF16)
        for half in range(2):
            g = 2 * pair + half
            hs = slice(half * LANES, (half + 1) * LANES)
            qh = _dot(qn, wqb_ref[:, g * QK_PAD:(g + 1) * QK_PAD])
            q_ref[0, g, :, :QK_NOPE] = _rms(qh[:, :QK_NOPE], gqn_ref[...]).astype(BF16)
            q_ref[0, g, :, QK_NOPE:] = _norm_rope(qh[:, QK_NOPE:], q_rope_t).astype(BF16)
            k_ref[0, g, :, :QK_NOPE] = _rms(k2[:, hs], gkn_ref[...]).astype(BF16)
            k_ref[0, g, :, QK_NOPE:] = kpe
            vt_ref[0, g] = v2t[hs, :]

    head_pair(0)

    gv = jax.nn.gelu(_dot_t(xn, wt_ref[v0:v0 + d]))
    mu = jnp.mean(gv, axis=-1, keepdims=True)
    vc = gv - mu
    var = jnp.mean(vc * vc, axis=-1, keepdims=True)
    vn = (vc * lax.rsqrt(var + LN_EPS) * lng_ref[...] + lnb_ref[...]).astype(BF16)
    row = lax.broadcasted_iota(I32, (SGU_CHUNK, SGU_CHUNK), 0)
    col = lax.broadcasted_iota(I32, (SGU_CHUNK, SGU_CHUNK), 1)
    causal = col <= row
    n_chunks = tm // SGU_CHUNK

    def sgu_pair(pair):
        ps = slice(pair * MXU_COLS, (pair + 1) * MXU_COLS)
        p0 = pair * MXU_COLS
        gu2 = jax.nn.gelu(_dot_t(xn, wt_ref[p0:p0 + MXU_COLS]))
        ga2 = jax.nn.sigmoid(_dot_t(xn, wt_ref[ga0 + p0:ga0 + p0 + MXU_COLS]))
        gb_ref[0, :, ps] = jax.nn.sigmoid(_dot_t(xn, wt_ref[gb0 + p0:gb0 + p0 + MXU_COLS])).astype(BF16)
        for half in range(2):
            g = 2 * pair + half
            hs = slice(half * LANES, (half + 1) * LANES)
            cs = slice(g * SGU_CHUNK, (g + 1) * SGU_CHUNK)
            wg = jnp.where(causal, sw_ref[g], 0.0).astype(BF16)
            vcat = jnp.concatenate([vn[c * SGU_CHUNK:(c + 1) * SGU_CHUNK, cs] for c in range(n_chunks)], axis=1)
            mixed = _dot(wg, vcat)
            for c in range(n_chunks):
                rs = slice(c * SGU_CHUNK, (c + 1) * SGU_CHUNK)
                m = mixed[:, c * SGU_CHUNK:(c + 1) * SGU_CHUNK] + sb_ref[:, cs]
                ma_ref[0, rs, cs] = (ga2[rs, hs] * gu2[rs, hs] * m).astype(BF16)

    n_pairs = SGU_GROUPS // 2
    for pair in range(n_pairs):
        if pair + 1 < n_pairs:
            head_pair(pair + 1)
        sgu_pair(pair)


def _mixer_prep(x, pos, consts, tm):
    b, s, d = x.shape
    grid = (b, s // tm)
    row_spec = lambda w: pl.BlockSpec((1, tm, w), lambda i, j: (i, j, 0))
    head_spec = lambda w: pl.BlockSpec((1, HEADS, tm, w), lambda i, j: (i, 0, j, 0))
    return pl.pallas_call(
        _mixer_prep_kernel,
        grid=grid,
        in_specs=[row_spec(d), row_spec(LANES)] + [_const_spec(c.shape) for c in consts],
        out_specs=[row_spec(d), row_spec(d), head_spec(QK_PAD), head_spec(QK_PAD),
                   pl.BlockSpec((1, HEADS, V_DIM, tm), lambda i, j: (i, 0, 0, j))],
        out_shape=[jax.ShapeDtypeStruct((b, s, d), BF16), jax.ShapeDtypeStruct((b, s, d), BF16),
                   jax.ShapeDtypeStruct((b, HEADS, s, QK_PAD), BF16),
                   jax.ShapeDtypeStruct((b, HEADS, s, QK_PAD), BF16),
                   jax.ShapeDtypeStruct((b, HEADS, V_DIM, s), BF16)],
        compiler_params=pltpu.CompilerParams(dimension_semantics=("arbitrary", "arbitrary"),
                                             vmem_limit_bytes=VMEM_LIMIT),
        name="mixer_prep",
    )(x, pos, *consts)


def _attn_kernel(q_ref, k_ref, vt_ref, ma_ref, gb_ref, *refs, tq):
    n_w = (len(refs) - 1) // 2
    o_ref = refs[n_w]
    for w_ref, wb_ref in zip(refs[:n_w], refs[n_w + 1:]):
        wb_ref[...] = w_ref[...].astype(BF16)
    s = q_ref.shape[2]
    key = lax.broadcasted_iota(I32, (tq, tq), 0)
    qry = lax.broadcasted_iota(I32, (tq, tq), 1)
    diag_mask = key <= qry
    heads = range(q_ref.shape[1])
    n_tiles = s // tq

    def score_dots(qi):
        return [lax.dot_general(k_ref[0, h, :(qi + 1) * tq, :], q_ref[0, h, qi * tq:(qi + 1) * tq, :],
                                (((1,), (1,)), ((), ())), preferred_element_type=F32) for h in heads]

    scores = score_dots(0)
    for qi in range(n_tiles):
        qs = slice(qi * tq, (qi + 1) * tq)
        n_keys = (qi + 1) * tq
        next_scores = score_dots(qi + 1) if qi + 1 < n_tiles else None
        probs, sums = [], []
        for h in heads:
            sc = scores[h]
            last = jnp.where(diag_mask, sc[n_keys - tq:], -jnp.inf)
            sc = last if qi == 0 else jnp.concatenate([sc[:n_keys - tq], last], axis=0)
            p = jnp.exp(sc - jnp.max(sc, axis=0, keepdims=True))
            sums.append(jnp.sum(p, axis=0, keepdims=True))
            probs.append(p.astype(BF16))
        for h in heads:
            acc = _dot(vt_ref[0, h, :, :n_keys], probs[h])
            cs = slice(h * V_DIM, (h + 1) * V_DIM)
            o = (acc / sums[h]).T
            o_ref[0, qs, cs] = (ma_ref[0, qs, cs].astype(F32) + gb_ref[0, qs, cs].astype(F32) * o).astype(BF16)
        scores = next_scores


def _attention(q, k, vt, ma, gb, expert_weights, tq, heads_per_step):
    b, h, s, _ = q.shape
    hp = heads_per_step
    n_steps = b * (h // hp)
    w_specs = [pl.BlockSpec((w.shape[0] // n_steps,) + w.shape[1:], lambda i, j: (i * (h // hp) + j, 0, 0))
               for w in expert_weights]
    col_spec = pl.BlockSpec((1, s, hp * V_DIM), lambda i, j: (i, 0, j))
    return pl.pallas_call(
        functools.partial(_attn_kernel, tq=tq),
        grid=(b, h // hp),
        in_specs=[pl.BlockSpec((1, hp, s, QK_PAD), lambda i, j: (i, j, 0, 0)),
                  pl.BlockSpec((1, hp, s, QK_PAD), lambda i, j: (i, j, 0, 0)),
                  pl.BlockSpec((1, hp, V_DIM, s), lambda i, j: (i, j, 0, 0)), col_spec, col_spec] + w_specs,
        out_specs=[col_spec] + w_specs,
        out_shape=[jax.ShapeDtypeStruct((b, s, h * V_DIM), BF16)]
                  + [jax.ShapeDtypeStruct(w.shape, BF16) for w in expert_weights],
        compiler_params=pltpu.CompilerParams(dimension_semantics=("arbitrary", "arbitrary"),
                                             vmem_limit_bytes=VMEM_LIMIT),
        name="mla_attention",
    )(q, k, vt, ma, gb, *expert_weights)


def _route(sel, scores):
    e, tm = sel.shape
    sel3 = sel.reshape(N_GROUPS, EXPERTS_PER_GROUP, tm)
    sub = lax.broadcasted_iota(I32, sel3.shape, 1)
    m1 = jnp.max(sel3, axis=1, keepdims=True)
    first = jnp.min(jnp.where(sel3 == m1, sub, EXPERTS_PER_GROUP), axis=1, keepdims=True)
    m2 = jnp.max(jnp.where(sub == first, -jnp.inf, sel3), axis=1, keepdims=True)
    gscore = (m1 + m2).reshape(N_GROUPS, tm)
    gid = lax.broadcasted_iota(I32, (N_GROUPS, tm), 0)
    grank = jnp.zeros((N_GROUPS, tm), I32)
    for g in range(N_GROUPS):
        other = gscore[g:g + 1]
        grank += ((other > gscore) | ((other == gscore) & (g < gid))).astype(I32)
    gmask = grank < TOPK_GROUPS
    emask = jnp.broadcast_to(gmask[:, None, :], sel3.shape).reshape(e, tm)
    remaining = jnp.where(emask, sel, -jnp.inf)
    eid = lax.broadcasted_iota(I32, (e, tm), 0)
    chosen = jnp.zeros((e, tm), jnp.bool_)
    for _ in range(TOP_K):
        best = jnp.max(remaining, axis=0, keepdims=True)
        pick = eid == jnp.min(jnp.where(remaining == best, eid, e), axis=0, keepdims=True)
        chosen = chosen | pick
        remaining = jnp.where(pick, -jnp.inf, remaining)
    w = jnp.where(chosen, scores, 0.0)
    return chosen, w / jnp.sum(w, axis=0, keepdims=True) * ROUTED_SCALE


def _post_attn_kernel(x_ref, merged_ref, wo_ref, gffn_ref, wrt_ref, br_ref,
                      x1_ref, hp_ref, comb_ref, rank_ref, cnt_ref, run_ref):
    tm = x_ref.shape[0]

    @pl.when(pl.program_id(0) == 0)
    def _():
        run_ref[...] = jnp.zeros_like(run_ref)

    x1 = x_ref[...] + _dot(merged_ref[...], wo_ref[...])
    x1_ref[...] = x1
    h2 = _rms(x1, gffn_ref[...])
    hp_ref[...] = _pack_rows(h2)
    logits_t = lax.dot_general(wrt_ref[...], h2, (((1,), (1,)), ((), ())),
                               preferred_element_type=F32, precision=lax.Precision.HIGHEST)
    scores = jax.nn.sigmoid(logits_t)
    chosen, comb_t = _route(scores + br_ref[...], scores)
    comb_ref[...] = comb_t
    a = lax.broadcasted_iota(I32, (tm, tm), 0)
    b = lax.broadcasted_iota(I32, (tm, tm), 1)
    before = (a < b).astype(BF16)
    chosen_f = chosen.astype(F32)
    prefix = _dot(chosen_f.astype(BF16), before)
    run = run_ref[:, 0:1]
    rank_ref[...] = jnp.where(chosen, prefix + run, -1.0)
    run_ref[...] += jnp.sum(chosen_f, axis=1, keepdims=True)
    cnt_ref[...] = run_ref[...]


def _post_attn(x, merged, consts, tm, chunk):
    t, d = x.shape[0] // MOE_CHUNKS, x.shape[1]
    first = chunk * (t // tm)
    row_spec = lambda w: pl.BlockSpec((tm, w), lambda i: (i, 0))
    col_spec = pl.BlockSpec((N_EXPERTS, tm), lambda i: (0, i))
    return pl.pallas_call(
        _post_attn_kernel,
        grid=(t // tm,),
        in_specs=[pl.BlockSpec((tm, d), lambda i: (first + i, 0))] * 2 + [_const_spec(c.shape) for c in consts],
        out_specs=[row_spec(d), row_spec(HALF), col_spec, col_spec, _const_spec((N_EXPERTS, LANES))],
        out_shape=[jax.ShapeDtypeStruct((t, d), F32), jax.ShapeDtypeStruct((t, HALF), U32),
                   jax.ShapeDtypeStruct((N_EXPERTS, t), F32), jax.ShapeDtypeStruct((N_EXPERTS, t), F32),
                   jax.ShapeDtypeStruct((N_EXPERTS, LANES), F32)],
        scratch_shapes=[pltpu.VMEM((N_EXPERTS, LANES), F32)],
        compiler_params=pltpu.CompilerParams(dimension_semantics=("arbitrary",), vmem_limit_bytes=VMEM_LIMIT),
        name="post_attn_router",
    )(x, merged, *consts)


def _route_lists_kernel(comb_ref, rank_ref, off_ref, lay_ref, pos_ref, w_ref):
    rank = rank_ref[...]
    chosen = rank >= 0.0
    e = rank.shape[0]
    lower = (lax.broadcasted_iota(I32, (e, e), 1) < lax.broadcasted_iota(I32, (e, e), 0)).astype(BF16)
    slot = _dot(lower, chosen.astype(BF16))
    pos_sorted = rank + off_ref[...]
    tile = jnp.floor(pos_sorted * (1.0 / ROW_TILE))
    pos_full = pos_sorted + jnp.floor((tile + 0.5) / lay_ref[0:1, 0:1]) * lay_ref[0:1, 1:2]
    comb = comb_ref[...]
    pos_rows, w_rows = [], []
    for k in range(TOP_K):
        pick = chosen & (slot == float(k))
        pos_rows.append(jnp.sum(jnp.where(pick, pos_full, 0.0), axis=0, keepdims=True))
        w_rows.append(jnp.sum(jnp.where(pick, comb, 0.0), axis=0, keepdims=True))
    pos_ref[...] = jnp.concatenate(pos_rows, axis=0).astype(I32)
    w_ref[...] = jnp.concatenate(w_rows, axis=0).T


def _route_lists(comb_t, rank_t, offsets, layout, tm):
    e, t = comb_t.shape
    col_spec = pl.BlockSpec((e, tm), lambda i: (0, i))
    return pl.pallas_call(
        _route_lists_kernel,
        grid=(t // tm,),
        in_specs=[col_spec, col_spec, _const_spec((e, 1)), _const_spec(layout.shape)],
        out_specs=[pl.BlockSpec((TOP_K, tm), lambda i: (0, i)), pl.BlockSpec((tm, TOP_K), lambda i: (i, 0))],
        out_shape=[jax.ShapeDtypeStruct((TOP_K, t), I32), jax.ShapeDtypeStruct((t, TOP_K), F32)],
        compiler_params=pltpu.CompilerParams(dimension_semantics=("arbitrary",)),
        name="route_lists",
    )(comb_t, rank_t, offsets, layout)


def _sc_mesh():
    return plsc.VectorSubcoreMesh(core_axis_name="c", subcore_axis_name="s",
                                  num_cores=SC_CORES, num_subcores=SC_SUBCORES)


def _sc_worker():
    return lax.axis_index("s") * SC_CORES + lax.axis_index("c")


def _sc_scatter_rows(src, pos_win, n_rows):
    t, width = src.shape
    n_win = t // SC_WINDOW
    per_w = n_win // (SC_CORES * SC_SUBCORES)

    @functools.partial(
        pl.kernel, mesh=_sc_mesh(), out_type=jax.ShapeDtypeStruct((n_rows, width), src.dtype),
        scratch_types=[pltpu.VMEM((TOP_K, SC_WINDOW), I32), pltpu.VMEM((SC_WINDOW, width), src.dtype),
                       pltpu.SemaphoreType.DMA],
        name="sc_scatter_rows")
    def run(src_hbm, pos_hbm, out_hbm, idx_v, rows_v, sem):
        base = _sc_worker() * per_w

        @pl.loop(0, per_w)
        def _(j):
            w = base + j
            pltpu.sync_copy(pos_hbm.at[w], idx_v)
            pltpu.sync_copy(src_hbm.at[pl.ds(w * SC_WINDOW, SC_WINDOW)], rows_v)
            copies = [pltpu.async_copy(rows_v, out_hbm.at[idx_v.at[k]], sem) for k in range(TOP_K)]
            for c in copies:
                c.wait()

    return run(src, pos_win)


def _sc_gather_rows(table, pos_win):
    n_win = pos_win.shape[0]
    width = table.shape[1]
    per_w = n_win // (SC_CORES * SC_SUBCORES)

    @functools.partial(
        pl.kernel, mesh=_sc_mesh(),
        out_type=jax.ShapeDtypeStruct((TOP_K, n_win * SC_WINDOW, width), table.dtype),
        scratch_types=[pltpu.VMEM((TOP_K, SC_WINDOW), I32), pltpu.VMEM((2, SC_WINDOW, width), table.dtype),
                       pltpu.SemaphoreType.DMA, pltpu.SemaphoreType.DMA],
        name="sc_gather_rows")
    def run(table_hbm, pos_hbm, out_hbm, idx_v, rows_v, gsem, wsem):
        base = _sc_worker() * per_w

        @pl.loop(0, per_w)
        def _(j):
            w = base + j
            pltpu.sync_copy(pos_hbm.at[w], idx_v)
            for k in range(TOP_K):
                buf = rows_v.at[k % 2]
                pltpu.async_copy(table_hbm.at[idx_v.at[k]], buf, gsem).wait()
                pltpu.async_copy(buf, out_hbm.at[k, pl.ds(w * SC_WINDOW, SC_WINDOW)], wsem).wait()

    return run(table, pos_win)


def _grouped_ffn_kernel(te_ref, used_ref, xs_ref, *refs):
    w_refs, ys_ref = refs[:-1], refs[-1]

    @pl.when(pl.program_id(0) < used_ref[0])
    def _():
        def gate_up(j):
            wg_ref, wu_ref = w_refs[3 * j:3 * j + 2]
            lo, hi = _unpack_rows(xs_ref[j])
            lo, hi = lo.astype(BF16), hi.astype(BF16)
            return (_dot(lo, wg_ref[0, :HALF]) + _dot(hi, wg_ref[0, HALF:]),
                    _dot(lo, wu_ref[0, :HALF]) + _dot(hi, wu_ref[0, HALF:]))

        pre = gate_up(0)
        for j in range(TILES_PER_STEP):
            nxt = gate_up(j + 1) if j + 1 < TILES_PER_STEP else None
            act = (jax.nn.silu(pre[0]) * pre[1]).astype(BF16)
            ys_ref[j] = _pack_rows(_dot(act, w_refs[3 * j + 2][0]))
            pre = nxt


def _grouped_ffn(tile_expert, steps_used, xs, wg, wu, wd):
    n_rows, half = xs.shape
    n_steps = n_rows // (TILES_PER_STEP * ROW_TILE)

    def step_of(i, used):
        return jnp.minimum(i, used[0] - 1)

    row_spec = pl.BlockSpec((TILES_PER_STEP, ROW_TILE, half), lambda i, te, used: (0, step_of(i, used), 0))

    def exp_map(j):
        return lambda i, te, used: (te[j * n_steps + step_of(i, used)], 0, 0)

    w_specs = [pl.BlockSpec((1,) + w.shape[1:], exp_map(j)) for j in range(TILES_PER_STEP) for w in (wg, wu, wd)]
    ys = pl.pallas_call(
        _grouped_ffn_kernel,
        grid_spec=pltpu.PrefetchScalarGridSpec(
            num_scalar_prefetch=2,
            grid=(n_steps,),
            in_specs=[row_spec] + w_specs,
            out_specs=row_spec),
        out_shape=jax.ShapeDtypeStruct((TILES_PER_STEP, n_steps * ROW_TILE, half), U32),
        compiler_params=pltpu.CompilerParams(dimension_semantics=("arbitrary",), vmem_limit_bytes=VMEM_LIMIT),
        name="grouped_ffn",
    )(tile_expert, steps_used, xs.reshape(TILES_PER_STEP, n_steps * ROW_TILE, half),
      *([wg, wu, wd] * TILES_PER_STEP))
    return ys.reshape(n_rows, half)


def _combine_kernel(yt_ref, w_ref, x1_ref, p_ref, gffn_ref, wsg_ref, wsu_ref, wsd_ref,
                    gpi_ref, wpg_ref, wpp_ref, gpo_ref, *rest):
    out_ref = rest[-1]
    x1 = x1_ref[...]
    w = w_ref[...]
    h = _rms(x1, gffn_ref[...]).astype(BF16)
    shared = _dot((jax.nn.silu(_dot(h, wsg_ref[...])) * _dot(h, wsu_ref[...])).astype(BF16), wsd_ref[...])
    proj = _rms(_dot(p_ref[...].astype(BF16), wpp_ref[...]), gpo_ref[...])
    acc_lo = jnp.zeros((x1.shape[0], HALF), F32)
    acc_hi = jnp.zeros((x1.shape[0], HALF), F32)
    for k in range(TOP_K):
        lo, hi = _unpack_rows(yt_ref[k])
        wk = w[:, k:k + 1]
        acc_lo += wk * lo
        acc_hi += wk * hi
    x2 = x1 + shared + jnp.concatenate([acc_lo, acc_hi], axis=-1)
    gate = jax.nn.sigmoid(_dot(_rms(x2, gpi_ref[...]).astype(BF16), wpg_ref[...]))
    out_ref[...] = x2 + gate * proj


def _combine(yt, w_tok, x1, p, consts, tm, chunk, out_so_far):
    t, d = x1.shape
    first = chunk * (t // tm)
    row_spec = lambda w: pl.BlockSpec((tm, w), lambda i: (i, 0))
    full_spec = lambda w: pl.BlockSpec((tm, w), lambda i: (first + i, 0))
    carried = [] if out_so_far is None else [out_so_far]
    return pl.pallas_call(
        _combine_kernel,
        grid=(t // tm,),
        in_specs=[pl.BlockSpec((TOP_K, tm, HALF), lambda i: (0, i, 0)), row_spec(TOP_K), row_spec(d),
                  full_spec(PLE_DIM)] + [_const_spec(c.shape) for c in consts]
                 + [pl.BlockSpec(memory_space=pl.ANY)] * len(carried),
        out_specs=full_spec(d),
        out_shape=jax.ShapeDtypeStruct((t * MOE_CHUNKS, d), F32),
        input_output_aliases={4 + len(consts): 0} if carried else {},
        compiler_params=pltpu.CompilerParams(dimension_semantics=("arbitrary",), vmem_limit_bytes=VMEM_LIMIT),
        name="combine_ple",
    )(yt, w_tok, x1, p, *consts, *carried)


def _rotate_half(a):
    half = QK_ROPE // 2
    return jnp.concatenate([-a[..., half:], a[..., :half]], axis=-1)


def _with_rotate_half(w):
    return jnp.concatenate([w, _rotate_half(w[..., -QK_ROPE:])], axis=-1)


def _rope_gain_table(g, scale):
    half = QK_ROPE // 2
    return (jnp.concatenate([g, g[half:], g[:half]]) * scale).reshape(1, LANES)


def _layer(x, p, pos, g_mix, w_in, sgu_ln_g, sgu_ln_b, sgu_w, sgu_b, mla_g_qa, mla_w_qb, mla_g_kva, mla_w_kvb,
           qk_g_q_nope, qk_g_k_nope, qk_g_q_rope, qk_g_k_rope, w_o, g_ffn, w_router, b_router,
           w_exp_gate, w_exp_up, w_exp_down, w_sh_gate, w_sh_up, w_sh_down,
           g_ple_in, w_ple_gate, w_ple_proj, g_ple_out):
    b, s, d = x.shape
    t = b * s
    row = lambda a: a.reshape(1, -1)
    sizes = [d, d, Q_LORA, KV_LORA, QK_ROPE, d, d]
    offs = [0]
    for sz in sizes:
        offs.append(offs[-1] + sz)
    w_in_t = w_in.T
    w_r_t = _with_rotate_half(w_in[:, offs[4]:offs[5]]).T
    wqb = _with_rotate_half(mla_w_qb.reshape(Q_LORA, HEADS, QK_DIM)).reshape(Q_LORA, HEADS * QK_PAD)
    wkvb = mla_w_kvb.reshape(KV_LORA, HEADS, QK_NOPE + V_DIM)
    wkb = wkvb[:, :, :QK_NOPE].reshape(KV_LORA, HEADS * QK_NOPE)
    wvbt = wkvb[:, :, QK_NOPE:].reshape(KV_LORA, HEADS * V_DIM).T
    sgu_bias = jnp.repeat(sgu_b.T, d // SGU_GROUPS, axis=1)
    q_scale = QK_DIM ** -0.5

    consts1 = [row(g_mix), w_in_t, w_r_t, row(sgu_ln_g), row(sgu_ln_b), sgu_w, sgu_bias,
               row(mla_g_qa), wqb, row(mla_g_kva), wkb, wvbt, row(qk_g_q_nope) * q_scale, row(qk_g_k_nope),
               _rope_gain_table(qk_g_q_rope, q_scale), _rope_gain_table(qk_g_k_rope, 1.0)]
    pos_lanes = jnp.broadcast_to(pos.astype(F32)[..., None], (b, s, LANES))
    ma, gb, q, k, vt = _mixer_prep(x, pos_lanes, consts1, tm=512)
    merged, w_gate_b, w_up_b, w_down_b = _attention(q, k, vt, ma, gb, (w_exp_gate, w_exp_up, w_exp_down),
                                                    tq=256, heads_per_step=2)

    consts3 = [w_o, row(g_ffn), w_router.T, b_router.reshape(-1, 1)]
    consts6 = [row(g_ffn), w_sh_gate, w_sh_up, w_sh_down, row(g_ple_in), w_ple_gate, w_ple_proj, row(g_ple_out)]
    x2d, merged2d, p2d = (a.reshape(t, -1) for a in (x, merged, p))
    tc = t // MOE_CHUNKS
    max_tiles = (tc * TOP_K) // ROW_TILE + N_EXPERTS
    tile_ids = jnp.arange(max_tiles, dtype=I32)
    n_steps = max_tiles // TILES_PER_STEP
    out = None
    for chunk in range(MOE_CHUNKS):
        x1, h_packed, comb_t, rank_t, counts = _post_attn(x2d, merged2d, consts3, tm=1024, chunk=chunk)

        counts = counts[:, 0].astype(I32)
        tiles_per_expert = (counts + ROW_TILE - 1) // ROW_TILE
        tile_end = jnp.cumsum(tiles_per_expert)
        offsets = ((tile_end - tiles_per_expert) * ROW_TILE).astype(F32).reshape(N_EXPERTS, 1)
        steps_used = (tile_end[-1:] + TILES_PER_STEP - 1) // TILES_PER_STEP
        sorted_tile = (tile_ids // n_steps) * steps_used + jnp.minimum(tile_ids % n_steps, steps_used - 1)
        tile_expert = jnp.minimum(jnp.sum((tile_end[None, :] <= sorted_tile[:, None]).astype(I32), axis=1),
                                  N_EXPERTS - 1)
        layout = jnp.concatenate([steps_used, (n_steps - steps_used) * ROW_TILE]).astype(F32).reshape(1, 2)

        pos_t, w_tok = _route_lists(comb_t, rank_t, offsets, layout, tm=2048)
        pos_win = pos_t.reshape(TOP_K, tc // SC_WINDOW, SC_WINDOW).transpose(1, 0, 2)

        xs = _sc_scatter_rows(h_packed, pos_win, max_tiles * ROW_TILE)
        ys = _grouped_ffn(tile_expert, steps_used, xs, w_gate_b, w_up_b, w_down_b)
        yt = _sc_gather_rows(ys, pos_win)
        out = _combine(yt, w_tok, x1, p2d, consts6, tm=512, chunk=chunk, out_so_far=out)
    return out.reshape(b, s, d)


def kernel(x, p, positions, g_mix, w_in, sgu_ln_g, sgu_ln_b, sgu_w, sgu_b, mla_g_qa, mla_w_qb, mla_g_kva, mla_w_kvb, qk_g_q_nope, qk_g_k_nope, qk_g_q_rope, qk_g_k_rope, w_o, g_ffn, w_router, b_router, w_exp_gate, w_exp_up, w_exp_down, w_sh_gate, w_sh_up, w_sh_down, g_ple_in, w_ple_gate, w_ple_proj, g_ple_out):
    params = (g_mix, w_in, sgu_ln_g, sgu_ln_b, sgu_w, sgu_b, mla_g_qa, mla_w_qb, mla_g_kva, mla_w_kvb,
              qk_g_q_nope, qk_g_k_nope, qk_g_q_rope, qk_g_k_rope, w_o, g_ffn, w_router, b_router,
              w_exp_gate, w_exp_up, w_exp_down, w_sh_gate, w_sh_up, w_sh_down,
              g_ple_in, w_ple_gate, w_ple_proj, g_ple_out)
    for l in range(g_mix.shape[0]):
        x = _layer(x, p[l], positions, *[a[l] for a in params])
    return x
```

```python
import functools
import math

import jax
import jax.numpy as jnp
from jax import lax
from jax.experimental import pallas as pl
from jax.experimental.pallas import tpu as pltpu
from jax.experimental.pallas import tpu_sc as plsc

D_MODEL = 1024
PLE_DIM = 256
SGU_CHUNK = 128
SGU_GROUPS = 8
V_DIM = 128
HEADS = 8
QK_NOPE = 128
QK_ROPE = 64
QK_DIM = QK_NOPE + QK_ROPE
QK_PAD = 256
Q_LORA = 384
KV_LORA = 256
ROPE_THETA = 10000.0
N_EXPERTS = 64
N_GROUPS = 8
EXPERTS_PER_GROUP = 8
TOPK_GROUPS = 4
TOP_K = 8
EXPERT_FF = 256
ROUTED_SCALE = 2.5
NORM_EPS = 1e-6
LN_EPS = 1e-5

LANES = 128
MXU_COLS = 256
VMEM_LIMIT = 56 * 1024 * 1024
SC_CORES = 2
SC_SUBCORES = 16
SC_WINDOW = 64
ROW_TILE = 256
TILES_PER_STEP = 8
MOE_CHUNKS = 2
HALF = D_MODEL // 2

F32 = jnp.float32
BF16 = jnp.bfloat16
U32 = jnp.uint32
I32 = jnp.int32


def _dot(a, b):
    return lax.dot_general(a, b, (((1,), (0,)), ((), ())), preferred_element_type=F32)


def _dot_t(a, bt):
    return lax.dot_general(a, bt, (((1,), (1,)), ((), ())), preferred_element_type=F32)


def _rms(xf, g):
    ms = jnp.sum(xf * xf, axis=-1, keepdims=True) * (1.0 / xf.shape[-1])
    return xf * lax.rsqrt(ms + NORM_EPS) * g


def _pack_rows(y):
    lo = pltpu.bitcast(y[:, :HALF].astype(BF16).astype(F32), U32) >> 16
    hi = pltpu.bitcast(y[:, HALF:].astype(BF16).astype(F32), U32) & jnp.uint32(0xFFFF0000)
    return lo | hi


def _unpack_rows(w):
    lo = pltpu.bitcast(w << 16, F32)
    hi = pltpu.bitcast(w & jnp.uint32(0xFFFF0000), F32)
    return lo, hi


def _rope_table(pos_f):
    lane = lax.broadcasted_iota(I32, (1, LANES), 1)
    freq = (lane % (QK_ROPE // 2)).astype(F32)
    inv_freq = jnp.exp(freq * (-math.log(ROPE_THETA) * 2.0 / QK_ROPE))
    phase = jnp.where(lane < QK_ROPE, 0.0, math.pi / 2)
    return jnp.cos(pos_f * inv_freq - phase)


def _norm_rope(piece, table_g):
    lane = lax.broadcasted_iota(I32, (1, LANES), 1)
    ssq = jnp.sum(jnp.where(lane < QK_ROPE, piece * piece, 0.0), axis=-1, keepdims=True)
    z = piece * lax.rsqrt(ssq * (1.0 / QK_ROPE) + NORM_EPS) * table_g
    return z + pltpu.roll(z, QK_ROPE, axis=1)


def _const_spec(shape):
    return pl.BlockSpec(shape, lambda *_: (0,) * len(shape), pipeline_mode=pl.Buffered(1))


def _mixer_prep_kernel(x_ref, pos_ref, g_mix_ref, wt_ref, wrt_ref,
                       lng_ref, lnb_ref, sw_ref, sb_ref, gqa_ref, wqb_ref, gkva_ref, wkb_ref, wvbt_ref,
                       gqn_ref, gkn_ref, gqr_ref, gkr_ref,
                       ma_ref, gb_ref, q_ref, k_ref, vt_ref):
    tm = x_ref.shape[1]
    d = x_ref.shape[2]
    v0, q0, kv0, ga0 = d, 2 * d, 2 * d + Q_LORA, 2 * d + Q_LORA + KV_LORA + QK_ROPE
    gb0 = ga0 + d
    xn = _rms(x_ref[0], g_mix_ref[...]).astype(BF16)

    rope_t = _rope_table(pos_ref[0])
    qn = _rms(_dot_t(xn, wt_ref[q0:q0 + Q_LORA]), gqa_ref[...]).astype(BF16)
    kvn = _rms(_dot_t(xn, wt_ref[kv0:kv0 + KV_LORA]), gkva_ref[...]).astype(BF16)
    lane = lax.broadcasted_iota(I32, (1, LANES), 1)
    kpe = jnp.where(lane < QK_ROPE, _norm_rope(_dot_t(xn, wrt_ref[...]), rope_t * gkr_ref[...]), 0.0).astype(BF16)
    q_rope_t = rope_t * gqr_ref[...]

    def head_pair(pair):
        ps = slice(pair * MXU_COLS, (pair + 1) * MXU_COLS)
        k2 = _dot(kvn, wkb_ref[:, ps])
        v2t = lax.dot_general(wvbt_ref[ps, :].astype(BF16), kvn, (((1,), (1,)), ((), ())),
                              preferred_element_type=F32).astype(BF16)
        for half in range(2):
            g = 2 * pair + half
            hs = slice(half * LANES, (half + 1) * LANES)
            qh = _dot(qn, wqb_ref[:, g * QK_PAD:(g + 1) * QK_PAD])
            q_ref[0, g, :, :QK_NOPE] = _rms(qh[:, :QK_NOPE], gqn_ref[...]).astype(BF16)
            q_ref[0, g, :, QK_NOPE:] = _norm_rope(qh[:, QK_NOPE:], q_rope_t).astype(BF16)
            k_ref[0, g, :, :QK_NOPE] = _rms(k2[:, hs], gkn_ref[...]).astype(BF16)
            k_ref[0, g, :, QK_NOPE:] = kpe
            vt_ref[0, g] = v2t[hs, :]

    head_pair(0)

    gv = jax.nn.gelu(_dot_t(xn, wt_ref[v0:v0 + d]))
    mu = jnp.mean(gv, axis=-1, keepdims=True)
    vc = gv - mu
    var = jnp.mean(vc * vc, axis=-1, keepdims=True)
    vn = (vc * lax.rsqrt(var + LN_EPS) * lng_ref[...] + lnb_ref[...]).astype(BF16)
    row = lax.broadcasted_iota(I32, (SGU_CHUNK, SGU_CHUNK), 0)
    col = lax.broadcasted_iota(I32, (SGU_CHUNK, SGU_CHUNK), 1)
    causal = col <= row
    n_chunks = tm // SGU_CHUNK

    def sgu_pair(pair):
        ps = slice(pair * MXU_COLS, (pair + 1) * MXU_COLS)
        p0 = pair * MXU_COLS
        gu2 = jax.nn.gelu(_dot_t(xn, wt_ref[p0:p0 + MXU_COLS]))
        ga2 = jax.nn.sigmoid(_dot_t(xn, wt_ref[ga0 + p0:ga0 + p0 + MXU_COLS]))
        gb_ref[0, :, ps] = jax.nn.sigmoid(_dot_t(xn, wt_ref[gb0 + p0:gb0 + p0 + MXU_COLS])).astype(BF16)
        for half in range(2):
            g = 2 * pair + half
            hs = slice(half * LANES, (half + 1) * LANES)
            cs = slice(g * SGU_CHUNK, (g + 1) * SGU_CHUNK)
            wg = jnp.where(causal, sw_ref[g], 0.0).astype(BF16)
            vcat = jnp.concatenate([vn[c * SGU_CHUNK:(c + 1) * SGU_CHUNK, cs] for c in range(n_chunks)], axis=1)
            mixed = _dot(wg, vcat)
            for c in range(n_chunks):
                rs = slice(c * SGU_CHUNK, (c + 1) * SGU_CHUNK)
                m = mixed[:, c * SGU_CHUNK:(c + 1) * SGU_CHUNK] + sb_ref[:, cs]
                ma_ref[0, rs, cs] = (ga2[rs, hs] * gu2[rs, hs] * m).astype(BF16)

    n_pairs = SGU_GROUPS // 2
    for pair in range(n_pairs):
        if pair + 1 < n_pairs:
            head_pair(pair + 1)
        sgu_pair(pair)


def _mixer_prep(x, pos, consts, tm):
    b, s, d = x.shape
    grid = (b, s // tm)
    row_spec = lambda w: pl.BlockSpec((1, tm, w), lambda i, j: (i, j, 0))
    head_spec = lambda w: pl.BlockSpec((1, HEADS, tm, w), lambda i, j: (i, 0, j, 0))
    return pl.pallas_call(
        _mixer_prep_kernel,
        grid=grid,
        in_specs=[row_spec(d), row_spec(LANES)] + [_const_spec(c.shape) for c in consts],
        out_specs=[row_spec(d), row_spec(d), head_spec(QK_PAD), head_spec(QK_PAD),
                   pl.BlockSpec((1, HEADS, V_DIM, tm), lambda i, j: (i, 0, 0, j))],
        out_shape=[jax.ShapeDtypeStruct((b, s, d), BF16), jax.ShapeDtypeStruct((b, s, d), BF16),
                   jax.ShapeDtypeStruct((b, HEADS, s, QK_PAD), BF16),
                   jax.ShapeDtypeStruct((b, HEADS, s, QK_PAD), BF16),
                   jax.ShapeDtypeStruct((b, HEADS, V_DIM, s), BF16)],
        compiler_params=pltpu.CompilerParams(dimension_semantics=("arbitrary", "arbitrary"),
                                             vmem_limit_bytes=VMEM_LIMIT),
        name="mixer_prep",
    )(x, pos, *consts)


def _attn_kernel(q_ref, k_ref, vt_ref, ma_ref, gb_ref, *refs, tq):
    n_w = (len(refs) - 1) // 2
    o_ref = refs[n_w]
    for w_ref, wb_ref in zip(refs[:n_w], refs[n_w + 1:]):
        wb_ref[...] = w_ref[...].astype(BF16)
    s = q_ref.shape[2]
    key = lax.broadcasted_iota(I32, (tq, tq), 0)
    qry = lax.broadcasted_iota(I32, (tq, tq), 1)
    diag_mask = key <= qry
    heads = range(q_ref.shape[1])
    n_tiles = s // tq

    def score_dots(qi):
        return [lax.dot_general(k_ref[0, h, :(qi + 1) * tq, :], q_ref[0, h, qi * tq:(qi + 1) * tq, :],
                                (((1,), (1,)), ((), ())), preferred_element_type=F32) for h in heads]

    scores = score_dots(0)
    for qi in range(n_tiles):
        qs = slice(qi * tq, (qi + 1) * tq)
        n_keys = (qi + 1) * tq
        next_scores = score_dots(qi + 1) if qi + 1 < n_tiles else None
        probs, sums = [], []
        for h in heads:
            sc = scores[h]
            last = jnp.where(diag_mask, sc[n_keys - tq:], -jnp.inf)
            sc = last if qi == 0 else jnp.concatenate([sc[:n_keys - tq], last], axis=0)
            p = jnp.exp(sc - jnp.max(sc, axis=0, keepdims=True))
            sums.append(jnp.sum(p, axis=0, keepdims=True))
            probs.append(p.astype(BF16))
        for h in heads:
            acc = _dot(vt_ref[0, h, :, :n_keys], probs[h])
            cs = slice(h * V_DIM, (h + 1) * V_DIM)
            o = (acc / sums[h]).T
            o_ref[0, qs, cs] = (ma_ref[0, qs, cs].astype(F32) + gb_ref[0, qs, cs].astype(F32) * o).astype(BF16)
        scores = next_scores


def _attention(q, k, vt, ma, gb, expert_weights, tq, heads_per_step):
    b, h, s, _ = q.shape
    hp = heads_per_step
    n_steps = b * (h // hp)
    w_specs = [pl.BlockSpec((w.shape[0] // n_steps,) + w.shape[1:], lambda i, j: (i * (h // hp) + j, 0, 0))
               for w in expert_weights]
    col_spec = pl.BlockSpec((1, s, hp * V_DIM), lambda i, j: (i, 0, j))
    return pl.pallas_call(
        functools.partial(_attn_kernel, tq=tq),
        grid=(b, h // hp),
        in_specs=[pl.BlockSpec((1, hp, s, QK_PAD), lambda i, j: (i, j, 0, 0)),
                  pl.BlockSpec((1, hp, s, QK_PAD), lambda i, j: (i, j, 0, 0)),
                  pl.BlockSpec((1, hp, V_DIM, s), lambda i, j: (i, j, 0, 0)), col_spec, col_spec] + w_specs,
        out_specs=[col_spec] + w_specs,
        out_shape=[jax.ShapeDtypeStruct((b, s, h * V_DIM), BF16)]
                  + [jax.ShapeDtypeStruct(w.shape, BF16) for w in expert_weights],
        compiler_params=pltpu.CompilerParams(dimension_semantics=("arbitrary", "arbitrary"),
                                             vmem_limit_bytes=VMEM_LIMIT),
        name="mla_attention",
    )(q, k, vt, ma, gb, *expert_weights)


def _route(sel, scores):
    e, tm = sel.shape
    sel3 = sel.reshape(N_GROUPS, EXPERTS_PER_GROUP, tm)
    sub = lax.broadcasted_iota(I32, sel3.shape, 1)
    m1 = jnp.max(sel3, axis=1, keepdims=True)
    first = jnp.min(jnp.where(sel3 == m1, sub, EXPERTS_PER_GROUP), axis=1, keepdims=True)
    m2 = jnp.max(jnp.where(sub == first, -jnp.inf, sel3), axis=1, keepdims=True)
    gscore = (m1 + m2).reshape(N_GROUPS, tm)
    gid = lax.broadcasted_iota(I32, (N_GROUPS, tm), 0)
    grank = jnp.zeros((N_GROUPS, tm), I32)
    for g in range(N_GROUPS):
        other = gscore[g:g + 1]
        grank += ((other > gscore) | ((other == gscore) & (g < gid))).astype(I32)
    gmask = grank < TOPK_GROUPS
    emask = jnp.broadcast_to(gmask[:, None, :], sel3.shape).reshape(e, tm)
    remaining = jnp.where(emask, sel, -jnp.inf)
    eid = lax.broadcasted_iota(I32, (e, tm), 0)
    chosen = jnp.zeros((e, tm), jnp.bool_)
    for _ in range(TOP_K):
        best = jnp.max(remaining, axis=0, keepdims=True)
        pick = eid == jnp.min(jnp.where(remaining == best, eid, e), axis=0, keepdims=True)
        chosen = chosen | pick
        remaining = jnp.where(pick, -jnp.inf, remaining)
    w = jnp.where(chosen, scores, 0.0)
    return chosen, w / jnp.sum(w, axis=0, keepdims=True) * ROUTED_SCALE


def _post_attn_kernel(x_ref, merged_ref, wo_ref, gffn_ref, wrt_ref, br_ref,
                      x1_ref, hp_ref, comb_ref, rank_ref, cnt_ref, run_ref):
    tm = x_ref.shape[0]

    @pl.when(pl.program_id(0) == 0)
    def _():
        run_ref[...] = jnp.zeros_like(run_ref)

    x1 = x_ref[...] + _dot(merged_ref[...], wo_ref[...])
    x1_ref[...] = x1
    h2 = _rms(x1, gffn_ref[...])
    hp_ref[...] = _pack_rows(h2)
    logits_t = lax.dot_general(wrt_ref[...], h2, (((1,), (1,)), ((), ())),
                               preferred_element_type=F32, precision=lax.Precision.HIGHEST)
    scores = jax.nn.sigmoid(logits_t)
    chosen, comb_t = _route(scores + br_ref[...], scores)
    comb_ref[...] = comb_t
    a = lax.broadcasted_iota(I32, (tm, tm), 0)
    b = lax.broadcasted_iota(I32, (tm, tm), 1)
    before = (a < b).astype(BF16)
    chosen_f = chosen.astype(F32)
    prefix = _dot(chosen_f.astype(BF16), before)
    run = run_ref[:, 0:1]
    rank_ref[...] = jnp.where(chosen, prefix + run, -1.0)
    run_ref[...] += jnp.sum(chosen_f, axis=1, keepdims=True)
    cnt_ref[...] = run_ref[...]


def _post_attn(x, merged, consts, tm, chunk):
    t, d = x.shape[0] // MOE_CHUNKS, x.shape[1]
    first = chunk * (t // tm)
    row_spec = lambda w: pl.BlockSpec((tm, w), lambda i: (i, 0))
    col_spec = pl.BlockSpec((N_EXPERTS, tm), lambda i: (0, i))
    return pl.pallas_call(
        _post_attn_kernel,
        grid=(t // tm,),
        in_specs=[pl.BlockSpec((tm, d), lambda i: (first + i, 0))] * 2 + [_const_spec(c.shape) for c in consts],
        out_specs=[row_spec(d), row_spec(HALF), col_spec, col_spec, _const_spec((N_EXPERTS, LANES))],
        out_shape=[jax.ShapeDtypeStruct((t, d), F32), jax.ShapeDtypeStruct((t, HALF), U32),
                   jax.ShapeDtypeStruct((N_EXPERTS, t), F32), jax.ShapeDtypeStruct((N_EXPERTS, t), F32),
                   jax.ShapeDtypeStruct((N_EXPERTS, LANES), F32)],
        scratch_shapes=[pltpu.VMEM((N_EXPERTS, LANES), F32)],
        compiler_params=pltpu.CompilerParams(dimension_semantics=("arbitrary",), vmem_limit_bytes=VMEM_LIMIT),
        name="post_attn_router",
    )(x, merged, *consts)


def _route_lists_kernel(comb_ref, rank_ref, off_ref, lay_ref, pos_ref, w_ref):
    rank = rank_ref[...]
    chosen = rank >= 0.0
    e = rank.shape[0]
    lower = (lax.broadcasted_iota(I32, (e, e), 1) < lax.broadcasted_iota(I32, (e, e), 0)).astype(BF16)
    slot = _dot(lower, chosen.astype(BF16))
    pos_sorted = rank + off_ref[...]
    tile = jnp.floor(pos_sorted * (1.0 / ROW_TILE))
    pos_full = pos_sorted + jnp.floor((tile + 0.5) / lay_ref[0:1, 0:1]) * lay_ref[0:1, 1:2]
    comb = comb_ref[...]
    pos_rows, w_rows = [], []
    for k in range(TOP_K):
        pick = chosen & (slot == float(k))
        pos_rows.append(jnp.sum(jnp.where(pick, pos_full, 0.0), axis=0, keepdims=True))
        w_rows.append(jnp.sum(jnp.where(pick, comb, 0.0), axis=0, keepdims=True))
    pos_ref[...] = jnp.concatenate(pos_rows, axis=0).astype(I32)
    w_ref[...] = jnp.concatenate(w_rows, axis=0).T


def _route_lists(comb_t, rank_t, offsets, layout, tm):
    e, t = comb_t.shape
    col_spec = pl.BlockSpec((e, tm), lambda i: (0, i))
    return pl.pallas_call(
        _route_lists_kernel,
        grid=(t // tm,),
        in_specs=[col_spec, col_spec, _const_spec((e, 1)), _const_spec(layout.shape)],
        out_specs=[pl.BlockSpec((TOP_K, tm), lambda i: (0, i)), pl.BlockSpec((tm, TOP_K), lambda i: (i, 0))],
        out_shape=[jax.ShapeDtypeStruct((TOP_K, t), I32), jax.ShapeDtypeStruct((t, TOP_K), F32)],
        compiler_params=pltpu.CompilerParams(dimension_semantics=("arbitrary",)),
        name="route_lists",
    )(comb_t, rank_t, offsets, layout)


def _sc_mesh():
    return plsc.VectorSubcoreMesh(core_axis_name="c", subcore_axis_name="s",
                                  num_cores=SC_CORES, num_subcores=SC_SUBCORES)


def _sc_worker():
    return lax.axis_index("s") * SC_CORES + lax.axis_index("c")


def _sc_scatter_rows(src, pos_win, n_rows):
    t, width = src.shape
    n_win = t // SC_WINDOW
    per_w = n_win // (SC_CORES * SC_SUBCORES)

    @functools.partial(
        pl.kernel, mesh=_sc_mesh(), out_type=jax.ShapeDtypeStruct((n_rows, width), src.dtype),
        scratch_types=[pltpu.VMEM((TOP_K, SC_WINDOW), I32), pltpu.VMEM((SC_WINDOW, width), src.dtype),
                       pltpu.SemaphoreType.DMA],
        name="sc_scatter_rows")
    def run(src_hbm, pos_hbm, out_hbm, idx_v, rows_v, sem):
        base = _sc_worker() * per_w

        @pl.loop(0, per_w)
        def _(j):
            w = base + j
            pltpu.sync_copy(pos_hbm.at[w], idx_v)
            pltpu.sync_copy(src_hbm.at[pl.ds(w * SC_WINDOW, SC_WINDOW)], rows_v)
            copies = [pltpu.async_copy(rows_v, out_hbm.at[idx_v.at[k]], sem) for k in range(TOP_K)]
            for c in copies:
                c.wait()

    return run(src, pos_win)


def _sc_gather_rows(table, pos_win):
    n_win = pos_win.shape[0]
    width = table.shape[1]
    per_w = n_win // (SC_CORES * SC_SUBCORES)

    @functools.partial(
        pl.kernel, mesh=_sc_mesh(),
        out_type=jax.ShapeDtypeStruct((TOP_K, n_win * SC_WINDOW, width), table.dtype),
        scratch_types=[pltpu.VMEM((TOP_K, SC_WINDOW), I32), pltpu.VMEM((2, SC_WINDOW, width), table.dtype),
                       pltpu.SemaphoreType.DMA, pltpu.SemaphoreType.DMA],
        name="sc_gather_rows")
    def run(table_hbm, pos_hbm, out_hbm, idx_v, rows_v, gsem, wsem):
        base = _sc_worker() * per_w

        @pl.loop(0, per_w)
        def _(j):
            w = base + j
            pltpu.sync_copy(pos_hbm.at[w], idx_v)
            for k in range(TOP_K):
                buf = rows_v.at[k % 2]
                pltpu.async_copy(table_hbm.at[idx_v.at[k]], buf, gsem).wait()
                pltpu.async_copy(buf, out_hbm.at[k, pl.ds(w * SC_WINDOW, SC_WINDOW)], wsem).wait()

    return run(table, pos_win)


def _grouped_ffn_kernel(te_ref, used_ref, xs_ref, *refs):
    w_refs, ys_ref = refs[:-1], refs[-1]

    @pl.when(pl.program_id(0) < used_ref[0])
    def _():
        def gate_up(j):
            wg_ref, wu_ref = w_refs[3 * j:3 * j + 2]
            lo, hi = _unpack_rows(xs_ref[j])
            lo, hi = lo.astype(BF16), hi.astype(BF16)
            return (_dot(lo, wg_ref[0, :HALF]) + _dot(hi, wg_ref[0, HALF:]),
                    _dot(lo, wu_ref[0, :HALF]) + _dot(hi, wu_ref[0, HALF:]))

        pre = gate_up(0)
        for j in range(TILES_PER_STEP):
            nxt = gate_up(j + 1) if j + 1 < TILES_PER_STEP else None
            act = (jax.nn.silu(pre[0]) * pre[1]).astype(BF16)
            ys_ref[j] = _pack_rows(_dot(act, w_refs[3 * j + 2][0]))
            pre = nxt


def _grouped_ffn(tile_expert, steps_used, xs, wg, wu, wd):
    n_rows, half = xs.shape
    n_steps = n_rows // (TILES_PER_STEP * ROW_TILE)

    def step_of(i, used):
        return jnp.minimum(i, used[0] - 1)

    row_spec = pl.BlockSpec((TILES_PER_STEP, ROW_TILE, half), lambda i, te, used: (0, step_of(i, used), 0))

    def exp_map(j):
        return lambda i, te, used: (te[j * n_steps + step_of(i, used)], 0, 0)

    w_specs = [pl.BlockSpec((1,) + w.shape[1:], exp_map(j)) for j in range(TILES_PER_STEP) for w in (wg, wu, wd)]
    ys = pl.pallas_call(
        _grouped_ffn_kernel,
        grid_spec=pltpu.PrefetchScalarGridSpec(
            num_scalar_prefetch=2,
            grid=(n_steps,),
            in_specs=[row_spec] + w_specs,
            out_specs=row_spec),
        out_shape=jax.ShapeDtypeStruct((TILES_PER_STEP, n_steps * ROW_TILE, half), U32),
        compiler_params=pltpu.CompilerParams(dimension_semantics=("arbitrary",), vmem_limit_bytes=VMEM_LIMIT),
        name="grouped_ffn",
    )(tile_expert, steps_used, xs.reshape(TILES_PER_STEP, n_steps * ROW_TILE, half),
      *([wg, wu, wd] * TILES_PER_STEP))
    return ys.reshape(n_rows, half)


def _combine_kernel(yt_ref, w_ref, x1_ref, p_ref, gffn_ref, wsg_ref, wsu_ref, wsd_ref,
                    gpi_ref, wpg_ref, wpp_ref, gpo_ref, *rest):
    out_ref = rest[-1]
    x1 = x1_ref[...]
    w = w_ref[...]
    h = _rms(x1, gffn_ref[...]).astype(BF16)
    shared = _dot((jax.nn.silu(_dot(h, wsg_ref[...])) * _dot(h, wsu_ref[...])).astype(BF16), wsd_ref[...])
    proj = _rms(_dot(p_ref[...].astype(BF16), wpp_ref[...]), gpo_ref[...])
    acc_lo = jnp.zeros((x1.shape[0], HALF), F32)
    acc_hi = jnp.zeros((x1.shape[0], HALF), F32)
    for k in range(TOP_K):
        lo, hi = _unpack_rows(yt_ref[k])
        wk = w[:, k:k + 1]
        acc_lo += wk * lo
        acc_hi += wk * hi
    x2 = x1 + shared + jnp.concatenate([acc_lo, acc_hi], axis=-1)
    gate = jax.nn.sigmoid(_dot(_rms(x2, gpi_ref[...]).astype(BF16), wpg_ref[...]))
    out_ref[...] = x2 + gate * proj


def _combine(yt, w_tok, x1, p, consts, tm, chunk, out_so_far):
    t, d = x1.shape
    first = chunk * (t // tm)
    row_spec = lambda w: pl.BlockSpec((tm, w), lambda i: (i, 0))
    full_spec = lambda w: pl.BlockSpec((tm, w), lambda i: (first + i, 0))
    carried = [] if out_so_far is None else [out_so_far]
    return pl.pallas_call(
        _combine_kernel,
        grid=(t // tm,),
        in_specs=[pl.BlockSpec((TOP_K, tm, HALF), lambda i: (0, i, 0)), row_spec(TOP_K), row_spec(d),
                  full_spec(PLE_DIM)] + [_const_spec(c.shape) for c in consts]
                 + [pl.BlockSpec(memory_space=pl.ANY)] * len(carried),
        out_specs=full_spec(d),
        out_shape=jax.ShapeDtypeStruct((t * MOE_CHUNKS, d), F32),
        input_output_aliases={4 + len(consts): 0} if carried else {},
        compiler_params=pltpu.CompilerParams(dimension_semantics=("arbitrary",), vmem_limit_bytes=VMEM_LIMIT),
        name="combine_ple",
    )(yt, w_tok, x1, p, *consts, *carried)


def _rotate_half(a):
    half = QK_ROPE // 2
    return jnp.concatenate([-a[..., half:], a[..., :half]], axis=-1)


def _with_rotate_half(w):
    return jnp.concatenate([w, _rotate_half(w[..., -QK_ROPE:])], axis=-1)


def _rope_gain_table(g, scale):
    half = QK_ROPE // 2
    return (jnp.concatenate([g, g[half:], g[:half]]) * scale).reshape(1, LANES)


def _layer(x, p, pos, g_mix, w_in, sgu_ln_g, sgu_ln_b, sgu_w, sgu_b, mla_g_qa, mla_w_qb, mla_g_kva, mla_w_kvb,
           qk_g_q_nope, qk_g_k_nope, qk_g_q_rope, qk_g_k_rope, w_o, g_ffn, w_router, b_router,
           w_exp_gate, w_exp_up, w_exp_down, w_sh_gate, w_sh_up, w_sh_down,
           g_ple_in, w_ple_gate, w_ple_proj, g_ple_out):
    b, s, d = x.shape
    t = b * s
    row = lambda a: a.reshape(1, -1)
    sizes = [d, d, Q_LORA, KV_LORA, QK_ROPE, d, d]
    offs = [0]
    for sz in sizes:
        offs.append(offs[-1] + sz)
    w_in_t = w_in.T
    w_r_t = _with_rotate_half(w_in[:, offs[4]:offs[5]]).T
    wqb = _with_rotate_half(mla_w_qb.reshape(Q_LORA, HEADS, QK_DIM)).reshape(Q_LORA, HEADS * QK_PAD)
    wkvb = mla_w_kvb.reshape(KV_LORA, HEADS, QK_NOPE + V_DIM)
    wkb = wkvb[:, :, :QK_NOPE].reshape(KV_LORA, HEADS * QK_NOPE)
    wvbt = wkvb[:, :, QK_NOPE:].reshape(KV_LORA, HEADS * V_DIM).T
    sgu_bias = jnp.repeat(sgu_b.T, d // SGU_GROUPS, axis=1)
    q_scale = QK_DIM ** -0.5

    consts1 = [row(g_mix), w_in_t, w_r_t, row(sgu_ln_g), row(sgu_ln_b), sgu_w, sgu_bias,
               row(mla_g_qa), wqb, row(mla_g_kva), wkb, wvbt, row(qk_g_q_nope) * q_scale, row(qk_g_k_nope),
               _rope_gain_table(qk_g_q_rope, q_scale), _rope_gain_table(qk_g_k_rope, 1.0)]
    pos_lanes = jnp.broadcast_to(pos.astype(F32)[..., None], (b, s, LANES))
    ma, gb, q, k, vt = _mixer_prep(x, pos_lanes, consts1, tm=512)
    merged, w_gate_b, w_up_b, w_down_b = _attention(q, k, vt, ma, gb, (w_exp_gate, w_exp_up, w_exp_down),
                                                    tq=256, heads_per_step=2)

    consts3 = [w_o, row(g_ffn), w_router.T, b_router.reshape(-1, 1)]
    consts6 = [row(g_ffn), w_sh_gate, w_sh_up, w_sh_down, row(g_ple_in), w_ple_gate, w_ple_proj, row(g_ple_out)]
    x2d, merged2d, p2d = (a.reshape(t, -1) for a in (x, merged, p))
    tc = t // MOE_CHUNKS
    max_tiles = (tc * TOP_K) // ROW_TILE + N_EXPERTS
    tile_ids = jnp.arange(max_tiles, dtype=I32)
    n_steps = max_tiles // TILES_PER_STEP
    out = None
    for chunk in range(MOE_CHUNKS):
        x1, h_packed, comb_t, rank_t, counts = _post_attn(x2d, merged2d, consts3, tm=1024, chunk=chunk)

        counts = counts[:, 0].astype(I32)
        tiles_per_expert = (counts + ROW_TILE - 1) // ROW_TILE
        tile_end = jnp.cumsum(tiles_per_expert)
        offsets = ((tile_end - tiles_per_expert) * ROW_TILE).astype(F32).reshape(N_EXPERTS, 1)
        steps_used = (tile_end[-1:] + TILES_PER_STEP - 1) // TILES_PER_STEP
        sorted_tile = (tile_ids // n_steps) * steps_used + jnp.minimum(tile_ids % n_steps, steps_used - 1)
        tile_expert = jnp.minimum(jnp.sum((tile_end[None, :] <= sorted_tile[:, None]).astype(I32), axis=1),
                                  N_EXPERTS - 1)
        layout = jnp.concatenate([steps_used, (n_steps - steps_used) * ROW_TILE]).astype(F32).reshape(1, 2)

        pos_t, w_tok = _route_lists(comb_t, rank_t, offsets, layout, tm=2048)
        pos_win = pos_t.reshape(TOP_K, tc // SC_WINDOW, SC_WINDOW).transpose(1, 0, 2)

        xs = _sc_scatter_rows(h_packed, pos_win, max_tiles * ROW_TILE)
        ys = _grouped_ffn(tile_expert, steps_used, xs, w_gate_b, w_up_b, w_down_b)
        yt = _sc_gather_rows(ys, pos_win)
        out = _combine(yt, w_tok, x1, p2d, consts6, tm=512, chunk=chunk, out_so_far=out)
    return out.reshape(b, s, d)


def kernel(x, p, positions, g_mix, w_in, sgu_ln_g, sgu_ln_b, sgu_w, sgu_b, mla_g_qa, mla_w_qb, mla_g_kva, mla_w_kvb, qk_g_q_nope, qk_g_k_nope, qk_g_q_rope, qk_g_k_rope, w_o, g_ffn, w_router, b_router, w_exp_gate, w_exp_up, w_exp_down, w_sh_gate, w_sh_up, w_sh_down, g_ple_in, w_ple_gate, w_ple_proj, g_ple_out):
    params = (g_mix, w_in, sgu_ln_g, sgu_ln_b, sgu_w, sgu_b, mla_g_qa, mla_w_qb, mla_g_kva, mla_w_kvb,
              qk_g_q_nope, qk_g_k_nope, qk_g_q_rope, qk_g_k_rope, w_o, g_ffn, w_router, b_router,
              w_exp_gate, w_exp_up, w_exp_down, w_sh_gate, w_sh_up, w_sh_down,
              g_ple_in, w_ple_gate, w_ple_proj, g_ple_out)
    for l in range(g_mix.shape[0]):
        x = _layer(x, p[l], positions, *[a[l] for a in params])
    return x
```

```python
import functools
import math

import jax
import jax.numpy as jnp
from jax import lax
from jax.experimental import pallas as pl
from jax.experimental.pallas import tpu as pltpu
from jax.experimental.pallas import tpu_sc as plsc

D_MODEL = 1024
PLE_DIM = 256
SGU_CHUNK = 128
SGU_GROUPS = 8
V_DIM = 128
HEADS = 8
QK_NOPE = 128
QK_ROPE = 64
QK_DIM = QK_NOPE + QK_ROPE
QK_PAD = 256
Q_LORA = 384
KV_LORA = 256
ROPE_THETA = 10000.0
N_EXPERTS = 64
N_GROUPS = 8
EXPERTS_PER_GROUP = 8
TOPK_GROUPS = 4
TOP_K = 8
EXPERT_FF = 256
ROUTED_SCALE = 2.5
NORM_EPS = 1e-6
LN_EPS = 1e-5

LANES = 128
MXU_COLS = 256
VMEM_LIMIT = 56 * 1024 * 1024
SC_CORES = 2
SC_SUBCORES = 16
SC_WINDOW = 64
ROW_TILE = 256
TILES_PER_STEP = 8
MOE_CHUNKS = 2
HALF = D_MODEL // 2

F32 = jnp.float32
BF16 = jnp.bfloat16
U32 = jnp.uint32
I32 = jnp.int32


def _dot(a, b):
    return lax.dot_general(a, b, (((1,), (0,)), ((), ())), preferred_element_type=F32)


def _dot_t(a, bt):
    return lax.dot_general(a, bt, (((1,), (1,)), ((), ())), preferred_element_type=F32)


def _rms(xf, g):
    ms = jnp.sum(xf * xf, axis=-1, keepdims=True) * (1.0 / xf.shape[-1])
    return xf * lax.rsqrt(ms + NORM_EPS) * g


def _pack_rows(y):
    lo = pltpu.bitcast(y[:, :HALF].astype(BF16).astype(F32), U32) >> 16
    hi = pltpu.bitcast(y[:, HALF:].astype(BF16).astype(F32), U32) & jnp.uint32(0xFFFF0000)
    return lo | hi


def _unpack_rows(w):
    lo = pltpu.bitcast(w << 16, F32)
    hi = pltpu.bitcast(w & jnp.uint32(0xFFFF0000), F32)
    return lo, hi


def _rope_table(pos_f):
    lane = lax.broadcasted_iota(I32, (1, LANES), 1)
    freq = (lane % (QK_ROPE // 2)).astype(F32)
    inv_freq = jnp.exp(freq * (-math.log(ROPE_THETA) * 2.0 / QK_ROPE))
    phase = jnp.where(lane < QK_ROPE, 0.0, math.pi / 2)
    return jnp.cos(pos_f * inv_freq - phase)


def _norm_rope(piece, table_g):
    lane = lax.broadcasted_iota(I32, (1, LANES), 1)
    ssq = jnp.sum(jnp.where(lane < QK_ROPE, piece * piece, 0.0), axis=-1, keepdims=True)
    z = piece * lax.rsqrt(ssq * (1.0 / QK_ROPE) + NORM_EPS) * table_g
    return z + pltpu.roll(z, QK_ROPE, axis=1)


def _const_spec(shape):
    return pl.BlockSpec(shape, lambda *_: (0,) * len(shape), pipeline_mode=pl.Buffered(1))


def _mixer_prep_kernel(x_ref, pos_ref, g_mix_ref, wt_ref, wrt_ref,
                       lng_ref, lnb_ref, sw_ref, sb_ref, gqa_ref, wqb_ref, gkva_ref, wkb_ref, wvbt_ref,
                       gqn_ref, gkn_ref, gqr_ref, gkr_ref,
                       ma_ref, gb_ref, q_ref, k_ref, vt_ref):
    tm = x_ref.shape[1]
    d = x_ref.shape[2]
    v0, q0, kv0, ga0 = d, 2 * d, 2 * d + Q_LORA, 2 * d + Q_LORA + KV_LORA + QK_ROPE
    gb0 = ga0 + d
    xn = _rms(x_ref[0], g_mix_ref[...]).astype(BF16)

    rope_t = _rope_table(pos_ref[0])
    qn = _rms(_dot_t(xn, wt_ref[q0:q0 + Q_LORA]), gqa_ref[...]).astype(BF16)
    kvn = _rms(_dot_t(xn, wt_ref[kv0:kv0 + KV_LORA]), gkva_ref[...]).astype(BF16)
    lane = lax.broadcasted_iota(I32, (1, LANES), 1)
    kpe = jnp.where(lane < QK_ROPE, _norm_rope(_dot_t(xn, wrt_ref[...]), rope_t * gkr_ref[...]), 0.0).astype(BF16)
    q_rope_t = rope_t * gqr_ref[...]

    def head_pair(pair):
        ps = slice(pair * MXU_COLS, (pair + 1) * MXU_COLS)
        k2 = _dot(kvn, wkb_ref[:, ps])
        v2t = lax.dot_general(wvbt_ref[ps, :].astype(BF16), kvn, (((1,), (1,)), ((), ())),
                              preferred_element_type=F32).astype(BF16)
        for half in range(2):
            g = 2 * pair + half
            hs = slice(half * LANES, (half + 1) * LANES)
            qh = _dot(qn, wqb_ref[:, g * QK_PAD:(g + 1) * QK_PAD])
            q_ref[0, g, :, :QK_NOPE] = _rms(qh[:, :QK_NOPE], gqn_ref[...]).astype(BF16)
            q_ref[0, g, :, QK_NOPE:] = _norm_rope(qh[:, QK_NOPE:], q_rope_t).astype(BF16)
            k_ref[0, g, :, :QK_NOPE] = _rms(k2[:, hs], gkn_ref[...]).astype(BF16)
            k_ref[0, g, :, QK_NOPE:] = kpe
            vt_ref[0, g] = v2t[hs, :]

    head_pair(0)

    gv = jax.nn.gelu(_dot_t(xn, wt_ref[v0:v0 + d]))
    mu = jnp.mean(gv, axis=-1, keepdims=True)
    vc = gv - mu
    var = jnp.mean(vc * vc, axis=-1, keepdims=True)
    vn = (vc * lax.rsqrt(var + LN_EPS) * lng_ref[...] + lnb_ref[...]).astype(BF16)
    row = lax.broadcasted_iota(I32, (SGU_CHUNK, SGU_CHUNK), 0)
    col = lax.broadcasted_iota(I32, (SGU_CHUNK, SGU_CHUNK), 1)
    causal = col <= row
    n_chunks = tm // SGU_CHUNK

    def sgu_pair(pair):
        ps = slice(pair * MXU_COLS, (pair + 1) * MXU_COLS)
        p0 = pair * MXU_COLS
        gu2 = jax.nn.gelu(_dot_t(xn, wt_ref[p0:p0 + MXU_COLS]))
        ga2 = jax.nn.sigmoid(_dot_t(xn, wt_ref[ga0 + p0:ga0 + p0 + MXU_COLS]))
        gb_ref[0, :, ps] = jax.nn.sigmoid(_dot_t(xn, wt_ref[gb0 + p0:gb0 + p0 + MXU_COLS])).astype(BF16)
        for half in range(2):
            g = 2 * pair + half
            hs = slice(half * LANES, (half + 1) * LANES)
            cs = slice(g * SGU_CHUNK, (g + 1) * SGU_CHUNK)
            wg = jnp.where(causal, sw_ref[g], 0.0).astype(BF16)
            vcat = jnp.concatenate([vn[c * SGU_CHUNK:(c + 1) * SGU_CHUNK, cs] for c in range(n_chunks)], axis=1)
            mixed = _dot(wg, vcat)
            for c in range(n_chunks):
                rs = slice(c * SGU_CHUNK, (c + 1) * SGU_CHUNK)
                m = mixed[:, c * SGU_CHUNK:(c + 1) * SGU_CHUNK] + sb_ref[:, cs]
                ma_ref[0, rs, cs] = (ga2[rs, hs] * gu2[rs, hs] * m).astype(BF16)

    n_pairs = SGU_GROUPS // 2
    for pair in range(n_pairs):
        if pair + 1 < n_pairs:
            head_pair(pair + 1)
        sgu_pair(pair)


def _mixer_prep(x, pos, consts, tm):
    b, s, d = x.shape
    grid = (b, s // tm)
    row_spec = lambda w: pl.BlockSpec((1, tm, w), lambda i, j: (i, j, 0))
    head_spec = lambda w: pl.BlockSpec((1, HEADS, tm, w), lambda i, j: (i, 0, j, 0))
    return pl.pallas_call(
        _mixer_prep_kernel,
        grid=grid,
        in_specs=[row_spec(d), row_spec(LANES)] + [_const_spec(c.shape) for c in consts],
        out_specs=[row_spec(d), row_spec(d), head_spec(QK_PAD), head_spec(QK_PAD),
                   pl.BlockSpec((1, HEADS, V_DIM, tm), lambda i, j: (i, 0, 0, j))],
        out_shape=[jax.ShapeDtypeStruct((b, s, d), BF16), jax.ShapeDtypeStruct((b, s, d), BF16),
                   jax.ShapeDtypeStruct((b, HEADS, s, QK_PAD), BF16),
                   jax.ShapeDtypeStruct((b, HEADS, s, QK_PAD), BF16),
                   jax.ShapeDtypeStruct((b, HEADS, V_DIM, s), BF16)],
        compiler_params=pltpu.CompilerParams(dimension_semantics=("arbitrary", "arbitrary"),
                                             vmem_limit_bytes=VMEM_LIMIT),
        name="mixer_prep",
    )(x, pos, *consts)


def _attn_kernel(q_ref, k_ref, vt_ref, ma_ref, gb_ref, *refs, tq):
    wg_ref, wu_ref, wd_ref, o_ref, w3_ref = refs
    w3_ref[:, 0] = wg_ref[...].astype(BF16)
    w3_ref[:, 1] = wu_ref[...].astype(BF16)
    for e in range(wd_ref.shape[0]):
        w3_ref[e, 2] = wd_ref[e].T.astype(BF16)
    s = q_ref.shape[2]
    key = lax.broadcasted_iota(I32, (tq, tq), 0)
    qry = lax.broadcasted_iota(I32, (tq, tq), 1)
    diag_mask = key <= qry
    heads = range(q_ref.shape[1])
    n_tiles = s // tq

    def score_dots(qi):
        return [lax.dot_general(k_ref[0, h, :(qi + 1) * tq, :], q_ref[0, h, qi * tq:(qi + 1) * tq, :],
                                (((1,), (1,)), ((), ())), preferred_element_type=F32) for h in heads]

    scores = score_dots(0)
    for qi in range(n_tiles):
        qs = slice(qi * tq, (qi + 1) * tq)
        n_keys = (qi + 1) * tq
        next_scores = score_dots(qi + 1) if qi + 1 < n_tiles else None
        probs, sums = [], []
        for h in heads:
            sc = scores[h]
            last = jnp.where(diag_mask, sc[n_keys - tq:], -jnp.inf)
            sc = last if qi == 0 else jnp.concatenate([sc[:n_keys - tq], last], axis=0)
            p = jnp.exp(sc - jnp.max(sc, axis=0, keepdims=True))
            sums.append(jnp.sum(p, axis=0, keepdims=True))
            probs.append(p.astype(BF16))
        for h in heads:
            acc = _dot(vt_ref[0, h, :, :n_keys], probs[h])
            cs = slice(h * V_DIM, (h + 1) * V_DIM)
            o = (acc / sums[h]).T
            o_ref[0, qs, cs] = (ma_ref[0, qs, cs].astype(F32) + gb_ref[0, qs, cs].astype(F32) * o).astype(BF16)
        scores = next_scores


def _attention(q, k, vt, ma, gb, expert_weights, tq, heads_per_step):
    b, h, s, _ = q.shape
    hp = heads_per_step
    n_steps = b * (h // hp)
    w_specs = [pl.BlockSpec((w.shape[0] // n_steps,) + w.shape[1:], lambda i, j: (i * (h // hp) + j, 0, 0))
               for w in expert_weights]
    col_spec = pl.BlockSpec((1, s, hp * V_DIM), lambda i, j: (i, 0, j))
    n_e, d, ff = expert_weights[0].shape
    w3_spec = pl.BlockSpec((n_e // n_steps, 3, d, ff), lambda i, j: (i * (h // hp) + j, 0, 0, 0))
    return pl.pallas_call(
        functools.partial(_attn_kernel, tq=tq),
        grid=(b, h // hp),
        in_specs=[pl.BlockSpec((1, hp, s, QK_PAD), lambda i, j: (i, j, 0, 0)),
                  pl.BlockSpec((1, hp, s, QK_PAD), lambda i, j: (i, j, 0, 0)),
                  pl.BlockSpec((1, hp, V_DIM, s), lambda i, j: (i, j, 0, 0)), col_spec, col_spec] + w_specs,
        out_specs=[col_spec, w3_spec],
        out_shape=[jax.ShapeDtypeStruct((b, s, h * V_DIM), BF16), jax.ShapeDtypeStruct((n_e, 3, d, ff), BF16)],
        compiler_params=pltpu.CompilerParams(dimension_semantics=("arbitrary", "arbitrary"),
                                             vmem_limit_bytes=VMEM_LIMIT),
        name="mla_attention",
    )(q, k, vt, ma, gb, *expert_weights)


def _route(sel, scores):
    e, tm = sel.shape
    sel3 = sel.reshape(N_GROUPS, EXPERTS_PER_GROUP, tm)
    sub = lax.broadcasted_iota(I32, sel3.shape, 1)
    m1 = jnp.max(sel3, axis=1, keepdims=True)
    first = jnp.min(jnp.where(sel3 == m1, sub, EXPERTS_PER_GROUP), axis=1, keepdims=True)
    m2 = jnp.max(jnp.where(sub == first, -jnp.inf, sel3), axis=1, keepdims=True)
    gscore = (m1 + m2).reshape(N_GROUPS, tm)
    gid = lax.broadcasted_iota(I32, (N_GROUPS, tm), 0)
    grank = jnp.zeros((N_GROUPS, tm), I32)
    for g in range(N_GROUPS):
        other = gscore[g:g + 1]
        grank += ((other > gscore) | ((other == gscore) & (g < gid))).astype(I32)
    gmask = grank < TOPK_GROUPS
    emask = jnp.broadcast_to(gmask[:, None, :], sel3.shape).reshape(e, tm)
    remaining = jnp.where(emask, sel, -jnp.inf)
    eid = lax.broadcasted_iota(I32, (e, tm), 0)
    chosen = jnp.zeros((e, tm), jnp.bool_)
    for _ in range(TOP_K):
        best = jnp.max(remaining, axis=0, keepdims=True)
        pick = eid == jnp.min(jnp.where(remaining == best, eid, e), axis=0, keepdims=True)
        chosen = chosen | pick
        remaining = jnp.where(pick, -jnp.inf, remaining)
    w = jnp.where(chosen, scores, 0.0)
    return chosen, w / jnp.sum(w, axis=0, keepdims=True) * ROUTED_SCALE


def _post_attn_kernel(x_ref, merged_ref, wo_ref, gffn_ref, wrt_ref, br_ref,
                      x1_ref, hp_ref, comb_ref, rank_ref, cnt_ref, run_ref):
    tm = x_ref.shape[0]

    @pl.when(pl.program_id(0) == 0)
    def _():
        run_ref[...] = jnp.zeros_like(run_ref)

    x1 = x_ref[...] + _dot(merged_ref[...], wo_ref[...])
    x1_ref[...] = x1
    h2 = _rms(x1, gffn_ref[...])
    hp_ref[...] = _pack_rows(h2)
    logits_t = lax.dot_general(wrt_ref[...], h2, (((1,), (1,)), ((), ())),
                               preferred_element_type=F32, precision=lax.Precision.HIGHEST)
    scores = jax.nn.sigmoid(logits_t)
    chosen, comb_t = _route(scores + br_ref[...], scores)
    comb_ref[...] = comb_t
    a = lax.broadcasted_iota(I32, (tm, tm), 0)
    b = lax.broadcasted_iota(I32, (tm, tm), 1)
    before = (a < b).astype(BF16)
    chosen_f = chosen.astype(F32)
    prefix = _dot(chosen_f.astype(BF16), before)
    run = run_ref[:, 0:1]
    rank_ref[...] = jnp.where(chosen, prefix + run, -1.0)
    run_ref[...] += jnp.sum(chosen_f, axis=1, keepdims=True)
    cnt_ref[...] = run_ref[...]


def _post_attn(x, merged, consts, tm, chunk):
    t, d = x.shape[0] // MOE_CHUNKS, x.shape[1]
    first = chunk * (t // tm)
    row_spec = lambda w: pl.BlockSpec((tm, w), lambda i: (i, 0))
    col_spec = pl.BlockSpec((N_EXPERTS, tm), lambda i: (0, i))
    return pl.pallas_call(
        _post_attn_kernel,
        grid=(t // tm,),
        in_specs=[pl.BlockSpec((tm, d), lambda i: (first + i, 0))] * 2 + [_const_spec(c.shape) for c in consts],
        out_specs=[row_spec(d), row_spec(HALF), col_spec, col_spec, _const_spec((N_EXPERTS, LANES))],
        out_shape=[jax.ShapeDtypeStruct((t, d), F32), jax.ShapeDtypeStruct((t, HALF), U32),
                   jax.ShapeDtypeStruct((N_EXPERTS, t), F32), jax.ShapeDtypeStruct((N_EXPERTS, t), F32),
                   jax.ShapeDtypeStruct((N_EXPERTS, LANES), F32)],
        scratch_shapes=[pltpu.VMEM((N_EXPERTS, LANES), F32)],
        compiler_params=pltpu.CompilerParams(dimension_semantics=("arbitrary",), vmem_limit_bytes=VMEM_LIMIT),
        name="post_attn_router",
    )(x, merged, *consts)


def _route_lists_kernel(comb_ref, rank_ref, off_ref, lay_ref, pos_ref, w_ref):
    rank = rank_ref[...]
    chosen = rank >= 0.0
    e = rank.shape[0]
    lower = (lax.broadcasted_iota(I32, (e, e), 1) < lax.broadcasted_iota(I32, (e, e), 0)).astype(BF16)
    slot = _dot(lower, chosen.astype(BF16))
    pos_sorted = rank + off_ref[...]
    tile = jnp.floor(pos_sorted * (1.0 / ROW_TILE))
    pos_full = pos_sorted + jnp.floor((tile + 0.5) / lay_ref[0:1, 0:1]) * lay_ref[0:1, 1:2]
    comb = comb_ref[...]
    pos_rows, w_rows = [], []
    for k in range(TOP_K):
        pick = chosen & (slot == float(k))
        pos_rows.append(jnp.sum(jnp.where(pick, pos_full, 0.0), axis=0, keepdims=True))
        w_rows.append(jnp.sum(jnp.where(pick, comb, 0.0), axis=0, keepdims=True))
    pos_ref[...] = jnp.concatenate(pos_rows, axis=0).astype(I32)
    w_ref[...] = jnp.concatenate(w_rows, axis=0).T


def _route_lists(comb_t, rank_t, offsets, layout, tm):
    e, t = comb_t.shape
    col_spec = pl.BlockSpec((e, tm), lambda i: (0, i))
    return pl.pallas_call(
        _route_lists_kernel,
        grid=(t // tm,),
        in_specs=[col_spec, col_spec, _const_spec((e, 1)), _const_spec(layout.shape)],
        out_specs=[pl.BlockSpec((TOP_K, tm), lambda i: (0, i)), pl.BlockSpec((tm, TOP_K), lambda i: (i, 0))],
        out_shape=[jax.ShapeDtypeStruct((TOP_K, t), I32), jax.ShapeDtypeStruct((t, TOP_K), F32)],
        compiler_params=pltpu.CompilerParams(dimension_semantics=("arbitrary",)),
        name="route_lists",
    )(comb_t, rank_t, offsets, layout)


def _sc_mesh():
    return plsc.VectorSubcoreMesh(core_axis_name="c", subcore_axis_name="s",
                                  num_cores=SC_CORES, num_subcores=SC_SUBCORES)


def _sc_worker():
    return lax.axis_index("s") * SC_CORES + lax.axis_index("c")


def _sc_scatter_rows(src, pos_win, n_rows):
    t, width = src.shape
    n_win = t // SC_WINDOW
    per_w = n_win // (SC_CORES * SC_SUBCORES)

    @functools.partial(
        pl.kernel, mesh=_sc_mesh(), out_type=jax.ShapeDtypeStruct((n_rows, width), src.dtype),
        scratch_types=[pltpu.VMEM((TOP_K, SC_WINDOW), I32), pltpu.VMEM((SC_WINDOW, width), src.dtype),
                       pltpu.SemaphoreType.DMA],
        name="sc_scatter_rows")
    def run(src_hbm, pos_hbm, out_hbm, idx_v, rows_v, sem):
        base = _sc_worker() * per_w

        @pl.loop(0, per_w)
        def _(j):
            w = base + j
            pltpu.sync_copy(pos_hbm.at[w], idx_v)
            pltpu.sync_copy(src_hbm.at[pl.ds(w * SC_WINDOW, SC_WINDOW)], rows_v)
            copies = [pltpu.async_copy(rows_v, out_hbm.at[idx_v.at[k]], sem) for k in range(TOP_K)]
            for c in copies:
                c.wait()

    return run(src, pos_win)


def _sc_gather_rows(table, pos_win):
    n_win = pos_win.shape[0]
    width = table.shape[1]
    per_w = n_win // (SC_CORES * SC_SUBCORES)

    @functools.partial(
        pl.kernel, mesh=_sc_mesh(),
        out_type=jax.ShapeDtypeStruct((TOP_K, n_win * SC_WINDOW, width), table.dtype),
        scratch_types=[pltpu.VMEM((TOP_K, SC_WINDOW), I32), pltpu.VMEM((2, SC_WINDOW, width), table.dtype),
                       pltpu.SemaphoreType.DMA, pltpu.SemaphoreType.DMA],
        name="sc_gather_rows")
    def run(table_hbm, pos_hbm, out_hbm, idx_v, rows_v, gsem, wsem):
        base = _sc_worker() * per_w

        @pl.loop(0, per_w)
        def _(j):
            w = base + j
            pltpu.sync_copy(pos_hbm.at[w], idx_v)
            for k in range(TOP_K):
                buf = rows_v.at[k % 2]
                pltpu.async_copy(table_hbm.at[idx_v.at[k]], buf, gsem).wait()
                pltpu.async_copy(buf, out_hbm.at[k, pl.ds(w * SC_WINDOW, SC_WINDOW)], wsem).wait()

    return run(table, pos_win)


def _grouped_ffn_kernel(te_ref, used_ref, xs_ref, *refs):
    w_refs, ys_ref = refs[:-1], refs[-1]

    @pl.when(pl.program_id(0) < used_ref[0])
    def _():
        def gate_up(j):
            w_ref = w_refs[j]
            lo, hi = _unpack_rows(xs_ref[j])
            lo, hi = lo.astype(BF16), hi.astype(BF16)
            return (_dot(lo, w_ref[0, 0, :HALF]) + _dot(hi, w_ref[0, 0, HALF:]),
                    _dot(lo, w_ref[0, 1, :HALF]) + _dot(hi, w_ref[0, 1, HALF:]))

        pre = gate_up(0)
        for j in range(TILES_PER_STEP):
            nxt = gate_up(j + 1) if j + 1 < TILES_PER_STEP else None
            act = (jax.nn.silu(pre[0]) * pre[1]).astype(BF16)
            ys_ref[j] = _pack_rows(_dot_t(act, w_refs[j][0, 2]))
            pre = nxt


def _grouped_ffn(tile_expert, steps_used, xs, w3):
    n_rows, half = xs.shape
    n_steps = n_rows // (TILES_PER_STEP * ROW_TILE)

    def step_of(i, used):
        return jnp.minimum(i, used[0] - 1)

    row_spec = pl.BlockSpec((TILES_PER_STEP, ROW_TILE, half), lambda i, te, used: (0, step_of(i, used), 0))

    def exp_map(j):
        return lambda i, te, used: (te[j * n_steps + step_of(i, used)], 0, 0, 0)

    w_specs = [pl.BlockSpec((1,) + w3.shape[1:], exp_map(j)) for j in range(TILES_PER_STEP)]
    ys = pl.pallas_call(
        _grouped_ffn_kernel,
        grid_spec=pltpu.PrefetchScalarGridSpec(
            num_scalar_prefetch=2,
            grid=(n_steps,),
            in_specs=[row_spec] + w_specs,
            out_specs=row_spec),
        out_shape=jax.ShapeDtypeStruct((TILES_PER_STEP, n_steps * ROW_TILE, half), U32),
        compiler_params=pltpu.CompilerParams(dimension_semantics=("arbitrary",), vmem_limit_bytes=VMEM_LIMIT),
        name="grouped_ffn",
    )(tile_expert, steps_used, xs.reshape(TILES_PER_STEP, n_steps * ROW_TILE, half),
      *([w3] * TILES_PER_STEP))
    return ys.reshape(n_rows, half)


def _combine_kernel(yt_ref, w_ref, x1_ref, p_ref, gffn_ref, wsg_ref, wsu_ref, wsd_ref,
                    gpi_ref, wpg_ref, wpp_ref, gpo_ref, *rest):
    out_ref = rest[-1]
    x1 = x1_ref[...]
    w = w_ref[...]
    h = _rms(x1, gffn_ref[...]).astype(BF16)
    shared = _dot((jax.nn.silu(_dot(h, wsg_ref[...])) * _dot(h, wsu_ref[...])).astype(BF16), wsd_ref[...])
    proj = _rms(_dot(p_ref[...].astype(BF16), wpp_ref[...]), gpo_ref[...])
    acc_lo = jnp.zeros((x1.shape[0], HALF), F32)
    acc_hi = jnp.zeros((x1.shape[0], HALF), F32)
    for k in range(TOP_K):
        lo, hi = _unpack_rows(yt_ref[k])
        wk = w[:, k:k + 1]
        acc_lo += wk * lo
        acc_hi += wk * hi
    x2 = x1 + shared + jnp.concatenate([acc_lo, acc_hi], axis=-1)
    gate = jax.nn.sigmoid(_dot(_rms(x2, gpi_ref[...]).astype(BF16), wpg_ref[...]))
    out_ref[...] = x2 + gate * proj


def _combine(yt, w_tok, x1, p, consts, tm, chunk, out_so_far):
    t, d = x1.shape
    first = chunk * (t // tm)
    row_spec = lambda w: pl.BlockSpec((tm, w), lambda i: (i, 0))
    full_spec = lambda w: pl.BlockSpec((tm, w), lambda i: (first + i, 0))
    carried = [] if out_so_far is None else [out_so_far]
    return pl.pallas_call(
        _combine_kernel,
        grid=(t // tm,),
        in_specs=[pl.BlockSpec((TOP_K, tm, HALF), lambda i: (0, i, 0)), row_spec(TOP_K), row_spec(d),
                  full_spec(PLE_DIM)] + [_const_spec(c.shape) for c in consts]
                 + [pl.BlockSpec(memory_space=pl.ANY)] * len(carried),
        out_specs=full_spec(d),
        out_shape=jax.ShapeDtypeStruct((t * MOE_CHUNKS, d), F32),
        input_output_aliases={4 + len(consts): 0} if carried else {},
        compiler_params=pltpu.CompilerParams(dimension_semantics=("arbitrary",), vmem_limit_bytes=VMEM_LIMIT),
        name="combine_ple",
    )(yt, w_tok, x1, p, *consts, *carried)


def _rotate_half(a):
    half = QK_ROPE // 2
    return jnp.concatenate([-a[..., half:], a[..., :half]], axis=-1)


def _with_rotate_half(w):
    return jnp.concatenate([w, _rotate_half(w[..., -QK_ROPE:])], axis=-1)


def _rope_gain_table(g, scale):
    half = QK_ROPE // 2
    return (jnp.concatenate([g, g[half:], g[:half]]) * scale).reshape(1, LANES)


def _layer(x, p, pos, g_mix, w_in, sgu_ln_g, sgu_ln_b, sgu_w, sgu_b, mla_g_qa, mla_w_qb, mla_g_kva, mla_w_kvb,
           qk_g_q_nope, qk_g_k_nope, qk_g_q_rope, qk_g_k_rope, w_o, g_ffn, w_router, b_router,
           w_exp_gate, w_exp_up, w_exp_down, w_sh_gate, w_sh_up, w_sh_down,
           g_ple_in, w_ple_gate, w_ple_proj, g_ple_out):
    b, s, d = x.shape
    t = b * s
    row = lambda a: a.reshape(1, -1)
    sizes = [d, d, Q_LORA, KV_LORA, QK_ROPE, d, d]
    offs = [0]
    for sz in sizes:
        offs.append(offs[-1] + sz)
    w_in_t = w_in.T
    w_r_t = _with_rotate_half(w_in[:, offs[4]:offs[5]]).T
    wqb = _with_rotate_half(mla_w_qb.reshape(Q_LORA, HEADS, QK_DIM)).reshape(Q_LORA, HEADS * QK_PAD)
    wkvb = mla_w_kvb.reshape(KV_LORA, HEADS, QK_NOPE + V_DIM)
    wkb = wkvb[:, :, :QK_NOPE].reshape(KV_LORA, HEADS * QK_NOPE)
    wvbt = wkvb[:, :, QK_NOPE:].reshape(KV_LORA, HEADS * V_DIM).T
    sgu_bias = jnp.repeat(sgu_b.T, d // SGU_GROUPS, axis=1)
    q_scale = QK_DIM ** -0.5

    consts1 = [row(g_mix), w_in_t, w_r_t, row(sgu_ln_g), row(sgu_ln_b), sgu_w, sgu_bias,
               row(mla_g_qa), wqb, row(mla_g_kva), wkb, wvbt, row(qk_g_q_nope) * q_scale, row(qk_g_k_nope),
               _rope_gain_table(qk_g_q_rope, q_scale), _rope_gain_table(qk_g_k_rope, 1.0)]
    pos_lanes = jnp.broadcast_to(pos.astype(F32)[..., None], (b, s, LANES))
    ma, gb, q, k, vt = _mixer_prep(x, pos_lanes, consts1, tm=512)
    merged, w_experts = _attention(q, k, vt, ma, gb, (w_exp_gate, w_exp_up, w_exp_down),
                                                    tq=256, heads_per_step=2)

    consts3 = [w_o, row(g_ffn), w_router.T, b_router.reshape(-1, 1)]
    consts6 = [row(g_ffn), w_sh_gate, w_sh_up, w_sh_down, row(g_ple_in), w_ple_gate, w_ple_proj, row(g_ple_out)]
    x2d, merged2d, p2d = (a.reshape(t, -1) for a in (x, merged, p))
    tc = t // MOE_CHUNKS
    max_tiles = (tc * TOP_K) // ROW_TILE + N_EXPERTS
    tile_ids = jnp.arange(max_tiles, dtype=I32)
    n_steps = max_tiles // TILES_PER_STEP
    out = None
    for chunk in range(MOE_CHUNKS):
        x1, h_packed, comb_t, rank_t, counts = _post_attn(x2d, merged2d, consts3, tm=1024, chunk=chunk)

        counts = counts[:, 0].astype(I32)
        tiles_per_expert = (counts + ROW_TILE - 1) // ROW_TILE
        tile_end = jnp.cumsum(tiles_per_expert)
        offsets = ((tile_end - tiles_per_expert) * ROW_TILE).astype(F32).reshape(N_EXPERTS, 1)
        steps_used = (tile_end[-1:] + TILES_PER_STEP - 1) // TILES_PER_STEP
        sorted_tile = (tile_ids // n_steps) * steps_used + jnp.minimum(tile_ids % n_steps, steps_used - 1)
        tile_expert = jnp.minimum(jnp.sum((tile_end[None, :] <= sorted_tile[:, None]).astype(I32), axis=1),
                                  N_EXPERTS - 1)
        layout = jnp.concatenate([steps_used, (n_steps - steps_used) * ROW_TILE]).astype(F32).reshape(1, 2)

        pos_t, w_tok = _route_lists(comb_t, rank_t, offsets, layout, tm=2048)
        pos_win = pos_t.reshape(TOP_K, tc // SC_WINDOW, SC_WINDOW).transpose(1, 0, 2)

        xs = _sc_scatter_rows(h_packed, pos_win, max_tiles * ROW_TILE)
        ys = _grouped_ffn(tile_expert, steps_used, xs, w_experts)
        yt = _sc_gather_rows(ys, pos_win)
        out = _combine(yt, w_tok, x1, p2d, consts6, tm=512, chunk=chunk, out_so_far=out)
    return out.reshape(b, s, d)


def kernel(x, p, positions, g_mix, w_in, sgu_ln_g, sgu_ln_b, sgu_w, sgu_b, mla_g_qa, mla_w_qb, mla_g_kva, mla_w_kvb, qk_g_q_nope, qk_g_k_nope, qk_g_q_rope, qk_g_k_rope, w_o, g_ffn, w_router, b_router, w_exp_gate, w_exp_up, w_exp_down, w_sh_gate, w_sh_up, w_sh_down, g_ple_in, w_ple_gate, w_ple_proj, g_ple_out):
    params = (g_mix, w_in, sgu_ln_g, sgu_ln_b, sgu_w, sgu_b, mla_g_qa, mla_w_qb, mla_g_kva, mla_w_kvb,
              qk_g_q_nope, qk_g_k_nope, qk_g_q_rope, qk_g_k_rope, w_o, g_ffn, w_router, b_router,
              w_exp_gate, w_exp_up, w_exp_down, w_sh_gate, w_sh_up, w_sh_down,
              g_ple_in, w_ple_gate, w_ple_proj, g_ple_out)
    for l in range(g_mix.shape[0]):
        x = _layer(x, p[l], positions, *[a[l] for a in params])
    return x
```

```python
import functools
import math

import jax
import jax.numpy as jnp
from jax import lax
from jax.experimental import pallas as pl
from jax.experimental.pallas import tpu as pltpu
from jax.experimental.pallas import tpu_sc as plsc

D_MODEL = 1024
PLE_DIM = 256
SGU_CHUNK = 128
SGU_GROUPS = 8
V_DIM = 128
HEADS = 8
QK_NOPE = 128
QK_ROPE = 64
QK_DIM = QK_NOPE + QK_ROPE
QK_PAD = 256
Q_LORA = 384
KV_LORA = 256
ROPE_THETA = 10000.0
N_EXPERTS = 64
N_GROUPS = 8
EXPERTS_PER_GROUP = 8
TOPK_GROUPS = 4
TOP_K = 8
EXPERT_FF = 256
ROUTED_SCALE = 2.5
NORM_EPS = 1e-6
LN_EPS = 1e-5

LANES = 128
MXU_COLS = 256
VMEM_LIMIT = 56 * 1024 * 1024
SC_CORES = 2
SC_SUBCORES = 16
SC_WINDOW = 64
ROW_TILE = 256
TILES_PER_STEP = 8
MOE_CHUNKS = 2
HALF = D_MODEL // 2

F32 = jnp.float32
BF16 = jnp.bfloat16
U32 = jnp.uint32
I32 = jnp.int32


def _dot(a, b):
    return lax.dot_general(a, b, (((1,), (0,)), ((), ())), preferred_element_type=F32)


def _dot_t(a, bt):
    return lax.dot_general(a, bt, (((1,), (1,)), ((), ())), preferred_element_type=F32)


def _rms(xf, g):
    ms = jnp.sum(xf * xf, axis=-1, keepdims=True) * (1.0 / xf.shape[-1])
    return xf * lax.rsqrt(ms + NORM_EPS) * g


def _pack_rows(y):
    lo = pltpu.bitcast(y[:, :HALF].astype(BF16).astype(F32), U32) >> 16
    hi = pltpu.bitcast(y[:, HALF:].astype(BF16).astype(F32), U32) & jnp.uint32(0xFFFF0000)
    return lo | hi


def _unpack_rows(w):
    lo = pltpu.bitcast(w << 16, F32)
    hi = pltpu.bitcast(w & jnp.uint32(0xFFFF0000), F32)
    return lo, hi


def _rope_table(pos_f):
    lane = lax.broadcasted_iota(I32, (1, LANES), 1)
    freq = (lane % (QK_ROPE // 2)).astype(F32)
    inv_freq = jnp.exp(freq * (-math.log(ROPE_THETA) * 2.0 / QK_ROPE))
    phase = jnp.where(lane < QK_ROPE, 0.0, math.pi / 2)
    return jnp.cos(pos_f * inv_freq - phase)


def _norm_rope(piece, table_g):
    lane = lax.broadcasted_iota(I32, (1, LANES), 1)
    ssq = jnp.sum(jnp.where(lane < QK_ROPE, piece * piece, 0.0), axis=-1, keepdims=True)
    z = piece * lax.rsqrt(ssq * (1.0 / QK_ROPE) + NORM_EPS) * table_g
    return z + pltpu.roll(z, QK_ROPE, axis=1)


def _const_spec(shape):
    return pl.BlockSpec(shape, lambda *_: (0,) * len(shape), pipeline_mode=pl.Buffered(1))


def _mixer_prep_kernel(x_ref, pos_ref, g_mix_ref, wt_ref, wrt_ref,
                       lng_ref, lnb_ref, sw_ref, sb_ref, gqa_ref, wqb_ref, gkva_ref, wkb_ref, wvbt_ref,
                       gqn_ref, gkn_ref, gqr_ref, gkr_ref,
                       ma_ref, gb_ref, q_ref, k_ref, vt_ref):
    tm = x_ref.shape[1]
    d = x_ref.shape[2]
    v0, q0, kv0, ga0 = d, 2 * d, 2 * d + Q_LORA, 2 * d + Q_LORA + KV_LORA + QK_ROPE
    gb0 = ga0 + d
    xn = _rms(x_ref[0], g_mix_ref[...]).astype(BF16)

    rope_t = _rope_table(pos_ref[0])
    qn = _rms(_dot_t(xn, wt_ref[q0:q0 + Q_LORA]), gqa_ref[...]).astype(BF16)
    kvn = _rms(_dot_t(xn, wt_ref[kv0:kv0 + KV_LORA]), gkva_ref[...]).astype(BF16)
    lane = lax.broadcasted_iota(I32, (1, LANES), 1)
    kpe = jnp.where(lane < QK_ROPE, _norm_rope(_dot_t(xn, wrt_ref[...]), rope_t * gkr_ref[...]), 0.0).astype(BF16)
    q_rope_t = rope_t * gqr_ref[...]

    def head_pair(pair):
        ps = slice(pair * MXU_COLS, (pair + 1) * MXU_COLS)
        k2 = _dot(kvn, wkb_ref[:, ps])
        v2t = lax.dot_general(wvbt_ref[ps, :].astype(BF16), kvn, (((1,), (1,)), ((), ())),
                              preferred_element_type=F32).astype(BF16)
        for half in range(2):
            g = 2 * pair + half
            hs = slice(half * LANES, (half + 1) * LANES)
            qh = _dot(qn, wqb_ref[:, g * QK_PAD:(g + 1) * QK_PAD])
            q_ref[0, g, :, :QK_NOPE] = _rms(qh[:, :QK_NOPE], gqn_ref[...]).astype(BF16)
            q_ref[0, g, :, QK_NOPE:] = _norm_rope(qh[:, QK_NOPE:], q_rope_t).astype(BF16)
            k_ref[0, g, :, :QK_NOPE] = _rms(k2[:, hs], gkn_ref[...]).astype(BF16)
            k_ref[0, g, :, QK_NOPE:] = kpe
            vt_ref[0, g] = v2t[hs, :]

    head_pair(0)

    gv = jax.nn.gelu(_dot_t(xn, wt_ref[v0:v0 + d]))
    mu = jnp.mean(gv, axis=-1, keepdims=True)
    vc = gv - mu
    var = jnp.mean(vc * vc, axis=-1, keepdims=True)
    vn = (vc * lax.rsqrt(var + LN_EPS) * lng_ref[...] + lnb_ref[...]).astype(BF16)
    row = lax.broadcasted_iota(I32, (SGU_CHUNK, SGU_CHUNK), 0)
    col = lax.broadcasted_iota(I32, (SGU_CHUNK, SGU_CHUNK), 1)
    causal = col <= row
    n_chunks = tm // SGU_CHUNK

    def sgu_pair(pair):
        ps = slice(pair * MXU_COLS, (pair + 1) * MXU_COLS)
        p0 = pair * MXU_COLS
        gu2 = jax.nn.gelu(_dot_t(xn, wt_ref[p0:p0 + MXU_COLS]))
        ga2 = jax.nn.sigmoid(_dot_t(xn, wt_ref[ga0 + p0:ga0 + p0 + MXU_COLS]))
        gb_ref[0, :, ps] = jax.nn.sigmoid(_dot_t(xn, wt_ref[gb0 + p0:gb0 + p0 + MXU_COLS])).astype(BF16)
        for half in range(2):
            g = 2 * pair + half
            hs = slice(half * LANES, (half + 1) * LANES)
            cs = slice(g * SGU_CHUNK, (g + 1) * SGU_CHUNK)
            wg = jnp.where(causal, sw_ref[g], 0.0).astype(BF16)
            vcat = jnp.concatenate([vn[c * SGU_CHUNK:(c + 1) * SGU_CHUNK, cs] for c in range(n_chunks)], axis=1)
            mixed = _dot(wg, vcat)
            for c in range(n_chunks):
                rs = slice(c * SGU_CHUNK, (c + 1) * SGU_CHUNK)
                m = mixed[:, c * SGU_CHUNK:(c + 1) * SGU_CHUNK] + sb_ref[:, cs]
                ma_ref[0, rs, cs] = (ga2[rs, hs] * gu2[rs, hs] * m).astype(BF16)

    n_pairs = SGU_GROUPS // 2
    for pair in range(n_pairs):
        if pair + 1 < n_pairs:
            head_pair(pair + 1)
        sgu_pair(pair)


def _mixer_prep(x, pos, consts, tm):
    b, s, d = x.shape
    grid = (b, s // tm)
    row_spec = lambda w: pl.BlockSpec((1, tm, w), lambda i, j: (i, j, 0))
    head_spec = lambda w: pl.BlockSpec((1, HEADS, tm, w), lambda i, j: (i, 0, j, 0))
    return pl.pallas_call(
        _mixer_prep_kernel,
        grid=grid,
        in_specs=[row_spec(d), row_spec(LANES)] + [_const_spec(c.shape) for c in consts],
        out_specs=[row_spec(d), row_spec(d), head_spec(QK_PAD), head_spec(QK_PAD),
                   pl.BlockSpec((1, HEADS, V_DIM, tm), lambda i, j: (i, 0, 0, j))],
        out_shape=[jax.ShapeDtypeStruct((b, s, d), BF16), jax.ShapeDtypeStruct((b, s, d), BF16),
                   jax.ShapeDtypeStruct((b, HEADS, s, QK_PAD), BF16),
                   jax.ShapeDtypeStruct((b, HEADS, s, QK_PAD), BF16),
                   jax.ShapeDtypeStruct((b, HEADS, V_DIM, s), BF16)],
        compiler_params=pltpu.CompilerParams(dimension_semantics=("arbitrary", "arbitrary"),
                                             vmem_limit_bytes=VMEM_LIMIT),
        name="mixer_prep",
    )(x, pos, *consts)


def _attn_kernel(q_ref, k_ref, vt_ref, ma_ref, gb_ref, *refs, tq):
    n_w = (len(refs) - 1) // 2
    o_ref = refs[n_w]
    for w_ref, wb_ref in zip(refs[:n_w], refs[n_w + 1:]):
        wb_ref[...] = w_ref[...].astype(BF16)
    s = q_ref.shape[2]
    key = lax.broadcasted_iota(I32, (tq, tq), 0)
    qry = lax.broadcasted_iota(I32, (tq, tq), 1)
    diag_mask = key <= qry
    heads = range(q_ref.shape[1])
    n_tiles = s // tq

    def score_dots(qi):
        return [lax.dot_general(k_ref[0, h, :(qi + 1) * tq, :], q_ref[0, h, qi * tq:(qi + 1) * tq, :],
                                (((1,), (1,)), ((), ())), preferred_element_type=F32) for h in heads]

    scores = score_dots(0)
    for qi in range(n_tiles):
        qs = slice(qi * tq, (qi + 1) * tq)
        n_keys = (qi + 1) * tq
        next_scores = score_dots(qi + 1) if qi + 1 < n_tiles else None
        probs, sums = [], []
        for h in heads:
            sc = scores[h]
            last = jnp.where(diag_mask, sc[n_keys - tq:], -jnp.inf)
            sc = last if qi == 0 else jnp.concatenate([sc[:n_keys - tq], last], axis=0)
            p = jnp.exp(sc - jnp.max(sc, axis=0, keepdims=True))
            sums.append(jnp.sum(p, axis=0, keepdims=True))
            probs.append(p.astype(BF16))
        for h in heads:
            acc = _dot(vt_ref[0, h, :, :n_keys], probs[h])
            cs = slice(h * V_DIM, (h + 1) * V_DIM)
            o = (acc / sums[h]).T
            o_ref[0, qs, cs] = (ma_ref[0, qs, cs].astype(F32) + gb_ref[0, qs, cs].astype(F32) * o).astype(BF16)
        scores = next_scores


def _attention(q, k, vt, ma, gb, expert_weights, tq, heads_per_step):
    b, h, s, _ = q.shape
    hp = heads_per_step
    n_steps = b * (h // hp)
    w_specs = [pl.BlockSpec((w.shape[0] // n_steps,) + w.shape[1:], lambda i, j: (i * (h // hp) + j, 0, 0))
               for w in expert_weights]
    col_spec = pl.BlockSpec((1, s, hp * V_DIM), lambda i, j: (i, 0, j))
    return pl.pallas_call(
        functools.partial(_attn_kernel, tq=tq),
        grid=(b, h // hp),
        in_specs=[pl.BlockSpec((1, hp, s, QK_PAD), lambda i, j: (i, j, 0, 0)),
                  pl.BlockSpec((1, hp, s, QK_PAD), lambda i, j: (i, j, 0, 0)),
                  pl.BlockSpec((1, hp, V_DIM, s), lambda i, j: (i, j, 0, 0)), col_spec, col_spec] + w_specs,
        out_specs=[col_spec] + w_specs,
        out_shape=[jax.ShapeDtypeStruct((b, s, h * V_DIM), BF16)]
                  + [jax.ShapeDtypeStruct(w.shape, BF16) for w in expert_weights],
        compiler_params=pltpu.CompilerParams(dimension_semantics=("arbitrary", "arbitrary"),
                                             vmem_limit_bytes=VMEM_LIMIT),
        name="mla_attention",
    )(q, k, vt, ma, gb, *expert_weights)


def _route(sel, scores):
    e, tm = sel.shape
    sel3 = sel.reshape(N_GROUPS, EXPERTS_PER_GROUP, tm)
    sub = lax.broadcasted_iota(I32, sel3.shape, 1)
    m1 = jnp.max(sel3, axis=1, keepdims=True)
    first = jnp.min(jnp.where(sel3 == m1, sub, EXPERTS_PER_GROUP), axis=1, keepdims=True)
    m2 = jnp.max(jnp.where(sub == first, -jnp.inf, sel3), axis=1, keepdims=True)
    gscore = (m1 + m2).reshape(N_GROUPS, tm)
    gid = lax.broadcasted_iota(I32, (N_GROUPS, tm), 0)
    grank = jnp.zeros((N_GROUPS, tm), I32)
    for g in range(N_GROUPS):
        other = gscore[g:g + 1]
        grank += ((other > gscore) | ((other == gscore) & (g < gid))).astype(I32)
    gmask = grank < TOPK_GROUPS
    emask = jnp.broadcast_to(gmask[:, None, :], sel3.shape).reshape(e, tm)
    remaining = jnp.where(emask, sel, -jnp.inf)
    eid = lax.broadcasted_iota(I32, (e, tm), 0)
    chosen = jnp.zeros((e, tm), jnp.bool_)
    for _ in range(TOP_K):
        best = jnp.max(remaining, axis=0, keepdims=True)
        pick = eid == jnp.min(jnp.where(remaining == best, eid, e), axis=0, keepdims=True)
        chosen = chosen | pick
        remaining = jnp.where(pick, -jnp.inf, remaining)
    w = jnp.where(chosen, scores, 0.0)
    return chosen, w / jnp.sum(w, axis=0, keepdims=True) * ROUTED_SCALE


def _post_attn_kernel(x_ref, merged_ref, wo_ref, gffn_ref, wrt_ref, br_ref,
                      x1_ref, hp_ref, comb_ref, rank_ref, cnt_ref, run_ref):
    tm = x_ref.shape[0]

    @pl.when(pl.program_id(0) == 0)
    def _():
        run_ref[...] = jnp.zeros_like(run_ref)

    x1 = x_ref[...] + _dot(merged_ref[...], wo_ref[...])
    x1_ref[...] = x1
    h2 = _rms(x1, gffn_ref[...])
    hp_ref[...] = _pack_rows(h2)
    logits_t = lax.dot_general(wrt_ref[...], h2, (((1,), (1,)), ((), ())),
                               preferred_element_type=F32, precision=lax.Precision.HIGHEST)
    scores = jax.nn.sigmoid(logits_t)
    chosen, comb_t = _route(scores + br_ref[...], scores)
    comb_ref[...] = comb_t
    a = lax.broadcasted_iota(I32, (tm, tm), 0)
    b = lax.broadcasted_iota(I32, (tm, tm), 1)
    before = (a < b).astype(BF16)
    chosen_f = chosen.astype(F32)
    prefix = _dot(chosen_f.astype(BF16), before)
    run = run_ref[:, 0:1]
    rank_ref[...] = jnp.where(chosen, prefix + run, -1.0)
    run_ref[...] += jnp.sum(chosen_f, axis=1, keepdims=True)
    cnt_ref[...] = run_ref[...]


def _post_attn(x, merged, consts, tm, chunk):
    t, d = x.shape[0] // MOE_CHUNKS, x.shape[1]
    first = chunk * (t // tm)
    row_spec = lambda w: pl.BlockSpec((tm, w), lambda i: (i, 0))
    col_spec = pl.BlockSpec((N_EXPERTS, tm), lambda i: (0, i))
    return pl.pallas_call(
        _post_attn_kernel,
        grid=(t // tm,),
        in_specs=[pl.BlockSpec((tm, d), lambda i: (first + i, 0))] * 2 + [_const_spec(c.shape) for c in consts],
        out_specs=[row_spec(d), row_spec(HALF), col_spec, col_spec, _const_spec((N_EXPERTS, LANES))],
        out_shape=[jax.ShapeDtypeStruct((t, d), F32), jax.ShapeDtypeStruct((t, HALF), U32),
                   jax.ShapeDtypeStruct((N_EXPERTS, t), F32), jax.ShapeDtypeStruct((N_EXPERTS, t), F32),
                   jax.ShapeDtypeStruct((N_EXPERTS, LANES), F32)],
        scratch_shapes=[pltpu.VMEM((N_EXPERTS, LANES), F32)],
        compiler_params=pltpu.CompilerParams(dimension_semantics=("arbitrary",), vmem_limit_bytes=VMEM_LIMIT),
        name="post_attn_router",
    )(x, merged, *consts)


def _route_lists_kernel(comb_ref, rank_ref, off_ref, lay_ref, pos_ref, w_ref):
    rank = rank_ref[...]
    chosen = rank >= 0.0
    e = rank.shape[0]
    lower = (lax.broadcasted_iota(I32, (e, e), 1) < lax.broadcasted_iota(I32, (e, e), 0)).astype(BF16)
    slot = _dot(lower, chosen.astype(BF16))
    pos_sorted = rank + off_ref[...]
    tile = jnp.floor(pos_sorted * (1.0 / ROW_TILE))
    pos_full = pos_sorted + jnp.floor((tile + 0.5) / lay_ref[0:1, 0:1]) * lay_ref[0:1, 1:2]
    comb = comb_ref[...]
    pos_rows, w_rows = [], []
    for k in range(TOP_K):
        pick = chosen & (slot == float(k))
        pos_rows.append(jnp.sum(jnp.where(pick, pos_full, 0.0), axis=0, keepdims=True))
        w_rows.append(jnp.sum(jnp.where(pick, comb, 0.0), axis=0, keepdims=True))
    pos_ref[...] = jnp.concatenate(pos_rows, axis=0).astype(I32)
    w_ref[...] = jnp.concatenate(w_rows, axis=0).T


def _route_lists(comb_t, rank_t, offsets, layout, tm):
    e, t = comb_t.shape
    col_spec = pl.BlockSpec((e, tm), lambda i: (0, i))
    return pl.pallas_call(
        _route_lists_kernel,
        grid=(t // tm,),
        in_specs=[col_spec, col_spec, _const_spec((e, 1)), _const_spec(layout.shape)],
        out_specs=[pl.BlockSpec((TOP_K, tm), lambda i: (0, i)), pl.BlockSpec((tm, TOP_K), lambda i: (i, 0))],
        out_shape=[jax.ShapeDtypeStruct((TOP_K, t), I32), jax.ShapeDtypeStruct((t, TOP_K), F32)],
        compiler_params=pltpu.CompilerParams(dimension_semantics=("arbitrary",)),
        name="route_lists",
    )(comb_t, rank_t, offsets, layout)


def _sc_mesh():
    return plsc.VectorSubcoreMesh(core_axis_name="c", subcore_axis_name="s",
                                  num_cores=SC_CORES, num_subcores=SC_SUBCORES)


def _sc_worker():
    return lax.axis_index("s") * SC_CORES + lax.axis_index("c")


def _sc_scatter_rows(src, pos_win, n_rows):
    t, width = src.shape
    n_win = t // SC_WINDOW
    per_w = n_win // (SC_CORES * SC_SUBCORES)

    @functools.partial(
        pl.kernel, mesh=_sc_mesh(), out_type=jax.ShapeDtypeStruct((n_rows, width), src.dtype),
        scratch_types=[pltpu.VMEM((TOP_K, SC_WINDOW), I32), pltpu.VMEM((SC_WINDOW, width), src.dtype),
                       pltpu.SemaphoreType.DMA],
        name="sc_scatter_rows")
    def run(src_hbm, pos_hbm, out_hbm, idx_v, rows_v, sem):
        base = _sc_worker() * per_w

        @pl.loop(0, per_w)
        def _(j):
            w = base + j
            pltpu.sync_copy(pos_hbm.at[w], idx_v)
            pltpu.sync_copy(src_hbm.at[pl.ds(w * SC_WINDOW, SC_WINDOW)], rows_v)
            copies = [pltpu.async_copy(rows_v, out_hbm.at[idx_v.at[k]], sem) for k in range(TOP_K)]
            for c in copies:
                c.wait()

    return run(src, pos_win)


def _sc_gather_rows(table, pos_win):
    n_win = pos_win.shape[0]
    width = table.shape[1]
    per_w = n_win // (SC_CORES * SC_SUBCORES)

    @functools.partial(
        pl.kernel, mesh=_sc_mesh(),
        out_type=jax.ShapeDtypeStruct((TOP_K, n_win * SC_WINDOW, width), table.dtype),
        scratch_types=[pltpu.VMEM((TOP_K, SC_WINDOW), I32), pltpu.VMEM((2, SC_WINDOW, width), table.dtype),
                       pltpu.SemaphoreType.DMA, pltpu.SemaphoreType.DMA],
        name="sc_gather_rows")
    def run(table_hbm, pos_hbm, out_hbm, idx_v, rows_v, gsem, wsem):
        base = _sc_worker() * per_w

        @pl.loop(0, per_w)
        def _(j):
            w = base + j
            pltpu.sync_copy(pos_hbm.at[w], idx_v)
            for k in range(TOP_K):
                buf = rows_v.at[k % 2]
                pltpu.async_copy(table_hbm.at[idx_v.at[k]], buf, gsem).wait()
                pltpu.async_copy(buf, out_hbm.at[k, pl.ds(w * SC_WINDOW, SC_WINDOW)], wsem).wait()

    return run(table, pos_win)


def _grouped_ffn_kernel(te_ref, used_ref, xs_ref, *refs):
    w_refs, ys_ref = refs[:-1], refs[-1]

    @pl.when(pl.program_id(0) < used_ref[0])
    def _():
        def gate_up(j):
            wg_ref, wu_ref = w_refs[3 * j:3 * j + 2]
            lo, hi = _unpack_rows(xs_ref[j])
            lo, hi = lo.astype(BF16), hi.astype(BF16)
            return (_dot(lo, wg_ref[0, :HALF]) + _dot(hi, wg_ref[0, HALF:]),
                    _dot(lo, wu_ref[0, :HALF]) + _dot(hi, wu_ref[0, HALF:]))

        pre = gate_up(0)
        for j in range(TILES_PER_STEP):
            nxt = gate_up(j + 1) if j + 1 < TILES_PER_STEP else None
            act = (jax.nn.silu(pre[0]) * pre[1]).astype(BF16)
            ys_ref[j] = _pack_rows(_dot(act, w_refs[3 * j + 2][0]))
            pre = nxt


def _grouped_ffn(tile_expert, steps_used, xs, wg, wu, wd):
    n_rows, half = xs.shape
    n_steps = n_rows // (TILES_PER_STEP * ROW_TILE)

    def step_of(i, used):
        return jnp.minimum(i, used[0] - 1)

    row_spec = pl.BlockSpec((TILES_PER_STEP, ROW_TILE, half), lambda i, te, used: (0, step_of(i, used), 0))

    def exp_map(j):
        return lambda i, te, used: (te[j * n_steps + step_of(i, used)], 0, 0)

    w_specs = [pl.BlockSpec((1,) + w.shape[1:], exp_map(j)) for j in range(TILES_PER_STEP) for w in (wg, wu, wd)]
    ys = pl.pallas_call(
        _grouped_ffn_kernel,
        grid_spec=pltpu.PrefetchScalarGridSpec(
            num_scalar_prefetch=2,
            grid=(n_steps,),
            in_specs=[row_spec] + w_specs,
            out_specs=row_spec),
        out_shape=jax.ShapeDtypeStruct((TILES_PER_STEP, n_steps * ROW_TILE, half), U32),
        compiler_params=pltpu.CompilerParams(dimension_semantics=("arbitrary",), vmem_limit_bytes=VMEM_LIMIT),
        name="grouped_ffn",
    )(tile_expert, steps_used, xs.reshape(TILES_PER_STEP, n_steps * ROW_TILE, half),
      *([wg, wu, wd] * TILES_PER_STEP))
    return ys.reshape(n_rows, half)


def _combine_kernel(yt_ref, w_ref, x1_ref, p_ref, gffn_ref, wsg_ref, wsu_ref, wsd_ref,
                    gpi_ref, wpg_ref, wpp_ref, gpo_ref, *rest):
    out_ref = rest[-1]
    n_sub = 2
    sub = x1_ref.shape[0] // n_sub
    wsg, wsu, wsd, wpg, wpp = (r[...].astype(BF16) for r in (wsg_ref, wsu_ref, wsd_ref, wpg_ref, wpp_ref))
    for i in range(n_sub):
        rs = slice(i * sub, (i + 1) * sub)
        x1 = x1_ref[rs]
        w = w_ref[rs]
        h = _rms(x1, gffn_ref[...]).astype(BF16)
        shared = _dot((jax.nn.silu(_dot(h, wsg)) * _dot(h, wsu)).astype(BF16), wsd)
        proj = _rms(_dot(p_ref[rs].astype(BF16), wpp), gpo_ref[...])
        acc_lo = jnp.zeros((sub, HALF), F32)
        acc_hi = jnp.zeros((sub, HALF), F32)
        for k in range(TOP_K):
            lo, hi = _unpack_rows(yt_ref[k, rs])
            wk = w[:, k:k + 1]
            acc_lo += wk * lo
            acc_hi += wk * hi
        x2 = x1 + shared + jnp.concatenate([acc_lo, acc_hi], axis=-1)
        gate = jax.nn.sigmoid(_dot(_rms(x2, gpi_ref[...]).astype(BF16), wpg))
        out_ref[rs] = x2 + gate * proj


def _combine(yt, w_tok, x1, p, consts, tm, chunk, out_so_far):
    t, d = x1.shape
    first = chunk * (t // tm)
    row_spec = lambda w: pl.BlockSpec((tm, w), lambda i: (i, 0))
    full_spec = lambda w: pl.BlockSpec((tm, w), lambda i: (first + i, 0))
    carried = [] if out_so_far is None else [out_so_far]
    return pl.pallas_call(
        _combine_kernel,
        grid=(t // tm,),
        in_specs=[pl.BlockSpec((TOP_K, tm, HALF), lambda i: (0, i, 0)), row_spec(TOP_K), row_spec(d),
                  full_spec(PLE_DIM)] + [_const_spec(c.shape) for c in consts]
                 + [pl.BlockSpec(memory_space=pl.ANY)] * len(carried),
        out_specs=full_spec(d),
        out_shape=jax.ShapeDtypeStruct((t * MOE_CHUNKS, d), F32),
        input_output_aliases={4 + len(consts): 0} if carried else {},
        compiler_params=pltpu.CompilerParams(dimension_semantics=("arbitrary",), vmem_limit_bytes=VMEM_LIMIT),
        name="combine_ple",
    )(yt, w_tok, x1, p, *consts, *carried)


def _rotate_half(a):
    half = QK_ROPE // 2
    return jnp.concatenate([-a[..., half:], a[..., :half]], axis=-1)


def _with_rotate_half(w):
    return jnp.concatenate([w, _rotate_half(w[..., -QK_ROPE:])], axis=-1)


def _rope_gain_table(g, scale):
    half = QK_ROPE // 2
    return (jnp.concatenate([g, g[half:], g[:half]]) * scale).reshape(1, LANES)


def _layer(x, p, pos, g_mix, w_in, sgu_ln_g, sgu_ln_b, sgu_w, sgu_b, mla_g_qa, mla_w_qb, mla_g_kva, mla_w_kvb,
           qk_g_q_nope, qk_g_k_nope, qk_g_q_rope, qk_g_k_rope, w_o, g_ffn, w_router, b_router,
           w_exp_gate, w_exp_up, w_exp_down, w_sh_gate, w_sh_up, w_sh_down,
           g_ple_in, w_ple_gate, w_ple_proj, g_ple_out):
    b, s, d = x.shape
    t = b * s
    row = lambda a: a.reshape(1, -1)
    sizes = [d, d, Q_LORA, KV_LORA, QK_ROPE, d, d]
    offs = [0]
    for sz in sizes:
        offs.append(offs[-1] + sz)
    w_in_t = w_in.T
    w_r_t = _with_rotate_half(w_in[:, offs[4]:offs[5]]).T
    wqb = _with_rotate_half(mla_w_qb.reshape(Q_LORA, HEADS, QK_DIM)).reshape(Q_LORA, HEADS * QK_PAD)
    wkvb = mla_w_kvb.reshape(KV_LORA, HEADS, QK_NOPE + V_DIM)
    wkb = wkvb[:, :, :QK_NOPE].reshape(KV_LORA, HEADS * QK_NOPE)
    wvbt = wkvb[:, :, QK_NOPE:].reshape(KV_LORA, HEADS * V_DIM).T
    sgu_bias = jnp.repeat(sgu_b.T, d // SGU_GROUPS, axis=1)
    q_scale = QK_DIM ** -0.5

    consts1 = [row(g_mix), w_in_t, w_r_t, row(sgu_ln_g), row(sgu_ln_b), sgu_w, sgu_bias,
               row(mla_g_qa), wqb, row(mla_g_kva), wkb, wvbt, row(qk_g_q_nope) * q_scale, row(qk_g_k_nope),
               _rope_gain_table(qk_g_q_rope, q_scale), _rope_gain_table(qk_g_k_rope, 1.0)]
    pos_lanes = jnp.broadcast_to(pos.astype(F32)[..., None], (b, s, LANES))
    ma, gb, q, k, vt = _mixer_prep(x, pos_lanes, consts1, tm=512)
    merged, w_gate_b, w_up_b, w_down_b = _attention(q, k, vt, ma, gb, (w_exp_gate, w_exp_up, w_exp_down),
                                                    tq=256, heads_per_step=2)

    consts3 = [w_o, row(g_ffn), w_router.T, b_router.reshape(-1, 1)]
    consts6 = [row(g_ffn), w_sh_gate, w_sh_up, w_sh_down, row(g_ple_in), w_ple_gate, w_ple_proj, row(g_ple_out)]
    x2d, merged2d, p2d = (a.reshape(t, -1) for a in (x, merged, p))
    tc = t // MOE_CHUNKS
    max_tiles = (tc * TOP_K) // ROW_TILE + N_EXPERTS
    tile_ids = jnp.arange(max_tiles, dtype=I32)
    n_steps = max_tiles // TILES_PER_STEP
    out = None
    for chunk in range(MOE_CHUNKS):
        x1, h_packed, comb_t, rank_t, counts = _post_attn(x2d, merged2d, consts3, tm=1024, chunk=chunk)

        counts = counts[:, 0].astype(I32)
        tiles_per_expert = (counts + ROW_TILE - 1) // ROW_TILE
        tile_end = jnp.cumsum(tiles_per_expert)
        offsets = ((tile_end - tiles_per_expert) * ROW_TILE).astype(F32).reshape(N_EXPERTS, 1)
        steps_used = (tile_end[-1:] + TILES_PER_STEP - 1) // TILES_PER_STEP
        sorted_tile = (tile_ids // n_steps) * steps_used + jnp.minimum(tile_ids % n_steps, steps_used - 1)
        tile_expert = jnp.minimum(jnp.sum((tile_end[None, :] <= sorted_tile[:, None]).astype(I32), axis=1),
                                  N_EXPERTS - 1)
        layout = jnp.concatenate([steps_used, (n_steps - steps_used) * ROW_TILE]).astype(F32).reshape(1, 2)

        pos_t, w_tok = _route_lists(comb_t, rank_t, offsets, layout, tm=2048)
        pos_win = pos_t.reshape(TOP_K, tc // SC_WINDOW, SC_WINDOW).transpose(1, 0, 2)

        xs = _sc_scatter_rows(h_packed, pos_win, max_tiles * ROW_TILE)
        ys = _grouped_ffn(tile_expert, steps_used, xs, w_gate_b, w_up_b, w_down_b)
        yt = _sc_gather_rows(ys, pos_win)
        out = _combine(yt, w_tok, x1, p2d, consts6, tm=512, chunk=chunk, out_so_far=out)
    return out.reshape(b, s, d)


def kernel(x, p, positions, g_mix, w_in, sgu_ln_g, sgu_ln_b, sgu_w, sgu_b, mla_g_qa, mla_w_qb, mla_g_kva, mla_w_kvb, qk_g_q_nope, qk_g_k_nope, qk_g_q_rope, qk_g_k_rope, w_o, g_ffn, w_router, b_router, w_exp_gate, w_exp_up, w_exp_down, w_sh_gate, w_sh_up, w_sh_down, g_ple_in, w_ple_gate, w_ple_proj, g_ple_out):
    params = (g_mix, w_in, sgu_ln_g, sgu_ln_b, sgu_w, sgu_b, mla_g_qa, mla_w_qb, mla_g_kva, mla_w_kvb,
              qk_g_q_nope, qk_g_k_nope, qk_g_q_rope, qk_g_k_rope, w_o, g_ffn, w_router, b_router,
              w_exp_gate, w_exp_up, w_exp_down, w_sh_gate, w_sh_up, w_sh_down,
              g_ple_in, w_ple_gate, w_ple_proj, g_ple_out)
    for l in range(g_mix.shape[0]):
        x = _layer(x, p[l], positions, *[a[l] for a in params])
    return x
```

```python
import functools
import math

import jax
import jax.numpy as jnp
from jax import lax
from jax.experimental import pallas as pl
from jax.experimental.pallas import tpu as pltpu
from jax.experimental.pallas import tpu_sc as plsc

D_MODEL = 1024
PLE_DIM = 256
SGU_CHUNK = 128
SGU_GROUPS = 8
V_DIM = 128
HEADS = 8
QK_NOPE = 128
QK_ROPE = 64
QK_DIM = QK_NOPE + QK_ROPE
QK_PAD = 256
Q_LORA = 384
KV_LORA = 256
ROPE_THETA = 10000.0
N_EXPERTS = 64
N_GROUPS = 8
EXPERTS_PER_GROUP = 8
TOPK_GROUPS = 4
TOP_K = 8
EXPERT_FF = 256
ROUTED_SCALE = 2.5
NORM_EPS = 1e-6
LN_EPS = 1e-5

LANES = 128
MXU_COLS = 256
VMEM_LIMIT = 56 * 1024 * 1024
SC_CORES = 2
SC_SUBCORES = 16
SC_WINDOW = 64
ROW_TILE = 256
TILES_PER_STEP = 8
MOE_CHUNKS = 2
HALF = D_MODEL // 2

F32 = jnp.float32
BF16 = jnp.bfloat16
U32 = jnp.uint32
I32 = jnp.int32


def _dot(a, b):
    return lax.dot_general(a, b, (((1,), (0,)), ((), ())), preferred_element_type=F32)


def _dot_t(a, bt):
    return lax.dot_general(a, bt, (((1,), (1,)), ((), ())), preferred_element_type=F32)


def _rms(xf, g):
    ms = jnp.sum(xf * xf, axis=-1, keepdims=True) * (1.0 / xf.shape[-1])
    return xf * lax.rsqrt(ms + NORM_EPS) * g


def _pack_rows(y):
    lo = pltpu.bitcast(y[:, :HALF].astype(BF16).astype(F32), U32) >> 16
    hi = pltpu.bitcast(y[:, HALF:].astype(BF16).astype(F32), U32) & jnp.uint32(0xFFFF0000)
    return lo | hi


def _unpack_rows(w):
    lo = pltpu.bitcast(w << 16, F32)
    hi = pltpu.bitcast(w & jnp.uint32(0xFFFF0000), F32)
    return lo, hi


def _rope_table(pos_f):
    lane = lax.broadcasted_iota(I32, (1, LANES), 1)
    freq = (lane % (QK_ROPE // 2)).astype(F32)
    inv_freq = jnp.exp(freq * (-math.log(ROPE_THETA) * 2.0 / QK_ROPE))
    phase = jnp.where(lane < QK_ROPE, 0.0, math.pi / 2)
    return jnp.cos(pos_f * inv_freq - phase)


def _norm_rope(piece, table_g):
    lane = lax.broadcasted_iota(I32, (1, LANES), 1)
    ssq = jnp.sum(jnp.where(lane < QK_ROPE, piece * piece, 0.0), axis=-1, keepdims=True)
    z = piece * lax.rsqrt(ssq * (1.0 / QK_ROPE) + NORM_EPS) * table_g
    return z + pltpu.roll(z, QK_ROPE, axis=1)


def _const_spec(shape):
    return pl.BlockSpec(shape, lambda *_: (0,) * len(shape), pipeline_mode=pl.Buffered(1))


def _mixer_prep_kernel(x_ref, pos_ref, g_mix_ref, wt_ref, wrt_ref,
                       lng_ref, lnb_ref, sw_ref, sb_ref, gqa_ref, wqb_ref, gkva_ref, wkb_ref, wvbt_ref,
                       gqn_ref, gkn_ref, gqr_ref, gkr_ref,
                       ma_ref, gb_ref, q_ref, k_ref, vt_ref):
    tm = x_ref.shape[1]
    d = x_ref.shape[2]
    v0, q0, kv0, ga0 = d, 2 * d, 2 * d + Q_LORA, 2 * d + Q_LORA + KV_LORA + QK_ROPE
    gb0 = ga0 + d
    xn = _rms(x_ref[0], g_mix_ref[...]).astype(BF16)

    rope_t = _rope_table(pos_ref[0])
    qn = _rms(_dot_t(xn, wt_ref[q0:q0 + Q_LORA]), gqa_ref[...]).astype(BF16)
    kvn = _rms(_dot_t(xn, wt_ref[kv0:kv0 + KV_LORA]), gkva_ref[...]).astype(BF16)
    lane = lax.broadcasted_iota(I32, (1, LANES), 1)
    kpe = jnp.where(lane < QK_ROPE, _norm_rope(_dot_t(xn, wrt_ref[...]), rope_t * gkr_ref[...]), 0.0).astype(BF16)
    q_rope_t = rope_t * gqr_ref[...]

    def head_pair(pair):
        ps = slice(pair * MXU_COLS, (pair + 1) * MXU_COLS)
        k2 = _dot(kvn, wkb_ref[:, ps])
        v2t = lax.dot_general(wvbt_ref[ps, :].astype(BF16), kvn, (((1,), (1,)), ((), ())),
                              preferred_element_type=F32).astype(BF16)
        for half in range(2):
            g = 2 * pair + half
            hs = slice(half * LANES, (half + 1) * LANES)
            qh = _dot(qn, wqb_ref[:, g * QK_PAD:(g + 1) * QK_PAD])
            q_ref[0, g, :, :QK_NOPE] = _rms(qh[:, :QK_NOPE], gqn_ref[...]).astype(BF16)
            q_ref[0, g, :, QK_NOPE:] = _norm_rope(qh[:, QK_NOPE:], q_rope_t).astype(BF16)
            k_ref[0, g, :, :QK_NOPE] = _rms(k2[:, hs], gkn_ref[...]).astype(BF16)
            k_ref[0, g, :, QK_NOPE:] = kpe
            vt_ref[0, g] = v2t[hs, :]

    head_pair(0)

    gv = jax.nn.gelu(_dot_t(xn, wt_ref[v0:v0 + d]))
    mu = jnp.mean(gv, axis=-1, keepdims=True)
    vc = gv - mu
    var = jnp.mean(vc * vc, axis=-1, keepdims=True)
    vn = (vc * lax.rsqrt(var + LN_EPS) * lng_ref[...] + lnb_ref[...]).astype(BF16)
    row = lax.broadcasted_iota(I32, (SGU_CHUNK, SGU_CHUNK), 0)
    col = lax.broadcasted_iota(I32, (SGU_CHUNK, SGU_CHUNK), 1)
    causal = col <= row
    n_chunks = tm // SGU_CHUNK

    def sgu_pair(pair):
        ps = slice(pair * MXU_COLS, (pair + 1) * MXU_COLS)
        p0 = pair * MXU_COLS
        gu2 = jax.nn.gelu(_dot_t(xn, wt_ref[p0:p0 + MXU_COLS]))
        ga2 = jax.nn.sigmoid(_dot_t(xn, wt_ref[ga0 + p0:ga0 + p0 + MXU_COLS]))
        gb_ref[0, :, ps] = jax.nn.sigmoid(_dot_t(xn, wt_ref[gb0 + p0:gb0 + p0 + MXU_COLS])).astype(BF16)
        for half in range(2):
            g = 2 * pair + half
            hs = slice(half * LANES, (half + 1) * LANES)
            cs = slice(g * SGU_CHUNK, (g + 1) * SGU_CHUNK)
            wg = jnp.where(causal, sw_ref[g], 0.0).astype(BF16)
            vcat = jnp.concatenate([vn[c * SGU_CHUNK:(c + 1) * SGU_CHUNK, cs] for c in range(n_chunks)], axis=1)
            mixed = _dot(wg, vcat)
            for c in range(n_chunks):
                rs = slice(c * SGU_CHUNK, (c + 1) * SGU_CHUNK)
                m = mixed[:, c * SGU_CHUNK:(c + 1) * SGU_CHUNK] + sb_ref[:, cs]
                ma_ref[0, rs, cs] = (ga2[rs, hs] * gu2[rs, hs] * m).astype(BF16)

    n_pairs = SGU_GROUPS // 2
    for pair in range(n_pairs):
        if pair + 1 < n_pairs:
            head_pair(pair + 1)
        sgu_pair(pair)


def _mixer_prep(x, pos, consts, tm):
    b, s, d = x.shape
    grid = (b, s // tm)
    row_spec = lambda w: pl.BlockSpec((1, tm, w), lambda i, j: (i, j, 0))
    head_spec = lambda w: pl.BlockSpec((1, HEADS, tm, w), lambda i, j: (i, 0, j, 0))
    return pl.pallas_call(
        _mixer_prep_kernel,
        grid=grid,
        in_specs=[row_spec(d), row_spec(LANES)] + [_const_spec(c.shape) for c in consts],
        out_specs=[row_spec(d), row_spec(d), head_spec(QK_PAD), head_spec(QK_PAD),
                   pl.BlockSpec((1, HEADS, V_DIM, tm), lambda i, j: (i, 0, 0, j))],
        out_shape=[jax.ShapeDtypeStruct((b, s, d), BF16), jax.ShapeDtypeStruct((b, s, d), BF16),
                   jax.ShapeDtypeStruct((b, HEADS, s, QK_PAD), BF16),
                   jax.ShapeDtypeStruct((b, HEADS, s, QK_PAD), BF16),
                   jax.ShapeDtypeStruct((b, HEADS, V_DIM, s), BF16)],
        compiler_params=pltpu.CompilerParams(dimension_semantics=("arbitrary", "arbitrary"),
                                             vmem_limit_bytes=VMEM_LIMIT),
        name="mixer_prep",
    )(x, pos, *consts)


def _attn_kernel(q_ref, k_ref, vt_ref, ma_ref, gb_ref, *refs, tq):
    n_w = (len(refs) - 1) // 2
    o_ref = refs[n_w]
    for w_ref, wb_ref in zip(refs[:n_w], refs[n_w + 1:]):
        wb_ref[...] = w_ref[...].astype(BF16)
    s = q_ref.shape[2]
    key = lax.broadcasted_iota(I32, (tq, tq), 0)
    qry = lax.broadcasted_iota(I32, (tq, tq), 1)
    diag_mask = key <= qry
    heads = range(q_ref.shape[1])
    n_tiles = s // tq

    def score_dots(qi):
        return [lax.dot_general(k_ref[0, h, :(qi + 1) * tq, :], q_ref[0, h, qi * tq:(qi + 1) * tq, :],
                                (((1,), (1,)), ((), ())), preferred_element_type=F32) for h in heads]

    scores = score_dots(0)
    for qi in range(n_tiles):
        qs = slice(qi * tq, (qi + 1) * tq)
        n_keys = (qi + 1) * tq
        next_scores = score_dots(qi + 1) if qi + 1 < n_tiles else None
        probs, sums = [], []
        for h in heads:
            sc = scores[h]
            last = jnp.where(diag_mask, sc[n_keys - tq:], -jnp.inf)
            sc = last if qi == 0 else jnp.concatenate([sc[:n_keys - tq], last], axis=0)
            p = jnp.exp(sc - jnp.max(sc, axis=0, keepdims=True))
            sums.append(jnp.sum(p, axis=0, keepdims=True))
            probs.append(p.astype(BF16))
        for h in heads:
            acc = _dot(vt_ref[0, h, :, :n_keys], probs[h])
            cs = slice(h * V_DIM, (h + 1) * V_DIM)
            o = (acc / sums[h]).T
            o_ref[0, qs, cs] = (ma_ref[0, qs, cs].astype(F32) + gb_ref[0, qs, cs].astype(F32) * o).astype(BF16)
        scores = next_scores


def _attention(q, k, vt, ma, gb, expert_weights, tq, heads_per_step):
    b, h, s, _ = q.shape
    hp = heads_per_step
    n_steps = b * (h // hp)
    w_specs = [pl.BlockSpec((w.shape[0] // n_steps,) + w.shape[1:], lambda i, j: (i * (h // hp) + j, 0, 0))
               for w in expert_weights]
    col_spec = pl.BlockSpec((1, s, hp * V_DIM), lambda i, j: (i, 0, j))
    return pl.pallas_call(
        functools.partial(_attn_kernel, tq=tq),
        grid=(b, h // hp),
        in_specs=[pl.BlockSpec((1, hp, s, QK_PAD), lambda i, j: (i, j, 0, 0)),
                  pl.BlockSpec((1, hp, s, QK_PAD), lambda i, j: (i, j, 0, 0)),
                  pl.BlockSpec((1, hp, V_DIM, s), lambda i, j: (i, j, 0, 0)), col_spec, col_spec] + w_specs,
        out_specs=[col_spec] + w_specs,
        out_shape=[jax.ShapeDtypeStruct((b, s, h * V_DIM), BF16)]
                  + [jax.ShapeDtypeStruct(w.shape, BF16) for w in expert_weights],
        compiler_params=pltpu.CompilerParams(dimension_semantics=("arbitrary", "arbitrary"),
                                             vmem_limit_bytes=VMEM_LIMIT),
        name="mla_attention",
    )(q, k, vt, ma, gb, *expert_weights)


def _route(sel, scores):
    e, tm = sel.shape
    sel3 = sel.reshape(N_GROUPS, EXPERTS_PER_GROUP, tm)
    sub = lax.broadcasted_iota(I32, sel3.shape, 1)
    m1 = jnp.max(sel3, axis=1, keepdims=True)
    first = jnp.min(jnp.where(sel3 == m1, sub, EXPERTS_PER_GROUP), axis=1, keepdims=True)
    m2 = jnp.max(jnp.where(sub == first, -jnp.inf, sel3), axis=1, keepdims=True)
    gscore = (m1 + m2).reshape(N_GROUPS, tm)
    gid = lax.broadcasted_iota(I32, (N_GROUPS, tm), 0)
    grank = jnp.zeros((N_GROUPS, tm), I32)
    for g in range(N_GROUPS):
        other = gscore[g:g + 1]
        grank += ((other > gscore) | ((other == gscore) & (g < gid))).astype(I32)
    gmask = grank < TOPK_GROUPS
    emask = jnp.broadcast_to(gmask[:, None, :], sel3.shape).reshape(e, tm)
    remaining = jnp.where(emask, sel, -jnp.inf)
    eid = lax.broadcasted_iota(I32, (e, tm), 0)
    chosen = jnp.zeros((e, tm), jnp.bool_)
    for _ in range(TOP_K):
        best = jnp.max(remaining, axis=0, keepdims=True)
        pick = eid == jnp.min(jnp.where(remaining == best, eid, e), axis=0, keepdims=True)
        chosen = chosen | pick
        remaining = jnp.where(pick, -jnp.inf, remaining)
    w = jnp.where(chosen, scores, 0.0)
    return chosen, w / jnp.sum(w, axis=0, keepdims=True) * ROUTED_SCALE


def _post_attn_kernel(x_ref, merged_ref, wo_ref, gffn_ref, wrt_ref, br_ref,
                      x1_ref, hp_ref, comb_ref, rank_ref, cnt_ref, run_ref):
    tm = x_ref.shape[0]

    @pl.when(pl.program_id(0) == 0)
    def _():
        run_ref[...] = jnp.zeros_like(run_ref)

    x1 = x_ref[...] + _dot(merged_ref[...], wo_ref[...])
    x1_ref[...] = x1
    h2 = _rms(x1, gffn_ref[...])
    hp_ref[...] = _pack_rows(h2)
    h_hi = h2.astype(BF16)
    h_lo = (h2 - h_hi.astype(F32)).astype(BF16)
    w = wrt_ref[...]
    w_hi = w.astype(BF16)
    w_lo = (w - w_hi.astype(F32)).astype(BF16)
    n_e = w.shape[0]
    both = _dot_t(jnp.concatenate([w_hi, w_lo], axis=0), h_hi)
    logits_t = both[:n_e] + both[n_e:] + _dot_t(w_hi, h_lo)
    scores = jax.nn.sigmoid(logits_t)
    chosen, comb_t = _route(scores + br_ref[...], scores)
    comb_ref[...] = comb_t
    a = lax.broadcasted_iota(I32, (tm, tm), 0)
    b = lax.broadcasted_iota(I32, (tm, tm), 1)
    before = (a < b).astype(BF16)
    chosen_f = chosen.astype(F32)
    prefix = _dot(chosen_f.astype(BF16), before)
    run = run_ref[:, 0:1]
    rank_ref[...] = jnp.where(chosen, prefix + run, -1.0)
    run_ref[...] += jnp.sum(chosen_f, axis=1, keepdims=True)
    cnt_ref[...] = run_ref[...]


def _post_attn(x, merged, consts, tm, chunk):
    t, d = x.shape[0] // MOE_CHUNKS, x.shape[1]
    first = chunk * (t // tm)
    row_spec = lambda w: pl.BlockSpec((tm, w), lambda i: (i, 0))
    col_spec = pl.BlockSpec((N_EXPERTS, tm), lambda i: (0, i))
    return pl.pallas_call(
        _post_attn_kernel,
        grid=(t // tm,),
        in_specs=[pl.BlockSpec((tm, d), lambda i: (first + i, 0))] * 2 + [_const_spec(c.shape) for c in consts],
        out_specs=[row_spec(d), row_spec(HALF), col_spec, col_spec, _const_spec((N_EXPERTS, LANES))],
        out_shape=[jax.ShapeDtypeStruct((t, d), F32), jax.ShapeDtypeStruct((t, HALF), U32),
                   jax.ShapeDtypeStruct((N_EXPERTS, t), F32), jax.ShapeDtypeStruct((N_EXPERTS, t), F32),
                   jax.ShapeDtypeStruct((N_EXPERTS, LANES), F32)],
        scratch_shapes=[pltpu.VMEM((N_EXPERTS, LANES), F32)],
        compiler_params=pltpu.CompilerParams(dimension_semantics=("arbitrary",), vmem_limit_bytes=VMEM_LIMIT),
        name="post_attn_router",
    )(x, merged, *consts)


def _route_lists_kernel(comb_ref, rank_ref, off_ref, lay_ref, pos_ref, w_ref):
    rank = rank_ref[...]
    chosen = rank >= 0.0
    e = rank.shape[0]
    lower = (lax.broadcasted_iota(I32, (e, e), 1) < lax.broadcasted_iota(I32, (e, e), 0)).astype(BF16)
    slot = _dot(lower, chosen.astype(BF16))
    pos_sorted = rank + off_ref[...]
    tile = jnp.floor(pos_sorted * (1.0 / ROW_TILE))
    pos_full = pos_sorted + jnp.floor((tile + 0.5) / lay_ref[0:1, 0:1]) * lay_ref[0:1, 1:2]
    comb = comb_ref[...]
    pos_rows, w_rows = [], []
    for k in range(TOP_K):
        pick = chosen & (slot == float(k))
        pos_rows.append(jnp.sum(jnp.where(pick, pos_full, 0.0), axis=0, keepdims=True))
        w_rows.append(jnp.sum(jnp.where(pick, comb, 0.0), axis=0, keepdims=True))
    pos_ref[...] = jnp.concatenate(pos_rows, axis=0).astype(I32)
    w_ref[...] = jnp.concatenate(w_rows, axis=0).T


def _route_lists(comb_t, rank_t, offsets, layout, tm):
    e, t = comb_t.shape
    col_spec = pl.BlockSpec((e, tm), lambda i: (0, i))
    return pl.pallas_call(
        _route_lists_kernel,
        grid=(t // tm,),
        in_specs=[col_spec, col_spec, _const_spec((e, 1)), _const_spec(layout.shape)],
        out_specs=[pl.BlockSpec((TOP_K, tm), lambda i: (0, i)), pl.BlockSpec((tm, TOP_K), lambda i: (i, 0))],
        out_shape=[jax.ShapeDtypeStruct((TOP_K, t), I32), jax.ShapeDtypeStruct((t, TOP_K), F32)],
        compiler_params=pltpu.CompilerParams(dimension_semantics=("arbitrary",)),
        name="route_lists",
    )(comb_t, rank_t, offsets, layout)


def _sc_mesh():
    return plsc.VectorSubcoreMesh(core_axis_name="c", subcore_axis_name="s",
                                  num_cores=SC_CORES, num_subcores=SC_SUBCORES)


def _sc_worker():
    return lax.axis_index("s") * SC_CORES + lax.axis_index("c")


def _sc_scatter_rows(src, pos_win, n_rows):
    t, width = src.shape
    n_win = t // SC_WINDOW
    per_w = n_win // (SC_CORES * SC_SUBCORES)

    @functools.partial(
        pl.kernel, mesh=_sc_mesh(), out_type=jax.ShapeDtypeStruct((n_rows, width), src.dtype),
        scratch_types=[pltpu.VMEM((TOP_K, SC_WINDOW), I32), pltpu.VMEM((SC_WINDOW, width), src.dtype),
                       pltpu.SemaphoreType.DMA],
        name="sc_scatter_rows")
    def run(src_hbm, pos_hbm, out_hbm, idx_v, rows_v, sem):
        base = _sc_worker() * per_w

        @pl.loop(0, per_w)
        def _(j):
            w = base + j
            pltpu.sync_copy(pos_hbm.at[w], idx_v)
            pltpu.sync_copy(src_hbm.at[pl.ds(w * SC_WINDOW, SC_WINDOW)], rows_v)
            copies = [pltpu.async_copy(rows_v, out_hbm.at[idx_v.at[k]], sem) for k in range(TOP_K)]
            for c in copies:
                c.wait()

    return run(src, pos_win)


def _sc_gather_rows(table, pos_win):
    n_win = pos_win.shape[0]
    width = table.shape[1]
    per_w = n_win // (SC_CORES * SC_SUBCORES)

    @functools.partial(
        pl.kernel, mesh=_sc_mesh(),
        out_type=jax.ShapeDtypeStruct((TOP_K, n_win * SC_WINDOW, width), table.dtype),
        scratch_types=[pltpu.VMEM((TOP_K, SC_WINDOW), I32), pltpu.VMEM((2, SC_WINDOW, width), table.dtype),
                       pltpu.SemaphoreType.DMA, pltpu.SemaphoreType.DMA],
        name="sc_gather_rows")
    def run(table_hbm, pos_hbm, out_hbm, idx_v, rows_v, gsem, wsem):
        base = _sc_worker() * per_w

        @pl.loop(0, per_w)
        def _(j):
            w = base + j
            pltpu.sync_copy(pos_hbm.at[w], idx_v)
            for k in range(TOP_K):
                buf = rows_v.at[k % 2]
                pltpu.async_copy(table_hbm.at[idx_v.at[k]], buf, gsem).wait()
                pltpu.async_copy(buf, out_hbm.at[k, pl.ds(w * SC_WINDOW, SC_WINDOW)], wsem).wait()

    return run(table, pos_win)


def _grouped_ffn_kernel(te_ref, used_ref, xs_ref, *refs):
    w_refs, ys_ref = refs[:-1], refs[-1]

    @pl.when(pl.program_id(0) < used_ref[0])
    def _():
        def gate_up(j):
            wg_ref, wu_ref = w_refs[3 * j:3 * j + 2]
            lo, hi = _unpack_rows(xs_ref[j])
            lo, hi = lo.astype(BF16), hi.astype(BF16)
            return (_dot(lo, wg_ref[0, :HALF]) + _dot(hi, wg_ref[0, HALF:]),
                    _dot(lo, wu_ref[0, :HALF]) + _dot(hi, wu_ref[0, HALF:]))

        pre = gate_up(0)
        for j in range(TILES_PER_STEP):
            nxt = gate_up(j + 1) if j + 1 < TILES_PER_STEP else None
            act = (jax.nn.silu(pre[0]) * pre[1]).astype(BF16)
            ys_ref[j] = _pack_rows(_dot(act, w_refs[3 * j + 2][0]))
            pre = nxt


def _grouped_ffn(tile_expert, steps_used, xs, wg, wu, wd):
    n_rows, half = xs.shape
    n_steps = n_rows // (TILES_PER_STEP * ROW_TILE)

    def step_of(i, used):
        return jnp.minimum(i, used[0] - 1)

    row_spec = pl.BlockSpec((TILES_PER_STEP, ROW_TILE, half), lambda i, te, used: (0, step_of(i, used), 0))

    def exp_map(j):
        return lambda i, te, used: (te[j * n_steps + step_of(i, used)], 0, 0)

    w_specs = [pl.BlockSpec((1,) + w.shape[1:], exp_map(j)) for j in range(TILES_PER_STEP) for w in (wg, wu, wd)]
    ys = pl.pallas_call(
        _grouped_ffn_kernel,
        grid_spec=pltpu.PrefetchScalarGridSpec(
            num_scalar_prefetch=2,
            grid=(n_steps,),
            in_specs=[row_spec] + w_specs,
            out_specs=row_spec),
        out_shape=jax.ShapeDtypeStruct((TILES_PER_STEP, n_steps * ROW_TILE, half), U32),
        compiler_params=pltpu.CompilerParams(dimension_semantics=("arbitrary",), vmem_limit_bytes=VMEM_LIMIT),
        name="grouped_ffn",
    )(tile_expert, steps_used, xs.reshape(TILES_PER_STEP, n_steps * ROW_TILE, half),
      *([wg, wu, wd] * TILES_PER_STEP))
    return ys.reshape(n_rows, half)


def _combine_kernel(yt_ref, w_ref, x1_ref, p_ref, gffn_ref, wsg_ref, wsu_ref, wsd_ref,
                    gpi_ref, wpg_ref, wpp_ref, gpo_ref, *rest):
    out_ref = rest[-1]
    n_sub = 2
    sub = x1_ref.shape[0] // n_sub
    wsg, wsu, wsd, wpg, wpp = (r[...].astype(BF16) for r in (wsg_ref, wsu_ref, wsd_ref, wpg_ref, wpp_ref))
    for i in range(n_sub):
        rs = slice(i * sub, (i + 1) * sub)
        x1 = x1_ref[rs]
        w = w_ref[rs]
        h = _rms(x1, gffn_ref[...]).astype(BF16)
        shared = _dot((jax.nn.silu(_dot(h, wsg)) * _dot(h, wsu)).astype(BF16), wsd)
        proj = _rms(_dot(p_ref[rs].astype(BF16), wpp), gpo_ref[...])
        acc_lo = jnp.zeros((sub, HALF), F32)
        acc_hi = jnp.zeros((sub, HALF), F32)
        for k in range(TOP_K):
            lo, hi = _unpack_rows(yt_ref[k, rs])
            wk = w[:, k:k + 1]
            acc_lo += wk * lo
            acc_hi += wk * hi
        x2 = x1 + shared + jnp.concatenate([acc_lo, acc_hi], axis=-1)
        gate = jax.nn.sigmoid(_dot(_rms(x2, gpi_ref[...]).astype(BF16), wpg))
        out_ref[rs] = x2 + gate * proj


def _combine(yt, w_tok, x1, p, consts, tm, chunk, out_so_far):
    t, d = x1.shape
    first = chunk * (t // tm)
    row_spec = lambda w: pl.BlockSpec((tm, w), lambda i: (i, 0))
    full_spec = lambda w: pl.BlockSpec((tm, w), lambda i: (first + i, 0))
    carried = [] if out_so_far is None else [out_so_far]
    return pl.pallas_call(
        _combine_kernel,
        grid=(t // tm,),
        in_specs=[pl.BlockSpec((TOP_K, tm, HALF), lambda i: (0, i, 0)), row_spec(TOP_K), row_spec(d),
                  full_spec(PLE_DIM)] + [_const_spec(c.shape) for c in consts]
                 + [pl.BlockSpec(memory_space=pl.ANY)] * len(carried),
        out_specs=full_spec(d),
        out_shape=jax.ShapeDtypeStruct((t * MOE_CHUNKS, d), F32),
        input_output_aliases={4 + len(consts): 0} if carried else {},
        compiler_params=pltpu.CompilerParams(dimension_semantics=("arbitrary",), vmem_limit_bytes=VMEM_LIMIT),
        name="combine_ple",
    )(yt, w_tok, x1, p, *consts, *carried)


def _rotate_half(a):
    half = QK_ROPE // 2
    return jnp.concatenate([-a[..., half:], a[..., :half]], axis=-1)


def _with_rotate_half(w):
    return jnp.concatenate([w, _rotate_half(w[..., -QK_ROPE:])], axis=-1)


def _rope_gain_table(g, scale):
    half = QK_ROPE // 2
    return (jnp.concatenate([g, g[half:], g[:half]]) * scale).reshape(1, LANES)


def _layer(x, p, pos, g_mix, w_in, sgu_ln_g, sgu_ln_b, sgu_w, sgu_b, mla_g_qa, mla_w_qb, mla_g_kva, mla_w_kvb,
           qk_g_q_nope, qk_g_k_nope, qk_g_q_rope, qk_g_k_rope, w_o, g_ffn, w_router, b_router,
           w_exp_gate, w_exp_up, w_exp_down, w_sh_gate, w_sh_up, w_sh_down,
           g_ple_in, w_ple_gate, w_ple_proj, g_ple_out):
    b, s, d = x.shape
    t = b * s
    row = lambda a: a.reshape(1, -1)
    sizes = [d, d, Q_LORA, KV_LORA, QK_ROPE, d, d]
    offs = [0]
    for sz in sizes:
        offs.append(offs[-1] + sz)
    w_in_t = w_in.T
    w_r_t = _with_rotate_half(w_in[:, offs[4]:offs[5]]).T
    wqb = _with_rotate_half(mla_w_qb.reshape(Q_LORA, HEADS, QK_DIM)).reshape(Q_LORA, HEADS * QK_PAD)
    wkvb = mla_w_kvb.reshape(KV_LORA, HEADS, QK_NOPE + V_DIM)
    wkb = wkvb[:, :, :QK_NOPE].reshape(KV_LORA, HEADS * QK_NOPE)
    wvbt = wkvb[:, :, QK_NOPE:].reshape(KV_LORA, HEADS * V_DIM).T
    sgu_bias = jnp.repeat(sgu_b.T, d // SGU_GROUPS, axis=1)
    q_scale = QK_DIM ** -0.5

    consts1 = [row(g_mix), w_in_t, w_r_t, row(sgu_ln_g), row(sgu_ln_b), sgu_w, sgu_bias,
               row(mla_g_qa), wqb, row(mla_g_kva), wkb, wvbt, row(qk_g_q_nope) * q_scale, row(qk_g_k_nope),
               _rope_gain_table(qk_g_q_rope, q_scale), _rope_gain_table(qk_g_k_rope, 1.0)]
    pos_lanes = jnp.broadcast_to(pos.astype(F32)[..., None], (b, s, LANES))
    ma, gb, q, k, vt = _mixer_prep(x, pos_lanes, consts1, tm=512)
    merged, w_gate_b, w_up_b, w_down_b = _attention(q, k, vt, ma, gb, (w_exp_gate, w_exp_up, w_exp_down),
                                                    tq=256, heads_per_step=2)

    consts3 = [w_o, row(g_ffn), w_router.T, b_router.reshape(-1, 1)]
    consts6 = [row(g_ffn), w_sh_gate, w_sh_up, w_sh_down, row(g_ple_in), w_ple_gate, w_ple_proj, row(g_ple_out)]
    x2d, merged2d, p2d = (a.reshape(t, -1) for a in (x, merged, p))
    tc = t // MOE_CHUNKS
    max_tiles = (tc * TOP_K) // ROW_TILE + N_EXPERTS
    tile_ids = jnp.arange(max_tiles, dtype=I32)
    n_steps = max_tiles // TILES_PER_STEP
    out = None
    for chunk in range(MOE_CHUNKS):
        x1, h_packed, comb_t, rank_t, counts = _post_attn(x2d, merged2d, consts3, tm=1024, chunk=chunk)

        counts = counts[:, 0].astype(I32)
        tiles_per_expert = (counts + ROW_TILE - 1) // ROW_TILE
        tile_end = jnp.cumsum(tiles_per_expert)
        offsets = ((tile_end - tiles_per_expert) * ROW_TILE).astype(F32).reshape(N_EXPERTS, 1)
        steps_used = (tile_end[-1:] + TILES_PER_STEP - 1) // TILES_PER_STEP
        sorted_tile = (tile_ids // n_steps) * steps_used + jnp.minimum(tile_ids % n_steps, steps_used - 1)
        tile_expert = jnp.minimum(jnp.sum((tile_end[None, :] <= sorted_tile[:, None]).astype(I32), axis=1),
                                  N_EXPERTS - 1)
        layout = jnp.concatenate([steps_used, (n_steps - steps_used) * ROW_TILE]).astype(F32).reshape(1, 2)

        pos_t, w_tok = _route_lists(comb_t, rank_t, offsets, layout, tm=2048)
        pos_win = pos_t.reshape(TOP_K, tc // SC_WINDOW, SC_WINDOW).transpose(1, 0, 2)

        xs = _sc_scatter_rows(h_packed, pos_win, max_tiles * ROW_TILE)
        ys = _grouped_ffn(tile_expert, steps_used, xs, w_gate_b, w_up_b, w_down_b)
        yt = _sc_gather_rows(ys, pos_win)
        out = _combine(yt, w_tok, x1, p2d, consts6, tm=512, chunk=chunk, out_so_far=out)
    return out.reshape(b, s, d)


def kernel(x, p, positions, g_mix, w_in, sgu_ln_g, sgu_ln_b, sgu_w, sgu_b, mla_g_qa, mla_w_qb, mla_g_kva, mla_w_kvb, qk_g_q_nope, qk_g_k_nope, qk_g_q_rope, qk_g_k_rope, w_o, g_ffn, w_router, b_router, w_exp_gate, w_exp_up, w_exp_down, w_sh_gate, w_sh_up, w_sh_down, g_ple_in, w_ple_gate, w_ple_proj, g_ple_out):
    params = (g_mix, w_in, sgu_ln_g, sgu_ln_b, sgu_w, sgu_b, mla_g_qa, mla_w_qb, mla_g_kva, mla_w_kvb,
              qk_g_q_nope, qk_g_k_nope, qk_g_q_rope, qk_g_k_rope, w_o, g_ffn, w_router, b_router,
              w_exp_gate, w_exp_up, w_exp_down, w_sh_gate, w_sh_up, w_sh_down,
              g_ple_in, w_ple_gate, w_ple_proj, g_ple_out)
    for l in range(g_mix.shape[0]):
        x = _layer(x, p[l], positions, *[a[l] for a in params])
    return x
```
